```python
import math, functools
import jax, jax.numpy as jnp
from jax import lax
import numpy as np

D_MODEL = 2048
BATCH = 4
SEQ = 2048
DEPTH = 1
DEC_BATCH = 128
DEC_SEQ = 4
PAST_LEN = 2048
PAGE_SIZE = 128

GDN_HEADS = 8
GDN_DK = 128
GDN_DV = 128
GDN_CONV = 4
GDN_CHUNK = 64
GDN_QK_W = GDN_HEADS * GDN_DK
GDN_V_W = GDN_HEADS * GDN_DV
DSA_HEADS = 8
DSA_KV_HEADS = 2
DSA_HEAD_DIM = 128
IDX_HEADS = 8
IDX_DIM = 64
IDX_SCALE = IDX_HEADS ** -0.5 * IDX_DIM ** -0.5
DSA_TOPK = 256
Q_BLOCK = 128
ROPE_THETA = 500000.0
ROPE_FRACTION = 4
D_FF = 5632
FFN_CONV = 3
NORM_EPS = 1e-6

kernel_name = "hybrid_gdn_dsa_convffn_decode_step"


def _in_sizes():
    return (2 * GDN_QK_W + GDN_V_W, GDN_V_W, GDN_HEADS, GDN_HEADS,
            DSA_HEADS * DSA_HEAD_DIM, DSA_KV_HEADS * DSA_HEAD_DIM, DSA_KV_HEADS * DSA_HEAD_DIM,
            IDX_HEADS * IDX_DIM, IDX_DIM, IDX_HEADS, D_MODEL, D_MODEL)


def _split_cols(y):
    cuts, acc = [], 0
    for s in _in_sizes()[:-1]:
        acc += s
        cuts.append(acc)
    return jnp.split(y, cuts, axis=-1)


def _rmsnorm(x, g):
    xf = x.astype(jnp.float32)
    y = xf * lax.rsqrt(jnp.mean(xf * xf, axis=-1, keepdims=True) + NORM_EPS)
    return (y * g.astype(jnp.float32)).astype(x.dtype)


def _l2norm(x):
    xf = x.astype(jnp.float32)
    return xf * lax.rsqrt(jnp.sum(xf * xf, axis=-1, keepdims=True) + NORM_EPS)


def _partial_rotary(x, pos):
    d = x.shape[-1]
    rot = d // ROPE_FRACTION
    half = rot // 2
    inv_freq = ROPE_THETA ** (-jnp.arange(half, dtype=jnp.float32) * (2.0 / rot))
    ang = pos.astype(jnp.float32)[:, None] * inv_freq[None, :]
    cos = jnp.cos(ang)[:, None, :]
    sin = jnp.sin(ang)[:, None, :]
    xf = x.astype(jnp.float32)
    x1, x2 = xf[..., :half], xf[..., half:rot]
    out = jnp.concatenate([x1 * cos - x2 * sin, x2 * cos + x1 * sin, xf[..., rot:]], axis=-1)
    return out.astype(x.dtype)


def _causal_dwconv(x, buf, w):
    width = w.shape[0]
    t = x.shape[1]
    xx = jnp.concatenate([buf.astype(x.dtype), x], axis=1)
    y = xx[:, 0:t] * w[0]
    for i in range(1, width):
        y = y + xx[:, i:i + t] * w[i]
    return y, xx[:, -(width - 1):]


def _gated_delta_chunked(q, k, v, g, beta, s0):
    f32 = jnp.float32
    B, T, H, DK = q.shape
    DV = v.shape[-1]
    C = min(GDN_CHUNK, T)
    n = -(-T // C)
    pad = n * C - T

    def prep(a):
        a = a.astype(f32)
        a = jnp.pad(a, [(0, 0), (0, pad)] + [(0, 0)] * (a.ndim - 2))
        a = a.reshape((B, n, C) + a.shape[2:])
        a = jnp.moveaxis(a, 3, 2)
        return jnp.moveaxis(a, 1, 0)

    q, k, v, g, beta = prep(q), prep(k), prep(v), prep(g), prep(beta)
    gc = jnp.cumsum(g, axis=-1)
    incl = jnp.tril(jnp.ones((C, C), bool))
    strict = jnp.tril(jnp.ones((C, C), bool), -1)
    decay = jnp.exp(jnp.where(incl, gc[..., :, None] - gc[..., None, :], -jnp.inf))
    kk = jnp.einsum('nbhcd,nbhjd->nbhcj', k, k)
    a_mat = jnp.where(strict, beta[..., :, None] * kk * decay, 0.0)
    lhs = a_mat + jnp.eye(C, dtype=f32)
    rhs = jnp.concatenate([beta[..., None] * v, (beta * jnp.exp(gc))[..., None] * k], axis=-1)
    sol = lax.linalg.triangular_solve(lhs, rhs, left_side=True, lower=True, unit_diagonal=True)
    u_v, w_k = sol[..., :DV], sol[..., DV:]
    qk = jnp.einsum('nbhcd,nbhjd->nbhcj', q, k) * decay
    q_dec = q * jnp.exp(gc)[..., None]
    k_end = k * jnp.exp(gc[..., -1:] - gc)[..., None]
    g_end = jnp.exp(gc[..., -1])

    def step(S, xs):
        u_v_i, w_k_i, qk_i, q_dec_i, k_end_i, g_end_i = xs
        u = u_v_i - jnp.einsum('bhck,bhkv->bhcv', w_k_i, S)
        o = jnp.einsum('bhck,bhkv->bhcv', q_dec_i, S) + jnp.einsum('bhcj,bhjv->bhcv', qk_i, u)
        S = S * g_end_i[..., None, None] + jnp.einsum('bhck,bhcv->bhkv', k_end_i, u)
        return S, o

    s_fin, o = lax.scan(step, s0.astype(f32), (u_v, w_k, qk, q_dec, k_end, g_end))
    o = jnp.moveaxis(o, 0, 1)
    o = jnp.moveaxis(o, 2, 3).reshape(B, n * C, H, DV)[:, :T]
    return o, s_fin


def _gather_rows(x, idx):
    return jax.vmap(lambda xb, ib: xb[ib])(x, idx)


def _indexer_topk(iq, iw, ik, pos_q, pos_k, topk):
    rel = jax.nn.relu(jnp.einsum('bqhd,bkd->bqhk', iq.astype(jnp.float32), ik.astype(jnp.float32)))
    score = jnp.einsum('bqhk,bqh->bqk', rel, iw.astype(jnp.float32) * IDX_SCALE)
    causal = pos_k[None, None, :] <= pos_q[None, :, None]
    score = jnp.where(causal, score, -jnp.inf)
    _, idx = lax.top_k(score, topk)
    valid = jnp.take(pos_k, idx) <= pos_q[None, :, None]
    return idx, valid


def _attend_selected(q, k_sel, v_sel, valid):
    B, Tq, H, HD = q.shape
    KV = k_sel.shape[-2]
    qg = q.reshape(B, Tq, KV, H // KV, HD).astype(jnp.float32)
    s = jnp.einsum('bqkgd,bqjkd->bqkgj', qg, k_sel.astype(jnp.float32)) * (HD ** -0.5)
    s = jnp.where(valid[:, :, None, None, :], s, -jnp.inf)
    p = jax.nn.softmax(s, axis=-1)
    o = jnp.einsum('bqkgj,bqjkd->bqkgd', p.astype(v_sel.dtype), v_sel)
    return o.reshape(B, Tq, H * HD)


def _dsa_prompt(q, k, v, iq, ik, iw, pos):
    B, T = q.shape[:2]
    topk = min(DSA_TOPK, T // 4)

    def block(i):
        s = i * Q_BLOCK
        sl = lambda a: lax.dynamic_slice_in_dim(a, s, Q_BLOCK, axis=1)
        pos_q = lax.dynamic_slice_in_dim(pos, s, Q_BLOCK)
        idx, valid = _indexer_topk(sl(iq), sl(iw), ik, pos_q, pos, topk)
        return _attend_selected(sl(q), _gather_rows(k, idx), _gather_rows(v, idx), valid)

    o = lax.map(block, jnp.arange(T // Q_BLOCK))
    return jnp.moveaxis(o, 0, 1).reshape(B, T, -1)


def _dsa_sample(q, k, v, iq, ik, iw, pos, cache_k, cache_v, cache_idx_k, page_table):
    DB, Ts = q.shape[:2]
    page = cache_k.shape[1]
    past = page_table.shape[1] * page
    ik_past = cache_idx_k[page_table].reshape(DB, past, IDX_DIM)
    ik_all = jnp.concatenate([ik_past.astype(ik.dtype), ik], axis=1)
    L = past + Ts
    topk = min(DSA_TOPK, L // 4)
    idx, valid = _indexer_topk(iq, iw, ik_all, pos, jnp.arange(L), topk)
    in_past = (idx < past)[..., None, None]
    pidx = jnp.minimum(idx, past - 1)
    phys = jax.vmap(lambda pt, ib: pt[ib // page])(page_table, pidx)
    off = pidx % page
    nidx = jnp.clip(idx - past, 0, Ts - 1)
    k_sel = jnp.where(in_past, cache_k[phys, off].astype(k.dtype), _gather_rows(k, nidx))
    v_sel = jnp.where(in_past, cache_v[phys, off].astype(v.dtype), _gather_rows(v, nidx))
    return _attend_selected(q, k_sel, v_sel, valid)


def _layer(x, c, pos, gdn_buf, gdn_s0, ffn_buf, attend, p):
    B, T, _ = x.shape
    ada = jnp.dot(jax.nn.silu(c), p["w_ada"]) + p["b_ada"]
    sh1, sc1, gt1, sh2, sc2, gt2 = jnp.split(ada[:, None, :], 6, axis=-1)
    h = _rmsnorm(x, p["g_norm1"]) * (1.0 + sc1) + sh1
    (qkv_raw, z, a_raw, b_raw, dq, dk, dv, iq, ik, iw, gate_gdn, gate_dsa) = _split_cols(jnp.dot(h, p["w_in"]))

    qkv, gdn_buf_new = _causal_dwconv(qkv_raw, gdn_buf, p["w_gdn_conv"])
    qkv = jax.nn.silu(qkv)
    gq, gk, gv = jnp.split(qkv, [GDN_QK_W, 2 * GDN_QK_W], axis=-1)
    gq = _l2norm(gq.reshape(B, T, GDN_HEADS, GDN_DK)) * (GDN_DK ** -0.5)
    gk = _l2norm(gk.reshape(B, T, GDN_HEADS, GDN_DK))
    gv = gv.reshape(B, T, GDN_HEADS, GDN_DV)
    log_alpha = -jnp.exp(p["a_log"].astype(jnp.float32)) * jax.nn.softplus(
        a_raw.astype(jnp.float32) + p["dt_bias"].astype(jnp.float32))
    beta = jax.nn.sigmoid(b_raw.astype(jnp.float32))
    o_gdn, s_new = _gated_delta_chunked(gq, gk, gv, log_alpha, beta, gdn_s0)
    o_gdn = _rmsnorm(o_gdn.astype(x.dtype), p["g_gdn_norm"]) * jax.nn.silu(z.reshape(B, T, GDN_HEADS, GDN_DV))
    o_gdn = o_gdn.reshape(B, T, GDN_V_W)

    dq = _partial_rotary(dq.reshape(B, T, DSA_HEADS, DSA_HEAD_DIM), pos)
    dk = _partial_rotary(dk.reshape(B, T, DSA_KV_HEADS, DSA_HEAD_DIM), pos)
    dv = dv.reshape(B, T, DSA_KV_HEADS, DSA_HEAD_DIM)
    iq = _partial_rotary(iq.reshape(B, T, IDX_HEADS, IDX_DIM), pos)
    ik = _partial_rotary(ik[:, :, None, :], pos)[:, :, 0, :]
    o_dsa = attend(dq, dk, dv, iq, ik, iw, pos)

    mix = (jax.nn.sigmoid(gate_gdn) * jnp.dot(o_gdn, p["w_gdn_out"])
           + jax.nn.sigmoid(gate_dsa) * jnp.dot(o_dsa, p["w_dsa_out"]))
    x = x + gt1 * jnp.dot(mix, p["w_o"])

    h2 = _rmsnorm(x, p["g_norm2"]) * (1.0 + sc2) + sh2
    u, ffn_buf_new = _causal_dwconv(jnp.dot(h2, p["w_up"]), ffn_buf, p["w_ffn_conv"])
    u_gate, u_val = jnp.split(u + p["b_ffn_conv"], 2, axis=-1)
    x = x + gt2 * jnp.dot(jax.nn.silu(u_gate) * u_val, p["w_down"])
    return x, (dk, dv, ik, s_new.astype(gdn_s0.dtype), gdn_buf_new, ffn_buf_new)


def setup_inputs(seed: int = 0) -> dict:
    key = jax.random.key(seed)
    ks = jax.random.split(key, 40)
    f32 = jnp.float32
    n_pages = PAST_LEN // PAGE_SIZE
    n_used = DEC_BATCH * n_pages
    n_pool = n_used + n_used // 4
    in_cols = sum(_in_sizes())
    conv_w = 2 * GDN_QK_W + GDN_V_W
    L = DEPTH

    def nrm(k, shape, scale):
        return jax.random.normal(k, shape, f32) * scale

    page_table = jax.random.permutation(ks[0], n_pool)[:n_used].reshape(DEC_BATCH, n_pages).astype(jnp.int32)
    dt = jnp.exp(jax.random.uniform(ks[1], (L, GDN_HEADS), f32, math.log(1e-3), math.log(1e-1)))
    dt_bias = dt + jnp.log(-jnp.expm1(-dt))
    a_log = jnp.log(jax.random.uniform(ks[2], (L, GDN_HEADS), f32, 1.0, 16.0))
    return {
        "x_prompt": nrm(ks[3], (BATCH, SEQ, D_MODEL), 1.0),
        "x_sample": nrm(ks[4], (DEC_BATCH, DEC_SEQ, D_MODEL), 1.0),
        "c_prompt": nrm(ks[5], (BATCH, D_MODEL), 1.0),
        "c_sample": nrm(ks[6], (DEC_BATCH, D_MODEL), 1.0),
        "cache_k": nrm(ks[7], (L, n_pool, PAGE_SIZE, DSA_KV_HEADS, DSA_HEAD_DIM), 1.0),
        "cache_v": nrm(ks[8], (L, n_pool, PAGE_SIZE, DSA_KV_HEADS, DSA_HEAD_DIM), 1.0),
        "cache_idx_k": nrm(ks[9], (L, n_pool, PAGE_SIZE, IDX_DIM), 1.0),
        "page_table": page_table,
        "state_gdn": nrm(ks[10], (L, DEC_BATCH, GDN_HEADS, GDN_DK, GDN_DV), 0.1),
        "state_gdn_conv": nrm(ks[11], (L, DEC_BATCH, GDN_CONV - 1, conv_w), 1.0),
        "state_ffn_conv": nrm(ks[12], (L, DEC_BATCH, FFN_CONV - 1, 2 * D_FF), 1.0),
        "w_ada": nrm(ks[13], (L, D_MODEL, 6 * D_MODEL), 0.5 * D_MODEL ** -0.5),
        "b_ada": nrm(ks[14], (L, 6 * D_MODEL), 0.02),
        "g_norm1": 1.0 + nrm(ks[15], (L, D_MODEL), 0.02),
        "w_in": nrm(ks[16], (L, D_MODEL, in_cols), D_MODEL ** -0.5),
        "w_gdn_conv": nrm(ks[17], (L, GDN_CONV, conv_w), GDN_CONV ** -0.5),
        "a_log": a_log,
        "dt_bias": dt_bias,
        "g_gdn_norm": 1.0 + nrm(ks[18], (L, GDN_DV), 0.02),
        "w_gdn_out": nrm(ks[19], (L, GDN_V_W, D_MODEL), GDN_V_W ** -0.5),
        "w_dsa_out": nrm(ks[20], (L, DSA_HEADS * DSA_HEAD_DIM, D_MODEL), (DSA_HEADS * DSA_HEAD_DIM) ** -0.5),
        "w_o": nrm(ks[21], (L, D_MODEL, D_MODEL), D_MODEL ** -0.5),
        "g_norm2": 1.0 + nrm(ks[22], (L, D_MODEL), 0.02),
        "w_up": nrm(ks[23], (L, D_MODEL, 2 * D_FF), D_MODEL ** -0.5),
        "w_ffn_conv": nrm(ks[24], (L, FFN_CONV, 2 * D_FF), FFN_CONV ** -0.5),
        "b_ffn_conv": nrm(ks[25], (L, 2 * D_FF), 0.02),
        "w_down": nrm(ks[26], (L, D_FF, D_MODEL), D_FF ** -0.5),
        "g_final": 1.0 + nrm(ks[27], (D_MODEL,), 0.02),
    }


def reference(x_prompt, x_sample, c_prompt, c_sample, cache_k, cache_v, cache_idx_k, page_table,
              state_gdn, state_gdn_conv, state_ffn_conv, w_ada, b_ada, g_norm1, w_in, w_gdn_conv,
              a_log, dt_bias, g_gdn_norm, w_gdn_out, w_dsa_out, w_o, g_norm2, w_up, w_ffn_conv,
              b_ffn_conv, w_down, g_final):
    layer_params = dict(w_ada=w_ada, b_ada=b_ada, g_norm1=g_norm1, w_in=w_in, w_gdn_conv=w_gdn_conv,
                        a_log=a_log, dt_bias=dt_bias, g_gdn_norm=g_gdn_norm, w_gdn_out=w_gdn_out,
                        w_dsa_out=w_dsa_out, w_o=w_o, g_norm2=g_norm2, w_up=w_up,
                        w_ffn_conv=w_ffn_conv, b_ffn_conv=b_ffn_conv, w_down=w_down)
    B, T, _ = x_prompt.shape
    Ts = x_sample.shape[1]
    dtp = x_prompt.dtype
    past = page_table.shape[1] * cache_k.shape[2]
    pos_p = jnp.arange(T)
    pos_s = past + jnp.arange(Ts)
    conv_w = 2 * GDN_QK_W + GDN_V_W

    h_p, h_s = x_prompt, x_sample
    outs_p, outs_s = [], []
    for l in range(DEPTH):
        p = {name: arr[l] for name, arr in layer_params.items()}
        h_p, st_p = _layer(h_p, c_prompt, pos_p,
                           jnp.zeros((B, GDN_CONV - 1, conv_w), dtp),
                           jnp.zeros((B, GDN_HEADS, GDN_DK, GDN_DV), dtp),
                           jnp.zeros((B, FFN_CONV - 1, 2 * D_FF), dtp),
                           _dsa_prompt, p)
        attend_s = functools.partial(_dsa_sample, cache_k=cache_k[l], cache_v=cache_v[l],
                                     cache_idx_k=cache_idx_k[l], page_table=page_table)
        h_s, st_s = _layer(h_s, c_sample, pos_s, state_gdn_conv[l], state_gdn[l], state_ffn_conv[l],
                           attend_s, p)
        outs_p.append(st_p)
        outs_s.append(st_s)

    y_prompt = _rmsnorm(h_p, g_final)
    y_sample = _rmsnorm(h_s, g_final)
    stk = lambda outs, i: jnp.stack([o[i] for o in outs])
    return (y_prompt, y_sample,
            stk(outs_p, 0), stk(outs_p, 1), stk(outs_p, 2), stk(outs_p, 3), stk(outs_p, 4), stk(outs_p, 5),
            stk(outs_s, 0), stk(outs_s, 1), stk(outs_s, 2), stk(outs_s, 3), stk(outs_s, 4), stk(outs_s, 5))
```

```python
import functools

import jax
import jax.numpy as jnp
from jax import lax
from jax.experimental import pallas as pl
from jax.experimental.pallas import tpu as pltpu

f32 = jnp.float32
bf16 = jnp.bfloat16

GDN_HEADS = 8
GDN_DK = 128
GDN_DV = 128
GDN_CONV = 4
GDN_CHUNK = 64
DSA_HEADS = 8
DSA_KV_HEADS = 2
DSA_HEAD_DIM = 128
IDX_HEADS = 8
IDX_DIM = 64
IDX_SCALE = IDX_HEADS ** -0.5 * IDX_DIM ** -0.5
DSA_TOPK = 256
ROPE_THETA = 500000.0
ROPE_FRACTION = 4
FFN_CONV = 3
NORM_EPS = 1e-6

LANES = 128
SUBLANES = 8
NEG = float(jnp.finfo(jnp.float32).min)
POS = float(jnp.finfo(jnp.float32).max)
BISECT_CAP = 4096

QKV_W = 2 * GDN_HEADS * GDN_DK + GDN_HEADS * GDN_DV
Z_W = GDN_HEADS * GDN_DV
DQ_W = DSA_HEADS * DSA_HEAD_DIM
DKV_W = DSA_KV_HEADS * DSA_HEAD_DIM
IQ_W = IDX_HEADS * IDX_DIM
OFF_QKV = 0
OFF_Z = OFF_QKV + QKV_W
OFF_DQ = OFF_Z + Z_W
OFF_DK = OFF_DQ + DQ_W
OFF_DV = OFF_DK + DKV_W
OFF_IQ = OFF_DV + DKV_W
OFF_GG = OFF_IQ + IQ_W
SM_IK = 0
SM_A = IDX_DIM
SM_B = SM_A + GDN_HEADS
SM_IW = SM_B + GDN_HEADS


def _sigmoid(x):
    return 1.0 / (1.0 + jnp.exp(-x))


def _silu(x):
    return x * _sigmoid(x)


def _dot(a, b):
    return jnp.dot(a, b, preferred_element_type=f32)


def _dot_nt(a, b):
    return lax.dot_general(a, b, (((1,), (1,)), ((), ())), preferred_element_type=f32)


def _dot_tn(a, b):
    return lax.dot_general(a, b, (((0,), (0,)), ((), ())), preferred_element_type=f32)


def _split3(a):
    hi = a.astype(bf16)
    lo = (a - hi.astype(f32)).astype(bf16)
    return hi, lo


def _dot3(a, b):
    ah, al = _split3(a)
    bh, bl = _split3(b)
    return _dot(ah, bh) + (_dot(ah, bl) + _dot(al, bh))


def _dot_exact(a, b):
    return jnp.dot(a, b, preferred_element_type=f32, precision=lax.Precision.HIGHEST)


def _params(vmem_mb=None, n_axes=1):
    kw = dict(dimension_semantics=("arbitrary",) * n_axes)
    if vmem_mb is not None:
        kw["vmem_limit_bytes"] = vmem_mb * 1024 * 1024
    return pltpu.CompilerParams(**kw)


def _ada_kernel(c_ref, w_ref, b_ref, o_ref):
    s = _silu(c_ref[...]).astype(bf16)
    o_ref[...] = _dot(s, w_ref[...].astype(bf16)) + b_ref[...]


def _ada(c_all, w_ada, b_ada):
    m, d = c_all.shape
    n = w_ada.shape[1]
    tn = 1024
    return pl.pallas_call(
        _ada_kernel,
        grid=(n // tn,),
        in_specs=[pl.BlockSpec((m, d), lambda j: (0, 0)),
                  pl.BlockSpec((d, tn), lambda j: (0, j)),
                  pl.BlockSpec((1, tn), lambda j: (0, j))],
        out_specs=pl.BlockSpec((m, tn), lambda j: (0, j)),
        out_shape=jax.ShapeDtypeStruct((m, n), f32),
        compiler_params=_params(48),
        name="ada",
    )(c_all, w_ada, b_ada.reshape(1, n))


def _mod_spec(group, d, rows_per_batch, tm, col):
    if group == "prompt":
        return pl.BlockSpec((None, 1, d), lambda i, *_: ((i * tm) // rows_per_batch, 0, col))
    return pl.BlockSpec((rows_per_batch, d), lambda i, *_: (0, col))


def _prep_kernel(x_ref, g_ref, sc_ref, sh_ref, o_ref):
    x = x_ref[...]
    y = x * lax.rsqrt(jnp.mean(x * x, axis=-1, keepdims=True) + NORM_EPS)
    o_ref[...] = ((y * g_ref[...]) * (1.0 + sc_ref[...]) + sh_ref[...]).astype(bf16)


def _prep(x, g, ada, group, rows_per_batch, tm):
    m, d = x.shape
    return pl.pallas_call(
        _prep_kernel,
        grid=(m // tm,),
        in_specs=[pl.BlockSpec((tm, d), lambda i: (i, 0)),
                  pl.BlockSpec((1, d), lambda i: (0, 0)),
                  _mod_spec(group, d, rows_per_batch, tm, 1),
                  _mod_spec(group, d, rows_per_batch, tm, 0)],
        out_specs=pl.BlockSpec((tm, d), lambda i: (i, 0)),
        out_shape=jax.ShapeDtypeStruct((m, d), bf16),
        compiler_params=_params(),
        name="prep_" + group,
    )(x, g.reshape(1, d), ada, ada)


def _mm_kernel(a_ref, w_ref, o_ref):
    o_ref[...] = _dot(a_ref[...], w_ref[...])


def _matmul(a, w, tm, tn, name):
    m, k = a.shape
    n = w.shape[1]
    return pl.pallas_call(
        _mm_kernel,
        grid=(n // tn, m // tm),
        in_specs=[pl.BlockSpec((tm, k), lambda j, i: (i, 0)),
                  pl.BlockSpec((k, tn), lambda j, i: (0, j))],
        out_specs=pl.BlockSpec((tm, tn), lambda j, i: (i, j)),
        out_shape=jax.ShapeDtypeStruct((m, n), f32),
        compiler_params=_params(48, 2),
        name=name,
    )(a, w)


def _l2n(x):
    return x * lax.rsqrt(jnp.sum(x * x, axis=-1, keepdims=True) + NORM_EPS)


def _gdn_gates(sm, alog, dtb):
    xa = sm + dtb
    softplus = jnp.maximum(xa, 0.0) + jnp.log1p(jnp.exp(-jnp.abs(xa)))
    return -jnp.exp(alog) * softplus, _sigmoid(sm)


def _gated_norm(o, gn, z):
    y = o * lax.rsqrt(jnp.mean(o * o, axis=-1, keepdims=True) + NORM_EPS)
    return (y * gn) * _silu(z)


def _gdn_prompt_kernel(qkv_ref, halo_ref, z_ref, sm_ref, wc_ref, alog_ref, dtb_ref, gn_ref,
                       o_ref, sfin_ref, s_ref):
    c = pl.program_id(1)
    C = qkv_ref.shape[0]

    @pl.when(c == 0)
    def _():
        s_ref[...] = jnp.zeros_like(s_ref)

    keep = (c > 0).astype(f32)
    g_all, beta_all = _gdn_gates(sm_ref[...], alog_ref[...], dtb_ref[...])
    ri = lax.broadcasted_iota(jnp.int32, (C, C), 0)
    ci = lax.broadcasted_iota(jnp.int32, (C, C), 1)
    incl = ri >= ci
    strict = ri > ci
    eye = ri == ci
    gc_all = _dot_exact(incl.astype(f32), g_all)
    ones = jnp.ones((C, C), f32)
    eye_f = eye.astype(f32)

    def conv(col):
        xs = jnp.concatenate([halo_ref[:, col:col + LANES] * keep, qkv_ref[:, col:col + LANES]], axis=0)
        w = wc_ref[:, col:col + LANES]
        y = xs[SUBLANES:] * w[GDN_CONV - 1:GDN_CONV]
        for i in range(GDN_CONV - 1):
            sft = GDN_CONV - 1 - i
            y = y + xs[SUBLANES - sft:SUBLANES - sft + C] * w[i:i + 1]
        return _silu(y)

    for h in range(GDN_HEADS):
        q = _l2n(conv(h * GDN_DK)) * (GDN_DK ** -0.5)
        k = _l2n(conv(GDN_HEADS * GDN_DK + h * GDN_DK))
        v = conv(2 * GDN_HEADS * GDN_DK + h * GDN_DV)
        gc = gc_all[:, SM_A + h:SM_A + h + 1]
        beta = beta_all[:, SM_B + h:SM_B + h + 1]
        eg = jnp.exp(gc)
        gc_row = _dot_exact(ones, jnp.where(eye, gc, 0.0))
        decay = jnp.where(incl, jnp.exp(jnp.where(incl, gc - gc_row, 0.0)), 0.0)
        kb = k.astype(bf16)
        kk = _dot_nt(kb, kb)
        a = jnp.where(strict, beta * kk * decay, 0.0)
        inv = eye_f - a
        pw = a
        n_sq = max(1, (C - 1).bit_length() - 1)
        for _ in range(n_sq):
            pw = _dot3(pw, pw)
            inv = inv + _dot3(inv, pw)
        rhs = jnp.concatenate([beta * v, (beta * eg) * k], axis=1)
        sol = _dot3(inv, rhs)
        u_v = sol[:, :GDN_DV]
        w_k = sol[:, GDN_DV:]
        qk = _dot_nt(q.astype(bf16), kb) * decay
        s = s_ref[h]
        sb = s.astype(bf16)
        u = u_v - _dot(w_k.astype(bf16), sb)
        ub = u.astype(bf16)
        o = _dot((q * eg).astype(bf16), sb) + _dot(qk.astype(bf16), ub)
        gl = gc[C - 1:C, :]
        k_end = k * jnp.exp(gl - gc)
        s_ref[h] = s * jnp.exp(gl) + _dot_tn(k_end.astype(bf16), ub)
        zh = z_ref[:, h * GDN_DV:(h + 1) * GDN_DV]
        o_ref[:, h * GDN_DV:(h + 1) * GDN_DV] = _gated_norm(o, gn_ref[...], zh).astype(bf16)

    @pl.when(c == pl.num_programs(1) - 1)
    def _():
        sfin_ref[...] = s_ref[...]


def _gdn_vecs(a_log, dt_bias, g_gdn_norm):
    alog = jnp.zeros((1, LANES), f32).at[0, SM_A:SM_A + GDN_HEADS].set(a_log)
    dtb = jnp.zeros((1, LANES), f32).at[0, SM_A:SM_A + GDN_HEADS].set(dt_bias)
    return alog, dtb, g_gdn_norm.reshape(1, GDN_DV)


def _gdn_prompt(y, w_conv, a_log, dt_bias, g_gdn_norm, nb, t):
    C = GDN_CHUNK
    nc = t // C
    small_blk = y.shape[1] // LANES - 1
    alog, dtb, gn = _gdn_vecs(a_log, dt_bias, g_gdn_norm)
    cst = lambda b, c: (0, 0)
    return pl.pallas_call(
        _gdn_prompt_kernel,
        grid=(nb, nc),
        in_specs=[pl.BlockSpec((C, QKV_W), lambda b, c: (b * nc + c, 0)),
                  pl.BlockSpec((SUBLANES, QKV_W),
                               lambda b, c: (jnp.maximum((b * nc + c) * (C // SUBLANES) - 1, 0), 0)),
                  pl.BlockSpec((C, Z_W), lambda b, c: (b * nc + c, OFF_Z // Z_W)),
                  pl.BlockSpec((C, LANES), lambda b, c: (b * nc + c, small_blk)),
                  pl.BlockSpec((GDN_CONV, QKV_W), cst),
                  pl.BlockSpec((1, LANES), cst),
                  pl.BlockSpec((1, LANES), cst),
                  pl.BlockSpec((1, GDN_DV), cst)],
        out_specs=[pl.BlockSpec((C, Z_W), lambda b, c: (b * nc + c, 0)),
                   pl.BlockSpec((None, GDN_HEADS, GDN_DK, GDN_DV), lambda b, c: (b, 0, 0, 0))],
        out_shape=[jax.ShapeDtypeStruct((nb * t, Z_W), bf16),
                   jax.ShapeDtypeStruct((nb, GDN_HEADS, GDN_DK, GDN_DV), f32)],
        scratch_shapes=[pltpu.VMEM((GDN_HEADS, GDN_DK, GDN_DV), f32)],
        compiler_params=_params(None, 2),
        name="gdn_prompt",
    )(y, y, y, y, w_conv, alog, dtb, gn)


def _gdn_sample_kernel(qkv_ref, buf_ref, z_ref, sm_ref, sin_ref, wc_ref, alog_ref, dtb_ref, gn_ref,
                       o_ref, sout_ref, q_s, k_s, v_s, a_s, b_s, o_s):
    ts, G, _ = qkv_ref.shape
    nbuf = GDN_CONV - 1

    for h in range(GDN_HEADS):
        for part, dst in ((0, q_s), (1, k_s), (2, v_s)):
            col = part * GDN_HEADS * GDN_DK + h * GDN_DK
            w = wc_ref[:, col:col + LANES]
            rows = [buf_ref[i, :, col:col + LANES] for i in range(nbuf)]
            rows += [qkv_ref[t, :, col:col + LANES] for t in range(ts)]
            for t in range(ts):
                y = rows[t] * w[0:1]
                for i in range(1, GDN_CONV):
                    y = y + rows[t + i] * w[i:i + 1]
                y = _silu(y)
                if part == 0:
                    y = _l2n(y) * (GDN_DK ** -0.5)
                elif part == 1:
                    y = _l2n(y)
                dst[t, :, h * LANES:(h + 1) * LANES] = y

    for t in range(ts):
        g_all, beta_all = _gdn_gates(sm_ref[t], alog_ref[...], dtb_ref[...])
        a_all = jnp.exp(g_all)
        for h in range(GDN_HEADS):
            a_s[t, :, h * LANES:(h + 1) * LANES] = jnp.broadcast_to(a_all[:, SM_A + h:SM_A + h + 1], (G, LANES))
            b_s[t, :, h * LANES:(h + 1) * LANES] = jnp.broadcast_to(beta_all[:, SM_B + h:SM_B + h + 1], (G, LANES))

    def head(h, carry):
        cols = pl.ds(pl.multiple_of(h * LANES, LANES), LANES)
        for i in range(G):
            row = slice(i, i + 1)
            kq = jnp.concatenate([k_s[t, row, cols] for t in range(ts)]
                                 + [q_s[t, row, cols] for t in range(ts)], axis=0)
            kq_t = kq.T
            s = sin_ref[i, h]
            for t in range(ts):
                kc = kq_t[:, t:t + 1]
                qc = kq_t[:, ts + t:ts + t + 1]
                a = a_s[t, row, cols]
                b = b_s[t, row, cols]
                ks = jnp.sum(s * kc, axis=0, keepdims=True)
                r = b * (v_s[t, row, cols] - a * ks)
                s = a * s + kc * r
                o_s[t, row, cols] = jnp.sum(s * qc, axis=0, keepdims=True)
            sout_ref[i, h] = s
        return carry

    lax.fori_loop(0, GDN_HEADS, head, 0)

    for t in range(ts):
        for h in range(GDN_HEADS):
            sl = slice(h * GDN_DV, (h + 1) * GDN_DV)
            o_ref[t, :, sl] = _gated_norm(o_s[t, :, sl], gn_ref[...], z_ref[t, :, sl]).astype(bf16)


def _gdn_sample(y3, buf3, state, w_conv, a_log, dt_bias, g_gdn_norm):
    ts, db, n = y3.shape
    G = SUBLANES
    small_blk = n // LANES - 1
    alog, dtb, gn = _gdn_vecs(a_log, dt_bias, g_gdn_norm)
    cst = lambda g: (0, 0)
    st_spec = pl.BlockSpec((G, GDN_HEADS, GDN_DK, GDN_DV), lambda g: (g, 0, 0, 0))
    scr = pltpu.VMEM((ts, G, Z_W), f32)
    return pl.pallas_call(
        _gdn_sample_kernel,
        grid=(db // G,),
        in_specs=[pl.BlockSpec((ts, G, QKV_W), lambda g: (0, g, 0)),
                  pl.BlockSpec((GDN_CONV - 1, G, QKV_W), lambda g: (0, g, 0)),
                  pl.BlockSpec((ts, G, Z_W), lambda g: (0, g, OFF_Z // Z_W)),
                  pl.BlockSpec((ts, G, LANES), lambda g: (0, g, small_blk)),
                  st_spec,
                  pl.BlockSpec((GDN_CONV, QKV_W), cst),
                  pl.BlockSpec((1, LANES), cst),
                  pl.BlockSpec((1, LANES), cst),
                  pl.BlockSpec((1, GDN_DV), cst)],
        out_specs=[pl.BlockSpec((ts, G, Z_W), lambda g: (0, g, 0)), st_spec],
        out_shape=[jax.ShapeDtypeStruct((ts, db, Z_W), bf16),
                   jax.ShapeDtypeStruct(state.shape, f32)],
        scratch_shapes=[scr, scr, scr, scr, scr, scr],
        compiler_params=_params(48),
        name="gdn_sample",
    )(y3, buf3, y3, y3, state, w_conv, alog, dtb, gn)


def _rope_tables(pos, rot, width):
    half = rot // 2
    inv_freq = ROPE_THETA ** (-jnp.arange(half, dtype=f32) * (2.0 / rot))
    ang = pos.astype(f32)[:, None] * inv_freq[None, :]
    cos, sin = jnp.cos(ang), jnp.sin(ang)
    n = pos.shape[0]
    z = lambda w: jnp.zeros((n, w), f32)
    cosw = jnp.concatenate([cos, cos, jnp.ones((n, width - rot), f32)], axis=1)
    sina = jnp.concatenate([-sin, z(width - half)], axis=1)
    sinb = jnp.concatenate([z(half), sin, z(width - rot)], axis=1)
    reps = LANES // width
    return tuple(jnp.tile(a, (1, reps)) for a in (cosw, sina, sinb))


def _rope_kernel(dq_ref, dk_ref, iq_ref, sm_ref, c1, sa1, sb1, c2, sa2, sb2,
                 dq_o, dk_o, iq_o, ik_o, sm_o):
    h1 = DSA_HEAD_DIM // ROPE_FRACTION // 2
    h2 = IDX_DIM // ROPE_FRACTION // 2

    def rot(x, c, sa, sb, half):
        return x * c[...] + pltpu.roll(x, LANES - half, 1) * sa[...] + pltpu.roll(x, half, 1) * sb[...]

    for j in range(DQ_W // LANES):
        sl = slice(j * LANES, (j + 1) * LANES)
        dq_o[:, sl] = rot(dq_ref[:, sl], c1, sa1, sb1, h1).astype(bf16)
    for j in range(DKV_W // LANES):
        sl = slice(j * LANES, (j + 1) * LANES)
        dk_o[:, sl] = rot(dk_ref[:, sl], c1, sa1, sb1, h1)
    for j in range(IQ_W // LANES):
        sl = slice(j * LANES, (j + 1) * LANES)
        iq_o[:, sl] = rot(iq_ref[:, sl], c2, sa2, sb2, h2).astype(bf16)
    sm = sm_ref[...]
    ik_o[...] = rot(sm, c2, sa2, sb2, h2)[:, :IDX_DIM]
    sm_o[...] = sm


def _rope(y, pos, tm, group, db=None):
    m, n = y.shape
    small_blk = n // LANES - 1
    t1 = _rope_tables(pos, DSA_HEAD_DIM // ROPE_FRACTION, DSA_HEAD_DIM)
    t2 = _rope_tables(pos, IDX_DIM // ROPE_FRACTION, IDX_DIM)
    tab_blocks = pos.shape[0] // tm
    tab = pl.BlockSpec((tm, LANES), lambda i: (i % tab_blocks, 0))
    if group == "prompt":
        omap = lambda i: (i, 0)
        rows = lambda w: m
        cols = lambda w: w
    else:
        ts = m // db
        omap = lambda i: (0, i)
        rows = lambda w: db
        cols = lambda w: ts * w
    out = lambda w, dt: jax.ShapeDtypeStruct((rows(w), cols(w)), dt)
    return pl.pallas_call(
        _rope_kernel,
        grid=(m // tm,),
        in_specs=[pl.BlockSpec((tm, DQ_W), lambda i: (i, OFF_DQ // DQ_W)),
                  pl.BlockSpec((tm, DKV_W), lambda i: (i, OFF_DK // DKV_W)),
                  pl.BlockSpec((tm, IQ_W), lambda i: (i, OFF_IQ // IQ_W)),
                  pl.BlockSpec((tm, LANES), lambda i: (i, small_blk)),
                  tab, tab, tab, tab, tab, tab],
        out_specs=[pl.BlockSpec((tm, DQ_W), omap),
                   pl.BlockSpec((tm, DKV_W), omap),
                   pl.BlockSpec((tm, IQ_W), omap),
                   pl.BlockSpec((tm, IDX_DIM), lambda i: (i, 0)),
                   pl.BlockSpec((tm, LANES), omap)],
        out_shape=[out(DQ_W, bf16), out(DKV_W, f32), out(IQ_W, bf16),
                   jax.ShapeDtypeStruct((m, IDX_DIM), f32), out(LANES, f32)],
        compiler_params=_params(),
        name="rope_" + group,
    )(y, y, y, y, *t1, *t2)


def _select_topk(x_ref, sel_ref, causal, k):
    R, L = x_ref.shape
    kf = float(k)
    x = x_ref[...]
    ncaus = jnp.sum(causal.astype(f32), axis=1, keepdims=True)
    lo0 = jnp.min(jnp.where(causal, x, POS), axis=1, keepdims=True)
    mx = jnp.max(x, axis=1, keepdims=True)
    few = ncaus <= kf
    hi0 = jnp.where(few, lo0, mx + (jnp.abs(mx) + 1.0))

    def cond(c):
        it, _, _, done = c
        return jnp.logical_and(it < BISECT_CAP, jnp.min(done) < 0.5)

    def body(c):
        it, lo, hi, done = c
        mid = 0.5 * lo + 0.5 * hi
        cnt = jnp.sum((x_ref[...] >= mid).astype(f32), axis=1, keepdims=True)
        collapsed = jnp.logical_or(mid <= lo, mid >= hi)
        found = cnt == kf
        live = jnp.logical_and(done < 0.5, jnp.logical_not(collapsed))
        lo = jnp.where(jnp.logical_and(live, cnt >= kf), mid, lo)
        hi = jnp.where(jnp.logical_and(live, cnt <= kf), mid, hi)
        done = jnp.where(jnp.logical_or(collapsed, found), 1.0, done)
        return it + 1, lo, hi, done

    _, lo, hi, _ = lax.while_loop(cond, body, (jnp.int32(0), lo0, hi0, few.astype(f32)))

    ge_hi = x >= hi
    band = jnp.logical_and(x >= lo, x < hi)
    sel_ref[...] = ge_hi.astype(f32)

    @pl.when(jnp.max(band.astype(f32)) > 0.5)
    def _():
        need = kf - jnp.sum(ge_hi.astype(f32), axis=1, keepdims=True)
        ai = lax.broadcasted_iota(jnp.int32, (LANES, LANES), 0)
        bi = lax.broadcasted_iota(jnp.int32, (LANES, LANES), 1)
        before = (ai < bi).astype(bf16)
        seen = jnp.zeros((R, 1), f32)
        for j in range(L // LANES):
            sl = slice(j * LANES, (j + 1) * LANES)
            xc = x_ref[:, sl]
            bc = jnp.logical_and(xc >= lo, xc < hi)
            bcf = bc.astype(f32)
            rank = seen + _dot(bcf.astype(bf16), before)
            take = jnp.logical_or(xc >= hi, jnp.logical_and(bc, rank < need))
            sel_ref[:, sl] = take.astype(f32)
            seen = seen + jnp.sum(bcf, axis=1, keepdims=True)


def _dsa_prompt_kernel(iq_ref, sm_ref, ik_ref, q_ref, k_ref, v_ref, o_ref,
                       ikb, kb, vb, x_s, sel_s, *, topk):
    qi = pl.program_id(1)
    R = iq_ref.shape[0]
    T = ik_ref.shape[0]

    @pl.when(qi == 0)
    def _():
        ikb[...] = ik_ref[...].astype(bf16)
        kb[...] = k_ref[...].astype(bf16)
        vb[...] = v_ref[...].astype(bf16)

    wgt = sm_ref[...] * IDX_SCALE
    score = jnp.zeros((R, T), f32)
    for h in range(IDX_HEADS):
        rel = jnp.maximum(_dot_nt(iq_ref[:, h * IDX_DIM:(h + 1) * IDX_DIM], ikb[...]), 0.0)
        score = score + rel * wgt[:, SM_IW + h:SM_IW + h + 1]
    key = lax.broadcasted_iota(jnp.int32, (R, T), 1)
    qpos = qi * R + lax.broadcasted_iota(jnp.int32, (R, T), 0)
    causal = key <= qpos
    x_s[...] = jnp.where(causal, score, NEG)
    _select_topk(x_s, sel_s, causal, topk)

    def head(h, carry):
        g = h // (DSA_HEADS // DSA_KV_HEADS)
        hc = pl.ds(pl.multiple_of(h * DSA_HEAD_DIM, DSA_HEAD_DIM), DSA_HEAD_DIM)
        gc = pl.ds(pl.multiple_of(g * DSA_HEAD_DIM, DSA_HEAD_DIM), DSA_HEAD_DIM)
        keep = jnp.logical_and(sel_s[...] > 0.5, causal)
        s = _dot_nt(q_ref[:, hc], kb[:, gc]) * (DSA_HEAD_DIM ** -0.5)
        s = jnp.where(keep, s, NEG)
        m = jnp.max(s, axis=1, keepdims=True)
        p = jnp.where(keep, jnp.exp(s - m), 0.0)
        l = jnp.sum(p, axis=1, keepdims=True)
        o_ref[:, hc] = (_dot(p.astype(bf16), vb[:, gc]) / l).astype(bf16)
        return carry

    lax.fori_loop(0, DSA_HEADS, head, 0)


def _dsa_prompt(iq, sm, ik, dq, dk, y, nb, t, topk):
    R = LANES
    nq = t // R
    kern = functools.partial(_dsa_prompt_kernel, topk=topk)
    return pl.pallas_call(
        kern,
        grid=(nb, nq),
        in_specs=[pl.BlockSpec((R, IQ_W), lambda b, i: (b * nq + i, 0)),
                  pl.BlockSpec((R, LANES), lambda b, i: (b * nq + i, 0)),
                  pl.BlockSpec((t, IDX_DIM), lambda b, i: (b, 0)),
                  pl.BlockSpec((R, DQ_W), lambda b, i: (b * nq + i, 0)),
                  pl.BlockSpec((t, DKV_W), lambda b, i: (b, 0)),
                  pl.BlockSpec((t, DKV_W), lambda b, i: (b, OFF_DV // DKV_W))],
        out_specs=pl.BlockSpec((R, DQ_W), lambda b, i: (b * nq + i, 0)),
        out_shape=jax.ShapeDtypeStruct((nb * t, DQ_W), bf16),
        scratch_shapes=[pltpu.VMEM((t, IDX_DIM), bf16), pltpu.VMEM((t, DKV_W), bf16),
                        pltpu.VMEM((t, DKV_W), bf16), pltpu.VMEM((R, t), f32), pltpu.VMEM((R, t), f32)],
        compiler_params=_params(48, 2),
        name="dsa_prompt",
    )(iq, sm, ik, dq, dk, y)


def _idx_score_kernel(pt_ref, iq_ref, sm_ref, ikn_ref, *rest, n_pages, page):
    pages = rest[:n_pages]
    o_ref = rest[n_pages]
    tp = iq_ref.shape[0]
    ts = o_ref.shape[0]
    past = n_pages * page
    lp = o_ref.shape[1]
    keys = jnp.concatenate([p[...] for p in pages]
                           + [ikn_ref[...], jnp.zeros((lp - past - tp, IDX_DIM), f32)], axis=0).astype(bf16)
    wgt = sm_ref[...] * IDX_SCALE
    iq = iq_ref[...].astype(bf16)
    score = jnp.zeros((tp, lp), f32)
    for h in range(IDX_HEADS):
        rel = jnp.maximum(_dot_nt(iq[:, h * IDX_DIM:(h + 1) * IDX_DIM], keys), 0.0)
        score = score + rel * wgt[:, SM_IW + h:SM_IW + h + 1]
    o_ref[...] = score[:ts]


def _idx_scores(page_table, iq, sm, ikn, cache_idx, lp, ts):
    db, tp, _ = iq.shape
    n_pages = page_table.shape[1]
    page = cache_idx.shape[1]
    kern = functools.partial(_idx_score_kernel, n_pages=n_pages, page=page)
    page_specs = [pl.BlockSpec((None, page, IDX_DIM), functools.partial(lambda b, pt, p: (pt[b, p], 0, 0), p=p))
                  for p in range(n_pages)]
    grid_spec = pltpu.PrefetchScalarGridSpec(
        num_scalar_prefetch=1,
        grid=(db,),
        in_specs=[pl.BlockSpec((None, tp, IQ_W), lambda b, pt: (b, 0, 0)),
                  pl.BlockSpec((None, tp, LANES), lambda b, pt: (b, 0, 0)),
                  pl.BlockSpec((None, tp, IDX_DIM), lambda b, pt: (b, 0, 0))] + page_specs,
        out_specs=pl.BlockSpec((None, ts, lp), lambda b, pt: (b, 0, 0)),
    )
    return pl.pallas_call(
        kern, grid_spec=grid_spec,
        out_shape=jax.ShapeDtypeStruct((db, ts, lp), f32),
        compiler_params=_params(),
        name="idx_scores_sample",
    )(page_table, iq, sm, ikn, *([cache_idx] * n_pages))


def _select_sample_kernel(x_ref, sel_ref, x_s, *, topk, past, ts):
    R, lp = x_ref.shape
    key = lax.broadcasted_iota(jnp.int32, (R, lp), 1)
    t = lax.broadcasted_iota(jnp.int32, (R, lp), 0) % ts
    causal = key <= past + t
    x_s[...] = jnp.where(causal, x_ref[...], NEG)
    _select_topk(x_s, sel_ref, causal, topk)
    sel_ref[...] = jnp.where(causal, sel_ref[...], 0.0)


def _select_sample(scores, topk, past, ts):
    m, lp = scores.shape
    R = LANES
    kern = functools.partial(_select_sample_kernel, topk=topk, past=past, ts=ts)
    return pl.pallas_call(
        kern, grid=(m // R,),
        in_specs=[pl.BlockSpec((R, lp), lambda i: (i, 0))],
        out_specs=pl.BlockSpec((R, lp), lambda i: (i, 0)),
        out_shape=jax.ShapeDtypeStruct((m, lp), f32),
        scratch_shapes=[pltpu.VMEM((R, lp), f32)],
        compiler_params=_params(),
        name="select_sample",
    )(scores)


def _dsa_sample_kernel(pt_ref, q_ref, sel_ref, kn_ref, vn_ref, *rest, n_pages, page):
    kp = rest[:n_pages]
    vp = rest[n_pages:2 * n_pages]
    o_ref = rest[2 * n_pages]
    tp = q_ref.shape[0]
    ts = sel_ref.shape[0]
    past = n_pages * page
    lp = sel_ref.shape[1]
    hpg = DSA_HEADS // DSA_KV_HEADS
    sel = jnp.concatenate([sel_ref[...], jnp.ones((tp - ts, lp), f32)], axis=0)
    keep = jnp.concatenate([sel] * hpg, axis=0) > 0.5
    padn = jnp.zeros((lp - past - tp, DSA_HEAD_DIM), f32)
    for g in range(DSA_KV_HEADS):
        gsl = slice(g * DSA_HEAD_DIM, (g + 1) * DSA_HEAD_DIM)
        kg = jnp.concatenate([p[pl.ds(g, page, stride=DSA_KV_HEADS), :] for p in kp]
                             + [kn_ref[:, gsl], padn], axis=0).astype(bf16)
        vg = jnp.concatenate([p[pl.ds(g, page, stride=DSA_KV_HEADS), :] for p in vp]
                             + [vn_ref[:, gsl], padn], axis=0).astype(bf16)
        qg = jnp.concatenate([q_ref[:, (g * hpg + j) * DSA_HEAD_DIM:(g * hpg + j + 1) * DSA_HEAD_DIM]
                              for j in range(hpg)], axis=0).astype(bf16)
        s = _dot_nt(qg, kg) * (DSA_HEAD_DIM ** -0.5)
        s = jnp.where(keep, s, NEG)
        m = jnp.max(s, axis=1, keepdims=True)
        p = jnp.where(keep, jnp.exp(s - m), 0.0)
        l = jnp.sum(p, axis=1, keepdims=True)
        o = _dot(p.astype(bf16), vg) / l
        for j in range(hpg):
            hsl = slice((g * hpg + j) * DSA_HEAD_DIM, (g * hpg + j + 1) * DSA_HEAD_DIM)
            o_ref[:, hsl] = o[j * tp:j * tp + ts]


def _dsa_sample(page_table, dq, sel, kn, vn, ck, cv):
    db, tp, _ = dq.shape
    ts = sel.shape[1]
    lp = sel.shape[2]
    n_pages = page_table.shape[1]
    rows = ck.shape[1]
    page = rows // DSA_KV_HEADS
    kern = functools.partial(_dsa_sample_kernel, n_pages=n_pages, page=page)
    pspec = [pl.BlockSpec((None, rows, DSA_HEAD_DIM), functools.partial(lambda b, pt, p: (pt[b, p], 0, 0), p=p))
             for p in range(n_pages)]
    per_b = lambda r, w: pl.BlockSpec((None, r, w), lambda b, pt: (b, 0, 0))
    grid_spec = pltpu.PrefetchScalarGridSpec(
        num_scalar_prefetch=1,
        grid=(db,),
        in_specs=[per_b(tp, DQ_W), per_b(ts, lp), per_b(tp, DKV_W), per_b(tp, DKV_W)] + pspec + pspec,
        out_specs=per_b(ts, DQ_W),
    )
    return pl.pallas_call(
        kern, grid_spec=grid_spec,
        out_shape=jax.ShapeDtypeStruct((db, ts, DQ_W), f32),
        compiler_params=_params(48),
        name="dsa_sample",
    )(page_table, dq, sel, kn, vn, *([ck] * n_pages), *([cv] * n_pages))


def _merge_kernel(og_ref, od_ref, gg_ref, gd_ref, x_ref, gt_ref, sc_ref, sh_ref, g2_ref,
                  wg_ref, wd_ref, wo_ref, x1_ref, h2_ref):
    mix = (_sigmoid(gg_ref[...]) * _dot(og_ref[...], wg_ref[...])
           + _sigmoid(gd_ref[...]) * _dot(od_ref[...], wd_ref[...]))
    x1 = x_ref[...] + gt_ref[...] * _dot(mix.astype(bf16), wo_ref[...])
    x1_ref[...] = x1
    y = x1 * lax.rsqrt(jnp.mean(x1 * x1, axis=-1, keepdims=True) + NORM_EPS)
    h2_ref[...] = ((y * g2_ref[...]) * (1.0 + sc_ref[...]) + sh_ref[...]).astype(bf16)


def _merge(o_gdn, o_dsa, y, x, ada, g2, wg, wd, wo, group, rows_per_batch, tm):
    m, d = x.shape
    row = lambda w, blk=0: pl.BlockSpec((tm, w), lambda i: (i, blk))
    res = lambda a: pl.BlockSpec(a.shape, lambda i: (0, 0), pipeline_mode=pl.Buffered(1))
    return pl.pallas_call(
        _merge_kernel,
        grid=(m // tm,),
        in_specs=[row(Z_W), row(DQ_W), row(d, OFF_GG // d), row(d, OFF_GG // d + 1), row(d),
                  _mod_spec(group, d, rows_per_batch, tm, 2),
                  _mod_spec(group, d, rows_per_batch, tm, 4),
                  _mod_spec(group, d, rows_per_batch, tm, 3),
                  pl.BlockSpec((1, d), lambda i: (0, 0)),
                  res(wg), res(wd), res(wo)],
        out_specs=[row(d), row(d)],
        out_shape=[jax.ShapeDtypeStruct((m, d), f32), jax.ShapeDtypeStruct((m, d), bf16)],
        compiler_params=_params(48),
        name="merge_" + group,
    )(o_gdn, o_dsa, y, y, x, ada, ada, ada, g2.reshape(1, d), wg, wd, wo)


def _ffn_epilogue(acc_ref, x1_ref, gt_ref, gf_ref, y_ref):
    tm = acc_ref.shape[0]
    r = gt_ref.shape[0] if gt_ref.shape[0] > 1 else tm
    for s0 in range(0, tm, r):
        sl = slice(s0, s0 + r)
        x2 = x1_ref[sl, :] + gt_ref[...] * acc_ref[sl, :]
        y_ref[sl, :] = (x2 * lax.rsqrt(jnp.mean(x2 * x2, axis=-1, keepdims=True) + NORM_EPS)) * gf_ref[...]


def _ffn_prompt_kernel(h_ref, halo_ref, wug_ref, wuv_ref, wcg_ref, wcv_ref, bg_ref, bv_ref, wd_ref,
                       x1_ref, gt_ref, gf_ref, y_ref, ug_ref, uv_ref, acc_ref, *, blocks_per_seq):
    i = pl.program_id(0)
    j = pl.program_id(1)
    tm = h_ref.shape[0]
    keep = ((i % blocks_per_seq) > 0).astype(f32)

    def branch(wu_ref, wc_ref, b_ref, ubuf_ref):
        u = _dot(h_ref[...], wu_ref[...])
        uh = _dot(halo_ref[...], wu_ref[...]) * keep
        ubuf_ref[...] = u[tm - SUBLANES:]
        ux = jnp.concatenate([uh, u], axis=0)
        w = wc_ref[...]
        y = u * w[FFN_CONV - 1:FFN_CONV] + b_ref[...]
        for t in range(FFN_CONV - 1):
            sft = FFN_CONV - 1 - t
            y = y + ux[SUBLANES - sft:SUBLANES - sft + tm] * w[t:t + 1]
        return y

    act = _silu(branch(wug_ref, wcg_ref, bg_ref, ug_ref)) * branch(wuv_ref, wcv_ref, bv_ref, uv_ref)
    part = _dot(act.astype(bf16), wd_ref[...])

    @pl.when(j == 0)
    def _():
        acc_ref[...] = part

    @pl.when(j > 0)
    def _():
        acc_ref[...] += part

    @pl.when(j == pl.num_programs(1) - 1)
    def _():
        _ffn_epilogue(acc_ref, x1_ref, gt_ref, gf_ref, y_ref)


def _ffn_sample_kernel(h_ref, buf_ref, wug_ref, wuv_ref, wcg_ref, wcv_ref, bg_ref, bv_ref, wd_ref,
                       x1_ref, gt_ref, gf_ref, y_ref, ug_ref, uv_ref, acc_ref, *, ts):
    j = pl.program_id(1)
    db = h_ref.shape[0] // ts
    nbuf = FFN_CONV - 1
    tf = wug_ref.shape[1]

    def branch(wu_ref, wc_ref, b_ref, ubuf_ref, half):
        u = _dot(h_ref[...], wu_ref[...])
        rows = [buf_ref[r, :, half * tf:(half + 1) * tf] for r in range(nbuf)]
        rows += [u[t * db:(t + 1) * db] for t in range(ts)]
        for r in range(nbuf):
            ubuf_ref[r] = rows[ts + r]
        w = wc_ref[...]
        outs = []
        for t in range(ts):
            y = rows[t] * w[0:1] + b_ref[...]
            for r in range(1, FFN_CONV):
                y = y + rows[t + r] * w[r:r + 1]
            outs.append(y)
        return jnp.concatenate(outs, axis=0)

    act = _silu(branch(wug_ref, wcg_ref, bg_ref, ug_ref, 0)) * branch(wuv_ref, wcv_ref, bv_ref, uv_ref, 1)
    part = _dot(act.astype(bf16), wd_ref[...])

    @pl.when(j == 0)
    def _():
        acc_ref[...] = part

    @pl.when(j > 0)
    def _():
        acc_ref[...] += part

    @pl.when(j == pl.num_programs(1) - 1)
    def _():
        _ffn_epilogue(acc_ref, x1_ref, gt_ref, gf_ref, y_ref)


def _ffn(h2, x1, ada, g_final, w_up, w_conv, b_conv, w_down, group, rows_per_batch, tm, tf, buf=None):
    m, d = h2.shape
    dff = w_down.shape[0]
    nj = dff // tf
    b2 = b_conv.reshape(1, 2 * dff)
    common_w = [pl.BlockSpec((d, tf), lambda i, j: (0, j)),
                pl.BlockSpec((d, tf), lambda i, j: (0, nj + j)),
                pl.BlockSpec((FFN_CONV, tf), lambda i, j: (0, j)),
                pl.BlockSpec((FFN_CONV, tf), lambda i, j: (0, nj + j)),
                pl.BlockSpec((1, tf), lambda i, j: (0, j)),
                pl.BlockSpec((1, tf), lambda i, j: (0, nj + j)),
                pl.BlockSpec((tf, d), lambda i, j: (j, 0))]
    tail = [pl.BlockSpec((tm, d), lambda i, j: (i, 0)),
            _mod_spec(group, d, rows_per_batch, tm, 5),
            pl.BlockSpec((1, d), lambda i, j: (0, 0))]
    y_spec = pl.BlockSpec((tm, d), lambda i, j: (i, 0))
    if group == "prompt":
        bps = rows_per_batch // tm
        nb = m // rows_per_batch
        kern = functools.partial(_ffn_prompt_kernel, blocks_per_seq=bps)
        first = [pl.BlockSpec((tm, d), lambda i, j: (i, 0)),
                 pl.BlockSpec((SUBLANES, d), lambda i, j: (jnp.maximum(i * (tm // SUBLANES) - 1, 0), 0))]
        ubuf_spec = pl.BlockSpec((None, SUBLANES, tf), lambda i, j: (i // bps, 0, j))
        ubuf_shape = jax.ShapeDtypeStruct((nb, SUBLANES, dff), f32)
        args = (h2, h2)
    else:
        ts = m // rows_per_batch
        kern = functools.partial(_ffn_sample_kernel, ts=ts)
        first = [pl.BlockSpec((tm, d), lambda i, j: (i, 0)),
                 pl.BlockSpec((FFN_CONV - 1, rows_per_batch, 2 * tf), lambda i, j: (0, 0, j))]
        ubuf_spec = pl.BlockSpec((FFN_CONV - 1, rows_per_batch, tf), lambda i, j: (0, 0, j))
        ubuf_shape = jax.ShapeDtypeStruct((FFN_CONV - 1, rows_per_batch, dff), f32)
        args = (h2, buf)
    return pl.pallas_call(
        kern,
        grid=(m // tm, nj),
        in_specs=first + common_w + tail,
        out_specs=[y_spec, ubuf_spec, ubuf_spec],
        out_shape=[jax.ShapeDtypeStruct((m, d), f32), ubuf_shape, ubuf_shape],
        scratch_shapes=[pltpu.VMEM((tm, d), f32)],
        compiler_params=_params(56, 2),
        name="ffn_" + group,
    )(*args, w_up, w_up, w_conv, w_conv, b2, b2, w_down, x1, ada, g_final.reshape(1, d))


def _pack_w_in(w_in, d):
    sizes = (QKV_W, Z_W, GDN_HEADS, GDN_HEADS, DQ_W, DKV_W, DKV_W, IQ_W, IDX_DIM, IDX_HEADS, d, d)
    segs, off = [], 0
    for s in sizes:
        segs.append(w_in[:, off:off + s])
        off += s
    qkv, z, a, b, dq, dk, dv, iq, ik, iw, gg, gd = segs
    pad = jnp.zeros((w_in.shape[0], LANES - (IDX_DIM + 2 * GDN_HEADS + IDX_HEADS)), w_in.dtype)
    small = jnp.concatenate([ik, a, b, iw, pad], axis=1)
    return jnp.concatenate([qkv, z, dq, dk, dv, iq, gg, gd, small], axis=1).astype(bf16)


def _ffn_tile(dff):
    for tf in (512, 256, 128):
        if dff % tf == 0:
            return tf
    raise ValueError("d_ff must be a multiple of 128")


def _mm_tile(n):
    for tn in (1152, 1024, 768, 512, 384, 256, 128):
        if n % tn == 0:
            return tn
    raise ValueError("projection width must be a multiple of 128")


def kernel(x_prompt, x_sample, c_prompt, c_sample, cache_k, cache_v, cache_idx_k, page_table, state_gdn, state_gdn_conv, state_ffn_conv, w_ada, b_ada, g_norm1, w_in, w_gdn_conv, a_log, dt_bias, g_gdn_norm, w_gdn_out, w_dsa_out, w_o, g_norm2, w_up, w_ffn_conv, b_ffn_conv, w_down, g_final):
    nb, t, d = x_prompt.shape
    db, ts, _ = x_sample.shape
    depth = w_ada.shape[0]
    assert depth == 1 and db == LANES and OFF_GG % d == 0 and ts >= GDN_CONV - 1
    n_pages = page_table.shape[1]
    page = cache_k.shape[2]
    past = n_pages * page
    dff = w_down.shape[1]
    l = 0

    n_c = nb + db
    pad_c = (-n_c) % SUBLANES
    c_all = jnp.concatenate([c_prompt, c_sample, jnp.zeros((pad_c, d), f32)], axis=0)
    ada = _ada(c_all, w_ada[l], b_ada[l])
    ada_p = ada[:nb].reshape(nb, 1, 6 * d)
    ada_s = ada[nb:nb + db]

    w_in_p = _pack_w_in(w_in[l], d)
    wg = w_gdn_out[l].astype(bf16)
    wd = w_dsa_out[l].astype(bf16)
    wo = w_o[l].astype(bf16)
    wup = w_up[l].astype(bf16)
    wdn = w_down[l].astype(bf16)
    tn = _mm_tile(w_in_p.shape[1])
    tf = _ffn_tile(dff)

    xp = x_prompt.reshape(nb * t, d)
    tm_p = min(512, t)
    h1 = _prep(xp, g_norm1[l], ada_p, "prompt", t, tm_p)
    yp = _matmul(h1, w_in_p, tm_p, tn, "in_proj_prompt")
    og_p, s_p = _gdn_prompt(yp, w_gdn_conv[l], a_log[l], dt_bias[l], g_gdn_norm[l], nb, t)
    dq_p, dk_p, iq_p, ik_p, sm_p = _rope(yp, jnp.arange(t), tm_p, "prompt")
    od_p = _dsa_prompt(iq_p, sm_p, ik_p, dq_p, dk_p, yp, nb, t, min(DSA_TOPK, t // 4))
    x1_p, h2_p = _merge(og_p, od_p, yp, xp, ada_p, g_norm2[l], wg, wd, wo, "prompt", t, min(256, t))
    y_p, ug_p, uv_p = _ffn(h2_p, x1_p, ada_p, g_final, wup, w_ffn_conv[l], b_ffn_conv[l], wdn,
                           "prompt", t, tm_p, tf)

    yp3 = yp.reshape(nb, t, -1)
    nfb = FFN_CONV - 1
    out_p = (
        y_p.reshape(nb, t, d),
        dk_p.reshape(1, nb, t, DSA_KV_HEADS, DSA_HEAD_DIM),
        yp3[:, :, OFF_DV:OFF_DV + DKV_W].reshape(1, nb, t, DSA_KV_HEADS, DSA_HEAD_DIM),
        ik_p.reshape(1, nb, t, IDX_DIM),
        s_p[None],
        yp3[:, t - (GDN_CONV - 1):, :QKV_W][None],
        jnp.concatenate([ug_p[:, SUBLANES - nfb:], uv_p[:, SUBLANES - nfb:]], axis=-1)[None],
    )

    xs = x_sample.transpose(1, 0, 2).reshape(ts * db, d)
    h1s = _prep(xs, g_norm1[l], ada_s, "sample", db, db)
    ys = _matmul(h1s, w_in_p, ts * db, tn, "in_proj_sample")
    ys3 = ys.reshape(ts, db, -1)
    og_s, s_s = _gdn_sample(ys3, state_gdn_conv[l].transpose(1, 0, 2), state_gdn[l],
                            w_gdn_conv[l], a_log[l], dt_bias[l], g_gdn_norm[l])
    pos_s = jnp.repeat(past + jnp.arange(ts), db)
    dq_s, dk_s, iq_s, ik_s, sm_s = _rope(ys, pos_s, db, "sample", db)
    ik_s_b = ik_s.reshape(ts, db, IDX_DIM).transpose(1, 0, 2)
    lp = past + LANES
    tp = -(-ts // SUBLANES) * SUBLANES
    pad_t = lambda a: jnp.pad(a.astype(f32), ((0, 0), (0, tp - ts), (0, 0)))
    scores = _idx_scores(page_table, pad_t(iq_s.reshape(db, ts, IQ_W)), pad_t(sm_s.reshape(db, ts, LANES)),
                         pad_t(ik_s_b), cache_idx_k[l], lp, ts)
    sel = _select_sample(scores.reshape(db * ts, lp), min(DSA_TOPK, (past + ts) // 4), past, ts)
    dv_s_b = ys3[:, :, OFF_DV:OFF_DV + DKV_W].transpose(1, 0, 2)
    ck = cache_k[l].reshape(cache_k.shape[1], page * DSA_KV_HEADS, DSA_HEAD_DIM)
    cv = cache_v[l].reshape(cache_v.shape[1], page * DSA_KV_HEADS, DSA_HEAD_DIM)
    od_s = _dsa_sample(page_table, pad_t(dq_s.reshape(db, ts, DQ_W)), sel.reshape(db, ts, lp),
                       pad_t(dk_s.reshape(db, ts, DKV_W)), pad_t(dv_s_b), ck, cv)
    od_s = od_s.astype(bf16).transpose(1, 0, 2).reshape(ts * db, DQ_W)
    x1_s, h2_s = _merge(og_s.reshape(ts * db, Z_W), od_s, ys, xs, ada_s, g_norm2[l], wg, wd, wo,
                        "sample", db, db)
    fb = state_ffn_conv[l].transpose(1, 0, 2)
    nj = dff // tf
    fb = jnp.concatenate([fb[:, :, :dff].reshape(nfb, db, nj, tf), fb[:, :, dff:].reshape(nfb, db, nj, tf)],
                         axis=-1).reshape(nfb, db, 2 * dff)
    y_s, ug_s, uv_s = _ffn(h2_s, x1_s, ada_s, g_final, wup, w_ffn_conv[l], b_ffn_conv[l], wdn,
                           "sample", db, ts * db, tf, buf=fb)

    out_s = (
        y_s.reshape(ts, db, d).transpose(1, 0, 2),
        dk_s.reshape(1, db, ts, DSA_KV_HEADS, DSA_HEAD_DIM),
        dv_s_b.reshape(1, db, ts, DSA_KV_HEADS, DSA_HEAD_DIM),
        ik_s_b[None],
        s_s[None],
        ys3[ts - (GDN_CONV - 1):, :, :QKV_W].transpose(1, 0, 2)[None],
        jnp.concatenate([ug_s, uv_s], axis=-1).transpose(1, 0, 2)[None],
    )
    return (out_p[0], out_s[0]) + out_p[1:] + out_s[1:]
```

```python
import functools

import jax
import jax.numpy as jnp
from jax import lax
from jax.experimental import pallas as pl
from jax.experimental.pallas import tpu as pltpu

f32 = jnp.float32
bf16 = jnp.bfloat16

GDN_HEADS = 8
GDN_DK = 128
GDN_DV = 128
GDN_CONV = 4
GDN_CHUNK = 64
GDN_GROUP = 4
DSA_HEADS = 8
DSA_KV_HEADS = 2
DSA_HEAD_DIM = 128
IDX_HEADS = 8
IDX_DIM = 64
IDX_SCALE = IDX_HEADS ** -0.5 * IDX_DIM ** -0.5
DSA_TOPK = 256
ROPE_THETA = 500000.0
ROPE_FRACTION = 4
FFN_CONV = 3
NORM_EPS = 1e-6

LANES = 128
SUBLANES = 8
NEG = float(jnp.finfo(jnp.float32).min)
POS = float(jnp.finfo(jnp.float32).max)
BISECT_UNROLL = 4
SNAP_FROM = 2
BISECT_CAP = 1024
DSA_KEY_TILE = 256

QKV_W = 2 * GDN_HEADS * GDN_DK + GDN_HEADS * GDN_DV
Z_W = GDN_HEADS * GDN_DV
DQ_W = DSA_HEADS * DSA_HEAD_DIM
DKV_W = DSA_KV_HEADS * DSA_HEAD_DIM
IQ_W = IDX_HEADS * IDX_DIM
OFF_QKV = 0
OFF_Z = OFF_QKV + QKV_W
OFF_DQ = OFF_Z + Z_W
OFF_DK = OFF_DQ + DQ_W
OFF_DV = OFF_DK + DKV_W
OFF_IQ = OFF_DV + DKV_W
OFF_GG = OFF_IQ + IQ_W
SM_IK = 0
SM_A = IDX_DIM
SM_B = SM_A + GDN_HEADS
SM_IW = SM_B + GDN_HEADS


def _sigmoid(x):
    return 1.0 / (1.0 + jnp.exp(-x))


def _silu(x):
    return x * _sigmoid(x)


def _dot(a, b):
    return jnp.dot(a, b, preferred_element_type=f32)


def _dot_nt(a, b):
    return lax.dot_general(a, b, (((1,), (1,)), ((), ())), preferred_element_type=f32)


def _dot_tn(a, b):
    return lax.dot_general(a, b, (((0,), (0,)), ((), ())), preferred_element_type=f32)


def _split3(a):
    hi = a.astype(bf16)
    lo = (a - hi.astype(f32)).astype(bf16)
    return hi, lo


def _mm3(a, b):
    ah, al = a
    bh, bl = b
    return _dot(ah, bh) + (_dot(ah, bl) + _dot(al, bh))


def _dot_exact(a, b):
    return jnp.dot(a, b, preferred_element_type=f32, precision=lax.Precision.HIGHEST)


def _params(vmem_mb=None, n_axes=1):
    kw = dict(dimension_semantics=("arbitrary",) * n_axes)
    if vmem_mb is not None:
        kw["vmem_limit_bytes"] = vmem_mb * 1024 * 1024
    return pltpu.CompilerParams(**kw)


def _ada_kernel(c_ref, w_ref, b_ref, o_ref):
    s = _silu(c_ref[...]).astype(bf16)
    o_ref[...] = _dot(s, w_ref[...].astype(bf16)) + b_ref[...]


def _ada(c_all, w_ada, b_ada):
    m, d = c_all.shape
    n = w_ada.shape[1]
    tn = 1024
    return pl.pallas_call(
        _ada_kernel,
        grid=(n // tn,),
        in_specs=[pl.BlockSpec((m, d), lambda j: (0, 0)),
                  pl.BlockSpec((d, tn), lambda j: (0, j)),
                  pl.BlockSpec((1, tn), lambda j: (0, j))],
        out_specs=pl.BlockSpec((m, tn), lambda j: (0, j)),
        out_shape=jax.ShapeDtypeStruct((m, n), f32),
        compiler_params=_params(48),
        name="ada",
    )(c_all, w_ada, b_ada.reshape(1, n))


def _mod_spec(group, d, rows_per_batch, tm, col):
    if group == "prompt":
        return pl.BlockSpec((None, 1, d), lambda i, *_: ((i * tm) // rows_per_batch, 0, col))
    return pl.BlockSpec((rows_per_batch, d), lambda i, *_: (0, col))


def _prep_kernel(x_ref, g_ref, sc_ref, sh_ref, o_ref):
    x = x_ref[...]
    y = x * lax.rsqrt(jnp.mean(x * x, axis=-1, keepdims=True) + NORM_EPS)
    o_ref[...] = ((y * g_ref[...]) * (1.0 + sc_ref[...]) + sh_ref[...]).astype(bf16)


def _prep(x, g, ada, group, rows_per_batch, tm):
    m, d = x.shape
    return pl.pallas_call(
        _prep_kernel,
        grid=(m // tm,),
        in_specs=[pl.BlockSpec((tm, d), lambda i: (i, 0)),
                  pl.BlockSpec((1, d), lambda i: (0, 0)),
                  _mod_spec(group, d, rows_per_batch, tm, 1),
                  _mod_spec(group, d, rows_per_batch, tm, 0)],
        out_specs=pl.BlockSpec((tm, d), lambda i: (i, 0)),
        out_shape=jax.ShapeDtypeStruct((m, d), bf16),
        compiler_params=_params(),
        name="prep_" + group,
    )(x, g.reshape(1, d), ada, ada)


def _mm_kernel(a_ref, w_ref, o_ref):
    o_ref[...] = _dot(a_ref[...], w_ref[...])


def _matmul(a, w, tm, tn, name):
    m, k = a.shape
    n = w.shape[1]
    return pl.pallas_call(
        _mm_kernel,
        grid=(n // tn, m // tm),
        in_specs=[pl.BlockSpec((tm, k), lambda j, i: (i, 0)),
                  pl.BlockSpec((k, tn), lambda j, i: (0, j))],
        out_specs=pl.BlockSpec((tm, tn), lambda j, i: (i, j)),
        out_shape=jax.ShapeDtypeStruct((m, n), f32),
        compiler_params=_params(48, 2),
        name=name,
    )(a, w)


def _l2n(x):
    return x * lax.rsqrt(jnp.sum(x * x, axis=-1, keepdims=True) + NORM_EPS)


def _gdn_gates(sm, alog, dtb):
    xa = sm + dtb
    softplus = jnp.maximum(xa, 0.0) + jnp.log1p(jnp.exp(-jnp.abs(xa)))
    return -jnp.exp(alog) * softplus, _sigmoid(sm)


def _gated_norm(o, gn, z):
    y = o * lax.rsqrt(jnp.mean(o * o, axis=-1, keepdims=True) + NORM_EPS)
    return (y * gn) * _silu(z)


def _gdn_prompt_kernel(qkv_ref, halo_ref, z_ref, sm_ref, wc_ref, alog_ref, dtb_ref, gn_ref,
                       o_ref, sfin_ref, s_ref):
    c = pl.program_id(1)
    C = qkv_ref.shape[0]

    @pl.when(c == 0)
    def _():
        s_ref[...] = jnp.zeros_like(s_ref)

    keep = (c > 0).astype(f32)
    g_all, beta_all = _gdn_gates(sm_ref[...], alog_ref[...], dtb_ref[...])
    ri = lax.broadcasted_iota(jnp.int32, (C, C), 0)
    ci = lax.broadcasted_iota(jnp.int32, (C, C), 1)
    gc_all = _dot_exact((ri >= ci).astype(f32), g_all)
    gc_t = gc_all.T

    def conv(col):
        xs = jnp.concatenate([halo_ref[:, col:col + LANES] * keep, qkv_ref[:, col:col + LANES]], axis=0)
        w = wc_ref[:, col:col + LANES]
        y = xs[SUBLANES:] * w[GDN_CONV - 1:GDN_CONV]
        for i in range(GDN_CONV - 1):
            sft = GDN_CONV - 1 - i
            y = y + xs[SUBLANES - sft:SUBLANES - sft + C] * w[i:i + 1]
        return _silu(y)

    N = GDN_GROUP * C
    rn = lax.broadcasted_iota(jnp.int32, (N, N), 0)
    cn = lax.broadcasted_iota(jnp.int32, (N, N), 1)
    same = (rn // C) == (cn // C)
    incl = jnp.logical_and(same, rn >= cn)
    strict = jnp.logical_and(same, rn > cn)
    eye_f = (rn == cn).astype(f32)
    n_sq = max(1, (C - 1).bit_length() - 1)
    groups = [list(range(g0, g0 + GDN_GROUP)) for g0 in range(0, GDN_HEADS, GDN_GROUP)]
    stack = lambda xs: jnp.concatenate(xs, axis=0)

    qs = [stack([_l2n(conv(h * GDN_DK)) * (GDN_DK ** -0.5) for h in hs]) for hs in groups]
    ks = [stack([_l2n(conv(GDN_HEADS * GDN_DK + h * GDN_DK)) for h in hs]) for hs in groups]
    vs = [stack([conv(2 * GDN_HEADS * GDN_DK + h * GDN_DV) for h in hs]) for hs in groups]
    gcs = [stack([gc_all[:, SM_A + h:SM_A + h + 1] for h in hs]) for hs in groups]
    betas = [stack([beta_all[:, SM_B + h:SM_B + h + 1] for h in hs]) for hs in groups]
    gc_rows = [jnp.concatenate([gc_t[SM_A + h:SM_A + h + 1, :] for h in hs], axis=1) for hs in groups]
    egs = [jnp.exp(gc) for gc in gcs]
    decays = [jnp.where(incl, jnp.exp(jnp.where(incl, gc - gr, 0.0)), 0.0) for gc, gr in zip(gcs, gc_rows)]
    kbs = [k.astype(bf16) for k in ks]
    a_s = [jnp.where(strict, b * _dot_nt(kb, kb) * dc, 0.0) for b, kb, dc in zip(betas, kbs, decays)]
    qks = [(_dot_nt(q.astype(bf16), kb) * dc).astype(bf16) for q, kb, dc in zip(qs, kbs, decays)]

    invs = [eye_f - a for a in a_s]
    pws = [_split3(a) for a in a_s]
    for _ in range(n_sq):
        pws = [_split3(_mm3(p, p)) for p in pws]
        invs = [inv + _mm3(_split3(inv), p) for inv, p in zip(invs, pws)]
    rhs = [jnp.concatenate([b * v, (b * eg) * k], axis=1) for b, v, eg, k in zip(betas, vs, egs, ks)]
    sols = [_mm3(_split3(inv), _split3(r)) for inv, r in zip(invs, rhs)]

    for gi, hs in enumerate(groups):
        sol, q, k, gc, eg = sols[gi], qs[gi], ks[gi], gcs[gi], egs[gi]
        q_dec = (q * eg).astype(bf16)
        w_k = sol[:, GDN_DV:].astype(bf16)
        rows = [slice(j * C, (j + 1) * C) for j in range(GDN_GROUP)]
        s_old = [s_ref[h] for h in hs]
        ws = [_dot(jnp.concatenate([w_k[r], q_dec[r]], axis=0), s.astype(bf16)) for r, s in zip(rows, s_old)]
        u = stack([sol[r, :GDN_DV] - w[:C] for r, w in zip(rows, ws)])
        ub = u.astype(bf16)
        o_intra = _dot(qks[gi], ub)
        for j, h in enumerate(hs):
            r = rows[j]
            gl = gc[r][C - 1:C, :]
            k_end = (k[r] * jnp.exp(gl - gc[r])).astype(bf16)
            s_ref[h] = s_old[j] * jnp.exp(gl) + _dot_tn(k_end, ub[r])
            o = ws[j][C:] + o_intra[r]
            zh = z_ref[:, h * GDN_DV:(h + 1) * GDN_DV]
            o_ref[:, h * GDN_DV:(h + 1) * GDN_DV] = _gated_norm(o, gn_ref[...], zh).astype(bf16)

    @pl.when(c == pl.num_programs(1) - 1)
    def _():
        sfin_ref[...] = s_ref[...]


def _gdn_vecs(a_log, dt_bias, g_gdn_norm):
    alog = jnp.zeros((1, LANES), f32).at[0, SM_A:SM_A + GDN_HEADS].set(a_log)
    dtb = jnp.zeros((1, LANES), f32).at[0, SM_A:SM_A + GDN_HEADS].set(dt_bias)
    return alog, dtb, g_gdn_norm.reshape(1, GDN_DV)


def _gdn_prompt(y, w_conv, a_log, dt_bias, g_gdn_norm, nb, t):
    C = GDN_CHUNK
    nc = t // C
    small_blk = y.shape[1] // LANES - 1
    alog, dtb, gn = _gdn_vecs(a_log, dt_bias, g_gdn_norm)
    cst = lambda b, c: (0, 0)
    return pl.pallas_call(
        _gdn_prompt_kernel,
        grid=(nb, nc),
        in_specs=[pl.BlockSpec((C, QKV_W), lambda b, c: (b * nc + c, 0)),
                  pl.BlockSpec((SUBLANES, QKV_W),
                               lambda b, c: (jnp.maximum((b * nc + c) * (C // SUBLANES) - 1, 0), 0)),
                  pl.BlockSpec((C, Z_W), lambda b, c: (b * nc + c, OFF_Z // Z_W)),
                  pl.BlockSpec((C, LANES), lambda b, c: (b * nc + c, small_blk)),
                  pl.BlockSpec((GDN_CONV, QKV_W), cst),
                  pl.BlockSpec((1, LANES), cst),
                  pl.BlockSpec((1, LANES), cst),
                  pl.BlockSpec((1, GDN_DV), cst)],
        out_specs=[pl.BlockSpec((C, Z_W), lambda b, c: (b * nc + c, 0)),
                   pl.BlockSpec((None, GDN_HEADS, GDN_DK, GDN_DV), lambda b, c: (b, 0, 0, 0))],
        out_shape=[jax.ShapeDtypeStruct((nb * t, Z_W), bf16),
                   jax.ShapeDtypeStruct((nb, GDN_HEADS, GDN_DK, GDN_DV), f32)],
        scratch_shapes=[pltpu.VMEM((GDN_HEADS, GDN_DK, GDN_DV), f32)],
        compiler_params=_params(None, 2),
        name="gdn_prompt",
    )(y, y, y, y, w_conv, alog, dtb, gn)


def _gdn_sample_kernel(qkv_ref, buf_ref, z_ref, sm_ref, sin_ref, wc_ref, alog_ref, dtb_ref, gn_ref,
                       o_ref, sout_ref, q_s, k_s, v_s, a_s, b_s, o_s):
    ts, G, _ = qkv_ref.shape
    nbuf = GDN_CONV - 1

    for h in range(GDN_HEADS):
        for part, dst in ((0, q_s), (1, k_s), (2, v_s)):
            col = part * GDN_HEADS * GDN_DK + h * GDN_DK
            w = wc_ref[:, col:col + LANES]
            rows = [buf_ref[i, :, col:col + LANES] for i in range(nbuf)]
            rows += [qkv_ref[t, :, col:col + LANES] for t in range(ts)]
            for t in range(ts):
                y = rows[t] * w[0:1]
                for i in range(1, GDN_CONV):
                    y = y + rows[t + i] * w[i:i + 1]
                y = _silu(y)
                if part == 0:
                    y = _l2n(y) * (GDN_DK ** -0.5)
                elif part == 1:
                    y = _l2n(y)
                dst[t, :, h * LANES:(h + 1) * LANES] = y

    for t in range(ts):
        g_all, beta_all = _gdn_gates(sm_ref[t], alog_ref[...], dtb_ref[...])
        a_all = jnp.exp(g_all)
        for h in range(GDN_HEADS):
            a_s[t, :, h * LANES:(h + 1) * LANES] = jnp.broadcast_to(a_all[:, SM_A + h:SM_A + h + 1], (G, LANES))
            b_s[t, :, h * LANES:(h + 1) * LANES] = jnp.broadcast_to(beta_all[:, SM_B + h:SM_B + h + 1], (G, LANES))

    def head(h, carry):
        cols = pl.ds(pl.multiple_of(h * LANES, LANES), LANES)
        for i in range(G):
            row = slice(i, i + 1)
            kq = jnp.concatenate([k_s[t, row, cols] for t in range(ts)]
                                 + [q_s[t, row, cols] for t in range(ts)], axis=0)
            kq_t = kq.T
            s = sin_ref[i, h]
            for t in range(ts):
                kc = kq_t[:, t:t + 1]
                qc = kq_t[:, ts + t:ts + t + 1]
                a = a_s[t, row, cols]
                b = b_s[t, row, cols]
                ks = jnp.sum(s * kc, axis=0, keepdims=True)
                r = b * (v_s[t, row, cols] - a * ks)
                s = a * s + kc * r
                o_s[t, row, cols] = jnp.sum(s * qc, axis=0, keepdims=True)
            sout_ref[i, h] = s
        return carry

    lax.fori_loop(0, GDN_HEADS, head, 0)

    for t in range(ts):
        for h in range(GDN_HEADS):
            sl = slice(h * GDN_DV, (h + 1) * GDN_DV)
            o_ref[t, :, sl] = _gated_norm(o_s[t, :, sl], gn_ref[...], z_ref[t, :, sl]).astype(bf16)


def _gdn_sample(y3, buf3, state, w_conv, a_log, dt_bias, g_gdn_norm):
    ts, db, n = y3.shape
    G = SUBLANES
    small_blk = n // LANES - 1
    alog, dtb, gn = _gdn_vecs(a_log, dt_bias, g_gdn_norm)
    cst = lambda g: (0, 0)
    st_spec = pl.BlockSpec((G, GDN_HEADS, GDN_DK, GDN_DV), lambda g: (g, 0, 0, 0))
    scr = pltpu.VMEM((ts, G, Z_W), f32)
    return pl.pallas_call(
        _gdn_sample_kernel,
        grid=(db // G,),
        in_specs=[pl.BlockSpec((ts, G, QKV_W), lambda g: (0, g, 0)),
                  pl.BlockSpec((GDN_CONV - 1, G, QKV_W), lambda g: (0, g, 0)),
                  pl.BlockSpec((ts, G, Z_W), lambda g: (0, g, OFF_Z // Z_W)),
                  pl.BlockSpec((ts, G, LANES), lambda g: (0, g, small_blk)),
                  st_spec,
                  pl.BlockSpec((GDN_CONV, QKV_W), cst),
                  pl.BlockSpec((1, LANES), cst),
                  pl.BlockSpec((1, LANES), cst),
                  pl.BlockSpec((1, GDN_DV), cst)],
        out_specs=[pl.BlockSpec((ts, G, Z_W), lambda g: (0, g, 0)), st_spec],
        out_shape=[jax.ShapeDtypeStruct((ts, db, Z_W), bf16),
                   jax.ShapeDtypeStruct(state.shape, f32)],
        scratch_shapes=[scr, scr, scr, scr, scr, scr],
        compiler_params=_params(48),
        name="gdn_sample",
    )(y3, buf3, y3, y3, state, w_conv, alog, dtb, gn)


def _rope_tables(pos, rot, width):
    half = rot // 2
    inv_freq = ROPE_THETA ** (-jnp.arange(half, dtype=f32) * (2.0 / rot))
    ang = pos.astype(f32)[:, None] * inv_freq[None, :]
    cos, sin = jnp.cos(ang), jnp.sin(ang)
    n = pos.shape[0]
    z = lambda w: jnp.zeros((n, w), f32)
    cosw = jnp.concatenate([cos, cos, jnp.ones((n, width - rot), f32)], axis=1)
    sina = jnp.concatenate([-sin, z(width - half)], axis=1)
    sinb = jnp.concatenate([z(half), sin, z(width - rot)], axis=1)
    reps = LANES // width
    return tuple(jnp.tile(a, (1, reps)) for a in (cosw, sina, sinb))


def _rope_kernel(dq_ref, dk_ref, iq_ref, sm_ref, c1, sa1, sb1, c2, sa2, sb2,
                 dq_o, dk_o, iq_o, ik_o, sm_o):
    h1 = DSA_HEAD_DIM // ROPE_FRACTION // 2
    h2 = IDX_DIM // ROPE_FRACTION // 2

    def rot(x, c, sa, sb, half):
        return x * c[...] + pltpu.roll(x, LANES - half, 1) * sa[...] + pltpu.roll(x, half, 1) * sb[...]

    for j in range(DQ_W // LANES):
        sl = slice(j * LANES, (j + 1) * LANES)
        dq_o[:, sl] = rot(dq_ref[:, sl], c1, sa1, sb1, h1).astype(bf16)
    for j in range(DKV_W // LANES):
        sl = slice(j * LANES, (j + 1) * LANES)
        dk_o[:, sl] = rot(dk_ref[:, sl], c1, sa1, sb1, h1)
    for j in range(IQ_W // LANES):
        sl = slice(j * LANES, (j + 1) * LANES)
        iq_o[:, sl] = rot(iq_ref[:, sl], c2, sa2, sb2, h2).astype(bf16)
    sm = sm_ref[...]
    ik_o[...] = rot(sm, c2, sa2, sb2, h2)[:, :IDX_DIM]
    sm_o[...] = sm


def _rope(y, pos, tm, group, db=None):
    m, n = y.shape
    small_blk = n // LANES - 1
    t1 = _rope_tables(pos, DSA_HEAD_DIM // ROPE_FRACTION, DSA_HEAD_DIM)
    t2 = _rope_tables(pos, IDX_DIM // ROPE_FRACTION, IDX_DIM)
    tab_blocks = pos.shape[0] // tm
    tab = pl.BlockSpec((tm, LANES), lambda i: (i % tab_blocks, 0))
    if group == "prompt":
        omap = lambda i: (i, 0)
        rows = lambda w: m
        cols = lambda w: w
    else:
        ts = m // db
        omap = lambda i: (0, i)
        rows = lambda w: db
        cols = lambda w: ts * w
    out = lambda w, dt: jax.ShapeDtypeStruct((rows(w), cols(w)), dt)
    return pl.pallas_call(
        _rope_kernel,
        grid=(m // tm,),
        in_specs=[pl.BlockSpec((tm, DQ_W), lambda i: (i, OFF_DQ // DQ_W)),
                  pl.BlockSpec((tm, DKV_W), lambda i: (i, OFF_DK // DKV_W)),
                  pl.BlockSpec((tm, IQ_W), lambda i: (i, OFF_IQ // IQ_W)),
                  pl.BlockSpec((tm, LANES), lambda i: (i, small_blk)),
                  tab, tab, tab, tab, tab, tab],
        out_specs=[pl.BlockSpec((tm, DQ_W), omap),
                   pl.BlockSpec((tm, DKV_W), omap),
                   pl.BlockSpec((tm, IQ_W), omap),
                   pl.BlockSpec((tm, IDX_DIM), lambda i: (i, 0)),
                   pl.BlockSpec((tm, LANES), omap)],
        out_shape=[out(DQ_W, bf16), out(DKV_W, f32), out(IQ_W, bf16),
                   jax.ShapeDtypeStruct((m, IDX_DIM), f32), out(LANES, f32)],
        compiler_params=_params(),
        name="rope_" + group,
    )(y, y, y, y, *t1, *t2)


def _topk_bias(x_s, bias_s, lo0, mx, few, nkt, kt_w, k):
    R = x_s.shape[0]
    kf = float(k)
    tile = lambda kt: pl.ds(pl.multiple_of(kt * kt_w, kt_w), kt_w)

    def count_ge(th):
        def step(kt, acc):
            return acc + (x_s[:, tile(kt)] >= th).astype(f32)
        acc = lax.fori_loop(0, nkt, step, jnp.zeros((R, kt_w), f32))
        return jnp.sum(acc, axis=1, keepdims=True)

    def bisect(lo, hi, done):
        mid = 0.5 * lo + 0.5 * hi
        cnt = count_ge(mid)
        collapsed = jnp.logical_or(mid <= lo, mid >= hi)
        live = jnp.logical_and(done < 0.5, jnp.logical_not(collapsed))
        lo = jnp.where(jnp.logical_and(live, cnt >= kf), mid, lo)
        hi = jnp.where(jnp.logical_and(live, cnt <= kf), mid, hi)
        done = jnp.where(jnp.logical_or(collapsed, cnt == kf), 1.0, done)
        return lo, hi, done

    def snap(lo, hi, done):
        def step(kt, c):
            a, b = c
            x = x_s[:, tile(kt)]
            return (jnp.minimum(a, jnp.where(x >= lo, x, POS)), jnp.maximum(b, jnp.where(x < hi, x, NEG)))
        a, b = lax.fori_loop(0, nkt, step, (jnp.full((R, kt_w), POS, f32), jnp.full((R, kt_w), NEG, f32)))
        a = jnp.min(a, axis=1, keepdims=True)
        b = jnp.max(b, axis=1, keepdims=True)
        live = done < 0.5
        return jnp.where(live, a, lo), jnp.where(jnp.logical_and(live, a >= b), 1.0, done)

    def cond(c):
        it, _, _, done = c
        return jnp.logical_and(it < BISECT_CAP, jnp.min(done) < 0.5)

    def body(c):
        it, lo, hi, done = c
        for _ in range(BISECT_UNROLL):
            lo, hi, done = bisect(lo, hi, done)
        lo, done = lax.cond(it >= SNAP_FROM, lambda: snap(lo, hi, done), lambda: (lo, done))
        return it + 1, lo, hi, done

    hi0 = jnp.where(few, lo0, mx + (jnp.abs(mx) + 1.0))
    _, lo, hi, _ = lax.while_loop(cond, body, (jnp.int32(0), lo0, hi0, few.astype(f32)))

    has_run = jnp.max(jnp.where(lo < hi, 1.0, 0.0)) > 0.5

    @pl.when(jnp.logical_not(has_run))
    def _():
        def step(kt, c):
            bias_s[:, tile(kt)] = jnp.where(x_s[:, tile(kt)] >= hi, 0.0, NEG)
            return c
        lax.fori_loop(0, nkt, step, 0)

    @pl.when(has_run)
    def _():
        need = kf - count_ge(hi)
        ai = lax.broadcasted_iota(jnp.int32, (kt_w, kt_w), 0)
        bi = lax.broadcasted_iota(jnp.int32, (kt_w, kt_w), 1)
        before = (ai < bi).astype(bf16)

        def step(kt, seen):
            x = x_s[:, tile(kt)]
            run = jnp.logical_and(x >= lo, x < hi)
            runf = run.astype(f32)
            rank = seen + _dot(runf.astype(bf16), before)
            take = jnp.logical_or(x >= hi, jnp.logical_and(run, rank < need))
            bias_s[:, tile(kt)] = jnp.where(take, 0.0, NEG)
            return seen + jnp.sum(runf, axis=1, keepdims=True)
        lax.fori_loop(0, nkt, step, jnp.zeros((R, 1), f32))


def _dsa_prompt_kernel(iq_ref, sm_ref, ik_ref, q_ref, k_ref, v_ref, o_ref,
                       ikb, kb, vb, x_s, bias_s, s_s, *, topk, kt_w):
    qi = pl.program_id(1)
    R = iq_ref.shape[0]
    nkt = ((qi + 1) * R + kt_w - 1) // kt_w
    tile = lambda kt: pl.ds(pl.multiple_of(kt * kt_w, kt_w), kt_w)

    @pl.when(qi == 0)
    def _():
        ikb[...] = ik_ref[...].astype(bf16)
        kb[...] = k_ref[...].astype(bf16)
        vb[...] = v_ref[...].astype(bf16)

    wgt = sm_ref[...] * IDX_SCALE
    qpos = qi * R + lax.broadcasted_iota(jnp.int32, (R, kt_w), 0)
    key0 = lax.broadcasted_iota(jnp.int32, (R, kt_w), 1)

    def score_tile(kt, c):
        mn, mx = c
        keys = ikb[tile(kt), :]
        sc = jnp.zeros((R, kt_w), f32)
        for h in range(IDX_HEADS):
            rel = jnp.maximum(_dot_nt(iq_ref[:, h * IDX_DIM:(h + 1) * IDX_DIM], keys), 0.0)
            sc = sc + rel * wgt[:, SM_IW + h:SM_IW + h + 1]
        causal = key0 + kt * kt_w <= qpos
        x_s[:, tile(kt)] = jnp.where(causal, sc, NEG)
        return jnp.minimum(mn, jnp.where(causal, sc, POS)), jnp.maximum(mx, jnp.where(causal, sc, NEG))

    mn, mx = lax.fori_loop(0, nkt, score_tile,
                           (jnp.full((R, kt_w), POS, f32), jnp.full((R, kt_w), NEG, f32)))
    n_causal = qi * R + lax.broadcasted_iota(jnp.int32, (R, 1), 0) + 1
    _topk_bias(x_s, bias_s, jnp.min(mn, axis=1, keepdims=True), jnp.max(mx, axis=1, keepdims=True),
               n_causal <= topk, nkt, kt_w, topk)

    def head(h, carry):
        g = h // (DSA_HEADS // DSA_KV_HEADS)
        hc = pl.ds(pl.multiple_of(h * DSA_HEAD_DIM, DSA_HEAD_DIM), DSA_HEAD_DIM)
        gc = pl.ds(pl.multiple_of(g * DSA_HEAD_DIM, DSA_HEAD_DIM), DSA_HEAD_DIM)
        qh = q_ref[:, hc]

        def logits(kt, mxa):
            s = _dot_nt(qh, kb[tile(kt), gc]) * (DSA_HEAD_DIM ** -0.5) + bias_s[:, tile(kt)]
            s_s[:, tile(kt)] = s
            return jnp.maximum(mxa, s)

        m = jnp.max(lax.fori_loop(0, nkt, logits, jnp.full((R, kt_w), NEG, f32)), axis=1, keepdims=True)

        def accum(kt, c):
            la, acc = c
            p = jnp.exp(s_s[:, tile(kt)] - m)
            return la + p, acc + _dot(p.astype(bf16), vb[tile(kt), gc])

        la, acc = lax.fori_loop(0, nkt, accum,
                                (jnp.zeros((R, kt_w), f32), jnp.zeros((R, DSA_HEAD_DIM), f32)))
        o_ref[:, hc] = (acc / jnp.sum(la, axis=1, keepdims=True)).astype(bf16)
        return carry

    lax.fori_loop(0, DSA_HEADS, head, 0)


def _dsa_prompt(iq, sm, ik, dq, dk, y, nb, t, topk):
    R = LANES
    nq = t // R
    kt_w = min(DSA_KEY_TILE, t)
    kern = functools.partial(_dsa_prompt_kernel, topk=topk, kt_w=kt_w)
    return pl.pallas_call(
        kern,
        grid=(nb, nq),
        in_specs=[pl.BlockSpec((R, IQ_W), lambda b, i: (b * nq + i, 0)),
                  pl.BlockSpec((R, LANES), lambda b, i: (b * nq + i, 0)),
                  pl.BlockSpec((t, IDX_DIM), lambda b, i: (b, 0)),
                  pl.BlockSpec((R, DQ_W), lambda b, i: (b * nq + i, 0)),
                  pl.BlockSpec((t, DKV_W), lambda b, i: (b, 0)),
                  pl.BlockSpec((t, DKV_W), lambda b, i: (b, OFF_DV // DKV_W))],
        out_specs=pl.BlockSpec((R, DQ_W), lambda b, i: (b * nq + i, 0)),
        out_shape=jax.ShapeDtypeStruct((nb * t, DQ_W), bf16),
        scratch_shapes=[pltpu.VMEM((t, IDX_DIM), bf16), pltpu.VMEM((t, DKV_W), bf16),
                        pltpu.VMEM((t, DKV_W), bf16), pltpu.VMEM((R, t), f32), pltpu.VMEM((R, t), f32),
                        pltpu.VMEM((R, t), f32)],
        compiler_params=_params(48, 2),
        name="dsa_prompt",
    )(iq, sm, ik, dq, dk, y)


def _idx_score_kernel(pt_ref, iq_ref, sm_ref, ikn_ref, *rest, n_pages, page):
    pages = rest[:n_pages]
    o_ref = rest[n_pages]
    tp = iq_ref.shape[0]
    ts = o_ref.shape[0]
    past = n_pages * page
    lp = o_ref.shape[1]
    keys = jnp.concatenate([p[...] for p in pages]
                           + [ikn_ref[...], jnp.zeros((lp - past - tp, IDX_DIM), f32)], axis=0).astype(bf16)
    wgt = sm_ref[...] * IDX_SCALE
    iq = iq_ref[...].astype(bf16)
    score = jnp.zeros((tp, lp), f32)
    for h in range(IDX_HEADS):
        rel = jnp.maximum(_dot_nt(iq[:, h * IDX_DIM:(h + 1) * IDX_DIM], keys), 0.0)
        score = score + rel * wgt[:, SM_IW + h:SM_IW + h + 1]
    o_ref[...] = score[:ts]


def _idx_scores(page_table, iq, sm, ikn, cache_idx, lp, ts):
    db, tp, _ = iq.shape
    n_pages = page_table.shape[1]
    page = cache_idx.shape[1]
    kern = functools.partial(_idx_score_kernel, n_pages=n_pages, page=page)
    page_specs = [pl.BlockSpec((None, page, IDX_DIM), functools.partial(lambda b, pt, p: (pt[b, p], 0, 0), p=p))
                  for p in range(n_pages)]
    grid_spec = pltpu.PrefetchScalarGridSpec(
        num_scalar_prefetch=1,
        grid=(db,),
        in_specs=[pl.BlockSpec((None, tp, IQ_W), lambda b, pt: (b, 0, 0)),
                  pl.BlockSpec((None, tp, LANES), lambda b, pt: (b, 0, 0)),
                  pl.BlockSpec((None, tp, IDX_DIM), lambda b, pt: (b, 0, 0))] + page_specs,
        out_specs=pl.BlockSpec((None, ts, lp), lambda b, pt: (b, 0, 0)),
    )
    return pl.pallas_call(
        kern, grid_spec=grid_spec,
        out_shape=jax.ShapeDtypeStruct((db, ts, lp), f32),
        compiler_params=_params(),
        name="idx_scores_sample",
    )(page_table, iq, sm, ikn, *([cache_idx] * n_pages))


def _select_sample_kernel(x_ref, bias_ref, x_s, *, topk, past, ts):
    R, lp = x_ref.shape
    key = lax.broadcasted_iota(jnp.int32, (R, lp), 1)
    t = lax.broadcasted_iota(jnp.int32, (R, lp), 0) % ts
    causal = key <= past + t
    x = x_ref[...]
    x_s[...] = jnp.where(causal, x, NEG)
    lo0 = jnp.min(jnp.where(causal, x, POS), axis=1, keepdims=True)
    mx = jnp.max(jnp.where(causal, x, NEG), axis=1, keepdims=True)
    n_causal = past + lax.broadcasted_iota(jnp.int32, (R, 1), 0) % ts + 1
    _topk_bias(x_s, bias_ref, lo0, mx, n_causal <= topk, lp // LANES, LANES, topk)


def _select_sample(scores, topk, past, ts):
    m, lp = scores.shape
    R = LANES
    kern = functools.partial(_select_sample_kernel, topk=topk, past=past, ts=ts)
    return pl.pallas_call(
        kern, grid=(m // R,),
        in_specs=[pl.BlockSpec((R, lp), lambda i: (i, 0))],
        out_specs=pl.BlockSpec((R, lp), lambda i: (i, 0)),
        out_shape=jax.ShapeDtypeStruct((m, lp), f32),
        scratch_shapes=[pltpu.VMEM((R, lp), f32)],
        compiler_params=_params(),
        name="select_sample",
    )(scores)


def _dsa_sample_kernel(pt_ref, q_ref, sel_ref, kn_ref, vn_ref, *rest, n_pages, page):
    kp = rest[:n_pages]
    vp = rest[n_pages:2 * n_pages]
    o_ref = rest[2 * n_pages]
    tp = q_ref.shape[0]
    ts = sel_ref.shape[0]
    past = n_pages * page
    lp = sel_ref.shape[1]
    hpg = DSA_HEADS // DSA_KV_HEADS
    bias = jnp.concatenate([sel_ref[...], jnp.zeros((tp - ts, lp), f32)], axis=0)
    bias = jnp.concatenate([bias] * hpg, axis=0)
    padn = jnp.zeros((lp - past - tp, DSA_HEAD_DIM), f32)
    for g in range(DSA_KV_HEADS):
        gsl = slice(g * DSA_HEAD_DIM, (g + 1) * DSA_HEAD_DIM)
        kg = jnp.concatenate([p[pl.ds(g, page, stride=DSA_KV_HEADS), :] for p in kp]
                             + [kn_ref[:, gsl], padn], axis=0).astype(bf16)
        vg = jnp.concatenate([p[pl.ds(g, page, stride=DSA_KV_HEADS), :] for p in vp]
                             + [vn_ref[:, gsl], padn], axis=0).astype(bf16)
        qg = jnp.concatenate([q_ref[:, (g * hpg + j) * DSA_HEAD_DIM:(g * hpg + j + 1) * DSA_HEAD_DIM]
                              for j in range(hpg)], axis=0).astype(bf16)
        s = _dot_nt(qg, kg) * (DSA_HEAD_DIM ** -0.5) + bias
        m = jnp.max(s, axis=1, keepdims=True)
        p = jnp.exp(s - m)
        l = jnp.sum(p, axis=1, keepdims=True)
        o = _dot(p.astype(bf16), vg) / l
        for j in range(hpg):
            hsl = slice((g * hpg + j) * DSA_HEAD_DIM, (g * hpg + j + 1) * DSA_HEAD_DIM)
            o_ref[:, hsl] = o[j * tp:j * tp + ts]


def _dsa_sample(page_table, dq, sel, kn, vn, ck, cv):
    db, tp, _ = dq.shape
    ts = sel.shape[1]
    lp = sel.shape[2]
    n_pages = page_table.shape[1]
    rows = ck.shape[1]
    page = rows // DSA_KV_HEADS
    kern = functools.partial(_dsa_sample_kernel, n_pages=n_pages, page=page)
    pspec = [pl.BlockSpec((None, rows, DSA_HEAD_DIM), functools.partial(lambda b, pt, p: (pt[b, p], 0, 0), p=p))
             for p in range(n_pages)]
    per_b = lambda r, w: pl.BlockSpec((None, r, w), lambda b, pt: (b, 0, 0))
    grid_spec = pltpu.PrefetchScalarGridSpec(
        num_scalar_prefetch=1,
        grid=(db,),
        in_specs=[per_b(tp, DQ_W), per_b(ts, lp), per_b(tp, DKV_W), per_b(tp, DKV_W)] + pspec + pspec,
        out_specs=per_b(ts, DQ_W),
    )
    return pl.pallas_call(
        kern, grid_spec=grid_spec,
        out_shape=jax.ShapeDtypeStruct((db, ts, DQ_W), f32),
        compiler_params=_params(48),
        name="dsa_sample",
    )(page_table, dq, sel, kn, vn, *([ck] * n_pages), *([cv] * n_pages))


def _merge_kernel(og_ref, od_ref, gg_ref, gd_ref, x_ref, gt_ref, sc_ref, sh_ref, g2_ref,
                  wg_ref, wd_ref, wo_ref, x1_ref, h2_ref):
    mix = (_sigmoid(gg_ref[...]) * _dot(og_ref[...], wg_ref[...])
           + _sigmoid(gd_ref[...]) * _dot(od_ref[...], wd_ref[...]))
    x1 = x_ref[...] + gt_ref[...] * _dot(mix.astype(bf16), wo_ref[...])
    x1_ref[...] = x1
    y = x1 * lax.rsqrt(jnp.mean(x1 * x1, axis=-1, keepdims=True) + NORM_EPS)
    h2_ref[...] = ((y * g2_ref[...]) * (1.0 + sc_ref[...]) + sh_ref[...]).astype(bf16)


def _merge(o_gdn, o_dsa, y, x, ada, g2, wg, wd, wo, group, rows_per_batch, tm):
    m, d = x.shape
    row = lambda w, blk=0: pl.BlockSpec((tm, w), lambda i: (i, blk))
    res = lambda a: pl.BlockSpec(a.shape, lambda i: (0, 0), pipeline_mode=pl.Buffered(1))
    return pl.pallas_call(
        _merge_kernel,
        grid=(m // tm,),
        in_specs=[row(Z_W), row(DQ_W), row(d, OFF_GG // d), row(d, OFF_GG // d + 1), row(d),
                  _mod_spec(group, d, rows_per_batch, tm, 2),
                  _mod_spec(group, d, rows_per_batch, tm, 4),
                  _mod_spec(group, d, rows_per_batch, tm, 3),
                  pl.BlockSpec((1, d), lambda i: (0, 0)),
                  res(wg), res(wd), res(wo)],
        out_specs=[row(d), row(d)],
        out_shape=[jax.ShapeDtypeStruct((m, d), f32), jax.ShapeDtypeStruct((m, d), bf16)],
        compiler_params=_params(48),
        name="merge_" + group,
    )(o_gdn, o_dsa, y, y, x, ada, ada, ada, g2.reshape(1, d), wg, wd, wo)


def _ffn_epilogue(acc_ref, x1_ref, gt_ref, gf_ref, y_ref):
    tm = acc_ref.shape[0]
    r = gt_ref.shape[0] if gt_ref.shape[0] > 1 else tm
    for s0 in range(0, tm, r):
        sl = slice(s0, s0 + r)
        x2 = x1_ref[sl, :] + gt_ref[...] * acc_ref[sl, :]
        y_ref[sl, :] = (x2 * lax.rsqrt(jnp.mean(x2 * x2, axis=-1, keepdims=True) + NORM_EPS)) * gf_ref[...]


def _ffn_prompt_kernel(h_ref, halo_ref, wug_ref, wuv_ref, wcg_ref, wcv_ref, bg_ref, bv_ref, wd_ref,
                       x1_ref, gt_ref, gf_ref, y_ref, ug_ref, uv_ref, acc_ref, *, blocks_per_seq):
    i = pl.program_id(0)
    j = pl.program_id(1)
    tm = h_ref.shape[0]
    keep = ((i % blocks_per_seq) > 0).astype(f32)

    def branch(wu_ref, wc_ref, b_ref, ubuf_ref):
        u = _dot(h_ref[...], wu_ref[...])
        uh = _dot(halo_ref[...], wu_ref[...]) * keep
        ubuf_ref[...] = u[tm - SUBLANES:]
        ux = jnp.concatenate([uh, u], axis=0)
        w = wc_ref[...]
        y = u * w[FFN_CONV - 1:FFN_CONV] + b_ref[...]
        for t in range(FFN_CONV - 1):
            sft = FFN_CONV - 1 - t
            y = y + ux[SUBLANES - sft:SUBLANES - sft + tm] * w[t:t + 1]
        return y

    act = _silu(branch(wug_ref, wcg_ref, bg_ref, ug_ref)) * branch(wuv_ref, wcv_ref, bv_ref, uv_ref)
    part = _dot(act.astype(bf16), wd_ref[...])

    @pl.when(j == 0)
    def _():
        acc_ref[...] = part

    @pl.when(j > 0)
    def _():
        acc_ref[...] += part

    @pl.when(j == pl.num_programs(1) - 1)
    def _():
        _ffn_epilogue(acc_ref, x1_ref, gt_ref, gf_ref, y_ref)


def _ffn_sample_kernel(h_ref, buf_ref, wug_ref, wuv_ref, wcg_ref, wcv_ref, bg_ref, bv_ref, wd_ref,
                       x1_ref, gt_ref, gf_ref, y_ref, ug_ref, uv_ref, acc_ref, *, ts):
    j = pl.program_id(1)
    db = h_ref.shape[0] // ts
    nbuf = FFN_CONV - 1
    tf = wug_ref.shape[1]

    def branch(wu_ref, wc_ref, b_ref, ubuf_ref, half):
        u = _dot(h_ref[...], wu_ref[...])
        rows = [buf_ref[r, :, half * tf:(half + 1) * tf] for r in range(nbuf)]
        rows += [u[t * db:(t + 1) * db] for t in range(ts)]
        for r in range(nbuf):
            ubuf_ref[r] = rows[ts + r]
        w = wc_ref[...]
        outs = []
        for t in range(ts):
            y = rows[t] * w[0:1] + b_ref[...]
            for r in range(1, FFN_CONV):
                y = y + rows[t + r] * w[r:r + 1]
            outs.append(y)
        return jnp.concatenate(outs, axis=0)

    act = _silu(branch(wug_ref, wcg_ref, bg_ref, ug_ref, 0)) * branch(wuv_ref, wcv_ref, bv_ref, uv_ref, 1)
    part = _dot(act.astype(bf16), wd_ref[...])

    @pl.when(j == 0)
    def _():
        acc_ref[...] = part

    @pl.when(j > 0)
    def _():
        acc_ref[...] += part

    @pl.when(j == pl.num_programs(1) - 1)
    def _():
        _ffn_epilogue(acc_ref, x1_ref, gt_ref, gf_ref, y_ref)


def _ffn(h2, x1, ada, g_final, w_up, w_conv, b_conv, w_down, group, rows_per_batch, tm, tf, buf=None):
    m, d = h2.shape
    dff = w_down.shape[0]
    nj = dff // tf
    b2 = b_conv.reshape(1, 2 * dff)
    common_w = [pl.BlockSpec((d, tf), lambda i, j: (0, j)),
                pl.BlockSpec((d, tf), lambda i, j: (0, nj + j)),
                pl.BlockSpec((FFN_CONV, tf), lambda i, j: (0, j)),
                pl.BlockSpec((FFN_CONV, tf), lambda i, j: (0, nj + j)),
                pl.BlockSpec((1, tf), lambda i, j: (0, j)),
                pl.BlockSpec((1, tf), lambda i, j: (0, nj + j)),
                pl.BlockSpec((tf, d), lambda i, j: (j, 0))]
    tail = [pl.BlockSpec((tm, d), lambda i, j: (i, 0)),
            _mod_spec(group, d, rows_per_batch, tm, 5),
            pl.BlockSpec((1, d), lambda i, j: (0, 0))]
    y_spec = pl.BlockSpec((tm, d), lambda i, j: (i, 0))
    if group == "prompt":
        bps = rows_per_batch // tm
        nb = m // rows_per_batch
        kern = functools.partial(_ffn_prompt_kernel, blocks_per_seq=bps)
        first = [pl.BlockSpec((tm, d), lambda i, j: (i, 0)),
                 pl.BlockSpec((SUBLANES, d), lambda i, j: (jnp.maximum(i * (tm // SUBLANES) - 1, 0), 0))]
        ubuf_spec = pl.BlockSpec((None, SUBLANES, tf), lambda i, j: (i, 0, j))
        ubuf_shape = jax.ShapeDtypeStruct((m // tm, SUBLANES, dff), f32)
        args = (h2, h2)
    else:
        ts = m // rows_per_batch
        kern = functools.partial(_ffn_sample_kernel, ts=ts)
        first = [pl.BlockSpec((tm, d), lambda i, j: (i, 0)),
                 pl.BlockSpec((FFN_CONV - 1, rows_per_batch, 2 * tf), lambda i, j: (0, 0, j))]
        ubuf_spec = pl.BlockSpec((FFN_CONV - 1, rows_per_batch, tf), lambda i, j: (0, 0, j))
        ubuf_shape = jax.ShapeDtypeStruct((FFN_CONV - 1, rows_per_batch, dff), f32)
        args = (h2, buf)
    return pl.pallas_call(
        kern,
        grid=(m // tm, nj),
        in_specs=first + common_w + tail,
        out_specs=[y_spec, ubuf_spec, ubuf_spec],
        out_shape=[jax.ShapeDtypeStruct((m, d), f32), ubuf_shape, ubuf_shape],
        scratch_shapes=[pltpu.VMEM((tm, d), f32)],
        compiler_params=_params(56, 2),
        name="ffn_" + group,
    )(*args, w_up, w_up, w_conv, w_conv, b2, b2, w_down, x1, ada, g_final.reshape(1, d))


def _pack_w_in(w_in, d):
    sizes = (QKV_W, Z_W, GDN_HEADS, GDN_HEADS, DQ_W, DKV_W, DKV_W, IQ_W, IDX_DIM, IDX_HEADS, d, d)
    segs, off = [], 0
    for s in sizes:
        segs.append(w_in[:, off:off + s])
        off += s
    qkv, z, a, b, dq, dk, dv, iq, ik, iw, gg, gd = segs
    pad = jnp.zeros((w_in.shape[0], LANES - (IDX_DIM + 2 * GDN_HEADS + IDX_HEADS)), w_in.dtype)
    small = jnp.concatenate([ik, a, b, iw, pad], axis=1)
    return jnp.concatenate([qkv, z, dq, dk, dv, iq, gg, gd, small], axis=1).astype(bf16)


def _ffn_tile(dff):
    for tf in (512, 256, 128):
        if dff % tf == 0:
            return tf
    raise ValueError("d_ff must be a multiple of 128")


def _mm_tile(n):
    for tn in (1152, 1024, 768, 512, 384, 256, 128):
        if n % tn == 0:
            return tn
    raise ValueError("projection width must be a multiple of 128")


def kernel(x_prompt, x_sample, c_prompt, c_sample, cache_k, cache_v, cache_idx_k, page_table, state_gdn, state_gdn_conv, state_ffn_conv, w_ada, b_ada, g_norm1, w_in, w_gdn_conv, a_log, dt_bias, g_gdn_norm, w_gdn_out, w_dsa_out, w_o, g_norm2, w_up, w_ffn_conv, b_ffn_conv, w_down, g_final):
    nb, t, d = x_prompt.shape
    db, ts, _ = x_sample.shape
    depth = w_ada.shape[0]
    assert depth == 1 and db == LANES and OFF_GG % d == 0 and ts >= GDN_CONV - 1
    n_pages = page_table.shape[1]
    page = cache_k.shape[2]
    past = n_pages * page
    dff = w_down.shape[1]
    l = 0

    n_c = nb + db
    pad_c = (-n_c) % SUBLANES
    c_all = jnp.concatenate([c_prompt, c_sample, jnp.zeros((pad_c, d), f32)], axis=0)
    ada = _ada(c_all, w_ada[l], b_ada[l])
    ada_p = ada[:nb].reshape(nb, 1, 6 * d)
    ada_s = ada[nb:nb + db]

    w_in_p = _pack_w_in(w_in[l], d)
    wg = w_gdn_out[l].astype(bf16)
    wd = w_dsa_out[l].astype(bf16)
    wo = w_o[l].astype(bf16)
    wup = w_up[l].astype(bf16)
    wdn = w_down[l].astype(bf16)
    tn = _mm_tile(w_in_p.shape[1])
    tf = _ffn_tile(dff)

    xp = x_prompt.reshape(nb * t, d)
    tm_p = min(512, t)
    h1 = _prep(xp, g_norm1[l], ada_p, "prompt", t, tm_p)
    yp = _matmul(h1, w_in_p, tm_p, tn, "in_proj_prompt")
    og_p, s_p = _gdn_prompt(yp, w_gdn_conv[l], a_log[l], dt_bias[l], g_gdn_norm[l], nb, t)
    dq_p, dk_p, iq_p, ik_p, sm_p = _rope(yp, jnp.arange(t), tm_p, "prompt")
    od_p = _dsa_prompt(iq_p, sm_p, ik_p, dq_p, dk_p, yp, nb, t, min(DSA_TOPK, t // 4))
    x1_p, h2_p = _merge(og_p, od_p, yp, xp, ada_p, g_norm2[l], wg, wd, wo, "prompt", t, min(256, t))
    y_p, ug_p, uv_p = _ffn(h2_p, x1_p, ada_p, g_final, wup, w_ffn_conv[l], b_ffn_conv[l], wdn,
                           "prompt", t, tm_p, tf)

    yp3 = yp.reshape(nb, t, -1)
    nfb = FFN_CONV - 1
    bps_p = t // tm_p
    out_p = (
        y_p.reshape(nb, t, d),
        dk_p.reshape(1, nb, t, DSA_KV_HEADS, DSA_HEAD_DIM),
        yp3[:, :, OFF_DV:OFF_DV + DKV_W].reshape(1, nb, t, DSA_KV_HEADS, DSA_HEAD_DIM),
        ik_p.reshape(1, nb, t, IDX_DIM),
        s_p[None],
        yp3[:, t - (GDN_CONV - 1):, :QKV_W][None],
        jnp.concatenate([ug_p[bps_p - 1::bps_p, SUBLANES - nfb:], uv_p[bps_p - 1::bps_p, SUBLANES - nfb:]],
                        axis=-1)[None],
    )

    xs = x_sample.transpose(1, 0, 2).reshape(ts * db, d)
    h1s = _prep(xs, g_norm1[l], ada_s, "sample", db, db)
    ys = _matmul(h1s, w_in_p, ts * db, tn, "in_proj_sample")
    ys3 = ys.reshape(ts, db, -1)
    og_s, s_s = _gdn_sample(ys3, state_gdn_conv[l].transpose(1, 0, 2), state_gdn[l],
                            w_gdn_conv[l], a_log[l], dt_bias[l], g_gdn_norm[l])
    pos_s = jnp.repeat(past + jnp.arange(ts), db)
    dq_s, dk_s, iq_s, ik_s, sm_s = _rope(ys, pos_s, db, "sample", db)
    ik_s_b = ik_s.reshape(ts, db, IDX_DIM).transpose(1, 0, 2)
    lp = past + LANES
    tp = -(-ts // SUBLANES) * SUBLANES
    pad_t = lambda a: jnp.pad(a.astype(f32), ((0, 0), (0, tp - ts), (0, 0)))
    scores = _idx_scores(page_table, pad_t(iq_s.reshape(db, ts, IQ_W)), pad_t(sm_s.reshape(db, ts, LANES)),
                         pad_t(ik_s_b), cache_idx_k[l], lp, ts)
    sel = _select_sample(scores.reshape(db * ts, lp), min(DSA_TOPK, (past + ts) // 4), past, ts)
    dv_s_b = ys3[:, :, OFF_DV:OFF_DV + DKV_W].transpose(1, 0, 2)
    ck = cache_k[l].reshape(cache_k.shape[1], page * DSA_KV_HEADS, DSA_HEAD_DIM)
    cv = cache_v[l].reshape(cache_v.shape[1], page * DSA_KV_HEADS, DSA_HEAD_DIM)
    od_s = _dsa_sample(page_table, pad_t(dq_s.reshape(db, ts, DQ_W)), sel.reshape(db, ts, lp),
                       pad_t(dk_s.reshape(db, ts, DKV_W)), pad_t(dv_s_b), ck, cv)
    od_s = od_s.astype(bf16).transpose(1, 0, 2).reshape(ts * db, DQ_W)
    x1_s, h2_s = _merge(og_s.reshape(ts * db, Z_W), od_s, ys, xs, ada_s, g_norm2[l], wg, wd, wo,
                        "sample", db, db)
    fb = state_ffn_conv[l].transpose(1, 0, 2)
    nj = dff // tf
    fb = jnp.concatenate([fb[:, :, :dff].reshape(nfb, db, nj, tf), fb[:, :, dff:].reshape(nfb, db, nj, tf)],
                         axis=-1).reshape(nfb, db, 2 * dff)
    y_s, ug_s, uv_s = _ffn(h2_s, x1_s, ada_s, g_final, wup, w_ffn_conv[l], b_ffn_conv[l], wdn,
                           "sample", db, ts * db, tf, buf=fb)

    out_s = (
        y_s.reshape(ts, db, d).transpose(1, 0, 2),
        dk_s.reshape(1, db, ts, DSA_KV_HEADS, DSA_HEAD_DIM),
        dv_s_b.reshape(1, db, ts, DSA_KV_HEADS, DSA_HEAD_DIM),
        ik_s_b[None],
        s_s[None],
        ys3[ts - (GDN_CONV - 1):, :, :QKV_W].transpose(1, 0, 2)[None],
        jnp.concatenate([ug_s, uv_s], axis=-1).transpose(1, 0, 2)[None],
    )
    return (out_p[0], out_s[0]) + out_p[1:] + out_s[1:]
```

```python
import functools

import jax
import jax.numpy as jnp
from jax import lax
from jax.experimental import pallas as pl
from jax.experimental.pallas import tpu as pltpu

f32 = jnp.float32
bf16 = jnp.bfloat16

GDN_HEADS = 8
GDN_DK = 128
GDN_DV = 128
GDN_CONV = 4
GDN_CHUNK = 64
GDN_GROUP = 4
DSA_HEADS = 8
DSA_KV_HEADS = 2
DSA_HEAD_DIM = 128
IDX_HEADS = 8
IDX_DIM = 64
IDX_SCALE = IDX_HEADS ** -0.5 * IDX_DIM ** -0.5
DSA_TOPK = 256
ROPE_THETA = 500000.0
ROPE_FRACTION = 4
FFN_CONV = 3
NORM_EPS = 1e-6

LANES = 128
SUBLANES = 8
NEG = float(jnp.finfo(jnp.float32).min)
POS = float(jnp.finfo(jnp.float32).max)
BISECT_UNROLL = 4
SNAP_FROM = 2
BISECT_CAP = 1024
DSA_KEY_TILE = 256
DSA_KEY_EXTENT = 512
DSA_HEAD_PAIR = 2
FFN_DOWN_TILE = 512

QKV_W = 2 * GDN_HEADS * GDN_DK + GDN_HEADS * GDN_DV
Z_W = GDN_HEADS * GDN_DV
DQ_W = DSA_HEADS * DSA_HEAD_DIM
DKV_W = DSA_KV_HEADS * DSA_HEAD_DIM
IQ_W = IDX_HEADS * IDX_DIM
OFF_QKV = 0
OFF_Z = OFF_QKV + QKV_W
OFF_DQ = OFF_Z + Z_W
OFF_DK = OFF_DQ + DQ_W
OFF_DV = OFF_DK + DKV_W
OFF_IQ = OFF_DV + DKV_W
OFF_GG = OFF_IQ + IQ_W
SM_IK = 0
SM_A = IDX_DIM
SM_B = SM_A + GDN_HEADS
SM_IW = SM_B + GDN_HEADS


def _sigmoid(x):
    return 1.0 / (1.0 + jnp.exp(-x))


def _silu(x):
    return x * _sigmoid(x)


def _dot(a, b):
    return jnp.dot(a, b, preferred_element_type=f32)


def _dot_nt(a, b):
    return lax.dot_general(a, b, (((1,), (1,)), ((), ())), preferred_element_type=f32)


def _dot_tn(a, b):
    return lax.dot_general(a, b, (((0,), (0,)), ((), ())), preferred_element_type=f32)


def _split3(a):
    hi = a.astype(bf16)
    lo = (a - hi.astype(f32)).astype(bf16)
    return hi, lo


def _mm3(a, b):
    ah, al = a
    bh, bl = b
    return _dot(ah, bh) + (_dot(ah, bl) + _dot(al, bh))


def _dot_exact(a, b):
    return jnp.dot(a, b, preferred_element_type=f32, precision=lax.Precision.HIGHEST)


def _params(vmem_mb=None, n_axes=1):
    kw = dict(dimension_semantics=("arbitrary",) * n_axes)
    if vmem_mb is not None:
        kw["vmem_limit_bytes"] = vmem_mb * 1024 * 1024
    return pltpu.CompilerParams(**kw)


def _ada_kernel(c_ref, w_ref, b_ref, o_ref):
    s = _silu(c_ref[...]).astype(bf16)
    o_ref[...] = _dot(s, w_ref[...].astype(bf16)) + b_ref[...]


def _ada(c_all, w_ada, b_ada):
    m, d = c_all.shape
    n = w_ada.shape[1]
    tn = 1024
    return pl.pallas_call(
        _ada_kernel,
        grid=(n // tn,),
        in_specs=[pl.BlockSpec((m, d), lambda j: (0, 0)),
                  pl.BlockSpec((d, tn), lambda j: (0, j)),
                  pl.BlockSpec((1, tn), lambda j: (0, j))],
        out_specs=pl.BlockSpec((m, tn), lambda j: (0, j)),
        out_shape=jax.ShapeDtypeStruct((m, n), f32),
        compiler_params=_params(48),
        name="ada",
    )(c_all, w_ada, b_ada.reshape(1, n))


def _mod_spec(group, d, rows_per_batch, tm, col):
    if group == "prompt":
        return pl.BlockSpec((None, 1, d), lambda i, *_: ((i * tm) // rows_per_batch, 0, col))
    return pl.BlockSpec((rows_per_batch, d), lambda i, *_: (0, col))


def _prep_kernel(x_ref, g_ref, sc_ref, sh_ref, o_ref):
    x = x_ref[...]
    y = x * lax.rsqrt(jnp.mean(x * x, axis=-1, keepdims=True) + NORM_EPS)
    o_ref[...] = ((y * g_ref[...]) * (1.0 + sc_ref[...]) + sh_ref[...]).astype(bf16)


def _prep(x, g, ada, group, rows_per_batch, tm):
    m, d = x.shape
    return pl.pallas_call(
        _prep_kernel,
        grid=(m // tm,),
        in_specs=[pl.BlockSpec((tm, d), lambda i: (i, 0)),
                  pl.BlockSpec((1, d), lambda i: (0, 0)),
                  _mod_spec(group, d, rows_per_batch, tm, 1),
                  _mod_spec(group, d, rows_per_batch, tm, 0)],
        out_specs=pl.BlockSpec((tm, d), lambda i: (i, 0)),
        out_shape=jax.ShapeDtypeStruct((m, d), bf16),
        compiler_params=_params(),
        name="prep_" + group,
    )(x, g.reshape(1, d), ada, ada)


def _mm_kernel(a_ref, w_ref, o_ref):
    o_ref[...] = _dot(a_ref[...], w_ref[...])


def _matmul(a, w, tm, tn, name):
    m, k = a.shape
    n = w.shape[1]
    return pl.pallas_call(
        _mm_kernel,
        grid=(n // tn, m // tm),
        in_specs=[pl.BlockSpec((tm, k), lambda j, i: (i, 0)),
                  pl.BlockSpec((k, tn), lambda j, i: (0, j))],
        out_specs=pl.BlockSpec((tm, tn), lambda j, i: (i, j)),
        out_shape=jax.ShapeDtypeStruct((m, n), f32),
        compiler_params=_params(48, 2),
        name=name,
    )(a, w)


def _l2n(x):
    return x * lax.rsqrt(jnp.sum(x * x, axis=-1, keepdims=True) + NORM_EPS)


def _gdn_gates(sm, alog, dtb):
    xa = sm + dtb
    softplus = jnp.maximum(xa, 0.0) + jnp.log1p(jnp.exp(-jnp.abs(xa)))
    return -jnp.exp(alog) * softplus, _sigmoid(sm)


def _gated_norm(o, gn, z):
    y = o * lax.rsqrt(jnp.mean(o * o, axis=-1, keepdims=True) + NORM_EPS)
    return (y * gn) * _silu(z)


def _gdn_prompt_kernel(qkv_ref, halo_ref, z_ref, sm_ref, wc_ref, alog_ref, dtb_ref, gn_ref,
                       o_ref, sfin_ref, s_ref):
    c = pl.program_id(1)
    C = qkv_ref.shape[0]

    @pl.when(c == 0)
    def _():
        s_ref[...] = jnp.zeros_like(s_ref)

    keep = (c > 0).astype(f32)
    g_all, beta_all = _gdn_gates(sm_ref[...], alog_ref[...], dtb_ref[...])
    ri = lax.broadcasted_iota(jnp.int32, (C, C), 0)
    ci = lax.broadcasted_iota(jnp.int32, (C, C), 1)
    gc_all = _dot_exact((ri >= ci).astype(f32), g_all)
    gc_t = gc_all.T

    def conv(col):
        xs = jnp.concatenate([halo_ref[:, col:col + LANES] * keep, qkv_ref[:, col:col + LANES]], axis=0)
        w = wc_ref[:, col:col + LANES]
        y = xs[SUBLANES:] * w[GDN_CONV - 1:GDN_CONV]
        for i in range(GDN_CONV - 1):
            sft = GDN_CONV - 1 - i
            y = y + xs[SUBLANES - sft:SUBLANES - sft + C] * w[i:i + 1]
        return _silu(y)

    N = GDN_GROUP * C
    rn = lax.broadcasted_iota(jnp.int32, (N, N), 0)
    cn = lax.broadcasted_iota(jnp.int32, (N, N), 1)
    same = (rn // C) == (cn // C)
    incl = jnp.logical_and(same, rn >= cn)
    strict = jnp.logical_and(same, rn > cn)
    eye_f = (rn == cn).astype(f32)
    n_sq = max(1, (C - 1).bit_length() - 1)
    groups = [list(range(g0, g0 + GDN_GROUP)) for g0 in range(0, GDN_HEADS, GDN_GROUP)]
    stack = lambda xs: jnp.concatenate(xs, axis=0)

    qs = [stack([_l2n(conv(h * GDN_DK)) * (GDN_DK ** -0.5) for h in hs]) for hs in groups]
    ks = [stack([_l2n(conv(GDN_HEADS * GDN_DK + h * GDN_DK)) for h in hs]) for hs in groups]
    vs = [stack([conv(2 * GDN_HEADS * GDN_DK + h * GDN_DV) for h in hs]) for hs in groups]
    gcs = [stack([gc_all[:, SM_A + h:SM_A + h + 1] for h in hs]) for hs in groups]
    betas = [stack([beta_all[:, SM_B + h:SM_B + h + 1] for h in hs]) for hs in groups]
    gc_rows = [jnp.concatenate([gc_t[SM_A + h:SM_A + h + 1, :] for h in hs], axis=1) for hs in groups]
    egs = [jnp.exp(gc) for gc in gcs]
    decays = [jnp.where(incl, jnp.exp(jnp.where(incl, gc - gr, 0.0)), 0.0) for gc, gr in zip(gcs, gc_rows)]
    kbs = [k.astype(bf16) for k in ks]
    a_s = [jnp.where(strict, b * _dot_nt(kb, kb) * dc, 0.0) for b, kb, dc in zip(betas, kbs, decays)]
    qks = [(_dot_nt(q.astype(bf16), kb) * dc).astype(bf16) for q, kb, dc in zip(qs, kbs, decays)]

    invs = [eye_f - a for a in a_s]
    pws = [_split3(a) for a in a_s]
    for _ in range(n_sq):
        pws = [_split3(_mm3(p, p)) for p in pws]
        invs = [inv + _mm3(_split3(inv), p) for inv, p in zip(invs, pws)]
    rhs = [jnp.concatenate([b * v, (b * eg) * k], axis=1) for b, v, eg, k in zip(betas, vs, egs, ks)]
    sols = [_mm3(_split3(inv), _split3(r)) for inv, r in zip(invs, rhs)]

    for gi, hs in enumerate(groups):
        sol, q, k, gc, eg = sols[gi], qs[gi], ks[gi], gcs[gi], egs[gi]
        q_dec = (q * eg).astype(bf16)
        w_k = sol[:, GDN_DV:].astype(bf16)
        rows = [slice(j * C, (j + 1) * C) for j in range(GDN_GROUP)]
        s_old = [s_ref[h] for h in hs]
        ws = [_dot(jnp.concatenate([w_k[r], q_dec[r]], axis=0), s.astype(bf16)) for r, s in zip(rows, s_old)]
        u = stack([sol[r, :GDN_DV] - w[:C] for r, w in zip(rows, ws)])
        ub = u.astype(bf16)
        o_intra = _dot(qks[gi], ub)
        for j, h in enumerate(hs):
            r = rows[j]
            gl = gc[r][C - 1:C, :]
            k_end = (k[r] * jnp.exp(gl - gc[r])).astype(bf16)
            s_ref[h] = s_old[j] * jnp.exp(gl) + _dot_tn(k_end, ub[r])
            o = ws[j][C:] + o_intra[r]
            zh = z_ref[:, h * GDN_DV:(h + 1) * GDN_DV]
            o_ref[:, h * GDN_DV:(h + 1) * GDN_DV] = _gated_norm(o, gn_ref[...], zh).astype(bf16)

    @pl.when(c == pl.num_programs(1) - 1)
    def _():
        sfin_ref[...] = s_ref[...]


def _gdn_vecs(a_log, dt_bias, g_gdn_norm):
    alog = jnp.zeros((1, LANES), f32).at[0, SM_A:SM_A + GDN_HEADS].set(a_log)
    dtb = jnp.zeros((1, LANES), f32).at[0, SM_A:SM_A + GDN_HEADS].set(dt_bias)
    return alog, dtb, g_gdn_norm.reshape(1, GDN_DV)


def _gdn_prompt(y, w_conv, a_log, dt_bias, g_gdn_norm, nb, t):
    C = GDN_CHUNK
    nc = t // C
    small_blk = y.shape[1] // LANES - 1
    alog, dtb, gn = _gdn_vecs(a_log, dt_bias, g_gdn_norm)
    cst = lambda b, c: (0, 0)
    return pl.pallas_call(
        _gdn_prompt_kernel,
        grid=(nb, nc),
        in_specs=[pl.BlockSpec((C, QKV_W), lambda b, c: (b * nc + c, 0)),
                  pl.BlockSpec((SUBLANES, QKV_W),
                               lambda b, c: (jnp.maximum((b * nc + c) * (C // SUBLANES) - 1, 0), 0)),
                  pl.BlockSpec((C, Z_W), lambda b, c: (b * nc + c, OFF_Z // Z_W)),
                  pl.BlockSpec((C, LANES), lambda b, c: (b * nc + c, small_blk)),
                  pl.BlockSpec((GDN_CONV, QKV_W), cst),
                  pl.BlockSpec((1, LANES), cst),
                  pl.BlockSpec((1, LANES), cst),
                  pl.BlockSpec((1, GDN_DV), cst)],
        out_specs=[pl.BlockSpec((C, Z_W), lambda b, c: (b * nc + c, 0)),
                   pl.BlockSpec((None, GDN_HEADS, GDN_DK, GDN_DV), lambda b, c: (b, 0, 0, 0))],
        out_shape=[jax.ShapeDtypeStruct((nb * t, Z_W), bf16),
                   jax.ShapeDtypeStruct((nb, GDN_HEADS, GDN_DK, GDN_DV), f32)],
        scratch_shapes=[pltpu.VMEM((GDN_HEADS, GDN_DK, GDN_DV), f32)],
        compiler_params=_params(None, 2),
        name="gdn_prompt",
    )(y, y, y, y, w_conv, alog, dtb, gn)


def _gdn_sample_kernel(qkv_ref, buf_ref, z_ref, sm_ref, sin_ref, wc_ref, alog_ref, dtb_ref, gn_ref,
                       o_ref, sout_ref, q_s, k_s, v_s, a_s, b_s, o_s):
    ts, G, _ = qkv_ref.shape
    nbuf = GDN_CONV - 1

    for h in range(GDN_HEADS):
        for part, dst in ((0, q_s), (1, k_s), (2, v_s)):
            col = part * GDN_HEADS * GDN_DK + h * GDN_DK
            w = wc_ref[:, col:col + LANES]
            rows = [buf_ref[i, :, col:col + LANES] for i in range(nbuf)]
            rows += [qkv_ref[t, :, col:col + LANES] for t in range(ts)]
            for t in range(ts):
                y = rows[t] * w[0:1]
                for i in range(1, GDN_CONV):
                    y = y + rows[t + i] * w[i:i + 1]
                y = _silu(y)
                if part == 0:
                    y = _l2n(y) * (GDN_DK ** -0.5)
                elif part == 1:
                    y = _l2n(y)
                dst[t, :, h * LANES:(h + 1) * LANES] = y

    for t in range(ts):
        g_all, beta_all = _gdn_gates(sm_ref[t], alog_ref[...], dtb_ref[...])
        a_all = jnp.exp(g_all)
        for h in range(GDN_HEADS):
            a_s[t, :, h * LANES:(h + 1) * LANES] = jnp.broadcast_to(a_all[:, SM_A + h:SM_A + h + 1], (G, LANES))
            b_s[t, :, h * LANES:(h + 1) * LANES] = jnp.broadcast_to(beta_all[:, SM_B + h:SM_B + h + 1], (G, LANES))

    def head(h, carry):
        cols = pl.ds(pl.multiple_of(h * LANES, LANES), LANES)
        for i in range(G):
            row = slice(i, i + 1)
            kq = jnp.concatenate([k_s[t, row, cols] for t in range(ts)]
                                 + [q_s[t, row, cols] for t in range(ts)], axis=0)
            kq_t = kq.T
            s = sin_ref[i, h]
            for t in range(ts):
                kc = kq_t[:, t:t + 1]
                qc = kq_t[:, ts + t:ts + t + 1]
                a = a_s[t, row, cols]
                b = b_s[t, row, cols]
                ks = jnp.sum(s * kc, axis=0, keepdims=True)
                r = b * (v_s[t, row, cols] - a * ks)
                s = a * s + kc * r
                o_s[t, row, cols] = jnp.sum(s * qc, axis=0, keepdims=True)
            sout_ref[i, h] = s
        return carry

    lax.fori_loop(0, GDN_HEADS, head, 0)

    for t in range(ts):
        for h in range(GDN_HEADS):
            sl = slice(h * GDN_DV, (h + 1) * GDN_DV)
            o_ref[t, :, sl] = _gated_norm(o_s[t, :, sl], gn_ref[...], z_ref[t, :, sl]).astype(bf16)


def _gdn_sample(y3, buf3, state, w_conv, a_log, dt_bias, g_gdn_norm):
    ts, db, n = y3.shape
    G = SUBLANES
    small_blk = n // LANES - 1
    alog, dtb, gn = _gdn_vecs(a_log, dt_bias, g_gdn_norm)
    cst = lambda g: (0, 0)
    st_spec = pl.BlockSpec((G, GDN_HEADS, GDN_DK, GDN_DV), lambda g: (g, 0, 0, 0))
    scr = pltpu.VMEM((ts, G, Z_W), f32)
    return pl.pallas_call(
        _gdn_sample_kernel,
        grid=(db // G,),
        in_specs=[pl.BlockSpec((ts, G, QKV_W), lambda g: (0, g, 0)),
                  pl.BlockSpec((GDN_CONV - 1, G, QKV_W), lambda g: (0, g, 0)),
                  pl.BlockSpec((ts, G, Z_W), lambda g: (0, g, OFF_Z // Z_W)),
                  pl.BlockSpec((ts, G, LANES), lambda g: (0, g, small_blk)),
                  st_spec,
                  pl.BlockSpec((GDN_CONV, QKV_W), cst),
                  pl.BlockSpec((1, LANES), cst),
                  pl.BlockSpec((1, LANES), cst),
                  pl.BlockSpec((1, GDN_DV), cst)],
        out_specs=[pl.BlockSpec((ts, G, Z_W), lambda g: (0, g, 0)), st_spec],
        out_shape=[jax.ShapeDtypeStruct((ts, db, Z_W), bf16),
                   jax.ShapeDtypeStruct(state.shape, f32)],
        scratch_shapes=[scr, scr, scr, scr, scr, scr],
        compiler_params=_params(48),
        name="gdn_sample",
    )(y3, buf3, y3, y3, state, w_conv, alog, dtb, gn)


def _rope_tables(pos, rot, width):
    half = rot // 2
    inv_freq = ROPE_THETA ** (-jnp.arange(half, dtype=f32) * (2.0 / rot))
    ang = pos.astype(f32)[:, None] * inv_freq[None, :]
    cos, sin = jnp.cos(ang), jnp.sin(ang)
    n = pos.shape[0]
    z = lambda w: jnp.zeros((n, w), f32)
    cosw = jnp.concatenate([cos, cos, jnp.ones((n, width - rot), f32)], axis=1)
    sina = jnp.concatenate([-sin, z(width - half)], axis=1)
    sinb = jnp.concatenate([z(half), sin, z(width - rot)], axis=1)
    reps = LANES // width
    return tuple(jnp.tile(a, (1, reps)) for a in (cosw, sina, sinb))


def _rope_kernel(dq_ref, dk_ref, iq_ref, sm_ref, c1, sa1, sb1, c2, sa2, sb2,
                 dq_o, dk_o, iq_o, ik_o, sm_o):
    h1 = DSA_HEAD_DIM // ROPE_FRACTION // 2
    h2 = IDX_DIM // ROPE_FRACTION // 2

    def rot(x, c, sa, sb, half):
        return x * c[...] + pltpu.roll(x, LANES - half, 1) * sa[...] + pltpu.roll(x, half, 1) * sb[...]

    for j in range(DQ_W // LANES):
        sl = slice(j * LANES, (j + 1) * LANES)
        dq_o[:, sl] = rot(dq_ref[:, sl], c1, sa1, sb1, h1).astype(bf16)
    for j in range(DKV_W // LANES):
        sl = slice(j * LANES, (j + 1) * LANES)
        dk_o[:, sl] = rot(dk_ref[:, sl], c1, sa1, sb1, h1)
    for j in range(IQ_W // LANES):
        sl = slice(j * LANES, (j + 1) * LANES)
        iq_o[:, sl] = rot(iq_ref[:, sl], c2, sa2, sb2, h2).astype(bf16)
    sm = sm_ref[...]
    ik_o[...] = rot(sm, c2, sa2, sb2, h2)[:, :IDX_DIM]
    sm_o[...] = sm


def _rope(y, pos, tm, group, db=None):
    m, n = y.shape
    small_blk = n // LANES - 1
    t1 = _rope_tables(pos, DSA_HEAD_DIM // ROPE_FRACTION, DSA_HEAD_DIM)
    t2 = _rope_tables(pos, IDX_DIM // ROPE_FRACTION, IDX_DIM)
    tab_blocks = pos.shape[0] // tm
    tab = pl.BlockSpec((tm, LANES), lambda i: (i % tab_blocks, 0))
    if group == "prompt":
        omap = lambda i: (i, 0)
        rows = lambda w: m
        cols = lambda w: w
    else:
        ts = m // db
        omap = lambda i: (0, i)
        rows = lambda w: db
        cols = lambda w: ts * w
    out = lambda w, dt: jax.ShapeDtypeStruct((rows(w), cols(w)), dt)
    return pl.pallas_call(
        _rope_kernel,
        grid=(m // tm,),
        in_specs=[pl.BlockSpec((tm, DQ_W), lambda i: (i, OFF_DQ // DQ_W)),
                  pl.BlockSpec((tm, DKV_W), lambda i: (i, OFF_DK // DKV_W)),
                  pl.BlockSpec((tm, IQ_W), lambda i: (i, OFF_IQ // IQ_W)),
                  pl.BlockSpec((tm, LANES), lambda i: (i, small_blk)),
                  tab, tab, tab, tab, tab, tab],
        out_specs=[pl.BlockSpec((tm, DQ_W), omap),
                   pl.BlockSpec((tm, DKV_W), omap),
                   pl.BlockSpec((tm, IQ_W), omap),
                   pl.BlockSpec((tm, IDX_DIM), lambda i: (i, 0)),
                   pl.BlockSpec((tm, LANES), omap)],
        out_shape=[out(DQ_W, bf16), out(DKV_W, f32), out(IQ_W, bf16),
                   jax.ShapeDtypeStruct((m, IDX_DIM), f32), out(LANES, f32)],
        compiler_params=_params(),
        name="rope_" + group,
    )(y, y, y, y, *t1, *t2)


def _topk_bias(x_s, bias_s, lo0, mx, few, nkt, kt_w, k):
    R = x_s.shape[0]
    kf = float(k)
    tiles = [slice(kt * kt_w, (kt + 1) * kt_w) for kt in range(nkt)]
    n_grp = 2 if R % (2 * SUBLANES) == 0 else 1
    grps = [slice(g * (R // n_grp), (g + 1) * (R // n_grp)) for g in range(n_grp)]

    def count_ge(rows, th):
        acc = jnp.where(x_s[rows, tiles[0]] >= th, 1.0, 0.0)
        for t in tiles[1:]:
            acc = acc + jnp.where(x_s[rows, t] >= th, 1.0, 0.0)
        return jnp.sum(acc, axis=1, keepdims=True)

    def bisect(rows, lo, hi, done):
        mid = 0.5 * lo + 0.5 * hi
        cnt = count_ge(rows, mid)
        collapsed = jnp.logical_or(mid <= lo, mid >= hi)
        live = jnp.logical_and(done < 0.5, jnp.logical_not(collapsed))
        lo = jnp.where(jnp.logical_and(live, cnt >= kf), mid, lo)
        hi = jnp.where(jnp.logical_and(live, cnt <= kf), mid, hi)
        done = jnp.where(jnp.logical_or(collapsed, cnt == kf), 1.0, done)
        return lo, hi, done

    def snap(rows, lo, hi, done):
        a = b = None
        for t in tiles:
            x = x_s[rows, t]
            at = jnp.where(x >= lo, x, POS)
            bt = jnp.where(x < hi, x, NEG)
            a = at if a is None else jnp.minimum(a, at)
            b = bt if b is None else jnp.maximum(b, bt)
        a = jnp.min(a, axis=1, keepdims=True)
        b = jnp.max(b, axis=1, keepdims=True)
        live = done < 0.5
        return jnp.where(live, a, lo), jnp.where(jnp.logical_and(live, a >= b), 1.0, done)

    def cond(c):
        it, st = c
        left = st[0][2]
        for s in st[1:]:
            left = jnp.minimum(left, s[2])
        return jnp.logical_and(it < BISECT_CAP, jnp.min(left) < 0.5)

    def body(c):
        it, st = c
        for _ in range(BISECT_UNROLL):
            st = [bisect(rows, *s) for rows, s in zip(grps, st)]

        def snapped():
            out = []
            for rows, (lo, hi, done) in zip(grps, st):
                lo2, done2 = snap(rows, lo, hi, done)
                out.append((lo2, hi, done2))
            return out

        st = lax.cond(it >= SNAP_FROM, snapped, lambda: st)
        return it + 1, st

    hi0 = jnp.where(few, lo0, mx + (jnp.abs(mx) + 1.0))
    st0 = [(lo0[rows], hi0[rows], few[rows].astype(f32)) for rows in grps]
    _, st = lax.while_loop(cond, body, (jnp.int32(0), st0))
    lo = jnp.concatenate([s[0] for s in st], axis=0)
    hi = jnp.concatenate([s[1] for s in st], axis=0)

    has_run = jnp.max(jnp.where(lo < hi, 1.0, 0.0)) > 0.5
    rows_all = slice(0, R)

    @pl.when(jnp.logical_not(has_run))
    def _():
        for t in tiles:
            bias_s[:, t] = jnp.where(x_s[:, t] >= hi, 0.0, NEG)

    @pl.when(has_run)
    def _():
        need = kf - count_ge(rows_all, hi)
        ai = lax.broadcasted_iota(jnp.int32, (kt_w, kt_w), 0)
        bi = lax.broadcasted_iota(jnp.int32, (kt_w, kt_w), 1)
        before = (ai < bi).astype(bf16)
        seen = jnp.zeros((R, 1), f32)
        for t in tiles:
            x = x_s[:, t]
            run = jnp.logical_and(x >= lo, x < hi)
            runf = run.astype(f32)
            rank = seen + _dot(runf.astype(bf16), before)
            take = jnp.logical_or(x >= hi, jnp.logical_and(run, rank < need))
            bias_s[:, t] = jnp.where(take, 0.0, NEG)
            seen = seen + jnp.sum(runf, axis=1, keepdims=True)


def _dsa_prompt_kernel(iq_ref, sm_ref, ik_ref, q_ref, k_ref, v_ref, o_ref,
                       ikb, kb, vb, x_s, bias_s, s_s, *, topk, kt_w, ext_w):
    qi = pl.program_id(1)
    R = iq_ref.shape[0]
    T = ik_ref.shape[0]

    @pl.when(qi == 0)
    def _():
        ikb[...] = ik_ref[...].astype(bf16)
        kb[...] = k_ref[...].astype(bf16)
        vb[...] = v_ref[...].astype(bf16)

    def block(ext):
        tiles = [slice(j * kt_w, (j + 1) * kt_w) for j in range(ext // kt_w)]
        wgt = sm_ref[...] * IDX_SCALE
        qpos = qi * R + lax.broadcasted_iota(jnp.int32, (R, kt_w), 0)
        key0 = lax.broadcasted_iota(jnp.int32, (R, kt_w), 1)
        mn = mx = None
        for j, t in enumerate(tiles):
            keys = ikb[t, :]
            sc = jnp.zeros((R, kt_w), f32)
            for h in range(IDX_HEADS):
                rel = jnp.maximum(_dot_nt(iq_ref[:, h * IDX_DIM:(h + 1) * IDX_DIM], keys), 0.0)
                sc = sc + rel * wgt[:, SM_IW + h:SM_IW + h + 1]
            causal = key0 + j * kt_w <= qpos
            x_s[:, t] = jnp.where(causal, sc, NEG)
            lo_t = jnp.where(causal, sc, POS)
            hi_t = jnp.where(causal, sc, NEG)
            mn = lo_t if mn is None else jnp.minimum(mn, lo_t)
            mx = hi_t if mx is None else jnp.maximum(mx, hi_t)
        n_causal = qi * R + lax.broadcasted_iota(jnp.int32, (R, 1), 0) + 1
        _topk_bias(x_s, bias_s, jnp.min(mn, axis=1, keepdims=True), jnp.max(mx, axis=1, keepdims=True),
                   n_causal <= topk, len(tiles), kt_w, topk)

        def heads(hp, carry):
            h0 = hp * DSA_HEAD_PAIR
            g = h0 // (DSA_HEADS // DSA_KV_HEADS)
            gc = pl.ds(pl.multiple_of(g * DSA_HEAD_DIM, DSA_HEAD_DIM), DSA_HEAD_DIM)
            hcs = [pl.ds(pl.multiple_of((h0 + i) * DSA_HEAD_DIM, DSA_HEAD_DIM), DSA_HEAD_DIM)
                   for i in range(DSA_HEAD_PAIR)]
            qs = [q_ref[:, hc] for hc in hcs]
            mxa = [None] * DSA_HEAD_PAIR
            for t in tiles:
                kt = kb[t, gc]
                bt = bias_s[:, t]
                for i in range(DSA_HEAD_PAIR):
                    s = _dot_nt(qs[i], kt) * (DSA_HEAD_DIM ** -0.5) + bt
                    s_s[i, :, t] = s
                    mxa[i] = s if mxa[i] is None else jnp.maximum(mxa[i], s)
            ms = [jnp.max(a, axis=1, keepdims=True) for a in mxa]
            la = [jnp.zeros((R, kt_w), f32)] * DSA_HEAD_PAIR
            acc = [jnp.zeros((R, DSA_HEAD_DIM), f32)] * DSA_HEAD_PAIR
            for t in tiles:
                vt = vb[t, gc]
                for i in range(DSA_HEAD_PAIR):
                    p = jnp.exp(s_s[i, :, t] - ms[i])
                    la[i] = la[i] + p
                    acc[i] = acc[i] + _dot(p.astype(bf16), vt)
            for i in range(DSA_HEAD_PAIR):
                o_ref[:, hcs[i]] = (acc[i] / jnp.sum(la[i], axis=1, keepdims=True)).astype(bf16)
            return carry

        lax.fori_loop(0, DSA_HEADS // DSA_HEAD_PAIR, heads, 0)

    for e in range(T // ext_w):
        pl.when(qi // (ext_w // R) == e)(functools.partial(block, (e + 1) * ext_w))


def _dsa_prompt(iq, sm, ik, dq, dk, y, nb, t, topk):
    R = LANES
    nq = t // R
    kt_w = min(DSA_KEY_TILE, t)
    ext_w = min(DSA_KEY_EXTENT, t)
    kern = functools.partial(_dsa_prompt_kernel, topk=topk, kt_w=kt_w, ext_w=ext_w)
    return pl.pallas_call(
        kern,
        grid=(nb, nq),
        in_specs=[pl.BlockSpec((R, IQ_W), lambda b, i: (b * nq + i, 0)),
                  pl.BlockSpec((R, LANES), lambda b, i: (b * nq + i, 0)),
                  pl.BlockSpec((t, IDX_DIM), lambda b, i: (b, 0)),
                  pl.BlockSpec((R, DQ_W), lambda b, i: (b * nq + i, 0)),
                  pl.BlockSpec((t, DKV_W), lambda b, i: (b, 0)),
                  pl.BlockSpec((t, DKV_W), lambda b, i: (b, OFF_DV // DKV_W))],
        out_specs=pl.BlockSpec((R, DQ_W), lambda b, i: (b * nq + i, 0)),
        out_shape=jax.ShapeDtypeStruct((nb * t, DQ_W), bf16),
        scratch_shapes=[pltpu.VMEM((t, IDX_DIM), bf16), pltpu.VMEM((t, DKV_W), bf16),
                        pltpu.VMEM((t, DKV_W), bf16), pltpu.VMEM((R, t), f32), pltpu.VMEM((R, t), f32),
                        pltpu.VMEM((DSA_HEAD_PAIR, R, t), f32)],
        compiler_params=_params(48, 2),
        name="dsa_prompt",
    )(iq, sm, ik, dq, dk, y)


def _idx_score_kernel(pt_ref, iq_ref, sm_ref, ikn_ref, *rest, n_pages, page):
    pages = rest[:n_pages]
    o_ref = rest[n_pages]
    tp = iq_ref.shape[0]
    ts = o_ref.shape[0]
    past = n_pages * page
    lp = o_ref.shape[1]
    keys_t = jnp.concatenate([p[...] for p in pages], axis=1).astype(bf16)
    keys_n = jnp.concatenate([ikn_ref[...], jnp.zeros((lp - past - tp, IDX_DIM), f32)], axis=0).astype(bf16)
    wgt = sm_ref[...] * IDX_SCALE
    iq = iq_ref[...].astype(bf16)
    score = jnp.zeros((tp, lp), f32)
    for h in range(IDX_HEADS):
        iqh = iq[:, h * IDX_DIM:(h + 1) * IDX_DIM]
        rel = jnp.maximum(jnp.concatenate([_dot(iqh, keys_t), _dot_nt(iqh, keys_n)], axis=1), 0.0)
        score = score + rel * wgt[:, SM_IW + h:SM_IW + h + 1]
    o_ref[...] = score[:ts]


def _idx_scores(page_table, iq, sm, ikn, cache_idx, lp, ts):
    db, tp, _ = iq.shape
    n_pages = page_table.shape[1]
    page = cache_idx.shape[2]
    kern = functools.partial(_idx_score_kernel, n_pages=n_pages, page=page)
    page_specs = [pl.BlockSpec((None, IDX_DIM, page), functools.partial(lambda b, pt, p: (pt[b, p], 0, 0), p=p))
                  for p in range(n_pages)]
    grid_spec = pltpu.PrefetchScalarGridSpec(
        num_scalar_prefetch=1,
        grid=(db,),
        in_specs=[pl.BlockSpec((None, tp, IQ_W), lambda b, pt: (b, 0, 0)),
                  pl.BlockSpec((None, tp, LANES), lambda b, pt: (b, 0, 0)),
                  pl.BlockSpec((None, tp, IDX_DIM), lambda b, pt: (b, 0, 0))] + page_specs,
        out_specs=pl.BlockSpec((None, ts, lp), lambda b, pt: (b, 0, 0)),
    )
    return pl.pallas_call(
        kern, grid_spec=grid_spec,
        out_shape=jax.ShapeDtypeStruct((db, ts, lp), f32),
        compiler_params=_params(),
        name="idx_scores_sample",
    )(page_table, iq, sm, ikn, *([cache_idx] * n_pages))


def _select_sample_kernel(x_ref, bias_ref, x_s, *, topk, past, ts):
    R, lp = x_ref.shape
    key = lax.broadcasted_iota(jnp.int32, (R, lp), 1)
    t = lax.broadcasted_iota(jnp.int32, (R, lp), 0) % ts
    causal = key <= past + t
    x = x_ref[...]
    x_s[...] = jnp.where(causal, x, NEG)
    lo0 = jnp.min(jnp.where(causal, x, POS), axis=1, keepdims=True)
    mx = jnp.max(jnp.where(causal, x, NEG), axis=1, keepdims=True)
    n_causal = past + lax.broadcasted_iota(jnp.int32, (R, 1), 0) % ts + 1
    _topk_bias(x_s, bias_ref, lo0, mx, n_causal <= topk, lp // LANES, LANES, topk)


def _select_sample(scores, topk, past, ts):
    m, lp = scores.shape
    R = LANES
    kern = functools.partial(_select_sample_kernel, topk=topk, past=past, ts=ts)
    return pl.pallas_call(
        kern, grid=(m // R,),
        in_specs=[pl.BlockSpec((R, lp), lambda i: (i, 0))],
        out_specs=pl.BlockSpec((R, lp), lambda i: (i, 0)),
        out_shape=jax.ShapeDtypeStruct((m, lp), f32),
        scratch_shapes=[pltpu.VMEM((R, lp), f32)],
        compiler_params=_params(),
        name="select_sample",
    )(scores)


def _dsa_sample_kernel(pt_ref, q_ref, sel_ref, kn_ref, vn_ref, *rest, n_pages, page):
    kp = rest[:n_pages]
    vp = rest[n_pages:2 * n_pages]
    o_ref = rest[2 * n_pages]
    tp = q_ref.shape[0]
    ts = sel_ref.shape[0]
    past = n_pages * page
    lp = sel_ref.shape[1]
    hpg = DSA_HEADS // DSA_KV_HEADS
    bias = jnp.concatenate([sel_ref[...], jnp.zeros((tp - ts, lp), f32)], axis=0)
    bias = jnp.concatenate([bias] * hpg, axis=0)
    padn = jnp.zeros((lp - past - tp, DSA_HEAD_DIM), f32)
    for g in range(DSA_KV_HEADS):
        gsl = slice(g * DSA_HEAD_DIM, (g + 1) * DSA_HEAD_DIM)
        kg = jnp.concatenate([p[pl.ds(g, page, stride=DSA_KV_HEADS), :] for p in kp]
                             + [kn_ref[:, gsl], padn], axis=0).astype(bf16)
        vg = jnp.concatenate([p[pl.ds(g, page, stride=DSA_KV_HEADS), :] for p in vp]
                             + [vn_ref[:, gsl], padn], axis=0).astype(bf16)
        qg = jnp.concatenate([q_ref[:, (g * hpg + j) * DSA_HEAD_DIM:(g * hpg + j + 1) * DSA_HEAD_DIM]
                              for j in range(hpg)], axis=0).astype(bf16)
        s = _dot_nt(qg, kg) * (DSA_HEAD_DIM ** -0.5) + bias
        m = jnp.max(s, axis=1, keepdims=True)
        p = jnp.exp(s - m)
        l = jnp.sum(p, axis=1, keepdims=True)
        o = _dot(p.astype(bf16), vg) / l
        for j in range(hpg):
            hsl = slice((g * hpg + j) * DSA_HEAD_DIM, (g * hpg + j + 1) * DSA_HEAD_DIM)
            o_ref[:, hsl] = o[j * tp:j * tp + ts]


def _dsa_sample(page_table, dq, sel, kn, vn, ck, cv):
    db, tp, _ = dq.shape
    ts = sel.shape[1]
    lp = sel.shape[2]
    n_pages = page_table.shape[1]
    rows = ck.shape[1]
    page = rows // DSA_KV_HEADS
    kern = functools.partial(_dsa_sample_kernel, n_pages=n_pages, page=page)
    pspec = [pl.BlockSpec((None, rows, DSA_HEAD_DIM), functools.partial(lambda b, pt, p: (pt[b, p], 0, 0), p=p))
             for p in range(n_pages)]
    per_b = lambda r, w: pl.BlockSpec((None, r, w), lambda b, pt: (b, 0, 0))
    grid_spec = pltpu.PrefetchScalarGridSpec(
        num_scalar_prefetch=1,
        grid=(db,),
        in_specs=[per_b(tp, DQ_W), per_b(ts, lp), per_b(tp, DKV_W), per_b(tp, DKV_W)] + pspec + pspec,
        out_specs=per_b(ts, DQ_W),
    )
    return pl.pallas_call(
        kern, grid_spec=grid_spec,
        out_shape=jax.ShapeDtypeStruct((db, ts, DQ_W), f32),
        compiler_params=_params(48),
        name="dsa_sample",
    )(page_table, dq, sel, kn, vn, *([ck] * n_pages), *([cv] * n_pages))


def _merge_kernel(og_ref, od_ref, gg_ref, gd_ref, x_ref, gt_ref, sc_ref, sh_ref, g2_ref,
                  wg_ref, wd_ref, wo_ref, x1_ref, h2_ref):
    mix = (_sigmoid(gg_ref[...]) * _dot(og_ref[...], wg_ref[...])
           + _sigmoid(gd_ref[...]) * _dot(od_ref[...], wd_ref[...]))
    x1 = x_ref[...] + gt_ref[...] * _dot(mix.astype(bf16), wo_ref[...])
    x1_ref[...] = x1
    y = x1 * lax.rsqrt(jnp.mean(x1 * x1, axis=-1, keepdims=True) + NORM_EPS)
    h2_ref[...] = ((y * g2_ref[...]) * (1.0 + sc_ref[...]) + sh_ref[...]).astype(bf16)


def _merge(o_gdn, o_dsa, y, x, ada, g2, wg, wd, wo, group, rows_per_batch, tm):
    m, d = x.shape
    row = lambda w, blk=0: pl.BlockSpec((tm, w), lambda i: (i, blk))
    res = lambda a: pl.BlockSpec(a.shape, lambda i: (0, 0), pipeline_mode=pl.Buffered(1))
    return pl.pallas_call(
        _merge_kernel,
        grid=(m // tm,),
        in_specs=[row(Z_W), row(DQ_W), row(d, OFF_GG // d), row(d, OFF_GG // d + 1), row(d),
                  _mod_spec(group, d, rows_per_batch, tm, 2),
                  _mod_spec(group, d, rows_per_batch, tm, 4),
                  _mod_spec(group, d, rows_per_batch, tm, 3),
                  pl.BlockSpec((1, d), lambda i: (0, 0)),
                  res(wg), res(wd), res(wo)],
        out_specs=[row(d), row(d)],
        out_shape=[jax.ShapeDtypeStruct((m, d), f32), jax.ShapeDtypeStruct((m, d), bf16)],
        compiler_params=_params(48),
        name="merge_" + group,
    )(o_gdn, o_dsa, y, y, x, ada, ada, ada, g2.reshape(1, d), wg, wd, wo)


def _ffn_epilogue(acc_ref, x1_ref, gt_ref, gf_ref, y_ref):
    tm = acc_ref.shape[0]
    r = gt_ref.shape[0] if gt_ref.shape[0] > 1 else tm
    for s0 in range(0, tm, r):
        sl = slice(s0, s0 + r)
        x2 = x1_ref[sl, :] + gt_ref[...] * acc_ref[sl, :]
        y_ref[sl, :] = (x2 * lax.rsqrt(jnp.mean(x2 * x2, axis=-1, keepdims=True) + NORM_EPS)) * gf_ref[...]


def _ffn_prompt_kernel(h_ref, halo_ref, wug_ref, wuv_ref, wcg_ref, wcv_ref, bg_ref, bv_ref, wd_ref,
                       x1_ref, gt_ref, gf_ref, y_ref, ug_ref, uv_ref, acc_ref, act_s, *, blocks_per_seq, n_up):
    i = pl.program_id(0)
    j = pl.program_id(1)
    tm = h_ref.shape[0]
    keep = ((i % blocks_per_seq) > 0).astype(f32)

    def branch(wu_ref, wc_ref, b_ref, ubuf_ref):
        u = _dot(h_ref[...], wu_ref[...])
        uh = _dot(halo_ref[...], wu_ref[...]) * keep
        ubuf_ref[...] = u[tm - SUBLANES:]
        ux = jnp.concatenate([uh, u], axis=0)
        w = wc_ref[...]
        y = u * w[FFN_CONV - 1:FFN_CONV] + b_ref[...]
        for t in range(FFN_CONV - 1):
            sft = FFN_CONV - 1 - t
            y = y + ux[SUBLANES - sft:SUBLANES - sft + tm] * w[t:t + 1]
        return y

    def up():
        return _silu(branch(wug_ref, wcg_ref, bg_ref, ug_ref)) * branch(wuv_ref, wcv_ref, bv_ref, uv_ref)

    _ffn_phases(j, n_up, up, act_s, wd_ref, acc_ref, x1_ref, gt_ref, gf_ref, y_ref)


def _ffn_phases(j, n_up, up, act_s, wd_ref, acc_ref, x1_ref, gt_ref, gf_ref, y_ref):
    tf = act_s.shape[1] // n_up
    tn = wd_ref.shape[1]

    @pl.when(j < n_up)
    def _():
        act_s[:, pl.ds(pl.multiple_of(j * tf, tf), tf)] = up().astype(bf16)

    @pl.when(j >= n_up)
    def _():
        acc_ref[:, pl.ds(pl.multiple_of((j - n_up) * tn, tn), tn)] = _dot(act_s[...], wd_ref[...])

    @pl.when(j == pl.num_programs(1) - 1)
    def _():
        _ffn_epilogue(acc_ref, x1_ref, gt_ref, gf_ref, y_ref)


def _ffn_sample_kernel(h_ref, buf_ref, wug_ref, wuv_ref, wcg_ref, wcv_ref, bg_ref, bv_ref, wd_ref,
                       x1_ref, gt_ref, gf_ref, y_ref, ug_ref, uv_ref, acc_ref, act_s, *, ts, n_up):
    j = pl.program_id(1)
    db = h_ref.shape[0] // ts
    nbuf = FFN_CONV - 1
    tf = wug_ref.shape[1]

    def branch(wu_ref, wc_ref, b_ref, ubuf_ref, half):
        u = _dot(h_ref[...], wu_ref[...])
        rows = [buf_ref[r, :, half * tf:(half + 1) * tf] for r in range(nbuf)]
        rows += [u[t * db:(t + 1) * db] for t in range(ts)]
        for r in range(nbuf):
            ubuf_ref[r] = rows[ts + r]
        w = wc_ref[...]
        outs = []
        for t in range(ts):
            y = rows[t] * w[0:1] + b_ref[...]
            for r in range(1, FFN_CONV):
                y = y + rows[t + r] * w[r:r + 1]
            outs.append(y)
        return jnp.concatenate(outs, axis=0)

    def up():
        return _silu(branch(wug_ref, wcg_ref, bg_ref, ug_ref, 0)) * branch(wuv_ref, wcv_ref, bv_ref, uv_ref, 1)

    _ffn_phases(j, n_up, up, act_s, wd_ref, acc_ref, x1_ref, gt_ref, gf_ref, y_ref)


def _ffn(h2, x1, ada, g_final, w_up, w_conv, b_conv, w_down, group, rows_per_batch, tm, tf, buf=None):
    m, d = h2.shape
    dff = w_down.shape[0]
    nj = dff // tf
    tn = min(FFN_DOWN_TILE, d)
    b2 = b_conv.reshape(1, 2 * dff)
    up = lambda j: jnp.minimum(j, nj - 1)
    common_w = [pl.BlockSpec((d, tf), lambda i, j: (0, up(j))),
                pl.BlockSpec((d, tf), lambda i, j: (0, nj + up(j))),
                pl.BlockSpec((FFN_CONV, tf), lambda i, j: (0, up(j))),
                pl.BlockSpec((FFN_CONV, tf), lambda i, j: (0, nj + up(j))),
                pl.BlockSpec((1, tf), lambda i, j: (0, up(j))),
                pl.BlockSpec((1, tf), lambda i, j: (0, nj + up(j))),
                pl.BlockSpec((dff, tn), lambda i, j: (0, jnp.maximum(j - nj, 0)))]
    tail = [pl.BlockSpec((tm, d), lambda i, j: (i, 0)),
            _mod_spec(group, d, rows_per_batch, tm, 5),
            pl.BlockSpec((1, d), lambda i, j: (0, 0))]
    y_spec = pl.BlockSpec((tm, d), lambda i, j: (i, 0))
    if group == "prompt":
        bps = rows_per_batch // tm
        nb = m // rows_per_batch
        kern = functools.partial(_ffn_prompt_kernel, blocks_per_seq=bps, n_up=nj)
        first = [pl.BlockSpec((tm, d), lambda i, j: (i, 0)),
                 pl.BlockSpec((SUBLANES, d), lambda i, j: (jnp.maximum(i * (tm // SUBLANES) - 1, 0), 0))]
        ubuf_spec = pl.BlockSpec((None, SUBLANES, tf), lambda i, j: (i, 0, up(j)))
        ubuf_shape = jax.ShapeDtypeStruct((m // tm, SUBLANES, dff), f32)
        args = (h2, h2)
    else:
        ts = m // rows_per_batch
        kern = functools.partial(_ffn_sample_kernel, ts=ts, n_up=nj)
        first = [pl.BlockSpec((tm, d), lambda i, j: (i, 0)),
                 pl.BlockSpec((FFN_CONV - 1, rows_per_batch, 2 * tf), lambda i, j: (0, 0, up(j)))]
        ubuf_spec = pl.BlockSpec((FFN_CONV - 1, rows_per_batch, tf), lambda i, j: (0, 0, up(j)))
        ubuf_shape = jax.ShapeDtypeStruct((FFN_CONV - 1, rows_per_batch, dff), f32)
        args = (h2, buf)
    return pl.pallas_call(
        kern,
        grid=(m // tm, nj + d // tn),
        in_specs=first + common_w + tail,
        out_specs=[y_spec, ubuf_spec, ubuf_spec],
        out_shape=[jax.ShapeDtypeStruct((m, d), f32), ubuf_shape, ubuf_shape],
        scratch_shapes=[pltpu.VMEM((tm, d), f32), pltpu.VMEM((tm, dff), bf16)],
        compiler_params=_params(56, 2),
        name="ffn_" + group,
    )(*args, w_up, w_up, w_conv, w_conv, b2, b2, w_down, x1, ada, g_final.reshape(1, d))


def _pack_w_in(w_in, d):
    sizes = (QKV_W, Z_W, GDN_HEADS, GDN_HEADS, DQ_W, DKV_W, DKV_W, IQ_W, IDX_DIM, IDX_HEADS, d, d)
    segs, off = [], 0
    for s in sizes:
        segs.append(w_in[:, off:off + s])
        off += s
    qkv, z, a, b, dq, dk, dv, iq, ik, iw, gg, gd = segs
    pad = jnp.zeros((w_in.shape[0], LANES - (IDX_DIM + 2 * GDN_HEADS + IDX_HEADS)), w_in.dtype)
    small = jnp.concatenate([ik, a, b, iw, pad], axis=1)
    return jnp.concatenate([qkv, z, dq, dk, dv, iq, gg, gd, small], axis=1).astype(bf16)


def _ffn_tile(dff):
    for tf in (512, 256, 128):
        if dff % tf == 0:
            return tf
    raise ValueError("d_ff must be a multiple of 128")


def _mm_tile(n):
    for tn in (1152, 1024, 768, 512, 384, 256, 128):
        if n % tn == 0:
            return tn
    raise ValueError("projection width must be a multiple of 128")


def kernel(x_prompt, x_sample, c_prompt, c_sample, cache_k, cache_v, cache_idx_k, page_table, state_gdn, state_gdn_conv, state_ffn_conv, w_ada, b_ada, g_norm1, w_in, w_gdn_conv, a_log, dt_bias, g_gdn_norm, w_gdn_out, w_dsa_out, w_o, g_norm2, w_up, w_ffn_conv, b_ffn_conv, w_down, g_final):
    nb, t, d = x_prompt.shape
    db, ts, _ = x_sample.shape
    depth = w_ada.shape[0]
    assert depth == 1 and db == LANES and OFF_GG % d == 0 and ts >= GDN_CONV - 1
    n_pages = page_table.shape[1]
    page = cache_k.shape[2]
    past = n_pages * page
    dff = w_down.shape[1]
    l = 0

    n_c = nb + db
    pad_c = (-n_c) % SUBLANES
    c_all = jnp.concatenate([c_prompt, c_sample, jnp.zeros((pad_c, d), f32)], axis=0)
    ada = _ada(c_all, w_ada[l], b_ada[l])
    ada_p = ada[:nb].reshape(nb, 1, 6 * d)
    ada_s = ada[nb:nb + db]

    w_in_p = _pack_w_in(w_in[l], d)
    wg = w_gdn_out[l].astype(bf16)
    wd = w_dsa_out[l].astype(bf16)
    wo = w_o[l].astype(bf16)
    wup = w_up[l].astype(bf16)
    wdn = w_down[l].astype(bf16)
    tn = _mm_tile(w_in_p.shape[1])
    tf = _ffn_tile(dff)

    xp = x_prompt.reshape(nb * t, d)
    tm_p = min(512, t)
    h1 = _prep(xp, g_norm1[l], ada_p, "prompt", t, tm_p)
    yp = _matmul(h1, w_in_p, tm_p, tn, "in_proj_prompt")
    og_p, s_p = _gdn_prompt(yp, w_gdn_conv[l], a_log[l], dt_bias[l], g_gdn_norm[l], nb, t)
    dq_p, dk_p, iq_p, ik_p, sm_p = _rope(yp, jnp.arange(t), tm_p, "prompt")
    od_p = _dsa_prompt(iq_p, sm_p, ik_p, dq_p, dk_p, yp, nb, t, min(DSA_TOPK, t // 4))
    x1_p, h2_p = _merge(og_p, od_p, yp, xp, ada_p, g_norm2[l], wg, wd, wo, "prompt", t, min(256, t))
    y_p, ug_p, uv_p = _ffn(h2_p, x1_p, ada_p, g_final, wup, w_ffn_conv[l], b_ffn_conv[l], wdn,
                           "prompt", t, tm_p, tf)

    yp3 = yp.reshape(nb, t, -1)
    nfb = FFN_CONV - 1
    bps_p = t // tm_p
    out_p = (
        y_p.reshape(nb, t, d),
        dk_p.reshape(1, nb, t, DSA_KV_HEADS, DSA_HEAD_DIM),
        yp3[:, :, OFF_DV:OFF_DV + DKV_W].reshape(1, nb, t, DSA_KV_HEADS, DSA_HEAD_DIM),
        ik_p.reshape(1, nb, t, IDX_DIM),
        s_p[None],
        yp3[:, t - (GDN_CONV - 1):, :QKV_W][None],
        jnp.concatenate([ug_p[bps_p - 1::bps_p, SUBLANES - nfb:], uv_p[bps_p - 1::bps_p, SUBLANES - nfb:]],
                        axis=-1)[None],
    )

    xs = x_sample.transpose(1, 0, 2).reshape(ts * db, d)
    h1s = _prep(xs, g_norm1[l], ada_s, "sample", db, db)
    ys = _matmul(h1s, w_in_p, ts * db, tn, "in_proj_sample")
    ys3 = ys.reshape(ts, db, -1)
    og_s, s_s = _gdn_sample(ys3, state_gdn_conv[l].transpose(1, 0, 2), state_gdn[l],
                            w_gdn_conv[l], a_log[l], dt_bias[l], g_gdn_norm[l])
    pos_s = jnp.repeat(past + jnp.arange(ts), db)
    dq_s, dk_s, iq_s, ik_s, sm_s = _rope(ys, pos_s, db, "sample", db)
    ik_s_b = ik_s.reshape(ts, db, IDX_DIM).transpose(1, 0, 2)
    lp = past + LANES
    tp = -(-ts // SUBLANES) * SUBLANES
    pad_t = lambda a: jnp.pad(a.astype(f32), ((0, 0), (0, tp - ts), (0, 0)))
    scores = _idx_scores(page_table, pad_t(iq_s.reshape(db, ts, IQ_W)), pad_t(sm_s.reshape(db, ts, LANES)),
                         pad_t(ik_s_b), jnp.swapaxes(cache_idx_k[l], 1, 2), lp, ts)
    sel = _select_sample(scores.reshape(db * ts, lp), min(DSA_TOPK, (past + ts) // 4), past, ts)
    dv_s_b = ys3[:, :, OFF_DV:OFF_DV + DKV_W].transpose(1, 0, 2)
    ck = cache_k[l].reshape(cache_k.shape[1], page * DSA_KV_HEADS, DSA_HEAD_DIM)
    cv = cache_v[l].reshape(cache_v.shape[1], page * DSA_KV_HEADS, DSA_HEAD_DIM)
    od_s = _dsa_sample(page_table, pad_t(dq_s.reshape(db, ts, DQ_W)), sel.reshape(db, ts, lp),
                       pad_t(dk_s.reshape(db, ts, DKV_W)), pad_t(dv_s_b), ck, cv)
    od_s = od_s.astype(bf16).transpose(1, 0, 2).reshape(ts * db, DQ_W)
    x1_s, h2_s = _merge(og_s.reshape(ts * db, Z_W), od_s, ys, xs, ada_s, g_norm2[l], wg, wd, wo,
                        "sample", db, db)
    fb = state_ffn_conv[l].transpose(1, 0, 2)
    nj = dff // tf
    fb = jnp.concatenate([fb[:, :, :dff].reshape(nfb, db, nj, tf), fb[:, :, dff:].reshape(nfb, db, nj, tf)],
                         axis=-1).reshape(nfb, db, 2 * dff)
    y_s, ug_s, uv_s = _ffn(h2_s, x1_s, ada_s, g_final, wup, w_ffn_conv[l], b_ffn_conv[l], wdn,
                           "sample", db, ts * db, tf, buf=fb)

    out_s = (
        y_s.reshape(ts, db, d).transpose(1, 0, 2),
        dk_s.reshape(1, db, ts, DSA_KV_HEADS, DSA_HEAD_DIM),
        dv_s_b.reshape(1, db, ts, DSA_KV_HEADS, DSA_HEAD_DIM),
        ik_s_b[None],
        s_s[None],
        ys3[ts - (GDN_CONV - 1):, :, :QKV_W].transpose(1, 0, 2)[None],
        jnp.concatenate([ug_s, uv_s], axis=-1).transpose(1, 0, 2)[None],
    )
    return (out_p[0], out_s[0]) + out_p[1:] + out_s[1:]
```

```python
import functools

import jax
import jax.numpy as jnp
from jax import lax
from jax.experimental import pallas as pl
from jax.experimental.pallas import tpu as pltpu

f32 = jnp.float32
bf16 = jnp.bfloat16

GDN_HEADS = 8
GDN_DK = 128
GDN_DV = 128
GDN_CONV = 4
GDN_CHUNK = 64
GDN_GROUP = 4
DSA_HEADS = 8
DSA_KV_HEADS = 2
DSA_HEAD_DIM = 128
IDX_HEADS = 8
IDX_DIM = 64
IDX_SCALE = IDX_HEADS ** -0.5 * IDX_DIM ** -0.5
DSA_TOPK = 256
ROPE_THETA = 500000.0
ROPE_FRACTION = 4
FFN_CONV = 3
NORM_EPS = 1e-6

LANES = 128
SUBLANES = 8
NEG = float(jnp.finfo(jnp.float32).min)
POS = float(jnp.finfo(jnp.float32).max)
BISECT_UNROLL = 4
SNAP_FROM = 2
BISECT_CAP = 1024
DSA_KEY_TILE = 256
DSA_KEY_EXTENT = 512
IN_PROJ_ROWS = 1024
IDX_BATCH = 4
DSA_HEAD_PAIR = 2
FFN_DOWN_TILE = 512

QKV_W = 2 * GDN_HEADS * GDN_DK + GDN_HEADS * GDN_DV
Z_W = GDN_HEADS * GDN_DV
DQ_W = DSA_HEADS * DSA_HEAD_DIM
DKV_W = DSA_KV_HEADS * DSA_HEAD_DIM
IQ_W = IDX_HEADS * IDX_DIM
OFF_QKV = 0
OFF_Z = OFF_QKV + QKV_W
OFF_DQ = OFF_Z + Z_W
OFF_DK = OFF_DQ + DQ_W
OFF_DV = OFF_DK + DKV_W
OFF_IQ = OFF_DV + DKV_W
OFF_GG = OFF_IQ + IQ_W
SM_IK = 0
SM_A = IDX_DIM
SM_B = SM_A + GDN_HEADS
SM_IW = SM_B + GDN_HEADS


def _sigmoid(x):
    return 1.0 / (1.0 + jnp.exp(-x))


def _silu(x):
    return x * _sigmoid(x)


def _dot(a, b):
    return jnp.dot(a, b, preferred_element_type=f32)


def _dot_nt(a, b):
    return lax.dot_general(a, b, (((1,), (1,)), ((), ())), preferred_element_type=f32)


def _dot_tn(a, b):
    return lax.dot_general(a, b, (((0,), (0,)), ((), ())), preferred_element_type=f32)


def _split3(a):
    hi = a.astype(bf16)
    lo = (a - hi.astype(f32)).astype(bf16)
    return hi, lo


def _mm3(a, b):
    ah, al = a
    bh, bl = b
    return _dot(ah, bh) + (_dot(ah, bl) + _dot(al, bh))


def _dot_exact(a, b):
    return jnp.dot(a, b, preferred_element_type=f32, precision=lax.Precision.HIGHEST)


def _params(vmem_mb=None, n_axes=1):
    kw = dict(dimension_semantics=("arbitrary",) * n_axes)
    if vmem_mb is not None:
        kw["vmem_limit_bytes"] = vmem_mb * 1024 * 1024
    return pltpu.CompilerParams(**kw)


def _ada_kernel(c_ref, w_ref, b_ref, o_ref):
    s = _silu(c_ref[...]).astype(bf16)
    o_ref[...] = _dot(s, w_ref[...].astype(bf16)) + b_ref[...]


def _ada(c_all, w_ada, b_ada):
    m, d = c_all.shape
    n = w_ada.shape[1]
    tn = 1024
    return pl.pallas_call(
        _ada_kernel,
        grid=(n // tn,),
        in_specs=[pl.BlockSpec((m, d), lambda j: (0, 0)),
                  pl.BlockSpec((d, tn), lambda j: (0, j)),
                  pl.BlockSpec((1, tn), lambda j: (0, j))],
        out_specs=pl.BlockSpec((m, tn), lambda j: (0, j)),
        out_shape=jax.ShapeDtypeStruct((m, n), f32),
        compiler_params=_params(48),
        name="ada",
    )(c_all, w_ada, b_ada.reshape(1, n))


def _mod_spec(group, d, rows_per_batch, tm, col):
    if group == "prompt":
        return pl.BlockSpec((None, 1, d), lambda i, *_: ((i * tm) // rows_per_batch, 0, col))
    return pl.BlockSpec((rows_per_batch, d), lambda i, *_: (0, col))


def _prep_kernel(x_ref, g_ref, sc_ref, sh_ref, o_ref):
    x = x_ref[...]
    y = x * lax.rsqrt(jnp.mean(x * x, axis=-1, keepdims=True) + NORM_EPS)
    o_ref[...] = ((y * g_ref[...]) * (1.0 + sc_ref[...]) + sh_ref[...]).astype(bf16)


def _prep(x, g, ada, group, rows_per_batch, tm):
    m, d = x.shape
    return pl.pallas_call(
        _prep_kernel,
        grid=(m // tm,),
        in_specs=[pl.BlockSpec((tm, d), lambda i: (i, 0)),
                  pl.BlockSpec((1, d), lambda i: (0, 0)),
                  _mod_spec(group, d, rows_per_batch, tm, 1),
                  _mod_spec(group, d, rows_per_batch, tm, 0)],
        out_specs=pl.BlockSpec((tm, d), lambda i: (i, 0)),
        out_shape=jax.ShapeDtypeStruct((m, d), bf16),
        compiler_params=_params(),
        name="prep_" + group,
    )(x, g.reshape(1, d), ada, ada)


def _mm_kernel(a_ref, w_ref, o_ref):
    o_ref[...] = _dot(a_ref[...], w_ref[...])


def _matmul(a, w, tm, tn, name):
    m, k = a.shape
    n = w.shape[1]
    return pl.pallas_call(
        _mm_kernel,
        grid=(n // tn, m // tm),
        in_specs=[pl.BlockSpec((tm, k), lambda j, i: (i, 0)),
                  pl.BlockSpec((k, tn), lambda j, i: (0, j))],
        out_specs=pl.BlockSpec((tm, tn), lambda j, i: (i, j)),
        out_shape=jax.ShapeDtypeStruct((m, n), f32),
        compiler_params=_params(48, 2),
        name=name,
    )(a, w)


def _l2n(x):
    return x * lax.rsqrt(jnp.sum(x * x, axis=-1, keepdims=True) + NORM_EPS)


def _gdn_gates(sm, alog, dtb):
    xa = sm + dtb
    softplus = jnp.maximum(xa, 0.0) + jnp.log1p(jnp.exp(-jnp.abs(xa)))
    return -jnp.exp(alog) * softplus, _sigmoid(sm)


def _gated_norm(o, gn, z):
    y = o * lax.rsqrt(jnp.mean(o * o, axis=-1, keepdims=True) + NORM_EPS)
    return (y * gn) * _silu(z)


def _gdn_prompt_kernel(qkv_ref, halo_ref, z_ref, sm_ref, wc_ref, alog_ref, dtb_ref, gn_ref,
                       o_ref, sfin_ref, s_ref):
    c = pl.program_id(1)
    C = qkv_ref.shape[0]

    @pl.when(c == 0)
    def _():
        s_ref[...] = jnp.zeros_like(s_ref)

    keep = (c > 0).astype(f32)
    g_all, beta_all = _gdn_gates(sm_ref[...], alog_ref[...], dtb_ref[...])
    ri = lax.broadcasted_iota(jnp.int32, (C, C), 0)
    ci = lax.broadcasted_iota(jnp.int32, (C, C), 1)
    gc_all = _dot_exact((ri >= ci).astype(f32), g_all)
    gc_t = gc_all.T

    def conv(col):
        xs = jnp.concatenate([halo_ref[:, col:col + LANES] * keep, qkv_ref[:, col:col + LANES]], axis=0)
        w = wc_ref[:, col:col + LANES]
        y = xs[SUBLANES:] * w[GDN_CONV - 1:GDN_CONV]
        for i in range(GDN_CONV - 1):
            sft = GDN_CONV - 1 - i
            y = y + xs[SUBLANES - sft:SUBLANES - sft + C] * w[i:i + 1]
        return _silu(y)

    N = GDN_GROUP * C
    rn = lax.broadcasted_iota(jnp.int32, (N, N), 0)
    cn = lax.broadcasted_iota(jnp.int32, (N, N), 1)
    same = (rn // C) == (cn // C)
    incl = jnp.logical_and(same, rn >= cn)
    strict = jnp.logical_and(same, rn > cn)
    eye_f = (rn == cn).astype(f32)
    n_sq = max(1, (C - 1).bit_length() - 1)
    groups = [list(range(g0, g0 + GDN_GROUP)) for g0 in range(0, GDN_HEADS, GDN_GROUP)]
    stack = lambda xs: jnp.concatenate(xs, axis=0)

    qs = [stack([_l2n(conv(h * GDN_DK)) * (GDN_DK ** -0.5) for h in hs]) for hs in groups]
    ks = [stack([_l2n(conv(GDN_HEADS * GDN_DK + h * GDN_DK)) for h in hs]) for hs in groups]
    vs = [stack([conv(2 * GDN_HEADS * GDN_DK + h * GDN_DV) for h in hs]) for hs in groups]
    gcs = [stack([gc_all[:, SM_A + h:SM_A + h + 1] for h in hs]) for hs in groups]
    betas = [stack([beta_all[:, SM_B + h:SM_B + h + 1] for h in hs]) for hs in groups]
    gc_rows = [jnp.concatenate([gc_t[SM_A + h:SM_A + h + 1, :] for h in hs], axis=1) for hs in groups]
    egs = [jnp.exp(gc) for gc in gcs]
    decays = [jnp.where(incl, jnp.exp(jnp.where(incl, gc - gr, 0.0)), 0.0) for gc, gr in zip(gcs, gc_rows)]
    kbs = [k.astype(bf16) for k in ks]
    a_s = [jnp.where(strict, b * _dot_nt(kb, kb) * dc, 0.0) for b, kb, dc in zip(betas, kbs, decays)]
    qks = [(_dot_nt(q.astype(bf16), kb) * dc).astype(bf16) for q, kb, dc in zip(qs, kbs, decays)]

    invs = [eye_f - a for a in a_s]
    pws = [a.astype(bf16) for a in a_s]
    for _ in range(n_sq):
        pws = [_dot(p, p).astype(bf16) for p in pws]
        invs = [inv + _dot(inv.astype(bf16), p) for inv, p in zip(invs, pws)]
    inv_s = [_split3(inv) for inv in invs]
    res = [eye_f - inv - _mm3(_split3(a), sp) for a, inv, sp in zip(a_s, invs, inv_s)]
    invs = [inv + _dot(sp[0], r.astype(bf16)) for inv, sp, r in zip(invs, inv_s, res)]
    rhs = [jnp.concatenate([b * v, (b * eg) * k], axis=1) for b, v, eg, k in zip(betas, vs, egs, ks)]
    sols = [_mm3(_split3(inv), _split3(r)) for inv, r in zip(invs, rhs)]

    for gi, hs in enumerate(groups):
        sol, q, k, gc, eg = sols[gi], qs[gi], ks[gi], gcs[gi], egs[gi]
        q_dec = (q * eg).astype(bf16)
        w_k = sol[:, GDN_DV:].astype(bf16)
        rows = [slice(j * C, (j + 1) * C) for j in range(GDN_GROUP)]
        s_old = [s_ref[h] for h in hs]
        ws = [_dot(jnp.concatenate([w_k[r], q_dec[r]], axis=0), s.astype(bf16)) for r, s in zip(rows, s_old)]
        u = stack([sol[r, :GDN_DV] - w[:C] for r, w in zip(rows, ws)])
        ub = u.astype(bf16)
        o_intra = _dot(qks[gi], ub)
        for j, h in enumerate(hs):
            r = rows[j]
            gl = gc[r][C - 1:C, :]
            k_end = (k[r] * jnp.exp(gl - gc[r])).astype(bf16)
            s_ref[h] = s_old[j] * jnp.exp(gl) + _dot_tn(k_end, ub[r])
            o = ws[j][C:] + o_intra[r]
            zh = z_ref[:, h * GDN_DV:(h + 1) * GDN_DV]
            o_ref[:, h * GDN_DV:(h + 1) * GDN_DV] = _gated_norm(o, gn_ref[...], zh).astype(bf16)

    @pl.when(c == pl.num_programs(1) - 1)
    def _():
        sfin_ref[...] = s_ref[...]


def _gdn_vecs(a_log, dt_bias, g_gdn_norm):
    alog = jnp.zeros((1, LANES), f32).at[0, SM_A:SM_A + GDN_HEADS].set(a_log)
    dtb = jnp.zeros((1, LANES), f32).at[0, SM_A:SM_A + GDN_HEADS].set(dt_bias)
    return alog, dtb, g_gdn_norm.reshape(1, GDN_DV)


def _gdn_prompt(y, w_conv, a_log, dt_bias, g_gdn_norm, nb, t):
    C = GDN_CHUNK
    nc = t // C
    small_blk = y.shape[1] // LANES - 1
    alog, dtb, gn = _gdn_vecs(a_log, dt_bias, g_gdn_norm)
    cst = lambda b, c: (0, 0)
    return pl.pallas_call(
        _gdn_prompt_kernel,
        grid=(nb, nc),
        in_specs=[pl.BlockSpec((C, QKV_W), lambda b, c: (b * nc + c, 0)),
                  pl.BlockSpec((SUBLANES, QKV_W),
                               lambda b, c: (jnp.maximum((b * nc + c) * (C // SUBLANES) - 1, 0), 0)),
                  pl.BlockSpec((C, Z_W), lambda b, c: (b * nc + c, OFF_Z // Z_W)),
                  pl.BlockSpec((C, LANES), lambda b, c: (b * nc + c, small_blk)),
                  pl.BlockSpec((GDN_CONV, QKV_W), cst),
                  pl.BlockSpec((1, LANES), cst),
                  pl.BlockSpec((1, LANES), cst),
                  pl.BlockSpec((1, GDN_DV), cst)],
        out_specs=[pl.BlockSpec((C, Z_W), lambda b, c: (b * nc + c, 0)),
                   pl.BlockSpec((None, GDN_HEADS, GDN_DK, GDN_DV), lambda b, c: (b, 0, 0, 0))],
        out_shape=[jax.ShapeDtypeStruct((nb * t, Z_W), bf16),
                   jax.ShapeDtypeStruct((nb, GDN_HEADS, GDN_DK, GDN_DV), f32)],
        scratch_shapes=[pltpu.VMEM((GDN_HEADS, GDN_DK, GDN_DV), f32)],
        compiler_params=_params(None, 2),
        name="gdn_prompt",
    )(y, y, y, y, w_conv, alog, dtb, gn)


def _gdn_sample_kernel(qkv_ref, buf_ref, z_ref, sm_ref, sin_ref, wc_ref, alog_ref, dtb_ref, gn_ref,
                       o_ref, sout_ref, q_s, k_s, v_s, a_s, b_s, o_s):
    ts, G, _ = qkv_ref.shape
    nbuf = GDN_CONV - 1

    for h in range(GDN_HEADS):
        for part, dst in ((0, q_s), (1, k_s), (2, v_s)):
            col = part * GDN_HEADS * GDN_DK + h * GDN_DK
            w = wc_ref[:, col:col + LANES]
            rows = [buf_ref[i, :, col:col + LANES] for i in range(nbuf)]
            rows += [qkv_ref[t, :, col:col + LANES] for t in range(ts)]
            for t in range(ts):
                y = rows[t] * w[0:1]
                for i in range(1, GDN_CONV):
                    y = y + rows[t + i] * w[i:i + 1]
                y = _silu(y)
                if part == 0:
                    y = _l2n(y) * (GDN_DK ** -0.5)
                elif part == 1:
                    y = _l2n(y)
                dst[t, :, h * LANES:(h + 1) * LANES] = y

    for t in range(ts):
        g_all, beta_all = _gdn_gates(sm_ref[t], alog_ref[...], dtb_ref[...])
        a_all = jnp.exp(g_all)
        for h in range(GDN_HEADS):
            a_s[t, :, h * LANES:(h + 1) * LANES] = jnp.broadcast_to(a_all[:, SM_A + h:SM_A + h + 1], (G, LANES))
            b_s[t, :, h * LANES:(h + 1) * LANES] = jnp.broadcast_to(beta_all[:, SM_B + h:SM_B + h + 1], (G, LANES))

    nv = 2 * ts
    sr = lax.broadcasted_iota(jnp.int32, (3 * nv, nv * LANES), 0)
    sc = lax.broadcasted_iota(jnp.int32, (3 * nv, nv * LANES), 1)
    spread = (sr % nv == sc // LANES).astype(bf16)

    def head(h, carry):
        cols = pl.ds(pl.multiple_of(h * LANES, LANES), LANES)
        for i in range(G):
            row = slice(i, i + 1)
            kq = jnp.concatenate([k_s[t, row, cols] for t in range(ts)]
                                 + [q_s[t, row, cols] for t in range(ts)], axis=0)
            hi = kq.astype(bf16).astype(f32)
            mid = (kq - hi).astype(bf16).astype(f32)
            lo = (kq - hi) - mid
            kq_b = _dot(jnp.concatenate([hi, mid, lo], axis=0).T.astype(bf16), spread)
            s = sin_ref[i, h]
            for t in range(ts):
                kc = kq_b[:, t * LANES:(t + 1) * LANES]
                qc = kq_b[:, (ts + t) * LANES:(ts + t + 1) * LANES]
                a = a_s[t, row, cols]
                b = b_s[t, row, cols]
                ks = jnp.sum(s * kc, axis=0, keepdims=True)
                r = b * (v_s[t, row, cols] - a * ks)
                s = a * s + kc * r
                o_s[t, row, cols] = jnp.sum(s * qc, axis=0, keepdims=True)
            sout_ref[i, h] = s
        return carry

    lax.fori_loop(0, GDN_HEADS, head, 0)

    for t in range(ts):
        for h in range(GDN_HEADS):
            sl = slice(h * GDN_DV, (h + 1) * GDN_DV)
            o_ref[t, :, sl] = _gated_norm(o_s[t, :, sl], gn_ref[...], z_ref[t, :, sl]).astype(bf16)


def _gdn_sample(y3, buf3, state, w_conv, a_log, dt_bias, g_gdn_norm):
    ts, db, n = y3.shape
    G = SUBLANES
    small_blk = n // LANES - 1
    alog, dtb, gn = _gdn_vecs(a_log, dt_bias, g_gdn_norm)
    cst = lambda g: (0, 0)
    st_spec = pl.BlockSpec((G, GDN_HEADS, GDN_DK, GDN_DV), lambda g: (g, 0, 0, 0))
    scr = pltpu.VMEM((ts, G, Z_W), f32)
    return pl.pallas_call(
        _gdn_sample_kernel,
        grid=(db // G,),
        in_specs=[pl.BlockSpec((ts, G, QKV_W), lambda g: (0, g, 0)),
                  pl.BlockSpec((GDN_CONV - 1, G, QKV_W), lambda g: (0, g, 0)),
                  pl.BlockSpec((ts, G, Z_W), lambda g: (0, g, OFF_Z // Z_W)),
                  pl.BlockSpec((ts, G, LANES), lambda g: (0, g, small_blk)),
                  st_spec,
                  pl.BlockSpec((GDN_CONV, QKV_W), cst),
                  pl.BlockSpec((1, LANES), cst),
                  pl.BlockSpec((1, LANES), cst),
                  pl.BlockSpec((1, GDN_DV), cst)],
        out_specs=[pl.BlockSpec((ts, G, Z_W), lambda g: (0, g, 0)), st_spec],
        out_shape=[jax.ShapeDtypeStruct((ts, db, Z_W), bf16),
                   jax.ShapeDtypeStruct(state.shape, f32)],
        scratch_shapes=[scr, scr, scr, scr, scr, scr],
        compiler_params=_params(48),
        name="gdn_sample",
    )(y3, buf3, y3, y3, state, w_conv, alog, dtb, gn)


def _rope_tables(pos, rot, width):
    half = rot // 2
    inv_freq = ROPE_THETA ** (-jnp.arange(half, dtype=f32) * (2.0 / rot))
    ang = pos.astype(f32)[:, None] * inv_freq[None, :]
    cos, sin = jnp.cos(ang), jnp.sin(ang)
    n = pos.shape[0]
    z = lambda w: jnp.zeros((n, w), f32)
    cosw = jnp.concatenate([cos, cos, jnp.ones((n, width - rot), f32)], axis=1)
    sina = jnp.concatenate([-sin, z(width - half)], axis=1)
    sinb = jnp.concatenate([z(half), sin, z(width - rot)], axis=1)
    reps = LANES // width
    return tuple(jnp.tile(a, (1, reps)) for a in (cosw, sina, sinb))


def _rope_kernel(dq_ref, dk_ref, iq_ref, sm_ref, c1, sa1, sb1, c2, sa2, sb2,
                 dq_o, dk_o, iq_o, ik_o, sm_o):
    h1 = DSA_HEAD_DIM // ROPE_FRACTION // 2
    h2 = IDX_DIM // ROPE_FRACTION // 2

    def rot(x, c, sa, sb, half):
        return x * c[...] + pltpu.roll(x, LANES - half, 1) * sa[...] + pltpu.roll(x, half, 1) * sb[...]

    for j in range(DQ_W // LANES):
        sl = slice(j * LANES, (j + 1) * LANES)
        dq_o[:, sl] = rot(dq_ref[:, sl], c1, sa1, sb1, h1).astype(bf16)
    for j in range(DKV_W // LANES):
        sl = slice(j * LANES, (j + 1) * LANES)
        dk_o[:, sl] = rot(dk_ref[:, sl], c1, sa1, sb1, h1)
    for j in range(IQ_W // LANES):
        sl = slice(j * LANES, (j + 1) * LANES)
        iq_o[:, sl] = rot(iq_ref[:, sl], c2, sa2, sb2, h2).astype(bf16)
    sm = sm_ref[...]
    ik_o[...] = rot(sm, c2, sa2, sb2, h2)[:, :IDX_DIM]
    sm_o[...] = sm


def _rope(y, pos, tm, group, db=None):
    m, n = y.shape
    small_blk = n // LANES - 1
    t1 = _rope_tables(pos, DSA_HEAD_DIM // ROPE_FRACTION, DSA_HEAD_DIM)
    t2 = _rope_tables(pos, IDX_DIM // ROPE_FRACTION, IDX_DIM)
    tab_blocks = pos.shape[0] // tm
    tab = pl.BlockSpec((tm, LANES), lambda i: (i % tab_blocks, 0))
    if group == "prompt":
        omap = lambda i: (i, 0)
        rows = lambda w: m
        cols = lambda w: w
    else:
        ts = m // db
        omap = lambda i: (0, i)
        rows = lambda w: db
        cols = lambda w: ts * w
    out = lambda w, dt: jax.ShapeDtypeStruct((rows(w), cols(w)), dt)
    return pl.pallas_call(
        _rope_kernel,
        grid=(m // tm,),
        in_specs=[pl.BlockSpec((tm, DQ_W), lambda i: (i, OFF_DQ // DQ_W)),
                  pl.BlockSpec((tm, DKV_W), lambda i: (i, OFF_DK // DKV_W)),
                  pl.BlockSpec((tm, IQ_W), lambda i: (i, OFF_IQ // IQ_W)),
                  pl.BlockSpec((tm, LANES), lambda i: (i, small_blk)),
                  tab, tab, tab, tab, tab, tab],
        out_specs=[pl.BlockSpec((tm, DQ_W), omap),
                   pl.BlockSpec((tm, DKV_W), omap),
                   pl.BlockSpec((tm, IQ_W), omap),
                   pl.BlockSpec((tm, IDX_DIM), lambda i: (i, 0)),
                   pl.BlockSpec((tm, LANES), omap)],
        out_shape=[out(DQ_W, bf16), out(DKV_W, f32), out(IQ_W, bf16),
                   jax.ShapeDtypeStruct((m, IDX_DIM), f32), out(LANES, f32)],
        compiler_params=_params(),
        name="rope_" + group,
    )(y, y, y, y, *t1, *t2)


def _topk_bias(x_s, bias_s, lo0, mx, few, nkt, kt_w, k):
    R = x_s.shape[0]
    kf = float(k)
    tiles = [slice(kt * kt_w, (kt + 1) * kt_w) for kt in range(nkt)]
    n_grp = 2 if R % (2 * SUBLANES) == 0 else 1
    grps = [slice(g * (R // n_grp), (g + 1) * (R // n_grp)) for g in range(n_grp)]

    def count_ge(rows, th):
        acc = jnp.where(x_s[rows, tiles[0]] >= th, 1.0, 0.0)
        for t in tiles[1:]:
            acc = acc + jnp.where(x_s[rows, t] >= th, 1.0, 0.0)
        return jnp.sum(acc, axis=1, keepdims=True)

    def bisect(rows, lo, hi, done):
        mid = 0.5 * lo + 0.5 * hi
        cnt = count_ge(rows, mid)
        collapsed = jnp.logical_or(mid <= lo, mid >= hi)
        live = jnp.logical_and(done < 0.5, jnp.logical_not(collapsed))
        lo = jnp.where(jnp.logical_and(live, cnt >= kf), mid, lo)
        hi = jnp.where(jnp.logical_and(live, cnt <= kf), mid, hi)
        done = jnp.where(jnp.logical_or(collapsed, cnt == kf), 1.0, done)
        return lo, hi, done

    def snap(rows, lo, hi, done):
        a = b = None
        for t in tiles:
            x = x_s[rows, t]
            at = jnp.where(x >= lo, x, POS)
            bt = jnp.where(x < hi, x, NEG)
            a = at if a is None else jnp.minimum(a, at)
            b = bt if b is None else jnp.maximum(b, bt)
        a = jnp.min(a, axis=1, keepdims=True)
        b = jnp.max(b, axis=1, keepdims=True)
        live = done < 0.5
        return jnp.where(live, a, lo), jnp.where(jnp.logical_and(live, a >= b), 1.0, done)

    def cond(c):
        it, st = c
        left = st[0][2]
        for s in st[1:]:
            left = jnp.minimum(left, s[2])
        return jnp.logical_and(it < BISECT_CAP, jnp.min(left) < 0.5)

    def body(c):
        it, st = c
        for _ in range(BISECT_UNROLL):
            st = [bisect(rows, *s) for rows, s in zip(grps, st)]

        def snapped():
            out = []
            for rows, (lo, hi, done) in zip(grps, st):
                lo2, done2 = snap(rows, lo, hi, done)
                out.append((lo2, hi, done2))
            return out

        st = lax.cond(it >= SNAP_FROM, snapped, lambda: st)
        return it + 1, st

    hi0 = jnp.where(few, lo0, mx + (jnp.abs(mx) + 1.0))
    st0 = [(lo0[rows], hi0[rows], few[rows].astype(f32)) for rows in grps]
    _, st = lax.while_loop(cond, body, (jnp.int32(0), st0))
    lo = jnp.concatenate([s[0] for s in st], axis=0)
    hi = jnp.concatenate([s[1] for s in st], axis=0)

    has_run = jnp.max(jnp.where(lo < hi, 1.0, 0.0)) > 0.5
    rows_all = slice(0, R)

    @pl.when(jnp.logical_not(has_run))
    def _():
        for t in tiles:
            bias_s[:, t] = jnp.where(x_s[:, t] >= hi, 0.0, NEG)

    @pl.when(has_run)
    def _():
        need = kf - count_ge(rows_all, hi)
        ai = lax.broadcasted_iota(jnp.int32, (kt_w, kt_w), 0)
        bi = lax.broadcasted_iota(jnp.int32, (kt_w, kt_w), 1)
        before = (ai < bi).astype(bf16)
        seen = jnp.zeros((R, 1), f32)
        for t in tiles:
            x = x_s[:, t]
            run = jnp.logical_and(x >= lo, x < hi)
            runf = run.astype(f32)
            rank = seen + _dot(runf.astype(bf16), before)
            take = jnp.logical_or(x >= hi, jnp.logical_and(run, rank < need))
            bias_s[:, t] = jnp.where(take, 0.0, NEG)
            seen = seen + jnp.sum(runf, axis=1, keepdims=True)


def _dsa_prompt_kernel(iq_ref, sm_ref, ik_ref, q_ref, k_ref, v_ref, o_ref,
                       ikb, kb, vb, x_s, bias_s, s_s, *, topk, kt_w, ext_w):
    qi = pl.program_id(1)
    R = iq_ref.shape[0]
    T = ik_ref.shape[0]

    @pl.when(qi == 0)
    def _():
        ikb[...] = ik_ref[...].astype(bf16)
        kb[...] = k_ref[...].astype(bf16)
        vb[...] = v_ref[...].astype(bf16)

    def block(ext):
        tiles = [slice(j * kt_w, (j + 1) * kt_w) for j in range(ext // kt_w)]
        wgt = sm_ref[...] * IDX_SCALE
        qpos = qi * R + lax.broadcasted_iota(jnp.int32, (R, kt_w), 0)
        key0 = lax.broadcasted_iota(jnp.int32, (R, kt_w), 1)
        mn = mx = None
        for j, t in enumerate(tiles):
            keys = ikb[t, :]
            sc = jnp.zeros((R, kt_w), f32)
            for h in range(IDX_HEADS):
                rel = jnp.maximum(_dot_nt(iq_ref[:, h * IDX_DIM:(h + 1) * IDX_DIM], keys), 0.0)
                sc = sc + rel * wgt[:, SM_IW + h:SM_IW + h + 1]
            causal = key0 + j * kt_w <= qpos
            x_s[:, t] = jnp.where(causal, sc, NEG)
            lo_t = jnp.where(causal, sc, POS)
            hi_t = jnp.where(causal, sc, NEG)
            mn = lo_t if mn is None else jnp.minimum(mn, lo_t)
            mx = hi_t if mx is None else jnp.maximum(mx, hi_t)
        n_causal = qi * R + lax.broadcasted_iota(jnp.int32, (R, 1), 0) + 1
        _topk_bias(x_s, bias_s, jnp.min(mn, axis=1, keepdims=True), jnp.max(mx, axis=1, keepdims=True),
                   n_causal <= topk, len(tiles), kt_w, topk)

        def heads(hp, carry):
            h0 = hp * DSA_HEAD_PAIR
            g = h0 // (DSA_HEADS // DSA_KV_HEADS)
            gc = pl.ds(pl.multiple_of(g * DSA_HEAD_DIM, DSA_HEAD_DIM), DSA_HEAD_DIM)
            hcs = [pl.ds(pl.multiple_of((h0 + i) * DSA_HEAD_DIM, DSA_HEAD_DIM), DSA_HEAD_DIM)
                   for i in range(DSA_HEAD_PAIR)]
            qs = [q_ref[:, hc] for hc in hcs]
            mxa = [None] * DSA_HEAD_PAIR
            for t in tiles:
                kt = kb[t, gc]
                bt = bias_s[:, t]
                for i in range(DSA_HEAD_PAIR):
                    s = _dot_nt(qs[i], kt) * (DSA_HEAD_DIM ** -0.5) + bt
                    s_s[i, :, t] = s
                    mxa[i] = s if mxa[i] is None else jnp.maximum(mxa[i], s)
            ms = [jnp.max(a, axis=1, keepdims=True) for a in mxa]
            la = [jnp.zeros((R, kt_w), f32)] * DSA_HEAD_PAIR
            acc = [jnp.zeros((R, DSA_HEAD_DIM), f32)] * DSA_HEAD_PAIR
            for t in tiles:
                vt = vb[t, gc]
                for i in range(DSA_HEAD_PAIR):
                    p = jnp.exp(s_s[i, :, t] - ms[i])
                    la[i] = la[i] + p
                    acc[i] = acc[i] + _dot(p.astype(bf16), vt)
            for i in range(DSA_HEAD_PAIR):
                o_ref[:, hcs[i]] = (acc[i] / jnp.sum(la[i], axis=1, keepdims=True)).astype(bf16)
            return carry

        lax.fori_loop(0, DSA_HEADS // DSA_HEAD_PAIR, heads, 0)

    for e in range(T // ext_w):
        pl.when(qi // (ext_w // R) == e)(functools.partial(block, (e + 1) * ext_w))


def _dsa_prompt(iq, sm, ik, dq, dk, y, nb, t, topk):
    R = LANES
    nq = t // R
    kt_w = min(DSA_KEY_TILE, t)
    ext_w = min(DSA_KEY_EXTENT, t)
    kern = functools.partial(_dsa_prompt_kernel, topk=topk, kt_w=kt_w, ext_w=ext_w)
    return pl.pallas_call(
        kern,
        grid=(nb, nq),
        in_specs=[pl.BlockSpec((R, IQ_W), lambda b, i: (b * nq + i, 0)),
                  pl.BlockSpec((R, LANES), lambda b, i: (b * nq + i, 0)),
                  pl.BlockSpec((t, IDX_DIM), lambda b, i: (b, 0)),
                  pl.BlockSpec((R, DQ_W), lambda b, i: (b * nq + i, 0)),
                  pl.BlockSpec((t, DKV_W), lambda b, i: (b, 0)),
                  pl.BlockSpec((t, DKV_W), lambda b, i: (b, OFF_DV // DKV_W))],
        out_specs=pl.BlockSpec((R, DQ_W), lambda b, i: (b * nq + i, 0)),
        out_shape=jax.ShapeDtypeStruct((nb * t, DQ_W), bf16),
        scratch_shapes=[pltpu.VMEM((t, IDX_DIM), bf16), pltpu.VMEM((t, DKV_W), bf16),
                        pltpu.VMEM((t, DKV_W), bf16), pltpu.VMEM((R, t), f32), pltpu.VMEM((R, t), f32),
                        pltpu.VMEM((DSA_HEAD_PAIR, R, t), f32)],
        compiler_params=_params(48, 2),
        name="dsa_prompt",
    )(iq, sm, ik, dq, dk, y)


def _idx_score_kernel(pt_ref, iq_ref, sm_ref, ikn_ref, *rest, n_pages, page):
    eb, tp, _ = iq_ref.shape
    o_ref = rest[eb * n_pages]
    ts = o_ref.shape[1]
    past = n_pages * page
    lp = o_ref.shape[2]
    for e in range(eb):
        pages = rest[e * n_pages:(e + 1) * n_pages]
        keys_t = jnp.concatenate([p[...] for p in pages], axis=1).astype(bf16)
        keys_n = jnp.concatenate([ikn_ref[e], jnp.zeros((lp - past - tp, IDX_DIM), f32)], axis=0).astype(bf16)
        wgt = sm_ref[e] * IDX_SCALE
        iq = iq_ref[e].astype(bf16)
        score = jnp.zeros((tp, lp), f32)
        for h in range(IDX_HEADS):
            iqh = iq[:, h * IDX_DIM:(h + 1) * IDX_DIM]
            rel = jnp.maximum(jnp.concatenate([_dot(iqh, keys_t), _dot_nt(iqh, keys_n)], axis=1), 0.0)
            score = score + rel * wgt[:, SM_IW + h:SM_IW + h + 1]
        o_ref[e] = score[:ts]


def _idx_scores(page_table, iq, sm, ikn, cache_idx, lp, ts):
    db, tp, _ = iq.shape
    n_pages = page_table.shape[1]
    page = cache_idx.shape[2]
    eb = IDX_BATCH
    kern = functools.partial(_idx_score_kernel, n_pages=n_pages, page=page)
    page_specs = [pl.BlockSpec((None, IDX_DIM, page),
                               functools.partial(lambda b, pt, e, p: (pt[b * eb + e, p], 0, 0), e=e, p=p))
                  for e in range(eb) for p in range(n_pages)]
    grid_spec = pltpu.PrefetchScalarGridSpec(
        num_scalar_prefetch=1,
        grid=(db // eb,),
        in_specs=[pl.BlockSpec((eb, tp, IQ_W), lambda b, pt: (b, 0, 0)),
                  pl.BlockSpec((eb, tp, LANES), lambda b, pt: (b, 0, 0)),
                  pl.BlockSpec((eb, tp, IDX_DIM), lambda b, pt: (b, 0, 0))] + page_specs,
        out_specs=pl.BlockSpec((eb, ts, lp), lambda b, pt: (b, 0, 0)),
    )
    return pl.pallas_call(
        kern, grid_spec=grid_spec,
        out_shape=jax.ShapeDtypeStruct((db, ts, lp), f32),
        compiler_params=_params(),
        name="idx_scores_sample",
    )(page_table, iq, sm, ikn, *([cache_idx] * (eb * n_pages)))


def _select_sample_kernel(x_ref, bias_ref, x_s, *, topk, past, ts):
    R, lp = x_ref.shape
    key = lax.broadcasted_iota(jnp.int32, (R, lp), 1)
    t = lax.broadcasted_iota(jnp.int32, (R, lp), 0) % ts
    causal = key <= past + t
    x = x_ref[...]
    x_s[...] = jnp.where(causal, x, NEG)
    lo0 = jnp.min(jnp.where(causal, x, POS), axis=1, keepdims=True)
    mx = jnp.max(jnp.where(causal, x, NEG), axis=1, keepdims=True)
    n_causal = past + lax.broadcasted_iota(jnp.int32, (R, 1), 0) % ts + 1
    _topk_bias(x_s, bias_ref, lo0, mx, n_causal <= topk, lp // LANES, LANES, topk)


def _select_sample(scores, topk, past, ts):
    m, lp = scores.shape
    R = LANES
    kern = functools.partial(_select_sample_kernel, topk=topk, past=past, ts=ts)
    return pl.pallas_call(
        kern, grid=(m // R,),
        in_specs=[pl.BlockSpec((R, lp), lambda i: (i, 0))],
        out_specs=pl.BlockSpec((R, lp), lambda i: (i, 0)),
        out_shape=jax.ShapeDtypeStruct((m, lp), f32),
        scratch_shapes=[pltpu.VMEM((R, lp), f32)],
        compiler_params=_params(),
        name="select_sample",
    )(scores)


def _dsa_sample_kernel(pt_ref, q_ref, sel_ref, kn_ref, vn_ref, *rest, n_pages, page):
    kp = rest[:n_pages]
    vp = rest[n_pages:2 * n_pages]
    o_ref = rest[2 * n_pages]
    tp = q_ref.shape[0]
    ts = sel_ref.shape[0]
    past = n_pages * page
    lp = sel_ref.shape[1]
    hpg = DSA_HEADS // DSA_KV_HEADS
    bias = jnp.concatenate([sel_ref[...], jnp.zeros((tp - ts, lp), f32)], axis=0)
    bias = jnp.concatenate([bias] * hpg, axis=0)
    padn = jnp.zeros((lp - past - tp, DSA_HEAD_DIM), f32)
    for g in range(DSA_KV_HEADS):
        gsl = slice(g * DSA_HEAD_DIM, (g + 1) * DSA_HEAD_DIM)
        kg = jnp.concatenate([p[pl.ds(g, page, stride=DSA_KV_HEADS), :] for p in kp]
                             + [kn_ref[:, gsl], padn], axis=0).astype(bf16)
        vg = jnp.concatenate([p[pl.ds(g, page, stride=DSA_KV_HEADS), :] for p in vp]
                             + [vn_ref[:, gsl], padn], axis=0).astype(bf16)
        qg = jnp.concatenate([q_ref[:, (g * hpg + j) * DSA_HEAD_DIM:(g * hpg + j + 1) * DSA_HEAD_DIM]
                              for j in range(hpg)], axis=0).astype(bf16)
        s = _dot_nt(qg, kg) * (DSA_HEAD_DIM ** -0.5) + bias
        m = jnp.max(s, axis=1, keepdims=True)
        p = jnp.exp(s - m)
        l = jnp.sum(p, axis=1, keepdims=True)
        o = _dot(p.astype(bf16), vg) / l
        for j in range(hpg):
            hsl = slice((g * hpg + j) * DSA_HEAD_DIM, (g * hpg + j + 1) * DSA_HEAD_DIM)
            o_ref[:, hsl] = o[j * tp:j * tp + ts]


def _dsa_sample(page_table, dq, sel, kn, vn, ck, cv):
    db, tp, _ = dq.shape
    ts = sel.shape[1]
    lp = sel.shape[2]
    n_pages = page_table.shape[1]
    rows = ck.shape[1]
    page = rows // DSA_KV_HEADS
    kern = functools.partial(_dsa_sample_kernel, n_pages=n_pages, page=page)
    pspec = [pl.BlockSpec((None, rows, DSA_HEAD_DIM), functools.partial(lambda b, pt, p: (pt[b, p], 0, 0), p=p))
             for p in range(n_pages)]
    per_b = lambda r, w: pl.BlockSpec((None, r, w), lambda b, pt: (b, 0, 0))
    grid_spec = pltpu.PrefetchScalarGridSpec(
        num_scalar_prefetch=1,
        grid=(db,),
        in_specs=[per_b(tp, DQ_W), per_b(ts, lp), per_b(tp, DKV_W), per_b(tp, DKV_W)] + pspec + pspec,
        out_specs=per_b(ts, DQ_W),
    )
    return pl.pallas_call(
        kern, grid_spec=grid_spec,
        out_shape=jax.ShapeDtypeStruct((db, ts, DQ_W), f32),
        compiler_params=_params(48),
        name="dsa_sample",
    )(page_table, dq, sel, kn, vn, *([ck] * n_pages), *([cv] * n_pages))


def _merge_kernel(og_ref, od_ref, gg_ref, gd_ref, x_ref, gt_ref, sc_ref, sh_ref, g2_ref,
                  wg_ref, wd_ref, wo_ref, x1_ref, h2_ref):
    mix = (_sigmoid(gg_ref[...]) * _dot(og_ref[...], wg_ref[...])
           + _sigmoid(gd_ref[...]) * _dot(od_ref[...], wd_ref[...]))
    x1 = x_ref[...] + gt_ref[...] * _dot(mix.astype(bf16), wo_ref[...])
    x1_ref[...] = x1
    y = x1 * lax.rsqrt(jnp.mean(x1 * x1, axis=-1, keepdims=True) + NORM_EPS)
    h2_ref[...] = ((y * g2_ref[...]) * (1.0 + sc_ref[...]) + sh_ref[...]).astype(bf16)


def _merge(o_gdn, o_dsa, y, x, ada, g2, wg, wd, wo, group, rows_per_batch, tm):
    m, d = x.shape
    row = lambda w, blk=0: pl.BlockSpec((tm, w), lambda i: (i, blk))
    res = lambda a: pl.BlockSpec(a.shape, lambda i: (0, 0), pipeline_mode=pl.Buffered(1))
    return pl.pallas_call(
        _merge_kernel,
        grid=(m // tm,),
        in_specs=[row(Z_W), row(DQ_W), row(d, OFF_GG // d), row(d, OFF_GG // d + 1), row(d),
                  _mod_spec(group, d, rows_per_batch, tm, 2),
                  _mod_spec(group, d, rows_per_batch, tm, 4),
                  _mod_spec(group, d, rows_per_batch, tm, 3),
                  pl.BlockSpec((1, d), lambda i: (0, 0)),
                  res(wg), res(wd), res(wo)],
        out_specs=[row(d), row(d)],
        out_shape=[jax.ShapeDtypeStruct((m, d), f32), jax.ShapeDtypeStruct((m, d), bf16)],
        compiler_params=_params(48),
        name="merge_" + group,
    )(o_gdn, o_dsa, y, y, x, ada, ada, ada, g2.reshape(1, d), wg, wd, wo)


def _ffn_epilogue(acc_ref, x1_ref, gt_ref, gf_ref, y_ref):
    tm = acc_ref.shape[0]
    r = gt_ref.shape[0] if gt_ref.shape[0] > 1 else tm
    for s0 in range(0, tm, r):
        sl = slice(s0, s0 + r)
        x2 = x1_ref[sl, :] + gt_ref[...] * acc_ref[sl, :]
        y_ref[sl, :] = (x2 * lax.rsqrt(jnp.mean(x2 * x2, axis=-1, keepdims=True) + NORM_EPS)) * gf_ref[...]


def _ffn_prompt_kernel(h_ref, halo_ref, wug_ref, wuv_ref, wcg_ref, wcv_ref, bg_ref, bv_ref, wd_ref,
                       x1_ref, gt_ref, gf_ref, y_ref, ug_ref, uv_ref, acc_ref, act_s, *, blocks_per_seq, n_up):
    i = pl.program_id(0)
    j = pl.program_id(1)
    tm = h_ref.shape[0]
    keep = ((i % blocks_per_seq) > 0).astype(f32)

    def branch(wu_ref, wc_ref, b_ref, ubuf_ref):
        u = _dot(h_ref[...], wu_ref[...])
        uh = _dot(halo_ref[...], wu_ref[...]) * keep
        ubuf_ref[...] = u[tm - SUBLANES:]
        ux = jnp.concatenate([uh, u], axis=0)
        w = wc_ref[...]
        y = u * w[FFN_CONV - 1:FFN_CONV] + b_ref[...]
        for t in range(FFN_CONV - 1):
            sft = FFN_CONV - 1 - t
            y = y + ux[SUBLANES - sft:SUBLANES - sft + tm] * w[t:t + 1]
        return y

    def up():
        return _silu(branch(wug_ref, wcg_ref, bg_ref, ug_ref)) * branch(wuv_ref, wcv_ref, bv_ref, uv_ref)

    _ffn_phases(j, n_up, up, act_s, wd_ref, acc_ref, x1_ref, gt_ref, gf_ref, y_ref)


def _ffn_phases(j, n_up, up, act_s, wd_ref, acc_ref, x1_ref, gt_ref, gf_ref, y_ref):
    tf = act_s.shape[1] // n_up
    tn = wd_ref.shape[1]

    @pl.when(j < n_up)
    def _():
        act_s[:, pl.ds(pl.multiple_of(j * tf, tf), tf)] = up().astype(bf16)

    @pl.when(j >= n_up)
    def _():
        acc_ref[:, pl.ds(pl.multiple_of((j - n_up) * tn, tn), tn)] = _dot(act_s[...], wd_ref[...])

    @pl.when(j == pl.num_programs(1) - 1)
    def _():
        _ffn_epilogue(acc_ref, x1_ref, gt_ref, gf_ref, y_ref)


def _ffn_sample_kernel(h_ref, buf_ref, wug_ref, wuv_ref, wcg_ref, wcv_ref, bg_ref, bv_ref, wd_ref,
                       x1_ref, gt_ref, gf_ref, y_ref, ug_ref, uv_ref, acc_ref, act_s, *, ts, n_up):
    j = pl.program_id(1)
    db = h_ref.shape[0] // ts
    nbuf = FFN_CONV - 1
    tf = wug_ref.shape[1]

    def branch(wu_ref, wc_ref, b_ref, ubuf_ref, half):
        u = _dot(h_ref[...], wu_ref[...])
        rows = [buf_ref[r, :, half * tf:(half + 1) * tf] for r in range(nbuf)]
        rows += [u[t * db:(t + 1) * db] for t in range(ts)]
        for r in range(nbuf):
            ubuf_ref[r] = rows[ts + r]
        w = wc_ref[...]
        outs = []
        for t in range(ts):
            y = rows[t] * w[0:1] + b_ref[...]
            for r in range(1, FFN_CONV):
                y = y + rows[t + r] * w[r:r + 1]
            outs.append(y)
        return jnp.concatenate(outs, axis=0)

    def up():
        return _silu(branch(wug_ref, wcg_ref, bg_ref, ug_ref, 0)) * branch(wuv_ref, wcv_ref, bv_ref, uv_ref, 1)

    _ffn_phases(j, n_up, up, act_s, wd_ref, acc_ref, x1_ref, gt_ref, gf_ref, y_ref)


def _ffn(h2, x1, ada, g_final, w_up, w_conv, b_conv, w_down, group, rows_per_batch, tm, tf, buf=None):
    m, d = h2.shape
    dff = w_down.shape[0]
    nj = dff // tf
    tn = min(FFN_DOWN_TILE, d)
    b2 = b_conv.reshape(1, 2 * dff)
    up = lambda j: jnp.minimum(j, nj - 1)
    common_w = [pl.BlockSpec((d, tf), lambda i, j: (0, up(j))),
                pl.BlockSpec((d, tf), lambda i, j: (0, nj + up(j))),
                pl.BlockSpec((FFN_CONV, tf), lambda i, j: (0, up(j))),
                pl.BlockSpec((FFN_CONV, tf), lambda i, j: (0, nj + up(j))),
                pl.BlockSpec((1, tf), lambda i, j: (0, up(j))),
                pl.BlockSpec((1, tf), lambda i, j: (0, nj + up(j))),
                pl.BlockSpec((dff, tn), lambda i, j: (0, jnp.maximum(j - nj, 0)))]
    tail = [pl.BlockSpec((tm, d), lambda i, j: (i, 0)),
            _mod_spec(group, d, rows_per_batch, tm, 5),
            pl.BlockSpec((1, d), lambda i, j: (0, 0))]
    y_spec = pl.BlockSpec((tm, d), lambda i, j: (i, 0))
    if group == "prompt":
        bps = rows_per_batch // tm
        nb = m // rows_per_batch
        kern = functools.partial(_ffn_prompt_kernel, blocks_per_seq=bps, n_up=nj)
        first = [pl.BlockSpec((tm, d), lambda i, j: (i, 0)),
                 pl.BlockSpec((SUBLANES, d), lambda i, j: (jnp.maximum(i * (tm // SUBLANES) - 1, 0), 0))]
        ubuf_spec = pl.BlockSpec((None, SUBLANES, tf), lambda i, j: (i, 0, up(j)))
        ubuf_shape = jax.ShapeDtypeStruct((m // tm, SUBLANES, dff), f32)
        args = (h2, h2)
    else:
        ts = m // rows_per_batch
        kern = functools.partial(_ffn_sample_kernel, ts=ts, n_up=nj)
        first = [pl.BlockSpec((tm, d), lambda i, j: (i, 0)),
                 pl.BlockSpec((FFN_CONV - 1, rows_per_batch, 2 * tf), lambda i, j: (0, 0, up(j)))]
        ubuf_spec = pl.BlockSpec((FFN_CONV - 1, rows_per_batch, tf), lambda i, j: (0, 0, up(j)))
        ubuf_shape = jax.ShapeDtypeStruct((FFN_CONV - 1, rows_per_batch, dff), f32)
        args = (h2, buf)
    return pl.pallas_call(
        kern,
        grid=(m // tm, nj + d // tn),
        in_specs=first + common_w + tail,
        out_specs=[y_spec, ubuf_spec, ubuf_spec],
        out_shape=[jax.ShapeDtypeStruct((m, d), f32), ubuf_shape, ubuf_shape],
        scratch_shapes=[pltpu.VMEM((tm, d), f32), pltpu.VMEM((tm, dff), bf16)],
        compiler_params=_params(56, 2),
        name="ffn_" + group,
    )(*args, w_up, w_up, w_conv, w_conv, b2, b2, w_down, x1, ada, g_final.reshape(1, d))


def _pack_w_in(w_in, d):
    sizes = (QKV_W, Z_W, GDN_HEADS, GDN_HEADS, DQ_W, DKV_W, DKV_W, IQ_W, IDX_DIM, IDX_HEADS, d, d)
    segs, off = [], 0
    for s in sizes:
        segs.append(w_in[:, off:off + s])
        off += s
    qkv, z, a, b, dq, dk, dv, iq, ik, iw, gg, gd = segs
    pad = jnp.zeros((w_in.shape[0], LANES - (IDX_DIM + 2 * GDN_HEADS + IDX_HEADS)), w_in.dtype)
    small = jnp.concatenate([ik, a, b, iw, pad], axis=1)
    return jnp.concatenate([qkv, z, dq, dk, dv, iq, gg, gd, small], axis=1).astype(bf16)


def _ffn_tile(dff):
    for tf in (512, 256, 128):
        if dff % tf == 0:
            return tf
    raise ValueError("d_ff must be a multiple of 128")


def _mm_tile(n):
    for tn in (1152, 1024, 768, 512, 384, 256, 128):
        if n % tn == 0:
            return tn
    raise ValueError("projection width must be a multiple of 128")


def kernel(x_prompt, x_sample, c_prompt, c_sample, cache_k, cache_v, cache_idx_k, page_table, state_gdn, state_gdn_conv, state_ffn_conv, w_ada, b_ada, g_norm1, w_in, w_gdn_conv, a_log, dt_bias, g_gdn_norm, w_gdn_out, w_dsa_out, w_o, g_norm2, w_up, w_ffn_conv, b_ffn_conv, w_down, g_final):
    nb, t, d = x_prompt.shape
    db, ts, _ = x_sample.shape
    depth = w_ada.shape[0]
    assert depth == 1 and db == LANES and OFF_GG % d == 0 and ts >= GDN_CONV - 1
    n_pages = page_table.shape[1]
    page = cache_k.shape[2]
    past = n_pages * page
    dff = w_down.shape[1]
    l = 0

    n_c = nb + db
    pad_c = (-n_c) % SUBLANES
    c_all = jnp.concatenate([c_prompt, c_sample, jnp.zeros((pad_c, d), f32)], axis=0)
    ada = _ada(c_all, w_ada[l], b_ada[l])
    ada_p = ada[:nb].reshape(nb, 1, 6 * d)
    ada_s = ada[nb:nb + db]

    w_in_p = _pack_w_in(w_in[l], d)
    wg = w_gdn_out[l].astype(bf16)
    wd = w_dsa_out[l].astype(bf16)
    wo = w_o[l].astype(bf16)
    wup = w_up[l].astype(bf16)
    wdn = w_down[l].astype(bf16)
    tn = _mm_tile(w_in_p.shape[1])
    tf = _ffn_tile(dff)

    xp = x_prompt.reshape(nb * t, d)
    tm_p = min(512, t)
    h1 = _prep(xp, g_norm1[l], ada_p, "prompt", t, tm_p)
    yp = _matmul(h1, w_in_p, min(IN_PROJ_ROWS, nb * t), tn, "in_proj_prompt")
    og_p, s_p = _gdn_prompt(yp, w_gdn_conv[l], a_log[l], dt_bias[l], g_gdn_norm[l], nb, t)
    dq_p, dk_p, iq_p, ik_p, sm_p = _rope(yp, jnp.arange(t), tm_p, "prompt")
    od_p = _dsa_prompt(iq_p, sm_p, ik_p, dq_p, dk_p, yp, nb, t, min(DSA_TOPK, t // 4))
    x1_p, h2_p = _merge(og_p, od_p, yp, xp, ada_p, g_norm2[l], wg, wd, wo, "prompt", t, min(256, t))
    y_p, ug_p, uv_p = _ffn(h2_p, x1_p, ada_p, g_final, wup, w_ffn_conv[l], b_ffn_conv[l], wdn,
                           "prompt", t, tm_p, tf)

    yp3 = yp.reshape(nb, t, -1)
    nfb = FFN_CONV - 1
    bps_p = t // tm_p
    out_p = (
        y_p.reshape(nb, t, d),
        dk_p.reshape(1, nb, t, DSA_KV_HEADS, DSA_HEAD_DIM),
        yp3[:, :, OFF_DV:OFF_DV + DKV_W].reshape(1, nb, t, DSA_KV_HEADS, DSA_HEAD_DIM),
        ik_p.reshape(1, nb, t, IDX_DIM),
        s_p[None],
        yp3[:, t - (GDN_CONV - 1):, :QKV_W][None],
        jnp.concatenate([ug_p[bps_p - 1::bps_p, SUBLANES - nfb:], uv_p[bps_p - 1::bps_p, SUBLANES - nfb:]],
                        axis=-1)[None],
    )

    xs = x_sample.transpose(1, 0, 2).reshape(ts * db, d)
    h1s = _prep(xs, g_norm1[l], ada_s, "sample", db, db)
    ys = _matmul(h1s, w_in_p, ts * db, tn, "in_proj_sample")
    ys3 = ys.reshape(ts, db, -1)
    og_s, s_s = _gdn_sample(ys3, state_gdn_conv[l].transpose(1, 0, 2), state_gdn[l],
                            w_gdn_conv[l], a_log[l], dt_bias[l], g_gdn_norm[l])
    pos_s = jnp.repeat(past + jnp.arange(ts), db)
    dq_s, dk_s, iq_s, ik_s, sm_s = _rope(ys, pos_s, db, "sample", db)
    ik_s_b = ik_s.reshape(ts, db, IDX_DIM).transpose(1, 0, 2)
    lp = past + LANES
    tp = -(-ts // SUBLANES) * SUBLANES
    pad_t = lambda a: jnp.pad(a.astype(f32), ((0, 0), (0, tp - ts), (0, 0)))
    scores = _idx_scores(page_table, pad_t(iq_s.reshape(db, ts, IQ_W)), pad_t(sm_s.reshape(db, ts, LANES)),
                         pad_t(ik_s_b), jnp.swapaxes(cache_idx_k[l], 1, 2), lp, ts)
    sel = _select_sample(scores.reshape(db * ts, lp), min(DSA_TOPK, (past + ts) // 4), past, ts)
    dv_s_b = ys3[:, :, OFF_DV:OFF_DV + DKV_W].transpose(1, 0, 2)
    ck = cache_k[l].reshape(cache_k.shape[1], page * DSA_KV_HEADS, DSA_HEAD_DIM)
    cv = cache_v[l].reshape(cache_v.shape[1], page * DSA_KV_HEADS, DSA_HEAD_DIM)
    od_s = _dsa_sample(page_table, pad_t(dq_s.reshape(db, ts, DQ_W)), sel.reshape(db, ts, lp),
                       pad_t(dk_s.reshape(db, ts, DKV_W)), pad_t(dv_s_b), ck, cv)
    od_s = od_s.astype(bf16).transpose(1, 0, 2).reshape(ts * db, DQ_W)
    x1_s, h2_s = _merge(og_s.reshape(ts * db, Z_W), od_s, ys, xs, ada_s, g_norm2[l], wg, wd, wo,
                        "sample", db, db)
    fb = state_ffn_conv[l].transpose(1, 0, 2)
    nj = dff // tf
    fb = jnp.concatenate([fb[:, :, :dff].reshape(nfb, db, nj, tf), fb[:, :, dff:].reshape(nfb, db, nj, tf)],
                         axis=-1).reshape(nfb, db, 2 * dff)
    y_s, ug_s, uv_s = _ffn(h2_s, x1_s, ada_s, g_final, wup, w_ffn_conv[l], b_ffn_conv[l], wdn,
                           "sample", db, ts * db, tf, buf=fb)

    out_s = (
        y_s.reshape(ts, db, d).transpose(1, 0, 2),
        dk_s.reshape(1, db, ts, DSA_KV_HEADS, DSA_HEAD_DIM),
        dv_s_b.reshape(1, db, ts, DSA_KV_HEADS, DSA_HEAD_DIM),
        ik_s_b[None],
        s_s[None],
        ys3[ts - (GDN_CONV - 1):, :, :QKV_W].transpose(1, 0, 2)[None],
        jnp.concatenate([ug_s, uv_s], axis=-1).transpose(1, 0, 2)[None],
    )
    return (out_p[0], out_s[0]) + out_p[1:] + out_s[1:]
```

```python
import functools

import jax
import jax.numpy as jnp
from jax import lax
from jax.experimental import pallas as pl
from jax.experimental.pallas import tpu as pltpu

f32 = jnp.float32
bf16 = jnp.bfloat16

GDN_HEADS = 8
GDN_DK = 128
GDN_DV = 128
GDN_CONV = 4
GDN_CHUNK = 64
GDN_GROUP = 4
DSA_HEADS = 8
DSA_KV_HEADS = 2
DSA_HEAD_DIM = 128
IDX_HEADS = 8
IDX_DIM = 64
IDX_SCALE = IDX_HEADS ** -0.5 * IDX_DIM ** -0.5
DSA_TOPK = 256
ROPE_THETA = 500000.0
ROPE_FRACTION = 4
FFN_CONV = 3
NORM_EPS = 1e-6

LANES = 128
SUBLANES = 8
NEG = float(jnp.finfo(jnp.float32).min)
POS = float(jnp.finfo(jnp.float32).max)
BISECT_UNROLL = 4
SNAP_FROM = 4
BISECT_CAP = 1024
DSA_KEY_TILE = 256
DSA_KEY_EXTENT = 512
IN_PROJ_ROWS = 1024
IDX_BATCH = 4
DSA_SAMPLE_BATCH = 2
DSA_HEAD_PAIR = 2
FFN_DOWN_TILE = 512
FFN_SUB_ROWS = 256

QKV_W = 2 * GDN_HEADS * GDN_DK + GDN_HEADS * GDN_DV
Z_W = GDN_HEADS * GDN_DV
DQ_W = DSA_HEADS * DSA_HEAD_DIM
DKV_W = DSA_KV_HEADS * DSA_HEAD_DIM
IQ_W = IDX_HEADS * IDX_DIM
OFF_Z = QKV_W
GDN_PROJ_W = QKV_W + Z_W
OFF_DQ = 0
OFF_DK = OFF_DQ + DQ_W
OFF_DV = OFF_DK + DKV_W
OFF_IQ = OFF_DV + DKV_W
DSA_PROJ_W = OFF_IQ + IQ_W
SM_IK = 0
SM_A = IDX_DIM
SM_B = SM_A + GDN_HEADS
SM_IW = SM_B + GDN_HEADS


def _sigmoid(x):
    return 1.0 / (1.0 + jnp.exp(-x))


def _silu(x):
    return x * _sigmoid(x)


def _dot(a, b):
    return jnp.dot(a, b, preferred_element_type=f32)


def _dot_nt(a, b):
    return lax.dot_general(a, b, (((1,), (1,)), ((), ())), preferred_element_type=f32)


def _dot_tn(a, b):
    return lax.dot_general(a, b, (((0,), (0,)), ((), ())), preferred_element_type=f32)


def _split3(a):
    hi = a.astype(bf16)
    lo = (a - hi.astype(f32)).astype(bf16)
    return hi, lo


def _mm3(a, b):
    ah, al = a
    bh, bl = b
    return _dot(ah, bh) + (_dot(ah, bl) + _dot(al, bh))


def _dot_exact(a, b):
    return jnp.dot(a, b, preferred_element_type=f32, precision=lax.Precision.HIGHEST)


def _params(vmem_mb=None, n_axes=1):
    kw = dict(dimension_semantics=("arbitrary",) * n_axes)
    if vmem_mb is not None:
        kw["vmem_limit_bytes"] = vmem_mb * 1024 * 1024
    return pltpu.CompilerParams(**kw)


def _ada_kernel(c_ref, w_ref, b_ref, o_ref):
    s = _silu(c_ref[...]).astype(bf16)
    o_ref[...] = _dot(s, w_ref[...].astype(bf16)) + b_ref[...]


def _ada(c_all, w_ada, b_ada):
    m, d = c_all.shape
    n = w_ada.shape[1]
    tn = 1024
    return pl.pallas_call(
        _ada_kernel,
        grid=(n // tn,),
        in_specs=[pl.BlockSpec((m, d), lambda j: (0, 0)),
                  pl.BlockSpec((d, tn), lambda j: (0, j)),
                  pl.BlockSpec((1, tn), lambda j: (0, j))],
        out_specs=pl.BlockSpec((m, tn), lambda j: (0, j)),
        out_shape=jax.ShapeDtypeStruct((m, n), f32),
        compiler_params=_params(48),
        name="ada",
    )(c_all, w_ada, b_ada.reshape(1, n))


def _mod_spec(group, d, rows_per_batch, tm, col):
    if group == "prompt":
        return pl.BlockSpec((None, 1, d), lambda i, *_: ((i * tm) // rows_per_batch, 0, col))
    return pl.BlockSpec((rows_per_batch, d), lambda i, *_: (0, col))


def _prep_kernel(x_ref, g_ref, sc_ref, sh_ref, o_ref):
    x = x_ref[...]
    y = x * lax.rsqrt(jnp.mean(x * x, axis=-1, keepdims=True) + NORM_EPS)
    o_ref[...] = ((y * g_ref[...]) * (1.0 + sc_ref[...]) + sh_ref[...]).astype(bf16)


def _prep(x, g, ada, group, rows_per_batch, tm):
    m, d = x.shape
    return pl.pallas_call(
        _prep_kernel,
        grid=(m // tm,),
        in_specs=[pl.BlockSpec((tm, d), lambda i: (i, 0)),
                  pl.BlockSpec((1, d), lambda i: (0, 0)),
                  _mod_spec(group, d, rows_per_batch, tm, 1),
                  _mod_spec(group, d, rows_per_batch, tm, 0)],
        out_specs=pl.BlockSpec((tm, d), lambda i: (i, 0)),
        out_shape=jax.ShapeDtypeStruct((m, d), bf16),
        compiler_params=_params(),
        name="prep_" + group,
    )(x, g.reshape(1, d), ada, ada)


def _mm_kernel(a_ref, w_ref, o_ref):
    o_ref[...] = _dot(a_ref[...], w_ref[...])


def _matmul(a, w, tm, tn, name):
    m, k = a.shape
    n = w.shape[1]
    return pl.pallas_call(
        _mm_kernel,
        grid=(n // tn, m // tm),
        in_specs=[pl.BlockSpec((tm, k), lambda j, i: (i, 0)),
                  pl.BlockSpec((k, tn), lambda j, i: (0, j))],
        out_specs=pl.BlockSpec((tm, tn), lambda j, i: (i, j)),
        out_shape=jax.ShapeDtypeStruct((m, n), f32),
        compiler_params=_params(48, 2),
        name=name,
    )(a, w)


def _l2n(x):
    return x * lax.rsqrt(jnp.sum(x * x, axis=-1, keepdims=True) + NORM_EPS)


def _gdn_gates(sm, alog, dtb):
    xa = sm + dtb
    softplus = jnp.maximum(xa, 0.0) + jnp.log1p(jnp.exp(-jnp.abs(xa)))
    return -jnp.exp(alog) * softplus, _sigmoid(sm)


def _gated_norm(o, gn, z):
    y = o * lax.rsqrt(jnp.mean(o * o, axis=-1, keepdims=True) + NORM_EPS)
    return (y * gn) * _silu(z)


def _gdn_prompt_kernel(qkv_ref, halo_ref, z_ref, sm_ref, wc_ref, alog_ref, dtb_ref, gn_ref,
                       o_ref, sfin_ref, s_ref):
    c = pl.program_id(1)
    C = qkv_ref.shape[0]

    @pl.when(c == 0)
    def _():
        s_ref[...] = jnp.zeros_like(s_ref)

    keep = (c > 0).astype(f32)
    g_all, beta_all = _gdn_gates(sm_ref[...], alog_ref[...], dtb_ref[...])
    ri = lax.broadcasted_iota(jnp.int32, (C, C), 0)
    ci = lax.broadcasted_iota(jnp.int32, (C, C), 1)
    gc_all = _dot_exact((ri >= ci).astype(f32), g_all)
    gc_t = gc_all.T

    def conv(col):
        xs = jnp.concatenate([halo_ref[:, col:col + LANES] * keep, qkv_ref[:, col:col + LANES]], axis=0)
        w = wc_ref[:, col:col + LANES]
        y = xs[SUBLANES:] * w[GDN_CONV - 1:GDN_CONV]
        for i in range(GDN_CONV - 1):
            sft = GDN_CONV - 1 - i
            y = y + xs[SUBLANES - sft:SUBLANES - sft + C] * w[i:i + 1]
        return _silu(y)

    N = GDN_GROUP * C
    rn = lax.broadcasted_iota(jnp.int32, (N, N), 0)
    cn = lax.broadcasted_iota(jnp.int32, (N, N), 1)
    same = (rn // C) == (cn // C)
    incl = jnp.logical_and(same, rn >= cn)
    strict = jnp.logical_and(same, rn > cn)
    eye_f = (rn == cn).astype(f32)
    n_sq = max(1, (C - 1).bit_length() - 1)
    groups = [list(range(g0, g0 + GDN_GROUP)) for g0 in range(0, GDN_HEADS, GDN_GROUP)]
    stack = lambda xs: jnp.concatenate(xs, axis=0)

    qs = [stack([_l2n(conv(h * GDN_DK)) * (GDN_DK ** -0.5) for h in hs]) for hs in groups]
    ks = [stack([_l2n(conv(GDN_HEADS * GDN_DK + h * GDN_DK)) for h in hs]) for hs in groups]
    vs = [stack([conv(2 * GDN_HEADS * GDN_DK + h * GDN_DV) for h in hs]) for hs in groups]
    gcs = [stack([gc_all[:, SM_A + h:SM_A + h + 1] for h in hs]) for hs in groups]
    betas = [stack([beta_all[:, SM_B + h:SM_B + h + 1] for h in hs]) for hs in groups]
    gc_rows = [jnp.concatenate([gc_t[SM_A + h:SM_A + h + 1, :] for h in hs], axis=1) for hs in groups]
    egs = [jnp.exp(gc) for gc in gcs]
    decays = [jnp.where(incl, jnp.exp(jnp.where(incl, gc - gr, 0.0)), 0.0) for gc, gr in zip(gcs, gc_rows)]
    kbs = [k.astype(bf16) for k in ks]
    a_s = [jnp.where(strict, b * _dot_nt(kb, kb) * dc, 0.0) for b, kb, dc in zip(betas, kbs, decays)]
    qks = [(_dot_nt(q.astype(bf16), kb) * dc).astype(bf16) for q, kb, dc in zip(qs, kbs, decays)]

    invs = [eye_f - a for a in a_s]
    pws = [a.astype(bf16) for a in a_s]
    for _ in range(n_sq):
        pws = [_dot(p, p).astype(bf16) for p in pws]
        invs = [inv + _dot(inv.astype(bf16), p) for inv, p in zip(invs, pws)]
    inv_s = [_split3(inv) for inv in invs]
    res = [eye_f - inv - _mm3(_split3(a), sp) for a, inv, sp in zip(a_s, invs, inv_s)]
    invs = [inv + _dot(sp[0], r.astype(bf16)) for inv, sp, r in zip(invs, inv_s, res)]
    rhs = [jnp.concatenate([b * v, (b * eg) * k], axis=1) for b, v, eg, k in zip(betas, vs, egs, ks)]
    sols = [_mm3(_split3(inv), _split3(r)) for inv, r in zip(invs, rhs)]

    for gi, hs in enumerate(groups):
        sol, q, k, gc, eg = sols[gi], qs[gi], ks[gi], gcs[gi], egs[gi]
        q_dec = (q * eg).astype(bf16)
        w_k = sol[:, GDN_DV:].astype(bf16)
        rows = [slice(j * C, (j + 1) * C) for j in range(GDN_GROUP)]
        s_old = [s_ref[h] for h in hs]
        ws = [_dot(jnp.concatenate([w_k[r], q_dec[r]], axis=0), s.astype(bf16)) for r, s in zip(rows, s_old)]
        u = stack([sol[r, :GDN_DV] - w[:C] for r, w in zip(rows, ws)])
        ub = u.astype(bf16)
        o_intra = _dot(qks[gi], ub)
        for j, h in enumerate(hs):
            r = rows[j]
            gl = gc[r][C - 1:C, :]
            k_end = (k[r] * jnp.exp(gl - gc[r])).astype(bf16)
            s_ref[h] = s_old[j] * jnp.exp(gl) + _dot_tn(k_end, ub[r])
            o = ws[j][C:] + o_intra[r]
            zh = z_ref[:, h * GDN_DV:(h + 1) * GDN_DV]
            o_ref[:, h * GDN_DV:(h + 1) * GDN_DV] = _gated_norm(o, gn_ref[...], zh).astype(bf16)

    @pl.when(c == pl.num_programs(1) - 1)
    def _():
        sfin_ref[...] = s_ref[...]


def _gdn_vecs(a_log, dt_bias, g_gdn_norm):
    alog = jnp.zeros((1, LANES), f32).at[0, SM_A:SM_A + GDN_HEADS].set(a_log)
    dtb = jnp.zeros((1, LANES), f32).at[0, SM_A:SM_A + GDN_HEADS].set(dt_bias)
    return alog, dtb, g_gdn_norm.reshape(1, GDN_DV)


def _gdn_prompt(y, y_small, w_conv, a_log, dt_bias, g_gdn_norm, nb, t):
    C = GDN_CHUNK
    nc = t // C
    alog, dtb, gn = _gdn_vecs(a_log, dt_bias, g_gdn_norm)
    cst = lambda b, c: (0, 0)
    return pl.pallas_call(
        _gdn_prompt_kernel,
        grid=(nb, nc),
        in_specs=[pl.BlockSpec((C, QKV_W), lambda b, c: (b * nc + c, 0)),
                  pl.BlockSpec((SUBLANES, QKV_W),
                               lambda b, c: (jnp.maximum((b * nc + c) * (C // SUBLANES) - 1, 0), 0)),
                  pl.BlockSpec((C, Z_W), lambda b, c: (b * nc + c, OFF_Z // Z_W)),
                  pl.BlockSpec((C, LANES), lambda b, c: (b * nc + c, 0)),
                  pl.BlockSpec((GDN_CONV, QKV_W), cst),
                  pl.BlockSpec((1, LANES), cst),
                  pl.BlockSpec((1, LANES), cst),
                  pl.BlockSpec((1, GDN_DV), cst)],
        out_specs=[pl.BlockSpec((C, Z_W), lambda b, c: (b * nc + c, 0)),
                   pl.BlockSpec((None, GDN_HEADS, GDN_DK, GDN_DV), lambda b, c: (b, 0, 0, 0))],
        out_shape=[jax.ShapeDtypeStruct((nb * t, Z_W), bf16),
                   jax.ShapeDtypeStruct((nb, GDN_HEADS, GDN_DK, GDN_DV), f32)],
        scratch_shapes=[pltpu.VMEM((GDN_HEADS, GDN_DK, GDN_DV), f32)],
        compiler_params=_params(None, 2),
        name="gdn_prompt",
    )(y, y, y, y_small, w_conv, alog, dtb, gn)


def _gdn_sample_kernel(qkv_ref, buf_ref, z_ref, sm_ref, sin_ref, wc_ref, alog_ref, dtb_ref, gn_ref,
                       o_ref, sout_ref, q_s, k_s, v_s, a_s, b_s, o_s):
    ts, G, _ = qkv_ref.shape
    nbuf = GDN_CONV - 1

    for h in range(GDN_HEADS):
        for part, dst in ((0, q_s), (1, k_s), (2, v_s)):
            col = part * GDN_HEADS * GDN_DK + h * GDN_DK
            w = wc_ref[:, col:col + LANES]
            rows = [buf_ref[i, :, col:col + LANES] for i in range(nbuf)]
            rows += [qkv_ref[t, :, col:col + LANES] for t in range(ts)]
            for t in range(ts):
                y = rows[t] * w[0:1]
                for i in range(1, GDN_CONV):
                    y = y + rows[t + i] * w[i:i + 1]
                y = _silu(y)
                if part == 0:
                    y = _l2n(y) * (GDN_DK ** -0.5)
                elif part == 1:
                    y = _l2n(y)
                dst[t, :, h * LANES:(h + 1) * LANES] = y

    for t in range(ts):
        g_all, beta_all = _gdn_gates(sm_ref[t], alog_ref[...], dtb_ref[...])
        a_all = jnp.exp(g_all)
        for h in range(GDN_HEADS):
            a_s[t, :, h * LANES:(h + 1) * LANES] = jnp.broadcast_to(a_all[:, SM_A + h:SM_A + h + 1], (G, LANES))
            b_s[t, :, h * LANES:(h + 1) * LANES] = jnp.broadcast_to(beta_all[:, SM_B + h:SM_B + h + 1], (G, LANES))

    nv = 2 * ts
    sr = lax.broadcasted_iota(jnp.int32, (3 * nv, nv * LANES), 0)
    sc = lax.broadcasted_iota(jnp.int32, (3 * nv, nv * LANES), 1)
    spread = (sr % nv == sc // LANES).astype(bf16)

    def head(h, carry):
        cols = pl.ds(pl.multiple_of(h * LANES, LANES), LANES)
        for i in range(G):
            row = slice(i, i + 1)
            kq = jnp.concatenate([k_s[t, row, cols] for t in range(ts)]
                                 + [q_s[t, row, cols] for t in range(ts)], axis=0)
            hi = kq.astype(bf16).astype(f32)
            mid = (kq - hi).astype(bf16).astype(f32)
            lo = (kq - hi) - mid
            kq_b = _dot(jnp.concatenate([hi, mid, lo], axis=0).T.astype(bf16), spread)
            s = sin_ref[i, h]
            for t in range(ts):
                kc = kq_b[:, t * LANES:(t + 1) * LANES]
                qc = kq_b[:, (ts + t) * LANES:(ts + t + 1) * LANES]
                a = a_s[t, row, cols]
                b = b_s[t, row, cols]
                ks = jnp.sum(s * kc, axis=0, keepdims=True)
                r = b * (v_s[t, row, cols] - a * ks)
                s = a * s + kc * r
                o_s[t, row, cols] = jnp.sum(s * qc, axis=0, keepdims=True)
            sout_ref[i, h] = s
        return carry

    lax.fori_loop(0, GDN_HEADS, head, 0)

    for t in range(ts):
        for h in range(GDN_HEADS):
            sl = slice(h * GDN_DV, (h + 1) * GDN_DV)
            o_ref[t, :, sl] = _gated_norm(o_s[t, :, sl], gn_ref[...], z_ref[t, :, sl]).astype(bf16)


def _gdn_sample(y3, y3_small, buf3, state, w_conv, a_log, dt_bias, g_gdn_norm):
    ts, db, _ = y3.shape
    G = SUBLANES
    alog, dtb, gn = _gdn_vecs(a_log, dt_bias, g_gdn_norm)
    cst = lambda g: (0, 0)
    st_spec = pl.BlockSpec((G, GDN_HEADS, GDN_DK, GDN_DV), lambda g: (g, 0, 0, 0))
    scr = pltpu.VMEM((ts, G, Z_W), f32)
    return pl.pallas_call(
        _gdn_sample_kernel,
        grid=(db // G,),
        in_specs=[pl.BlockSpec((ts, G, QKV_W), lambda g: (0, g, 0)),
                  pl.BlockSpec((GDN_CONV - 1, G, QKV_W), lambda g: (0, g, 0)),
                  pl.BlockSpec((ts, G, Z_W), lambda g: (0, g, OFF_Z // Z_W)),
                  pl.BlockSpec((ts, G, LANES), lambda g: (0, g, 0)),
                  st_spec,
                  pl.BlockSpec((GDN_CONV, QKV_W), cst),
                  pl.BlockSpec((1, LANES), cst),
                  pl.BlockSpec((1, LANES), cst),
                  pl.BlockSpec((1, GDN_DV), cst)],
        out_specs=[pl.BlockSpec((ts, G, Z_W), lambda g: (0, g, 0)), st_spec],
        out_shape=[jax.ShapeDtypeStruct((ts, db, Z_W), bf16),
                   jax.ShapeDtypeStruct(state.shape, f32)],
        scratch_shapes=[scr, scr, scr, scr, scr, scr],
        compiler_params=_params(48),
        name="gdn_sample",
    )(y3, buf3, y3, y3_small, state, w_conv, alog, dtb, gn)


def _rope_tables(pos, rot, width):
    half = rot // 2
    inv_freq = ROPE_THETA ** (-jnp.arange(half, dtype=f32) * (2.0 / rot))
    ang = pos.astype(f32)[:, None] * inv_freq[None, :]
    cos, sin = jnp.cos(ang), jnp.sin(ang)
    n = pos.shape[0]
    z = lambda w: jnp.zeros((n, w), f32)
    cosw = jnp.concatenate([cos, cos, jnp.ones((n, width - rot), f32)], axis=1)
    sina = jnp.concatenate([-sin, z(width - half)], axis=1)
    sinb = jnp.concatenate([z(half), sin, z(width - rot)], axis=1)
    reps = LANES // width
    return tuple(jnp.tile(a, (1, reps)) for a in (cosw, sina, sinb))


def _rope_kernel(dq_ref, dk_ref, iq_ref, sm_ref, c1, sa1, sb1, c2, sa2, sb2,
                 dq_o, dk_o, iq_o, ik_o, sm_o):
    h1 = DSA_HEAD_DIM // ROPE_FRACTION // 2
    h2 = IDX_DIM // ROPE_FRACTION // 2

    def rot(x, c, sa, sb, half):
        return x * c[...] + pltpu.roll(x, LANES - half, 1) * sa[...] + pltpu.roll(x, half, 1) * sb[...]

    for j in range(DQ_W // LANES):
        sl = slice(j * LANES, (j + 1) * LANES)
        dq_o[:, sl] = rot(dq_ref[:, sl], c1, sa1, sb1, h1).astype(bf16)
    for j in range(DKV_W // LANES):
        sl = slice(j * LANES, (j + 1) * LANES)
        dk_o[:, sl] = rot(dk_ref[:, sl], c1, sa1, sb1, h1)
    for j in range(IQ_W // LANES):
        sl = slice(j * LANES, (j + 1) * LANES)
        iq_o[:, sl] = rot(iq_ref[:, sl], c2, sa2, sb2, h2).astype(bf16)
    sm = sm_ref[...]
    ik_o[...] = rot(sm, c2, sa2, sb2, h2)[:, :IDX_DIM]
    sm_o[...] = sm


def _rope(y, y_small, pos, tm, group, db=None):
    m = y.shape[0]
    t1 =_rope_tables(pos, DSA_HEAD_DIM // ROPE_FRACTION, DSA_HEAD_DIM)
    t2 = _rope_tables(pos, IDX_DIM // ROPE_FRACTION, IDX_DIM)
    tab_blocks = pos.shape[0] // tm
    tab = pl.BlockSpec((tm, LANES), lambda i: (i % tab_blocks, 0))
    if group == "prompt":
        omap = lambda i: (i, 0)
        rows = lambda w: m
        cols = lambda w: w
    else:
        ts = m // db
        omap = lambda i: (0, i)
        rows = lambda w: db
        cols = lambda w: ts * w
    out = lambda w, dt: jax.ShapeDtypeStruct((rows(w), cols(w)), dt)
    return pl.pallas_call(
        _rope_kernel,
        grid=(m // tm,),
        in_specs=[pl.BlockSpec((tm, DQ_W), lambda i: (i, OFF_DQ // DQ_W)),
                  pl.BlockSpec((tm, DKV_W), lambda i: (i, OFF_DK // DKV_W)),
                  pl.BlockSpec((tm, IQ_W), lambda i: (i, OFF_IQ // IQ_W)),
                  pl.BlockSpec((tm, LANES), lambda i: (i, 0)),
                  tab, tab, tab, tab, tab, tab],
        out_specs=[pl.BlockSpec((tm, DQ_W), omap),
                   pl.BlockSpec((tm, DKV_W), omap),
                   pl.BlockSpec((tm, IQ_W), omap),
                   pl.BlockSpec((tm, IDX_DIM), lambda i: (i, 0)),
                   pl.BlockSpec((tm, LANES), omap)],
        out_shape=[out(DQ_W, bf16), out(DKV_W, f32), out(IQ_W, bf16),
                   jax.ShapeDtypeStruct((m, IDX_DIM), f32), out(LANES, f32)],
        compiler_params=_params(),
        name="rope_" + group,
    )(y, y, y, y_small, *t1, *t2)


def _topk_bias(x_s, bias_s, lo0, mx, few, nkt, kt_w, k):
    R = x_s.shape[0]
    kf = float(k)
    tiles = [slice(kt * kt_w, (kt + 1) * kt_w) for kt in range(nkt)]
    n_grp = 2 if R % (2 * SUBLANES) == 0 else 1
    grps = [slice(g * (R // n_grp), (g + 1) * (R // n_grp)) for g in range(n_grp)]

    def count_ge(rows, th):
        acc = jnp.where(x_s[rows, tiles[0]] >= th, 1.0, 0.0)
        for t in tiles[1:]:
            acc = acc + jnp.where(x_s[rows, t] >= th, 1.0, 0.0)
        return jnp.sum(acc, axis=1, keepdims=True)

    def bisect(rows, lo, hi, done):
        mid = 0.5 * lo + 0.5 * hi
        cnt = count_ge(rows, mid)
        collapsed = jnp.logical_or(mid <= lo, mid >= hi)
        live = jnp.logical_and(done < 0.5, jnp.logical_not(collapsed))
        lo = jnp.where(jnp.logical_and(live, cnt >= kf), mid, lo)
        hi = jnp.where(jnp.logical_and(live, cnt <= kf), mid, hi)
        done = jnp.where(jnp.logical_or(collapsed, cnt == kf), 1.0, done)
        return lo, hi, done

    def snap(rows, lo, hi, done):
        a = b = None
        for t in tiles:
            x = x_s[rows, t]
            at = jnp.where(x >= lo, x, POS)
            bt = jnp.where(x < hi, x, NEG)
            a = at if a is None else jnp.minimum(a, at)
            b = bt if b is None else jnp.maximum(b, bt)
        a = jnp.min(a, axis=1, keepdims=True)
        b = jnp.max(b, axis=1, keepdims=True)
        live = done < 0.5
        return jnp.where(live, a, lo), jnp.where(jnp.logical_and(live, a >= b), 1.0, done)

    def cond(c):
        it, st = c
        left = st[0][2]
        for s in st[1:]:
            left = jnp.minimum(left, s[2])
        return jnp.logical_and(it < BISECT_CAP, jnp.min(left) < 0.5)

    def body(c):
        it, st = c
        for _ in range(BISECT_UNROLL):
            st = [bisect(rows, *s) for rows, s in zip(grps, st)]

        def snapped():
            out = []
            for rows, (lo, hi, done) in zip(grps, st):
                lo2, done2 = snap(rows, lo, hi, done)
                out.append((lo2, hi, done2))
            return out

        st = lax.cond(it >= SNAP_FROM, snapped, lambda: st)
        return it + 1, st

    hi0 = jnp.where(few, lo0, mx + (jnp.abs(mx) + 1.0))
    st0 = [(lo0[rows], hi0[rows], few[rows].astype(f32)) for rows in grps]
    _, st = lax.while_loop(cond, body, (jnp.int32(0), st0))
    lo = jnp.concatenate([s[0] for s in st], axis=0)
    hi = jnp.concatenate([s[1] for s in st], axis=0)

    has_run = jnp.max(jnp.where(lo < hi, 1.0, 0.0)) > 0.5
    rows_all = slice(0, R)

    @pl.when(jnp.logical_not(has_run))
    def _():
        for t in tiles:
            bias_s[:, t] = jnp.where(x_s[:, t] >= hi, 0.0, NEG)

    @pl.when(has_run)
    def _():
        need = kf - count_ge(rows_all, hi)
        ai = lax.broadcasted_iota(jnp.int32, (kt_w, kt_w), 0)
        bi = lax.broadcasted_iota(jnp.int32, (kt_w, kt_w), 1)
        before = (ai < bi).astype(bf16)
        seen = jnp.zeros((R, 1), f32)
        for t in tiles:
            x = x_s[:, t]
            run = jnp.logical_and(x >= lo, x < hi)
            runf = run.astype(f32)
            rank = seen + _dot(runf.astype(bf16), before)
            take = jnp.logical_or(x >= hi, jnp.logical_and(run, rank < need))
            bias_s[:, t] = jnp.where(take, 0.0, NEG)
            seen = seen + jnp.sum(runf, axis=1, keepdims=True)


def _dsa_prompt_kernel(iq_ref, sm_ref, ik_ref, q_ref, k_ref, v_ref, o_ref,
                       ikb, kb, vb, x_s, bias_s, s_s, *, topk, kt_w, ext_w):
    qi = pl.program_id(1)
    R = iq_ref.shape[0]
    T = ik_ref.shape[0]

    @pl.when(qi == 0)
    def _():
        ikb[...] = ik_ref[...].astype(bf16)
        kb[...] = k_ref[...].astype(bf16)
        vb[...] = v_ref[...].astype(bf16)

    def block(ext):
        tiles = [slice(j * kt_w, (j + 1) * kt_w) for j in range(ext // kt_w)]
        wgt = sm_ref[...] * IDX_SCALE
        qpos = qi * R + lax.broadcasted_iota(jnp.int32, (R, kt_w), 0)
        key0 = lax.broadcasted_iota(jnp.int32, (R, kt_w), 1)
        mn = mx = None
        for j, t in enumerate(tiles):
            keys = ikb[t, :]
            sc = jnp.zeros((R, kt_w), f32)
            for h in range(IDX_HEADS):
                rel = jnp.maximum(_dot_nt(iq_ref[:, h * IDX_DIM:(h + 1) * IDX_DIM], keys), 0.0)
                sc = sc + rel * wgt[:, SM_IW + h:SM_IW + h + 1]
            causal = key0 + j * kt_w <= qpos
            x_s[:, t] = jnp.where(causal, sc, NEG)
            lo_t = jnp.where(causal, sc, POS)
            hi_t = jnp.where(causal, sc, NEG)
            mn = lo_t if mn is None else jnp.minimum(mn, lo_t)
            mx = hi_t if mx is None else jnp.maximum(mx, hi_t)
        n_causal = qi * R + lax.broadcasted_iota(jnp.int32, (R, 1), 0) + 1
        _topk_bias(x_s, bias_s, jnp.min(mn, axis=1, keepdims=True), jnp.max(mx, axis=1, keepdims=True),
                   n_causal <= topk, len(tiles), kt_w, topk)

        def heads(hp, carry):
            h0 = hp * DSA_HEAD_PAIR
            g = h0 // (DSA_HEADS // DSA_KV_HEADS)
            gc = pl.ds(pl.multiple_of(g * DSA_HEAD_DIM, DSA_HEAD_DIM), DSA_HEAD_DIM)
            hcs = [pl.ds(pl.multiple_of((h0 + i) * DSA_HEAD_DIM, DSA_HEAD_DIM), DSA_HEAD_DIM)
                   for i in range(DSA_HEAD_PAIR)]
            qs = [q_ref[:, hc] for hc in hcs]
            mxa = [None] * DSA_HEAD_PAIR
            for t in tiles:
                kt = kb[t, gc]
                bt = bias_s[:, t]
                for i in range(DSA_HEAD_PAIR):
                    s = _dot_nt(qs[i], kt) * (DSA_HEAD_DIM ** -0.5) + bt
                    s_s[i, :, t] = s
                    mxa[i] = s if mxa[i] is None else jnp.maximum(mxa[i], s)
            ms = [jnp.max(a, axis=1, keepdims=True) for a in mxa]
            la = [jnp.zeros((R, kt_w), f32)] * DSA_HEAD_PAIR
            acc = [jnp.zeros((R, DSA_HEAD_DIM), f32)] * DSA_HEAD_PAIR
            for t in tiles:
                vt = vb[t, gc]
                for i in range(DSA_HEAD_PAIR):
                    p = jnp.exp(s_s[i, :, t] - ms[i])
                    la[i] = la[i] + p
                    acc[i] = acc[i] + _dot(p.astype(bf16), vt)
            for i in range(DSA_HEAD_PAIR):
                o_ref[:, hcs[i]] = (acc[i] / jnp.sum(la[i], axis=1, keepdims=True)).astype(bf16)
            return carry

        lax.fori_loop(0, DSA_HEADS // DSA_HEAD_PAIR, heads, 0)

    for e in range(T // ext_w):
        pl.when(qi // (ext_w // R) == e)(functools.partial(block, (e + 1) * ext_w))


def _dsa_prompt(iq, sm, ik, dq, dk, y, nb, t, topk):
    R = LANES
    nq = t // R
    kt_w = min(DSA_KEY_TILE, t)
    ext_w = min(DSA_KEY_EXTENT, t)
    kern = functools.partial(_dsa_prompt_kernel, topk=topk, kt_w=kt_w, ext_w=ext_w)
    return pl.pallas_call(
        kern,
        grid=(nb, nq),
        in_specs=[pl.BlockSpec((R, IQ_W), lambda b, i: (b * nq + i, 0)),
                  pl.BlockSpec((R, LANES), lambda b, i: (b * nq + i, 0)),
                  pl.BlockSpec((t, IDX_DIM), lambda b, i: (b, 0)),
                  pl.BlockSpec((R, DQ_W), lambda b, i: (b * nq + i, 0)),
                  pl.BlockSpec((t, DKV_W), lambda b, i: (b, 0)),
                  pl.BlockSpec((t, DKV_W), lambda b, i: (b, OFF_DV // DKV_W))],
        out_specs=pl.BlockSpec((R, DQ_W), lambda b, i: (b * nq + i, 0)),
        out_shape=jax.ShapeDtypeStruct((nb * t, DQ_W), bf16),
        scratch_shapes=[pltpu.VMEM((t, IDX_DIM), bf16), pltpu.VMEM((t, DKV_W), bf16),
                        pltpu.VMEM((t, DKV_W), bf16), pltpu.VMEM((R, t), f32), pltpu.VMEM((R, t), f32),
                        pltpu.VMEM((DSA_HEAD_PAIR, R, t), f32)],
        compiler_params=_params(48, 2),
        name="dsa_prompt",
    )(iq, sm, ik, dq, dk, y)


def _idx_score_kernel(pt_ref, iq_ref, sm_ref, ikn_ref, *rest, n_pages, page):
    eb, tp, _ = iq_ref.shape
    o_ref = rest[eb * n_pages]
    ts = o_ref.shape[1]
    past = n_pages * page
    lp = o_ref.shape[2]
    for e in range(eb):
        pages = rest[e * n_pages:(e + 1) * n_pages]
        keys_t = jnp.concatenate([p[...] for p in pages], axis=1).astype(bf16)
        keys_n = jnp.concatenate([ikn_ref[e], jnp.zeros((lp - past - tp, IDX_DIM), f32)], axis=0).astype(bf16)
        wgt = sm_ref[e] * IDX_SCALE
        iq = iq_ref[e].astype(bf16)
        score = jnp.zeros((tp, lp), f32)
        for h in range(IDX_HEADS):
            iqh = iq[:, h * IDX_DIM:(h + 1) * IDX_DIM]
            rel = jnp.maximum(jnp.concatenate([_dot(iqh, keys_t), _dot_nt(iqh, keys_n)], axis=1), 0.0)
            score = score + rel * wgt[:, SM_IW + h:SM_IW + h + 1]
        o_ref[e] = score[:ts]


def _idx_scores(page_table, iq, sm, ikn, cache_idx, lp, ts):
    db, tp, _ = iq.shape
    n_pages = page_table.shape[1]
    page = cache_idx.shape[2]
    eb = IDX_BATCH
    kern = functools.partial(_idx_score_kernel, n_pages=n_pages, page=page)
    page_specs = [pl.BlockSpec((None, IDX_DIM, page),
                               functools.partial(lambda b, pt, e, p: (pt[b * eb + e, p], 0, 0), e=e, p=p))
                  for e in range(eb) for p in range(n_pages)]
    grid_spec = pltpu.PrefetchScalarGridSpec(
        num_scalar_prefetch=1,
        grid=(db // eb,),
        in_specs=[pl.BlockSpec((eb, tp, IQ_W), lambda b, pt: (b, 0, 0)),
                  pl.BlockSpec((eb, tp, LANES), lambda b, pt: (b, 0, 0)),
                  pl.BlockSpec((eb, tp, IDX_DIM), lambda b, pt: (b, 0, 0))] + page_specs,
        out_specs=pl.BlockSpec((eb, ts, lp), lambda b, pt: (b, 0, 0)),
    )
    return pl.pallas_call(
        kern, grid_spec=grid_spec,
        out_shape=jax.ShapeDtypeStruct((db, ts, lp), f32),
        compiler_params=_params(),
        name="idx_scores_sample",
    )(page_table, iq, sm, ikn, *([cache_idx] * (eb * n_pages)))


def _select_sample_kernel(x_ref, bias_ref, x_s, *, topk, past, ts):
    R, lp = x_ref.shape
    key = lax.broadcasted_iota(jnp.int32, (R, lp), 1)
    t = lax.broadcasted_iota(jnp.int32, (R, lp), 0) % ts
    causal = key <= past + t
    x = x_ref[...]
    x_s[...] = jnp.where(causal, x, NEG)
    lo0 = jnp.min(jnp.where(causal, x, POS), axis=1, keepdims=True)
    mx = jnp.max(jnp.where(causal, x, NEG), axis=1, keepdims=True)
    n_causal = past + lax.broadcasted_iota(jnp.int32, (R, 1), 0) % ts + 1
    _topk_bias(x_s, bias_ref, lo0, mx, n_causal <= topk, lp // LANES, LANES, topk)


def _select_sample(scores, topk, past, ts):
    m, lp = scores.shape
    R = LANES
    kern = functools.partial(_select_sample_kernel, topk=topk, past=past, ts=ts)
    return pl.pallas_call(
        kern, grid=(m // R,),
        in_specs=[pl.BlockSpec((R, lp), lambda i: (i, 0))],
        out_specs=pl.BlockSpec((R, lp), lambda i: (i, 0)),
        out_shape=jax.ShapeDtypeStruct((m, lp), f32),
        scratch_shapes=[pltpu.VMEM((R, lp), f32)],
        compiler_params=_params(),
        name="select_sample",
    )(scores)


def _dsa_sample_kernel(pt_ref, q_ref, sel_ref, kn_ref, vn_ref, *rest, n_pages, page):
    eb, tp, _ = q_ref.shape
    o_ref = rest[2 * eb * n_pages]
    ts = sel_ref.shape[1]
    past = n_pages * page
    lp = sel_ref.shape[2]
    hpg = DSA_HEADS // DSA_KV_HEADS
    padn = jnp.zeros((lp - past - tp, DSA_HEAD_DIM), f32)
    for e in range(eb):
        kp = rest[e * n_pages:(e + 1) * n_pages]
        vp = rest[(eb + e) * n_pages:(eb + e + 1) * n_pages]
        bias = jnp.concatenate([sel_ref[e], jnp.zeros((tp - ts, lp), f32)], axis=0)
        bias = jnp.concatenate([bias] * hpg, axis=0)
        for g in range(DSA_KV_HEADS):
            gsl = slice(g * DSA_HEAD_DIM, (g + 1) * DSA_HEAD_DIM)
            kg = jnp.concatenate([p[pl.ds(g, page, stride=DSA_KV_HEADS), :] for p in kp]
                                 + [kn_ref[e, :, gsl], padn], axis=0).astype(bf16)
            vg = jnp.concatenate([p[pl.ds(g, page, stride=DSA_KV_HEADS), :] for p in vp]
                                 + [vn_ref[e, :, gsl], padn], axis=0).astype(bf16)
            qg = jnp.concatenate([q_ref[e, :, (g * hpg + j) * DSA_HEAD_DIM:(g * hpg + j + 1) * DSA_HEAD_DIM]
                                  for j in range(hpg)], axis=0).astype(bf16)
            s = _dot_nt(qg, kg) * (DSA_HEAD_DIM ** -0.5) + bias
            m = jnp.max(s, axis=1, keepdims=True)
            p = jnp.exp(s - m)
            l = jnp.sum(p, axis=1, keepdims=True)
            o = _dot(p.astype(bf16), vg) / l
            for j in range(hpg):
                hsl = slice((g * hpg + j) * DSA_HEAD_DIM, (g * hpg + j + 1) * DSA_HEAD_DIM)
                o_ref[e, :, hsl] = o[j * tp:j * tp + ts]


def _dsa_sample(page_table, dq, sel, kn, vn, ck, cv):
    db, tp, _ = dq.shape
    ts = sel.shape[1]
    lp = sel.shape[2]
    n_pages = page_table.shape[1]
    rows = ck.shape[1]
    page = rows // DSA_KV_HEADS
    eb = DSA_SAMPLE_BATCH
    kern = functools.partial(_dsa_sample_kernel, n_pages=n_pages, page=page)
    pspec = [pl.BlockSpec((None, rows, DSA_HEAD_DIM),
                          functools.partial(lambda b, pt, e, p: (pt[b * eb + e, p], 0, 0), e=e, p=p))
             for e in range(eb) for p in range(n_pages)]
    per_b = lambda r, w: pl.BlockSpec((eb, r, w), lambda b, pt: (b, 0, 0))
    grid_spec = pltpu.PrefetchScalarGridSpec(
        num_scalar_prefetch=1,
        grid=(db // eb,),
        in_specs=[per_b(tp, DQ_W), per_b(ts, lp), per_b(tp, DKV_W), per_b(tp, DKV_W)] + pspec + pspec,
        out_specs=per_b(ts, DQ_W),
    )
    return pl.pallas_call(
        kern, grid_spec=grid_spec,
        out_shape=jax.ShapeDtypeStruct((db, ts, DQ_W), f32),
        compiler_params=_params(48),
        name="dsa_sample",
    )(page_table, dq, sel, kn, vn, *([ck] * (eb * n_pages)), *([cv] * (eb * n_pages)))


def _merge_kernel(og_ref, od_ref, gg_ref, gd_ref, x_ref, gt_ref, sc_ref, sh_ref, g2_ref,
                  wg_ref, wd_ref, wo_ref, x1_ref, h2_ref):
    mix = (_sigmoid(gg_ref[...]) * _dot(og_ref[...], wg_ref[...])
           + _sigmoid(gd_ref[...]) * _dot(od_ref[...], wd_ref[...]))
    x1 = x_ref[...] + gt_ref[...] * _dot(mix.astype(bf16), wo_ref[...])
    x1_ref[...] = x1
    y = x1 * lax.rsqrt(jnp.mean(x1 * x1, axis=-1, keepdims=True) + NORM_EPS)
    h2_ref[...] = ((y * g2_ref[...]) * (1.0 + sc_ref[...]) + sh_ref[...]).astype(bf16)


def _merge(o_gdn, o_dsa, y, x, ada, g2, wg, wd, wo, group, rows_per_batch, tm):
    m, d = x.shape
    row = lambda w, blk=0: pl.BlockSpec((tm, w), lambda i: (i, blk))
    res = lambda a: pl.BlockSpec(a.shape, lambda i: (0, 0), pipeline_mode=pl.Buffered(1))
    return pl.pallas_call(
        _merge_kernel,
        grid=(m // tm,),
        in_specs=[row(Z_W), row(DQ_W), row(d, 0), row(d, 1), row(d),
                  _mod_spec(group, d, rows_per_batch, tm, 2),
                  _mod_spec(group, d, rows_per_batch, tm, 4),
                  _mod_spec(group, d, rows_per_batch, tm, 3),
                  pl.BlockSpec((1, d), lambda i: (0, 0)),
                  res(wg), res(wd), res(wo)],
        out_specs=[row(d), row(d)],
        out_shape=[jax.ShapeDtypeStruct((m, d), f32), jax.ShapeDtypeStruct((m, d), bf16)],
        compiler_params=_params(48),
        name="merge_" + group,
    )(o_gdn, o_dsa, y, y, x, ada, ada, ada, g2.reshape(1, d), wg, wd, wo)


def _ffn_epilogue(acc_ref, x1_ref, gt_ref, gf_ref, y_ref):
    tm = acc_ref.shape[0]
    r = gt_ref.shape[0] if gt_ref.shape[0] > 1 else tm
    for s0 in range(0, tm, r):
        sl = slice(s0, s0 + r)
        x2 = x1_ref[sl, :] + gt_ref[...] * acc_ref[sl, :]
        y_ref[sl, :] = (x2 * lax.rsqrt(jnp.mean(x2 * x2, axis=-1, keepdims=True) + NORM_EPS)) * gf_ref[...]


def _ffn_prompt_kernel(h_ref, wug_ref, wuv_ref, wcg_ref, wcv_ref, bg_ref, bv_ref, wd_ref,
                       x1_ref, gt_ref, gf_ref, y_ref, ug_ref, uv_ref, acc_ref, act_s, prev_s,
                       *, blocks_per_seq, n_up):
    i = pl.program_id(0)
    j = pl.program_id(1)
    tm = h_ref.shape[0]
    keep = (i % blocks_per_seq) > 0

    sr = min(FFN_SUB_ROWS, tm)

    def conv(u, prev, wc_ref, b_ref):
        ux = jnp.concatenate([prev, u], axis=0)
        w = wc_ref[...]
        y = u * w[FFN_CONV - 1:FFN_CONV] + b_ref[...]
        for t in range(FFN_CONV - 1):
            sft = FFN_CONV - 1 - t
            y = y + ux[SUBLANES - sft:SUBLANES - sft + sr] * w[t:t + 1]
        return y

    @pl.when(jnp.logical_and(i == 0, j == 0))
    def _():
        prev_s[...] = jnp.zeros_like(prev_s)

    def up(store):
        pg = jnp.where(keep, prev_s[j, 0], 0.0)
        pv = jnp.where(keep, prev_s[j, 1], 0.0)
        for s in range(tm // sr):
            rows = slice(s * sr, (s + 1) * sr)
            hs = h_ref[rows, :]
            ug = _dot(hs, wug_ref[...])
            uv = _dot(hs, wuv_ref[...])
            store(rows, _silu(conv(ug, pg, wcg_ref, bg_ref)) * conv(uv, pv, wcv_ref, bv_ref))
            pg = ug[sr - SUBLANES:]
            pv = uv[sr - SUBLANES:]
        ug_ref[...] = pg
        uv_ref[...] = pv
        prev_s[j, 0] = pg
        prev_s[j, 1] = pv

    _ffn_phases(j, n_up, up, act_s, wd_ref, acc_ref, x1_ref, gt_ref, gf_ref, y_ref)


def _ffn_phases(j, n_up, up, act_s, wd_ref, acc_ref, x1_ref, gt_ref, gf_ref, y_ref):
    tf = act_s.shape[1] // n_up
    tn = wd_ref.shape[1]

    @pl.when(j < n_up)
    def _():
        cols = pl.ds(pl.multiple_of(j * tf, tf), tf)

        def store(rows, act):
            act_s[rows, cols] = act.astype(bf16)
        up(store)

    @pl.when(j >= n_up)
    def _():
        acc_ref[:, pl.ds(pl.multiple_of((j - n_up) * tn, tn), tn)] = _dot(act_s[...], wd_ref[...])

    @pl.when(j == pl.num_programs(1) - 1)
    def _():
        _ffn_epilogue(acc_ref, x1_ref, gt_ref, gf_ref, y_ref)


def _ffn_sample_kernel(h_ref, buf_ref, wug_ref, wuv_ref, wcg_ref, wcv_ref, bg_ref, bv_ref, wd_ref,
                       x1_ref, gt_ref, gf_ref, y_ref, ug_ref, uv_ref, acc_ref, act_s, *, ts, n_up):
    j = pl.program_id(1)
    db = h_ref.shape[0] // ts
    nbuf = FFN_CONV - 1
    tf = wug_ref.shape[1]

    def branch(wu_ref, wc_ref, b_ref, ubuf_ref, half):
        u = _dot(h_ref[...], wu_ref[...])
        rows = [buf_ref[r, :, half * tf:(half + 1) * tf] for r in range(nbuf)]
        rows += [u[t * db:(t + 1) * db] for t in range(ts)]
        for r in range(nbuf):
            ubuf_ref[r] = rows[ts + r]
        w = wc_ref[...]
        outs = []
        for t in range(ts):
            y = rows[t] * w[0:1] + b_ref[...]
            for r in range(1, FFN_CONV):
                y = y + rows[t + r] * w[r:r + 1]
            outs.append(y)
        return jnp.concatenate(outs, axis=0)

    def up(store):
        store(slice(0, ts * db),
              _silu(branch(wug_ref, wcg_ref, bg_ref, ug_ref, 0)) * branch(wuv_ref, wcv_ref, bv_ref, uv_ref, 1))

    _ffn_phases(j, n_up, up, act_s, wd_ref, acc_ref, x1_ref, gt_ref, gf_ref, y_ref)


def _ffn(h2, x1, ada, g_final, w_up, w_conv, b_conv, w_down, group, rows_per_batch, tm, tf, buf=None):
    m, d = h2.shape
    dff = w_down.shape[0]
    nj = dff // tf
    tn = min(FFN_DOWN_TILE, d)
    b2 = b_conv.reshape(1, 2 * dff)
    up = lambda j: jnp.minimum(j, nj - 1)
    common_w = [pl.BlockSpec((d, tf), lambda i, j: (0, up(j))),
                pl.BlockSpec((d, tf), lambda i, j: (0, nj + up(j))),
                pl.BlockSpec((FFN_CONV, tf), lambda i, j: (0, up(j))),
                pl.BlockSpec((FFN_CONV, tf), lambda i, j: (0, nj + up(j))),
                pl.BlockSpec((1, tf), lambda i, j: (0, up(j))),
                pl.BlockSpec((1, tf), lambda i, j: (0, nj + up(j))),
                pl.BlockSpec((dff, tn), lambda i, j: (0, jnp.maximum(j - nj, 0)))]
    tail = [pl.BlockSpec((tm, d), lambda i, j: (i, 0)),
            _mod_spec(group, d, rows_per_batch, tm, 5),
            pl.BlockSpec((1, d), lambda i, j: (0, 0))]
    y_spec = pl.BlockSpec((tm, d), lambda i, j: (i, 0))
    if group == "prompt":
        bps = rows_per_batch // tm
        nb = m // rows_per_batch
        kern = functools.partial(_ffn_prompt_kernel, blocks_per_seq=bps, n_up=nj)
        first = [pl.BlockSpec((tm, d), lambda i, j: (i, 0))]
        ubuf_spec = pl.BlockSpec((None, SUBLANES, tf), lambda i, j: (i, 0, up(j)))
        ubuf_shape = jax.ShapeDtypeStruct((m // tm, SUBLANES, dff), f32)
        args = (h2,)
        extra_scratch = [pltpu.VMEM((nj, 2, SUBLANES, tf), f32)]
    else:
        extra_scratch = []
        ts = m // rows_per_batch
        kern = functools.partial(_ffn_sample_kernel, ts=ts, n_up=nj)
        first = [pl.BlockSpec((tm, d), lambda i, j: (i, 0)),
                 pl.BlockSpec((FFN_CONV - 1, rows_per_batch, 2 * tf), lambda i, j: (0, 0, up(j)))]
        ubuf_spec = pl.BlockSpec((FFN_CONV - 1, rows_per_batch, tf), lambda i, j: (0, 0, up(j)))
        ubuf_shape = jax.ShapeDtypeStruct((FFN_CONV - 1, rows_per_batch, dff), f32)
        args = (h2, buf)
    return pl.pallas_call(
        kern,
        grid=(m // tm, nj + d // tn),
        in_specs=first + common_w + tail,
        out_specs=[y_spec, ubuf_spec, ubuf_spec],
        out_shape=[jax.ShapeDtypeStruct((m, d), f32), ubuf_shape, ubuf_shape],
        scratch_shapes=[pltpu.VMEM((tm, d), f32), pltpu.VMEM((tm, dff), bf16)] + extra_scratch,
        compiler_params=_params(56, 2),
        name="ffn_" + group,
    )(*args, w_up, w_up, w_conv, w_conv, b2, b2, w_down, x1, ada, g_final.reshape(1, d))


def _split_w_in(w_in, d):
    sizes = (QKV_W, Z_W, GDN_HEADS, GDN_HEADS, DQ_W, DKV_W, DKV_W, IQ_W, IDX_DIM, IDX_HEADS, d, d)
    offs = [0]
    for s in sizes:
        offs.append(offs[-1] + s)
    seg = lambda i, j: w_in[:, offs[i]:offs[j]]
    pad = jnp.zeros((w_in.shape[0], LANES - (IDX_DIM + 2 * GDN_HEADS + IDX_HEADS)), w_in.dtype)
    small = jnp.concatenate([seg(8, 9), seg(2, 4), seg(9, 10), pad], axis=1)
    return tuple(w.astype(bf16) for w in (seg(0, 2), seg(4, 8), seg(10, 12), small))


def _in_proj(h, w_parts, tm, name):
    return tuple(_matmul(h, w, tm, _mm_tile(w.shape[1]), name + "_" + part)
                 for w, part in zip(w_parts, ("gdn", "dsa", "gate", "small")))


def _ffn_tile(dff):
    for tf in (512, 256, 128):
        if dff % tf == 0:
            return tf
    raise ValueError("d_ff must be a multiple of 128")


def _mm_tile(n):
    for tn in (1024, 512, 256, 128):
        if n % tn == 0:
            return tn
    raise ValueError("projection width must be a multiple of 128")


def kernel(x_prompt, x_sample, c_prompt, c_sample, cache_k, cache_v, cache_idx_k, page_table, state_gdn, state_gdn_conv, state_ffn_conv, w_ada, b_ada, g_norm1, w_in, w_gdn_conv, a_log, dt_bias, g_gdn_norm, w_gdn_out, w_dsa_out, w_o, g_norm2, w_up, w_ffn_conv, b_ffn_conv, w_down, g_final):
    nb, t, d = x_prompt.shape
    db, ts, _ = x_sample.shape
    depth = w_ada.shape[0]
    assert depth == 1 and db == LANES and ts >= GDN_CONV - 1
    n_pages = page_table.shape[1]
    page = cache_k.shape[2]
    past = n_pages * page
    dff = w_down.shape[1]
    l = 0

    n_c = nb + db
    pad_c = (-n_c) % SUBLANES
    c_all = jnp.concatenate([c_prompt, c_sample, jnp.zeros((pad_c, d), f32)], axis=0)
    ada = _ada(c_all, w_ada[l], b_ada[l])
    ada_p = ada[:nb].reshape(nb, 1, 6 * d)
    ada_s = ada[nb:nb + db]

    w_in_parts = _split_w_in(w_in[l], d)
    wg = w_gdn_out[l].astype(bf16)
    wd = w_dsa_out[l].astype(bf16)
    wo = w_o[l].astype(bf16)
    wup = w_up[l].astype(bf16)
    wdn = w_down[l].astype(bf16)
    tf = _ffn_tile(dff)

    xp = x_prompt.reshape(nb * t, d)
    tm_p = min(512, t)
    h1 = _prep(xp, g_norm1[l], ada_p, "prompt", t, tm_p)
    yp_gdn, yp_dsa, yp_gate, yp_small = _in_proj(h1, w_in_parts, min(IN_PROJ_ROWS, nb * t), "in_proj_prompt")
    og_p, s_p = _gdn_prompt(yp_gdn, yp_small, w_gdn_conv[l], a_log[l], dt_bias[l], g_gdn_norm[l], nb, t)
    dq_p, dk_p, iq_p, ik_p, sm_p = _rope(yp_dsa, yp_small, jnp.arange(t), tm_p, "prompt")
    od_p = _dsa_prompt(iq_p, sm_p, ik_p, dq_p, dk_p, yp_dsa, nb, t, min(DSA_TOPK, t // 4))
    x1_p, h2_p = _merge(og_p, od_p, yp_gate, xp, ada_p, g_norm2[l], wg, wd, wo, "prompt", t, min(256, t))
    y_p, ug_p, uv_p = _ffn(h2_p, x1_p, ada_p, g_final, wup, w_ffn_conv[l], b_ffn_conv[l], wdn,
                           "prompt", t, tm_p, tf)

    nfb = FFN_CONV - 1
    bps_p = t // tm_p
    out_p = (
        y_p.reshape(nb, t, d),
        dk_p.reshape(1, nb, t, DSA_KV_HEADS, DSA_HEAD_DIM),
        yp_dsa[:, OFF_DV:OFF_DV + DKV_W].reshape(1, nb, t, DSA_KV_HEADS, DSA_HEAD_DIM),
        ik_p.reshape(1, nb, t, IDX_DIM),
        s_p[None],
        yp_gdn.reshape(nb, t, -1)[:, t - (GDN_CONV - 1):, :QKV_W][None],
        jnp.concatenate([ug_p[bps_p - 1::bps_p, SUBLANES - nfb:], uv_p[bps_p - 1::bps_p, SUBLANES - nfb:]],
                        axis=-1)[None],
    )

    xs = x_sample.transpose(1, 0, 2).reshape(ts * db, d)
    h1s = _prep(xs, g_norm1[l], ada_s, "sample", db, db)
    ys_gdn, ys_dsa, ys_gate, ys_small = _in_proj(h1s, w_in_parts, ts * db, "in_proj_sample")
    ys_gdn3 = ys_gdn.reshape(ts, db, -1)
    og_s, s_s = _gdn_sample(ys_gdn3, ys_small.reshape(ts, db, LANES), state_gdn_conv[l].transpose(1, 0, 2),
                            state_gdn[l], w_gdn_conv[l], a_log[l], dt_bias[l], g_gdn_norm[l])
    pos_s = jnp.repeat(past + jnp.arange(ts), db)
    dq_s, dk_s, iq_s, ik_s, sm_s = _rope(ys_dsa, ys_small, pos_s, db, "sample", db)
    ik_s_b = ik_s.reshape(ts, db, IDX_DIM).transpose(1, 0, 2)
    lp = past + LANES
    tp = -(-ts // SUBLANES) * SUBLANES
    pad_t = lambda a: jnp.pad(a.astype(f32), ((0, 0), (0, tp - ts), (0, 0)))
    scores = _idx_scores(page_table, pad_t(iq_s.reshape(db, ts, IQ_W)), pad_t(sm_s.reshape(db, ts, LANES)),
                         pad_t(ik_s_b), jnp.swapaxes(cache_idx_k[l], 1, 2), lp, ts)
    sel = _select_sample(scores.reshape(db * ts, lp), min(DSA_TOPK, (past + ts) // 4), past, ts)
    dv_s_b = ys_dsa.reshape(ts, db, -1)[:, :, OFF_DV:OFF_DV + DKV_W].transpose(1, 0, 2)
    ck = cache_k[l].reshape(cache_k.shape[1], page * DSA_KV_HEADS, DSA_HEAD_DIM)
    cv = cache_v[l].reshape(cache_v.shape[1], page * DSA_KV_HEADS, DSA_HEAD_DIM)
    od_s = _dsa_sample(page_table, pad_t(dq_s.reshape(db, ts, DQ_W)), sel.reshape(db, ts, lp),
                       pad_t(dk_s.reshape(db, ts, DKV_W)), pad_t(dv_s_b), ck, cv)
    od_s = od_s.astype(bf16).transpose(1, 0, 2).reshape(ts * db, DQ_W)
    x1_s, h2_s = _merge(og_s.reshape(ts * db, Z_W), od_s, ys_gate, xs, ada_s, g_norm2[l], wg, wd, wo,
                        "sample", db, db)
    fb = state_ffn_conv[l].transpose(1, 0, 2)
    nj = dff // tf
    fb = jnp.concatenate([fb[:, :, :dff].reshape(nfb, db, nj, tf), fb[:, :, dff:].reshape(nfb, db, nj, tf)],
                         axis=-1).reshape(nfb, db, 2 * dff)
    y_s, ug_s, uv_s = _ffn(h2_s, x1_s, ada_s, g_final, wup, w_ffn_conv[l], b_ffn_conv[l], wdn,
                           "sample", db, ts * db, tf, buf=fb)

    out_s = (
        y_s.reshape(ts, db, d).transpose(1, 0, 2),
        dk_s.reshape(1, db, ts, DSA_KV_HEADS, DSA_HEAD_DIM),
        dv_s_b.reshape(1, db, ts, DSA_KV_HEADS, DSA_HEAD_DIM),
        ik_s_b[None],
        s_s[None],
        ys_gdn3[ts - (GDN_CONV - 1):, :, :QKV_W].transpose(1, 0, 2)[None],
        jnp.concatenate([ug_s, uv_s], axis=-1).transpose(1, 0, 2)[None],
    )
    return (out_p[0], out_s[0]) + out_p[1:] + out_s[1:]
```

```python
import functools

import jax
import jax.numpy as jnp
from jax import lax
from jax.experimental import pallas as pl
from jax.experimental.pallas import tpu as pltpu

f32 = jnp.float32
bf16 = jnp.bfloat16

GDN_HEADS = 8
GDN_DK = 128
GDN_DV = 128
GDN_CONV = 4
GDN_CHUNK = 64
GDN_GROUP = 4
GDN_SEQS = 2
DSA_HEADS = 8
DSA_KV_HEADS = 2
DSA_HEAD_DIM = 128
IDX_HEADS = 8
IDX_DIM = 64
IDX_SCALE = IDX_HEADS ** -0.5 * IDX_DIM ** -0.5
DSA_TOPK = 256
ROPE_THETA = 500000.0
ROPE_FRACTION = 4
FFN_CONV = 3
NORM_EPS = 1e-6

LANES = 128
SUBLANES = 8
NEG = float(jnp.finfo(jnp.float32).min)
POS = float(jnp.finfo(jnp.float32).max)
BISECT_UNROLL = 4
SNAP_FROM = 4
BISECT_CAP = 1024
DSA_KEY_TILE = 256
DSA_KEY_EXTENT = 512
IN_PROJ_ROWS = 1024
IDX_BATCH = 4
DSA_SAMPLE_BATCH = 2
DSA_HEAD_PAIR = 2
FFN_DOWN_TILE = 512
FFN_SUB_ROWS = 128

QKV_W = 2 * GDN_HEADS * GDN_DK + GDN_HEADS * GDN_DV
Z_W = GDN_HEADS * GDN_DV
DQ_W = DSA_HEADS * DSA_HEAD_DIM
DKV_W = DSA_KV_HEADS * DSA_HEAD_DIM
IQ_W = IDX_HEADS * IDX_DIM
OFF_Z = QKV_W
GDN_PROJ_W = QKV_W + Z_W
OFF_DQ = 0
OFF_DK = OFF_DQ + DQ_W
OFF_DV = OFF_DK + DKV_W
OFF_IQ = OFF_DV + DKV_W
DSA_PROJ_W = OFF_IQ + IQ_W
SM_IK = 0
SM_A = IDX_DIM
SM_B = SM_A + GDN_HEADS
SM_IW = SM_B + GDN_HEADS


def _sigmoid(x):
    return 1.0 / (1.0 + jnp.exp(-x))


def _silu(x):
    return x * _sigmoid(x)


def _dot(a, b):
    return jnp.dot(a, b, preferred_element_type=f32)


def _dot_nt(a, b):
    return lax.dot_general(a, b, (((1,), (1,)), ((), ())), preferred_element_type=f32)


def _dot_tn(a, b):
    return lax.dot_general(a, b, (((0,), (0,)), ((), ())), preferred_element_type=f32)


def _split3(a):
    hi = a.astype(bf16)
    lo = (a - hi.astype(f32)).astype(bf16)
    return hi, lo


def _mm3(a, b):
    ah, al = a
    bh, bl = b
    return _dot(ah, bh) + (_dot(ah, bl) + _dot(al, bh))


def _dot_exact(a, b):
    return jnp.dot(a, b, preferred_element_type=f32, precision=lax.Precision.HIGHEST)


def _params(vmem_mb=None, n_axes=1):
    kw = dict(dimension_semantics=("arbitrary",) * n_axes)
    if vmem_mb is not None:
        kw["vmem_limit_bytes"] = vmem_mb * 1024 * 1024
    return pltpu.CompilerParams(**kw)


def _ada_kernel(c_ref, w_ref, b_ref, o_ref):
    s = _silu(c_ref[...]).astype(bf16)
    o_ref[...] = _dot(s, w_ref[...].astype(bf16)) + b_ref[...]


def _ada(c_all, w_ada, b_ada):
    m, d = c_all.shape
    n = w_ada.shape[1]
    tn = 1024
    return pl.pallas_call(
        _ada_kernel,
        grid=(n // tn,),
        in_specs=[pl.BlockSpec((m, d), lambda j: (0, 0)),
                  pl.BlockSpec((d, tn), lambda j: (0, j)),
                  pl.BlockSpec((1, tn), lambda j: (0, j))],
        out_specs=pl.BlockSpec((m, tn), lambda j: (0, j)),
        out_shape=jax.ShapeDtypeStruct((m, n), f32),
        compiler_params=_params(48),
        name="ada",
    )(c_all, w_ada, b_ada.reshape(1, n))


def _mod_spec(group, d, rows_per_batch, tm, col):
    if group == "prompt":
        return pl.BlockSpec((None, 1, d), lambda i, *_: ((i * tm) // rows_per_batch, 0, col))
    return pl.BlockSpec((rows_per_batch, d), lambda i, *_: (0, col))


def _prep_kernel(x_ref, g_ref, sc_ref, sh_ref, o_ref):
    x = x_ref[...]
    y = x * lax.rsqrt(jnp.mean(x * x, axis=-1, keepdims=True) + NORM_EPS)
    o_ref[...] = ((y * g_ref[...]) * (1.0 + sc_ref[...]) + sh_ref[...]).astype(bf16)


def _prep(x, g, ada, group, rows_per_batch, tm):
    m, d = x.shape
    return pl.pallas_call(
        _prep_kernel,
        grid=(m // tm,),
        in_specs=[pl.BlockSpec((tm, d), lambda i: (i, 0)),
                  pl.BlockSpec((1, d), lambda i: (0, 0)),
                  _mod_spec(group, d, rows_per_batch, tm, 1),
                  _mod_spec(group, d, rows_per_batch, tm, 0)],
        out_specs=pl.BlockSpec((tm, d), lambda i: (i, 0)),
        out_shape=jax.ShapeDtypeStruct((m, d), bf16),
        compiler_params=_params(),
        name="prep_" + group,
    )(x, g.reshape(1, d), ada, ada)


def _mm_kernel(a_ref, w_ref, o_ref):
    o_ref[...] = _dot(a_ref[...], w_ref[...])


def _matmul(a, w, tm, tn, name):
    m, k = a.shape
    n = w.shape[1]
    return pl.pallas_call(
        _mm_kernel,
        grid=(n // tn, m // tm),
        in_specs=[pl.BlockSpec((tm, k), lambda j, i: (i, 0)),
                  pl.BlockSpec((k, tn), lambda j, i: (0, j))],
        out_specs=pl.BlockSpec((tm, tn), lambda j, i: (i, j)),
        out_shape=jax.ShapeDtypeStruct((m, n), f32),
        compiler_params=_params(48, 2),
        name=name,
    )(a, w)


def _l2n(x):
    return x * lax.rsqrt(jnp.sum(x * x, axis=-1, keepdims=True) + NORM_EPS)


def _gdn_gates(sm, alog, dtb):
    xa = sm + dtb
    softplus = jnp.maximum(xa, 0.0) + jnp.log1p(jnp.exp(-jnp.abs(xa)))
    return -jnp.exp(alog) * softplus, _sigmoid(sm)


def _gated_norm(o, gn, z):
    y = o * lax.rsqrt(jnp.mean(o * o, axis=-1, keepdims=True) + NORM_EPS)
    return (y * gn) * _silu(z)


def _gdn_prompt_kernel(qkv_ref, halo_ref, z_ref, sm_ref, wc_ref, alog_ref, dtb_ref, gn_ref,
                       o_ref, sfin_ref, s_ref):
    c = pl.program_id(1)
    n_seq, C, _ = qkv_ref.shape

    @pl.when(c == 0)
    def _():
        s_ref[...] = jnp.zeros_like(s_ref)

    keep = (c > 0).astype(f32)
    ri = lax.broadcasted_iota(jnp.int32, (C, C), 0)
    ci = lax.broadcasted_iota(jnp.int32, (C, C), 1)
    tril = (ri >= ci).astype(f32)
    gates = [_gdn_gates(sm_ref[s], alog_ref[...], dtb_ref[...]) for s in range(n_seq)]
    beta_all = [g[1] for g in gates]
    gc_all = [_dot_exact(tril, g[0]) for g in gates]
    gc_t = [g.T for g in gc_all]

    def conv(s, col):
        xs = jnp.concatenate([halo_ref[s, :, col:col + LANES] * keep, qkv_ref[s, :, col:col + LANES]], axis=0)
        w = wc_ref[:, col:col + LANES]
        y = xs[SUBLANES:] * w[GDN_CONV - 1:GDN_CONV]
        for i in range(GDN_CONV - 1):
            sft = GDN_CONV - 1 - i
            y = y + xs[SUBLANES - sft:SUBLANES - sft + C] * w[i:i + 1]
        return _silu(y)

    N = GDN_GROUP * C
    rn = lax.broadcasted_iota(jnp.int32, (N, N), 0)
    cn = lax.broadcasted_iota(jnp.int32, (N, N), 1)
    same = (rn // C) == (cn // C)
    incl = jnp.logical_and(same, rn >= cn)
    strict = jnp.logical_and(same, rn > cn)
    eye_f = (rn == cn).astype(f32)
    n_sq = max(1, (C - 1).bit_length() - 1)
    groups = [(s, list(range(g0, g0 + GDN_GROUP))) for s in range(n_seq)
              for g0 in range(0, GDN_HEADS, GDN_GROUP)]
    stack = lambda xs: jnp.concatenate(xs, axis=0)

    qs = [stack([_l2n(conv(s, h * GDN_DK)) * (GDN_DK ** -0.5) for h in hs]) for s, hs in groups]
    ks = [stack([_l2n(conv(s, GDN_HEADS * GDN_DK + h * GDN_DK)) for h in hs]) for s, hs in groups]
    vs = [stack([conv(s, 2 * GDN_HEADS * GDN_DK + h * GDN_DV) for h in hs]) for s, hs in groups]
    gcs = [stack([gc_all[s][:, SM_A + h:SM_A + h + 1] for h in hs]) for s, hs in groups]
    betas = [stack([beta_all[s][:, SM_B + h:SM_B + h + 1] for h in hs]) for s, hs in groups]
    gc_rows = [jnp.concatenate([gc_t[s][SM_A + h:SM_A + h + 1, :] for h in hs], axis=1)
               for s, hs in groups]
    egs = [jnp.exp(gc) for gc in gcs]
    decays = [jnp.where(incl, jnp.exp(jnp.where(incl, gc - gr, 0.0)), 0.0) for gc, gr in zip(gcs, gc_rows)]
    kbs = [k.astype(bf16) for k in ks]
    a_s = [jnp.where(strict, b * _dot_nt(kb, kb) * dc, 0.0) for b, kb, dc in zip(betas, kbs, decays)]
    qks = [(_dot_nt(q.astype(bf16), kb) * dc).astype(bf16) for q, kb, dc in zip(qs, kbs, decays)]

    invs = [eye_f - a for a in a_s]
    pws = [a.astype(bf16) for a in a_s]
    for _ in range(n_sq):
        pws = [_dot(p, p).astype(bf16) for p in pws]
        invs = [inv + _dot(inv.astype(bf16), p) for inv, p in zip(invs, pws)]
    inv_s = [_split3(inv) for inv in invs]
    res = [eye_f - inv - _mm3(_split3(a), sp) for a, inv, sp in zip(a_s, invs, inv_s)]
    invs = [inv + _dot(sp[0], r.astype(bf16)) for inv, sp, r in zip(invs, inv_s, res)]
    rhs = [jnp.concatenate([b * v, (b * eg) * k], axis=1) for b, v, eg, k in zip(betas, vs, egs, ks)]
    sols = [_mm3(_split3(inv), _split3(r)) for inv, r in zip(invs, rhs)]

    for gi, (s, hs) in enumerate(groups):
        sol, q, k, gc, eg = sols[gi], qs[gi], ks[gi], gcs[gi], egs[gi]
        q_dec = (q * eg).astype(bf16)
        w_k = sol[:, GDN_DV:].astype(bf16)
        rows = [slice(j * C, (j + 1) * C) for j in range(GDN_GROUP)]
        s_old = [s_ref[s, h] for h in hs]
        ws = [_dot(jnp.concatenate([w_k[r], q_dec[r]], axis=0), so.astype(bf16)) for r, so in zip(rows, s_old)]
        u = stack([sol[r, :GDN_DV] - w[:C] for r, w in zip(rows, ws)])
        ub = u.astype(bf16)
        o_intra = _dot(qks[gi], ub)
        for j, h in enumerate(hs):
            r = rows[j]
            gl = gc[r][C - 1:C, :]
            k_end = (k[r] * jnp.exp(gl - gc[r])).astype(bf16)
            s_ref[s, h] = s_old[j] * jnp.exp(gl) + _dot_tn(k_end, ub[r])
            o = ws[j][C:] + o_intra[r]
            zh = z_ref[s, :, h * GDN_DV:(h + 1) * GDN_DV]
            o_ref[s, :, h * GDN_DV:(h + 1) * GDN_DV] = _gated_norm(o, gn_ref[...], zh).astype(bf16)

    @pl.when(c == pl.num_programs(1) - 1)
    def _():
        sfin_ref[...] = s_ref[...]


def _gdn_vecs(a_log, dt_bias, g_gdn_norm):
    alog = jnp.zeros((1, LANES), f32).at[0, SM_A:SM_A + GDN_HEADS].set(a_log)
    dtb = jnp.zeros((1, LANES), f32).at[0, SM_A:SM_A + GDN_HEADS].set(dt_bias)
    return alog, dtb, g_gdn_norm.reshape(1, GDN_DV)


def _gdn_prompt(y, y_small, w_conv, a_log, dt_bias, g_gdn_norm, nb, t):
    C = GDN_CHUNK
    nc = t // C
    ns = GDN_SEQS if nb % GDN_SEQS == 0 else 1
    alog, dtb, gn = _gdn_vecs(a_log, dt_bias, g_gdn_norm)
    cst = lambda b, c: (0, 0)
    y3 = y.reshape(nb, t, -1)
    ysm3 = y_small.reshape(nb, t, LANES)
    o, s_fin = pl.pallas_call(
        _gdn_prompt_kernel,
        grid=(nb // ns, nc),
        in_specs=[pl.BlockSpec((ns, C, QKV_W), lambda b, c: (b, c, 0)),
                  pl.BlockSpec((ns, SUBLANES, QKV_W),
                               lambda b, c: (b, jnp.maximum(c * (C // SUBLANES) - 1, 0), 0)),
                  pl.BlockSpec((ns, C, Z_W), lambda b, c: (b, c, OFF_Z // Z_W)),
                  pl.BlockSpec((ns, C, LANES), lambda b, c: (b, c, 0)),
                  pl.BlockSpec((GDN_CONV, QKV_W), cst),
                  pl.BlockSpec((1, LANES), cst),
                  pl.BlockSpec((1, LANES), cst),
                  pl.BlockSpec((1, GDN_DV), cst)],
        out_specs=[pl.BlockSpec((ns, C, Z_W), lambda b, c: (b, c, 0)),
                   pl.BlockSpec((ns, GDN_HEADS, GDN_DK, GDN_DV), lambda b, c: (b, 0, 0, 0))],
        out_shape=[jax.ShapeDtypeStruct((nb, t, Z_W), bf16),
                   jax.ShapeDtypeStruct((nb, GDN_HEADS, GDN_DK, GDN_DV), f32)],
        scratch_shapes=[pltpu.VMEM((ns, GDN_HEADS, GDN_DK, GDN_DV), f32)],
        compiler_params=_params(None, 2),
        name="gdn_prompt",
    )(y3, y3, y3, ysm3, w_conv, alog, dtb, gn)
    return o.reshape(nb * t, Z_W), s_fin


def _gdn_sample_kernel(qkv_ref, buf_ref, z_ref, sm_ref, sin_ref, wc_ref, alog_ref, dtb_ref, gn_ref,
                       o_ref, sout_ref, q_s, k_s, v_s, a_s, b_s, o_s):
    ts, G, _ = qkv_ref.shape
    nbuf = GDN_CONV - 1

    for h in range(GDN_HEADS):
        for part, dst in ((0, q_s), (1, k_s), (2, v_s)):
            col = part * GDN_HEADS * GDN_DK + h * GDN_DK
            w = wc_ref[:, col:col + LANES]
            rows = [buf_ref[i, :, col:col + LANES] for i in range(nbuf)]
            rows += [qkv_ref[t, :, col:col + LANES] for t in range(ts)]
            for t in range(ts):
                y = rows[t] * w[0:1]
                for i in range(1, GDN_CONV):
                    y = y + rows[t + i] * w[i:i + 1]
                y = _silu(y)
                if part == 0:
                    y = _l2n(y) * (GDN_DK ** -0.5)
                elif part == 1:
                    y = _l2n(y)
                dst[t, :, h * LANES:(h + 1) * LANES] = y

    for t in range(ts):
        g_all, beta_all = _gdn_gates(sm_ref[t], alog_ref[...], dtb_ref[...])
        a_all = jnp.exp(g_all)
        for h in range(GDN_HEADS):
            a_s[t, :, h * LANES:(h + 1) * LANES] = jnp.broadcast_to(a_all[:, SM_A + h:SM_A + h + 1], (G, LANES))
            b_s[t, :, h * LANES:(h + 1) * LANES] = jnp.broadcast_to(beta_all[:, SM_B + h:SM_B + h + 1], (G, LANES))

    nv = 2 * ts
    sr = lax.broadcasted_iota(jnp.int32, (3 * nv, nv * LANES), 0)
    sc = lax.broadcasted_iota(jnp.int32, (3 * nv, nv * LANES), 1)
    spread = (sr % nv == sc // LANES).astype(bf16)

    def head(h, carry):
        cols = pl.ds(pl.multiple_of(h * LANES, LANES), LANES)
        for i in range(G):
            row = slice(i, i + 1)
            kq = jnp.concatenate([k_s[t, row, cols] for t in range(ts)]
                                 + [q_s[t, row, cols] for t in range(ts)], axis=0)
            hi = kq.astype(bf16).astype(f32)
            mid = (kq - hi).astype(bf16).astype(f32)
            lo = (kq - hi) - mid
            kq_b = _dot(jnp.concatenate([hi, mid, lo], axis=0).T.astype(bf16), spread)
            s = sin_ref[i, h]
            for t in range(ts):
                kc = kq_b[:, t * LANES:(t + 1) * LANES]
                qc = kq_b[:, (ts + t) * LANES:(ts + t + 1) * LANES]
                a = a_s[t, row, cols]
                b = b_s[t, row, cols]
                ks = jnp.sum(s * kc, axis=0, keepdims=True)
                r = b * (v_s[t, row, cols] - a * ks)
                s = a * s + kc * r
                o_s[t, row, cols] = jnp.sum(s * qc, axis=0, keepdims=True)
            sout_ref[i, h] = s
        return carry

    lax.fori_loop(0, GDN_HEADS, head, 0)

    for t in range(ts):
        for h in range(GDN_HEADS):
            sl = slice(h * GDN_DV, (h + 1) * GDN_DV)
            o_ref[t, :, sl] = _gated_norm(o_s[t, :, sl], gn_ref[...], z_ref[t, :, sl]).astype(bf16)


def _gdn_sample(y3, y3_small, buf3, state, w_conv, a_log, dt_bias, g_gdn_norm):
    ts, db, _ = y3.shape
    G = SUBLANES
    alog, dtb, gn = _gdn_vecs(a_log, dt_bias, g_gdn_norm)
    cst = lambda g: (0, 0)
    st_spec = pl.BlockSpec((G, GDN_HEADS, GDN_DK, GDN_DV), lambda g: (g, 0, 0, 0))
    scr = pltpu.VMEM((ts, G, Z_W), f32)
    return pl.pallas_call(
        _gdn_sample_kernel,
        grid=(db // G,),
        in_specs=[pl.BlockSpec((ts, G, QKV_W), lambda g: (0, g, 0)),
                  pl.BlockSpec((GDN_CONV - 1, G, QKV_W), lambda g: (0, g, 0)),
                  pl.BlockSpec((ts, G, Z_W), lambda g: (0, g, OFF_Z // Z_W)),
                  pl.BlockSpec((ts, G, LANES), lambda g: (0, g, 0)),
                  st_spec,
                  pl.BlockSpec((GDN_CONV, QKV_W), cst),
                  pl.BlockSpec((1, LANES), cst),
                  pl.BlockSpec((1, LANES), cst),
                  pl.BlockSpec((1, GDN_DV), cst)],
        out_specs=[pl.BlockSpec((ts, G, Z_W), lambda g: (0, g, 0)), st_spec],
        out_shape=[jax.ShapeDtypeStruct((ts, db, Z_W), bf16),
                   jax.ShapeDtypeStruct(state.shape, f32)],
        scratch_shapes=[scr, scr, scr, scr, scr, scr],
        compiler_params=_params(48),
        name="gdn_sample",
    )(y3, buf3, y3, y3_small, state, w_conv, alog, dtb, gn)


def _rope_tables(pos, rot, width):
    half = rot // 2
    inv_freq = ROPE_THETA ** (-jnp.arange(half, dtype=f32) * (2.0 / rot))
    ang = pos.astype(f32)[:, None] * inv_freq[None, :]
    cos, sin = jnp.cos(ang), jnp.sin(ang)
    n = pos.shape[0]
    z = lambda w: jnp.zeros((n, w), f32)
    cosw = jnp.concatenate([cos, cos, jnp.ones((n, width - rot), f32)], axis=1)
    sina = jnp.concatenate([-sin, z(width - half)], axis=1)
    sinb = jnp.concatenate([z(half), sin, z(width - rot)], axis=1)
    reps = LANES // width
    return tuple(jnp.tile(a, (1, reps)) for a in (cosw, sina, sinb))


def _rope_kernel(dq_ref, dk_ref, iq_ref, sm_ref, c1, sa1, sb1, c2, sa2, sb2,
                 dq_o, dk_o, iq_o, ik_o, sm_o):
    h1 = DSA_HEAD_DIM // ROPE_FRACTION // 2
    h2 = IDX_DIM // ROPE_FRACTION // 2

    def rot(x, c, sa, sb, half):
        return x * c[...] + pltpu.roll(x, LANES - half, 1) * sa[...] + pltpu.roll(x, half, 1) * sb[...]

    for j in range(DQ_W // LANES):
        sl = slice(j * LANES, (j + 1) * LANES)
        dq_o[:, sl] = rot(dq_ref[:, sl], c1, sa1, sb1, h1).astype(bf16)
    for j in range(DKV_W // LANES):
        sl = slice(j * LANES, (j + 1) * LANES)
        dk_o[:, sl] = rot(dk_ref[:, sl], c1, sa1, sb1, h1)
    for j in range(IQ_W // LANES):
        sl = slice(j * LANES, (j + 1) * LANES)
        iq_o[:, sl] = rot(iq_ref[:, sl], c2, sa2, sb2, h2).astype(bf16)
    sm = sm_ref[...]
    ik_o[...] = rot(sm, c2, sa2, sb2, h2)[:, :IDX_DIM]
    sm_o[...] = sm


def _rope(y, y_small, pos, tm, group, db=None):
    m = y.shape[0]
    t1 =_rope_tables(pos, DSA_HEAD_DIM // ROPE_FRACTION, DSA_HEAD_DIM)
    t2 = _rope_tables(pos, IDX_DIM // ROPE_FRACTION, IDX_DIM)
    tab_blocks = pos.shape[0] // tm
    tab = pl.BlockSpec((tm, LANES), lambda i: (i % tab_blocks, 0))
    if group == "prompt":
        omap = lambda i: (i, 0)
        rows = lambda w: m
        cols = lambda w: w
    else:
        ts = m // db
        omap = lambda i: (0, i)
        rows = lambda w: db
        cols = lambda w: ts * w
    out = lambda w, dt: jax.ShapeDtypeStruct((rows(w), cols(w)), dt)
    return pl.pallas_call(
        _rope_kernel,
        grid=(m // tm,),
        in_specs=[pl.BlockSpec((tm, DQ_W), lambda i: (i, OFF_DQ // DQ_W)),
                  pl.BlockSpec((tm, DKV_W), lambda i: (i, OFF_DK // DKV_W)),
                  pl.BlockSpec((tm, IQ_W), lambda i: (i, OFF_IQ // IQ_W)),
                  pl.BlockSpec((tm, LANES), lambda i: (i, 0)),
                  tab, tab, tab, tab, tab, tab],
        out_specs=[pl.BlockSpec((tm, DQ_W), omap),
                   pl.BlockSpec((tm, DKV_W), omap),
                   pl.BlockSpec((tm, IQ_W), omap),
                   pl.BlockSpec((tm, IDX_DIM), lambda i: (i, 0)),
                   pl.BlockSpec((tm, LANES), omap)],
        out_shape=[out(DQ_W, bf16), out(DKV_W, f32), out(IQ_W, bf16),
                   jax.ShapeDtypeStruct((m, IDX_DIM), f32), out(LANES, f32)],
        compiler_params=_params(),
        name="rope_" + group,
    )(y, y, y, y_small, *t1, *t2)


def _topk_bias(x_s, bias_s, lo0, mx, few, nkt, kt_w, k):
    R = x_s.shape[0]
    kf = float(k)
    tiles = [slice(kt * kt_w, (kt + 1) * kt_w) for kt in range(nkt)]
    n_grp = 2 if R % (2 * SUBLANES) == 0 else 1
    grps = [slice(g * (R // n_grp), (g + 1) * (R // n_grp)) for g in range(n_grp)]

    def count_ge(rows, th):
        acc = jnp.where(x_s[rows, tiles[0]] >= th, 1.0, 0.0)
        for t in tiles[1:]:
            acc = acc + jnp.where(x_s[rows, t] >= th, 1.0, 0.0)
        return jnp.sum(acc, axis=1, keepdims=True)

    def bisect(rows, lo, hi, done):
        mid = 0.5 * lo + 0.5 * hi
        cnt = count_ge(rows, mid)
        collapsed = jnp.logical_or(mid <= lo, mid >= hi)
        live = jnp.logical_and(done < 0.5, jnp.logical_not(collapsed))
        lo = jnp.where(jnp.logical_and(live, cnt >= kf), mid, lo)
        hi = jnp.where(jnp.logical_and(live, cnt <= kf), mid, hi)
        done = jnp.where(jnp.logical_or(collapsed, cnt == kf), 1.0, done)
        return lo, hi, done

    def snap(rows, lo, hi, done):
        a = b = None
        for t in tiles:
            x = x_s[rows, t]
            at = jnp.where(x >= lo, x, POS)
            bt = jnp.where(x < hi, x, NEG)
            a = at if a is None else jnp.minimum(a, at)
            b = bt if b is None else jnp.maximum(b, bt)
        a = jnp.min(a, axis=1, keepdims=True)
        b = jnp.max(b, axis=1, keepdims=True)
        live = done < 0.5
        return jnp.where(live, a, lo), jnp.where(jnp.logical_and(live, a >= b), 1.0, done)

    def cond(c):
        it, st = c
        left = st[0][2]
        for s in st[1:]:
            left = jnp.minimum(left, s[2])
        return jnp.logical_and(it < BISECT_CAP, jnp.min(left) < 0.5)

    def body(c):
        it, st = c
        for _ in range(BISECT_UNROLL):
            st = [bisect(rows, *s) for rows, s in zip(grps, st)]

        def snapped():
            out = []
            for rows, (lo, hi, done) in zip(grps, st):
                lo2, done2 = snap(rows, lo, hi, done)
                out.append((lo2, hi, done2))
            return out

        st = lax.cond(it >= SNAP_FROM, snapped, lambda: st)
        return it + 1, st

    hi0 = jnp.where(few, lo0, mx + (jnp.abs(mx) + 1.0))
    st0 = [(lo0[rows], hi0[rows], few[rows].astype(f32)) for rows in grps]
    _, st = lax.while_loop(cond, body, (jnp.int32(0), st0))
    lo = jnp.concatenate([s[0] for s in st], axis=0)
    hi = jnp.concatenate([s[1] for s in st], axis=0)

    has_run = jnp.max(jnp.where(lo < hi, 1.0, 0.0)) > 0.5
    rows_all = slice(0, R)

    @pl.when(jnp.logical_not(has_run))
    def _():
        for t in tiles:
            bias_s[:, t] = jnp.where(x_s[:, t] >= hi, 0.0, NEG)

    @pl.when(has_run)
    def _():
        need = kf - count_ge(rows_all, hi)
        ai = lax.broadcasted_iota(jnp.int32, (kt_w, kt_w), 0)
        bi = lax.broadcasted_iota(jnp.int32, (kt_w, kt_w), 1)
        before = (ai < bi).astype(bf16)
        seen = jnp.zeros((R, 1), f32)
        for t in tiles:
            x = x_s[:, t]
            run = jnp.logical_and(x >= lo, x < hi)
            runf = run.astype(f32)
            rank = seen + _dot(runf.astype(bf16), before)
            take = jnp.logical_or(x >= hi, jnp.logical_and(run, rank < need))
            bias_s[:, t] = jnp.where(take, 0.0, NEG)
            seen = seen + jnp.sum(runf, axis=1, keepdims=True)


def _dsa_prompt_kernel(iq_ref, sm_ref, ik_ref, q_ref, k_ref, v_ref, o_ref,
                       ikb, kb, vb, x_s, bias_s, s_s, *, topk, kt_w, ext_w):
    qi = pl.program_id(1)
    R = iq_ref.shape[0]
    T = ik_ref.shape[0]

    @pl.when(qi == 0)
    def _():
        ikb[...] = ik_ref[...].astype(bf16)
        kb[...] = k_ref[...].astype(bf16)
        vb[...] = v_ref[...].astype(bf16)

    def block(ext):
        tiles = [slice(j * kt_w, (j + 1) * kt_w) for j in range(ext // kt_w)]
        wgt = sm_ref[...] * IDX_SCALE
        qpos = qi * R + lax.broadcasted_iota(jnp.int32, (R, kt_w), 0)
        key0 = lax.broadcasted_iota(jnp.int32, (R, kt_w), 1)
        mn = mx = None
        for j, t in enumerate(tiles):
            keys = ikb[t, :]
            sc = jnp.zeros((R, kt_w), f32)
            for h in range(IDX_HEADS):
                rel = jnp.maximum(_dot_nt(iq_ref[:, h * IDX_DIM:(h + 1) * IDX_DIM], keys), 0.0)
                sc = sc + rel * wgt[:, SM_IW + h:SM_IW + h + 1]
            causal = key0 + j * kt_w <= qpos
            x_s[:, t] = jnp.where(causal, sc, NEG)
            lo_t = jnp.where(causal, sc, POS)
            hi_t = jnp.where(causal, sc, NEG)
            mn = lo_t if mn is None else jnp.minimum(mn, lo_t)
            mx = hi_t if mx is None else jnp.maximum(mx, hi_t)
        n_causal = qi * R + lax.broadcasted_iota(jnp.int32, (R, 1), 0) + 1
        _topk_bias(x_s, bias_s, jnp.min(mn, axis=1, keepdims=True), jnp.max(mx, axis=1, keepdims=True),
                   n_causal <= topk, len(tiles), kt_w, topk)

        def heads(hp, carry):
            h0 = hp * DSA_HEAD_PAIR
            g = h0 // (DSA_HEADS // DSA_KV_HEADS)
            gc = pl.ds(pl.multiple_of(g * DSA_HEAD_DIM, DSA_HEAD_DIM), DSA_HEAD_DIM)
            hcs = [pl.ds(pl.multiple_of((h0 + i) * DSA_HEAD_DIM, DSA_HEAD_DIM), DSA_HEAD_DIM)
                   for i in range(DSA_HEAD_PAIR)]
            qs = [q_ref[:, hc] for hc in hcs]
            mxa = [None] * DSA_HEAD_PAIR
            for t in tiles:
                kt = kb[t, gc]
                bt = bias_s[:, t]
                for i in range(DSA_HEAD_PAIR):
                    s = _dot_nt(qs[i], kt) * (DSA_HEAD_DIM ** -0.5) + bt
                    s_s[i, :, t] = s
                    mxa[i] = s if mxa[i] is None else jnp.maximum(mxa[i], s)
            ms = [jnp.max(a, axis=1, keepdims=True) for a in mxa]
            la = [jnp.zeros((R, kt_w), f32)] * DSA_HEAD_PAIR
            acc = [jnp.zeros((R, DSA_HEAD_DIM), f32)] * DSA_HEAD_PAIR
            for t in tiles:
                vt = vb[t, gc]
                for i in range(DSA_HEAD_PAIR):
                    p = jnp.exp(s_s[i, :, t] - ms[i])
                    la[i] = la[i] + p
                    acc[i] = acc[i] + _dot(p.astype(bf16), vt)
            for i in range(DSA_HEAD_PAIR):
                o_ref[:, hcs[i]] = (acc[i] / jnp.sum(la[i], axis=1, keepdims=True)).astype(bf16)
            return carry

        lax.fori_loop(0, DSA_HEADS // DSA_HEAD_PAIR, heads, 0)

    for e in range(T // ext_w):
        pl.when(qi // (ext_w // R) == e)(functools.partial(block, (e + 1) * ext_w))


def _dsa_prompt(iq, sm, ik, dq, dk, y, nb, t, topk):
    R = LANES
    nq = t // R
    kt_w = min(DSA_KEY_TILE, t)
    ext_w = min(DSA_KEY_EXTENT, t)
    kern = functools.partial(_dsa_prompt_kernel, topk=topk, kt_w=kt_w, ext_w=ext_w)
    return pl.pallas_call(
        kern,
        grid=(nb, nq),
        in_specs=[pl.BlockSpec((R, IQ_W), lambda b, i: (b * nq + i, 0)),
                  pl.BlockSpec((R, LANES), lambda b, i: (b * nq + i, 0)),
                  pl.BlockSpec((t, IDX_DIM), lambda b, i: (b, 0)),
                  pl.BlockSpec((R, DQ_W), lambda b, i: (b * nq + i, 0)),
                  pl.BlockSpec((t, DKV_W), lambda b, i: (b, 0)),
                  pl.BlockSpec((t, DKV_W), lambda b, i: (b, OFF_DV // DKV_W))],
        out_specs=pl.BlockSpec((R, DQ_W), lambda b, i: (b * nq + i, 0)),
        out_shape=jax.ShapeDtypeStruct((nb * t, DQ_W), bf16),
        scratch_shapes=[pltpu.VMEM((t, IDX_DIM), bf16), pltpu.VMEM((t, DKV_W), bf16),
                        pltpu.VMEM((t, DKV_W), bf16), pltpu.VMEM((R, t), f32), pltpu.VMEM((R, t), f32),
                        pltpu.VMEM((DSA_HEAD_PAIR, R, t), f32)],
        compiler_params=_params(48, 2),
        name="dsa_prompt",
    )(iq, sm, ik, dq, dk, y)


def _idx_score_kernel(pt_ref, iq_ref, sm_ref, ikn_ref, *rest, n_pages, page):
    eb, tp, _ = iq_ref.shape
    o_ref = rest[eb * n_pages]
    ts = o_ref.shape[1]
    past = n_pages * page
    lp = o_ref.shape[2]
    for e in range(eb):
        pages = rest[e * n_pages:(e + 1) * n_pages]
        keys_t = jnp.concatenate([p[...] for p in pages], axis=1).astype(bf16)
        keys_n = jnp.concatenate([ikn_ref[e], jnp.zeros((lp - past - tp, IDX_DIM), f32)], axis=0).astype(bf16)
        wgt = sm_ref[e] * IDX_SCALE
        iq = iq_ref[e].astype(bf16)
        score = jnp.zeros((tp, lp), f32)
        for h in range(IDX_HEADS):
            iqh = iq[:, h * IDX_DIM:(h + 1) * IDX_DIM]
            rel = jnp.maximum(jnp.concatenate([_dot(iqh, keys_t), _dot_nt(iqh, keys_n)], axis=1), 0.0)
            score = score + rel * wgt[:, SM_IW + h:SM_IW + h + 1]
        o_ref[e] = score[:ts]


def _idx_scores(page_table, iq, sm, ikn, cache_idx, lp, ts):
    db, tp, _ = iq.shape
    n_pages = page_table.shape[1]
    page = cache_idx.shape[2]
    eb = IDX_BATCH
    kern = functools.partial(_idx_score_kernel, n_pages=n_pages, page=page)
    page_specs = [pl.BlockSpec((None, IDX_DIM, page),
                               functools.partial(lambda b, pt, e, p: (pt[b * eb + e, p], 0, 0), e=e, p=p))
                  for e in range(eb) for p in range(n_pages)]
    grid_spec = pltpu.PrefetchScalarGridSpec(
        num_scalar_prefetch=1,
        grid=(db // eb,),
        in_specs=[pl.BlockSpec((eb, tp, IQ_W), lambda b, pt: (b, 0, 0)),
                  pl.BlockSpec((eb, tp, LANES), lambda b, pt: (b, 0, 0)),
                  pl.BlockSpec((eb, tp, IDX_DIM), lambda b, pt: (b, 0, 0))] + page_specs,
        out_specs=pl.BlockSpec((eb, ts, lp), lambda b, pt: (b, 0, 0)),
    )
    return pl.pallas_call(
        kern, grid_spec=grid_spec,
        out_shape=jax.ShapeDtypeStruct((db, ts, lp), f32),
        compiler_params=_params(),
        name="idx_scores_sample",
    )(page_table, iq, sm, ikn, *([cache_idx] * (eb * n_pages)))


def _select_sample_kernel(x_ref, bias_ref, x_s, *, topk, past, ts):
    R, lp = x_ref.shape
    key = lax.broadcasted_iota(jnp.int32, (R, lp), 1)
    t = lax.broadcasted_iota(jnp.int32, (R, lp), 0) % ts
    causal = key <= past + t
    x = x_ref[...]
    x_s[...] = jnp.where(causal, x, NEG)
    lo0 = jnp.min(jnp.where(causal, x, POS), axis=1, keepdims=True)
    mx = jnp.max(jnp.where(causal, x, NEG), axis=1, keepdims=True)
    n_causal = past + lax.broadcasted_iota(jnp.int32, (R, 1), 0) % ts + 1
    _topk_bias(x_s, bias_ref, lo0, mx, n_causal <= topk, lp // LANES, LANES, topk)


def _select_sample(scores, topk, past, ts):
    m, lp = scores.shape
    R = LANES
    kern = functools.partial(_select_sample_kernel, topk=topk, past=past, ts=ts)
    return pl.pallas_call(
        kern, grid=(m // R,),
        in_specs=[pl.BlockSpec((R, lp), lambda i: (i, 0))],
        out_specs=pl.BlockSpec((R, lp), lambda i: (i, 0)),
        out_shape=jax.ShapeDtypeStruct((m, lp), f32),
        scratch_shapes=[pltpu.VMEM((R, lp), f32)],
        compiler_params=_params(),
        name="select_sample",
    )(scores)


def _dsa_sample_kernel(pt_ref, q_ref, sel_ref, kn_ref, vn_ref, *rest, n_pages, page):
    eb, tp, _ = q_ref.shape
    o_ref = rest[2 * eb * n_pages]
    ts = sel_ref.shape[1]
    past = n_pages * page
    lp = sel_ref.shape[2]
    hpg = DSA_HEADS // DSA_KV_HEADS
    padn = jnp.zeros((lp - past - tp, DSA_HEAD_DIM), f32)
    for e in range(eb):
        kp = rest[e * n_pages:(e + 1) * n_pages]
        vp = rest[(eb + e) * n_pages:(eb + e + 1) * n_pages]
        bias = jnp.concatenate([sel_ref[e], jnp.zeros((tp - ts, lp), f32)], axis=0)
        bias = jnp.concatenate([bias] * hpg, axis=0)
        for g in range(DSA_KV_HEADS):
            gsl = slice(g * DSA_HEAD_DIM, (g + 1) * DSA_HEAD_DIM)
            kg = jnp.concatenate([p[pl.ds(g, page, stride=DSA_KV_HEADS), :] for p in kp]
                                 + [kn_ref[e, :, gsl], padn], axis=0).astype(bf16)
            vg = jnp.concatenate([p[pl.ds(g, page, stride=DSA_KV_HEADS), :] for p in vp]
                                 + [vn_ref[e, :, gsl], padn], axis=0).astype(bf16)
            qg = jnp.concatenate([q_ref[e, :, (g * hpg + j) * DSA_HEAD_DIM:(g * hpg + j + 1) * DSA_HEAD_DIM]
                                  for j in range(hpg)], axis=0).astype(bf16)
            s = _dot_nt(qg, kg) * (DSA_HEAD_DIM ** -0.5) + bias
            m = jnp.max(s, axis=1, keepdims=True)
            p = jnp.exp(s - m)
            l = jnp.sum(p, axis=1, keepdims=True)
            o = _dot(p.astype(bf16), vg) / l
            for j in range(hpg):
                hsl = slice((g * hpg + j) * DSA_HEAD_DIM, (g * hpg + j + 1) * DSA_HEAD_DIM)
                o_ref[e, :, hsl] = o[j * tp:j * tp + ts]


def _dsa_sample(page_table, dq, sel, kn, vn, ck, cv):
    db, tp, _ = dq.shape
    ts = sel.shape[1]
    lp = sel.shape[2]
    n_pages = page_table.shape[1]
    rows = ck.shape[1]
    page = rows // DSA_KV_HEADS
    eb = DSA_SAMPLE_BATCH
    kern = functools.partial(_dsa_sample_kernel, n_pages=n_pages, page=page)
    pspec = [pl.BlockSpec((None, rows, DSA_HEAD_DIM),
                          functools.partial(lambda b, pt, e, p: (pt[b * eb + e, p], 0, 0), e=e, p=p))
             for e in range(eb) for p in range(n_pages)]
    per_b = lambda r, w: pl.BlockSpec((eb, r, w), lambda b, pt: (b, 0, 0))
    grid_spec = pltpu.PrefetchScalarGridSpec(
        num_scalar_prefetch=1,
        grid=(db // eb,),
        in_specs=[per_b(tp, DQ_W), per_b(ts, lp), per_b(tp, DKV_W), per_b(tp, DKV_W)] + pspec + pspec,
        out_specs=per_b(ts, DQ_W),
    )
    return pl.pallas_call(
        kern, grid_spec=grid_spec,
        out_shape=jax.ShapeDtypeStruct((db, ts, DQ_W), f32),
        compiler_params=_params(48),
        name="dsa_sample",
    )(page_table, dq, sel, kn, vn, *([ck] * (eb * n_pages)), *([cv] * (eb * n_pages)))


def _merge_kernel(og_ref, od_ref, gg_ref, gd_ref, x_ref, gt_ref, sc_ref, sh_ref, g2_ref,
                  wg_ref, wd_ref, wo_ref, x1_ref, h2_ref):
    mix = (_sigmoid(gg_ref[...]) * _dot(og_ref[...], wg_ref[...])
           + _sigmoid(gd_ref[...]) * _dot(od_ref[...], wd_ref[...]))
    x1 = x_ref[...] + gt_ref[...] * _dot(mix.astype(bf16), wo_ref[...])
    x1_ref[...] = x1
    y = x1 * lax.rsqrt(jnp.mean(x1 * x1, axis=-1, keepdims=True) + NORM_EPS)
    h2_ref[...] = ((y * g2_ref[...]) * (1.0 + sc_ref[...]) + sh_ref[...]).astype(bf16)


def _merge(o_gdn, o_dsa, y, x, ada, g2, wg, wd, wo, group, rows_per_batch, tm):
    m, d = x.shape
    row = lambda w, blk=0: pl.BlockSpec((tm, w), lambda i: (i, blk))
    res = lambda a: pl.BlockSpec(a.shape, lambda i: (0, 0), pipeline_mode=pl.Buffered(1))
    return pl.pallas_call(
        _merge_kernel,
        grid=(m // tm,),
        in_specs=[row(Z_W), row(DQ_W), row(d, 0), row(d, 1), row(d),
                  _mod_spec(group, d, rows_per_batch, tm, 2),
                  _mod_spec(group, d, rows_per_batch, tm, 4),
                  _mod_spec(group, d, rows_per_batch, tm, 3),
                  pl.BlockSpec((1, d), lambda i: (0, 0)),
                  res(wg), res(wd), res(wo)],
        out_specs=[row(d), row(d)],
        out_shape=[jax.ShapeDtypeStruct((m, d), f32), jax.ShapeDtypeStruct((m, d), bf16)],
        compiler_params=_params(48),
        name="merge_" + group,
    )(o_gdn, o_dsa, y, y, x, ada, ada, ada, g2.reshape(1, d), wg, wd, wo)


def _ffn_epilogue(acc_ref, x1_ref, gt_ref, gf_ref, y_ref):
    tm = acc_ref.shape[0]
    r = gt_ref.shape[0] if gt_ref.shape[0] > 1 else tm
    for s0 in range(0, tm, r):
        sl = slice(s0, s0 + r)
        x2 = x1_ref[sl, :] + gt_ref[...] * acc_ref[sl, :]
        y_ref[sl, :] = (x2 * lax.rsqrt(jnp.mean(x2 * x2, axis=-1, keepdims=True) + NORM_EPS)) * gf_ref[...]


def _ffn_prompt_kernel(h_ref, wug_ref, wuv_ref, wcg_ref, wcv_ref, bg_ref, bv_ref, wd_ref,
                       x1_ref, gt_ref, gf_ref, y_ref, ug_ref, uv_ref, acc_ref, act_s, prev_s,
                       *, blocks_per_seq, n_up):
    i = pl.program_id(0)
    j = pl.program_id(1)
    tm = h_ref.shape[0]
    keep = (i % blocks_per_seq) > 0

    sr = min(FFN_SUB_ROWS, tm)

    def conv(u, prev, wc_ref, b_ref):
        ux = jnp.concatenate([prev, u], axis=0)
        w = wc_ref[...]
        y = u * w[FFN_CONV - 1:FFN_CONV] + b_ref[...]
        for t in range(FFN_CONV - 1):
            sft = FFN_CONV - 1 - t
            y = y + ux[SUBLANES - sft:SUBLANES - sft + sr] * w[t:t + 1]
        return y

    @pl.when(jnp.logical_and(i == 0, j == 0))
    def _():
        prev_s[...] = jnp.zeros_like(prev_s)

    def up(store):
        pg = jnp.where(keep, prev_s[j, 0], 0.0)
        pv = jnp.where(keep, prev_s[j, 1], 0.0)
        n_sub = tm // sr
        rows = [slice(s * sr, (s + 1) * sr) for s in range(n_sub)]

        def project(s):
            hs = h_ref[rows[s], :]
            return _dot(hs, wug_ref[...]), _dot(hs, wuv_ref[...])

        cur = project(0)
        for s in range(n_sub):
            nxt = project(s + 1) if s + 1 < n_sub else None
            ug, uv = cur
            store(rows[s], _silu(conv(ug, pg, wcg_ref, bg_ref)) * conv(uv, pv, wcv_ref, bv_ref))
            pg = ug[sr - SUBLANES:]
            pv = uv[sr - SUBLANES:]
            cur = nxt
        ug_ref[...] = pg
        uv_ref[...] = pv
        prev_s[j, 0] = pg
        prev_s[j, 1] = pv

    _ffn_phases(j, n_up, up, act_s, wd_ref, acc_ref, x1_ref, gt_ref, gf_ref, y_ref)


def _ffn_phases(j, n_up, up, act_s, wd_ref, acc_ref, x1_ref, gt_ref, gf_ref, y_ref):
    tf = act_s.shape[1] // n_up
    tn = wd_ref.shape[1]

    @pl.when(j < n_up)
    def _():
        cols = pl.ds(pl.multiple_of(j * tf, tf), tf)

        def store(rows, act):
            act_s[rows, cols] = act.astype(bf16)
        up(store)

    @pl.when(j >= n_up)
    def _():
        acc_ref[:, pl.ds(pl.multiple_of((j - n_up) * tn, tn), tn)] = _dot(act_s[...], wd_ref[...])

    @pl.when(j == pl.num_programs(1) - 1)
    def _():
        _ffn_epilogue(acc_ref, x1_ref, gt_ref, gf_ref, y_ref)


def _ffn_sample_kernel(h_ref, buf_ref, wug_ref, wuv_ref, wcg_ref, wcv_ref, bg_ref, bv_ref, wd_ref,
                       x1_ref, gt_ref, gf_ref, y_ref, ug_ref, uv_ref, acc_ref, act_s, *, ts, n_up):
    j = pl.program_id(1)
    db = h_ref.shape[0] // ts
    nbuf = FFN_CONV - 1
    tf = wug_ref.shape[1]

    def branch(wu_ref, wc_ref, b_ref, ubuf_ref, half):
        u = _dot(h_ref[...], wu_ref[...])
        rows = [buf_ref[r, :, half * tf:(half + 1) * tf] for r in range(nbuf)]
        rows += [u[t * db:(t + 1) * db] for t in range(ts)]
        for r in range(nbuf):
            ubuf_ref[r] = rows[ts + r]
        w = wc_ref[...]
        outs = []
        for t in range(ts):
            y = rows[t] * w[0:1] + b_ref[...]
            for r in range(1, FFN_CONV):
                y = y + rows[t + r] * w[r:r + 1]
            outs.append(y)
        return jnp.concatenate(outs, axis=0)

    def up(store):
        store(slice(0, ts * db),
              _silu(branch(wug_ref, wcg_ref, bg_ref, ug_ref, 0)) * branch(wuv_ref, wcv_ref, bv_ref, uv_ref, 1))

    _ffn_phases(j, n_up, up, act_s, wd_ref, acc_ref, x1_ref, gt_ref, gf_ref, y_ref)


def _ffn(h2, x1, ada, g_final, w_up, w_conv, b_conv, w_down, group, rows_per_batch, tm, tf, buf=None):
    m, d = h2.shape
    dff = w_down.shape[0]
    nj = dff // tf
    tn = min(FFN_DOWN_TILE, d)
    b2 = b_conv.reshape(1, 2 * dff)
    up = lambda j: jnp.minimum(j, nj - 1)
    common_w = [pl.BlockSpec((d, tf), lambda i, j: (0, up(j))),
                pl.BlockSpec((d, tf), lambda i, j: (0, nj + up(j))),
                pl.BlockSpec((FFN_CONV, tf), lambda i, j: (0, up(j))),
                pl.BlockSpec((FFN_CONV, tf), lambda i, j: (0, nj + up(j))),
                pl.BlockSpec((1, tf), lambda i, j: (0, up(j))),
                pl.BlockSpec((1, tf), lambda i, j: (0, nj + up(j))),
                pl.BlockSpec((dff, tn), lambda i, j: (0, jnp.maximum(j - nj, 0)))]
    tail = [pl.BlockSpec((tm, d), lambda i, j: (i, 0)),
            _mod_spec(group, d, rows_per_batch, tm, 5),
            pl.BlockSpec((1, d), lambda i, j: (0, 0))]
    y_spec = pl.BlockSpec((tm, d), lambda i, j: (i, 0))
    if group == "prompt":
        bps = rows_per_batch // tm
        nb = m // rows_per_batch
        kern = functools.partial(_ffn_prompt_kernel, blocks_per_seq=bps, n_up=nj)
        first = [pl.BlockSpec((tm, d), lambda i, j: (i, 0))]
        ubuf_spec = pl.BlockSpec((None, SUBLANES, tf), lambda i, j: (i, 0, up(j)))
        ubuf_shape = jax.ShapeDtypeStruct((m // tm, SUBLANES, dff), f32)
        args = (h2,)
        extra_scratch = [pltpu.VMEM((nj, 2, SUBLANES, tf), f32)]
    else:
        extra_scratch = []
        ts = m // rows_per_batch
        kern = functools.partial(_ffn_sample_kernel, ts=ts, n_up=nj)
        first = [pl.BlockSpec((tm, d), lambda i, j: (i, 0)),
                 pl.BlockSpec((FFN_CONV - 1, rows_per_batch, 2 * tf), lambda i, j: (0, 0, up(j)))]
        ubuf_spec = pl.BlockSpec((FFN_CONV - 1, rows_per_batch, tf), lambda i, j: (0, 0, up(j)))
        ubuf_shape = jax.ShapeDtypeStruct((FFN_CONV - 1, rows_per_batch, dff), f32)
        args = (h2, buf)
    return pl.pallas_call(
        kern,
        grid=(m // tm, nj + d // tn),
        in_specs=first + common_w + tail,
        out_specs=[y_spec, ubuf_spec, ubuf_spec],
        out_shape=[jax.ShapeDtypeStruct((m, d), f32), ubuf_shape, ubuf_shape],
        scratch_shapes=[pltpu.VMEM((tm, d), f32), pltpu.VMEM((tm, dff), bf16)] + extra_scratch,
        compiler_params=_params(56, 2),
        name="ffn_" + group,
    )(*args, w_up, w_up, w_conv, w_conv, b2, b2, w_down, x1, ada, g_final.reshape(1, d))


def _split_w_in(w_in, d):
    sizes = (QKV_W, Z_W, GDN_HEADS, GDN_HEADS, DQ_W, DKV_W, DKV_W, IQ_W, IDX_DIM, IDX_HEADS, d, d)
    offs = [0]
    for s in sizes:
        offs.append(offs[-1] + s)
    seg = lambda i, j: w_in[:, offs[i]:offs[j]]
    pad = jnp.zeros((w_in.shape[0], LANES - (IDX_DIM + 2 * GDN_HEADS + IDX_HEADS)), w_in.dtype)
    small = jnp.concatenate([seg(8, 9), seg(2, 4), seg(9, 10), pad], axis=1)
    return tuple(w.astype(bf16) for w in (seg(0, 2), seg(4, 8), seg(10, 12), small))


def _in_proj(h, w_parts, tm, name):
    return tuple(_matmul(h, w, tm, _mm_tile(w.shape[1]), name + "_" + part)
                 for w, part in zip(w_parts, ("gdn", "dsa", "gate", "small")))


def _ffn_tile(dff):
    for tf in (512, 256, 128):
        if dff % tf == 0:
            return tf
    raise ValueError("d_ff must be a multiple of 128")


def _mm_tile(n):
    for tn in (1024, 512, 256, 128):
        if n % tn == 0:
            return tn
    raise ValueError("projection width must be a multiple of 128")


def kernel(x_prompt, x_sample, c_prompt, c_sample, cache_k, cache_v, cache_idx_k, page_table, state_gdn, state_gdn_conv, state_ffn_conv, w_ada, b_ada, g_norm1, w_in, w_gdn_conv, a_log, dt_bias, g_gdn_norm, w_gdn_out, w_dsa_out, w_o, g_norm2, w_up, w_ffn_conv, b_ffn_conv, w_down, g_final):
    nb, t, d = x_prompt.shape
    db, ts, _ = x_sample.shape
    depth = w_ada.shape[0]
    assert depth == 1 and db == LANES and ts >= GDN_CONV - 1
    n_pages = page_table.shape[1]
    page = cache_k.shape[2]
    past = n_pages * page
    dff = w_down.shape[1]
    l = 0

    n_c = nb + db
    pad_c = (-n_c) % SUBLANES
    c_all = jnp.concatenate([c_prompt, c_sample, jnp.zeros((pad_c, d), f32)], axis=0)
    ada = _ada(c_all, w_ada[l], b_ada[l])
    ada_p = ada[:nb].reshape(nb, 1, 6 * d)
    ada_s = ada[nb:nb + db]

    w_in_parts = _split_w_in(w_in[l], d)
    wg = w_gdn_out[l].astype(bf16)
    wd = w_dsa_out[l].astype(bf16)
    wo = w_o[l].astype(bf16)
    wup = w_up[l].astype(bf16)
    wdn = w_down[l].astype(bf16)
    tf = _ffn_tile(dff)

    xp = x_prompt.reshape(nb * t, d)
    tm_p = min(512, t)
    h1 = _prep(xp, g_norm1[l], ada_p, "prompt", t, tm_p)
    yp_gdn, yp_dsa, yp_gate, yp_small = _in_proj(h1, w_in_parts, min(IN_PROJ_ROWS, nb * t), "in_proj_prompt")
    og_p, s_p = _gdn_prompt(yp_gdn, yp_small, w_gdn_conv[l], a_log[l], dt_bias[l], g_gdn_norm[l], nb, t)
    dq_p, dk_p, iq_p, ik_p, sm_p = _rope(yp_dsa, yp_small, jnp.arange(t), tm_p, "prompt")
    od_p = _dsa_prompt(iq_p, sm_p, ik_p, dq_p, dk_p, yp_dsa, nb, t, min(DSA_TOPK, t // 4))
    x1_p, h2_p = _merge(og_p, od_p, yp_gate, xp, ada_p, g_norm2[l], wg, wd, wo, "prompt", t, min(256, t))
    y_p, ug_p, uv_p = _ffn(h2_p, x1_p, ada_p, g_final, wup, w_ffn_conv[l], b_ffn_conv[l], wdn,
                           "prompt", t, tm_p, tf)

    nfb = FFN_CONV - 1
    bps_p = t // tm_p
    out_p = (
        y_p.reshape(nb, t, d),
        dk_p.reshape(1, nb, t, DSA_KV_HEADS, DSA_HEAD_DIM),
        yp_dsa[:, OFF_DV:OFF_DV + DKV_W].reshape(1, nb, t, DSA_KV_HEADS, DSA_HEAD_DIM),
        ik_p.reshape(1, nb, t, IDX_DIM),
        s_p[None],
        yp_gdn.reshape(nb, t, -1)[:, t - (GDN_CONV - 1):, :QKV_W][None],
        jnp.concatenate([ug_p[bps_p - 1::bps_p, SUBLANES - nfb:], uv_p[bps_p - 1::bps_p, SUBLANES - nfb:]],
                        axis=-1)[None],
    )

    xs = x_sample.transpose(1, 0, 2).reshape(ts * db, d)
    h1s = _prep(xs, g_norm1[l], ada_s, "sample", db, db)
    ys_gdn, ys_dsa, ys_gate, ys_small = _in_proj(h1s, w_in_parts, ts * db, "in_proj_sample")
    ys_gdn3 = ys_gdn.reshape(ts, db, -1)
    og_s, s_s = _gdn_sample(ys_gdn3, ys_small.reshape(ts, db, LANES), state_gdn_conv[l].transpose(1, 0, 2),
                            state_gdn[l], w_gdn_conv[l], a_log[l], dt_bias[l], g_gdn_norm[l])
    pos_s = jnp.repeat(past + jnp.arange(ts), db)
    dq_s, dk_s, iq_s, ik_s, sm_s = _rope(ys_dsa, ys_small, pos_s, db, "sample", db)
    ik_s_b = ik_s.reshape(ts, db, IDX_DIM).transpose(1, 0, 2)
    lp = past + LANES
    tp = -(-ts // SUBLANES) * SUBLANES
    pad_t = lambda a: jnp.pad(a.astype(f32), ((0, 0), (0, tp - ts), (0, 0)))
    scores = _idx_scores(page_table, pad_t(iq_s.reshape(db, ts, IQ_W)), pad_t(sm_s.reshape(db, ts, LANES)),
                         pad_t(ik_s_b), jnp.swapaxes(cache_idx_k[l], 1, 2), lp, ts)
    sel = _select_sample(scores.reshape(db * ts, lp), min(DSA_TOPK, (past + ts) // 4), past, ts)
    dv_s_b = ys_dsa.reshape(ts, db, -1)[:, :, OFF_DV:OFF_DV + DKV_W].transpose(1, 0, 2)
    ck = cache_k[l].reshape(cache_k.shape[1], page * DSA_KV_HEADS, DSA_HEAD_DIM)
    cv = cache_v[l].reshape(cache_v.shape[1], page * DSA_KV_HEADS, DSA_HEAD_DIM)
    od_s = _dsa_sample(page_table, pad_t(dq_s.reshape(db, ts, DQ_W)), sel.reshape(db, ts, lp),
                       pad_t(dk_s.reshape(db, ts, DKV_W)), pad_t(dv_s_b), ck, cv)
    od_s = od_s.astype(bf16).transpose(1, 0, 2).reshape(ts * db, DQ_W)
    x1_s, h2_s = _merge(og_s.reshape(ts * db, Z_W), od_s, ys_gate, xs, ada_s, g_norm2[l], wg, wd, wo,
                        "sample", db, db)
    fb = state_ffn_conv[l].transpose(1, 0, 2)
    nj = dff // tf
    fb = jnp.concatenate([fb[:, :, :dff].reshape(nfb, db, nj, tf), fb[:, :, dff:].reshape(nfb, db, nj, tf)],
                         axis=-1).reshape(nfb, db, 2 * dff)
    y_s, ug_s, uv_s = _ffn(h2_s, x1_s, ada_s, g_final, wup, w_ffn_conv[l], b_ffn_conv[l], wdn,
                           "sample", db, ts * db, tf, buf=fb)

    out_s = (
        y_s.reshape(ts, db, d).transpose(1, 0, 2),
        dk_s.reshape(1, db, ts, DSA_KV_HEADS, DSA_HEAD_DIM),
        dv_s_b.reshape(1, db, ts, DSA_KV_HEADS, DSA_HEAD_DIM),
        ik_s_b[None],
        s_s[None],
        ys_gdn3[ts - (GDN_CONV - 1):, :, :QKV_W].transpose(1, 0, 2)[None],
        jnp.concatenate([ug_s, uv_s], axis=-1).transpose(1, 0, 2)[None],
    )
    return (out_p[0], out_s[0]) + out_p[1:] + out_s[1:]
```

```python
import functools

import jax
import jax.numpy as jnp
from jax import lax
from jax.experimental import pallas as pl
from jax.experimental.pallas import tpu as pltpu

f32 = jnp.float32
bf16 = jnp.bfloat16

GDN_HEADS = 8
GDN_DK = 128
GDN_DV = 128
GDN_CONV = 4
GDN_CHUNK = 64
GDN_GROUP = 4
GDN_SEQS = 4
DSA_HEADS = 8
DSA_KV_HEADS = 2
DSA_HEAD_DIM = 128
IDX_HEADS = 8
IDX_DIM = 64
IDX_SCALE = IDX_HEADS ** -0.5 * IDX_DIM ** -0.5
DSA_TOPK = 256
ROPE_THETA = 500000.0
ROPE_FRACTION = 4
FFN_CONV = 3
NORM_EPS = 1e-6

LANES = 128
SUBLANES = 8
NEG = float(jnp.finfo(jnp.float32).min)
POS = float(jnp.finfo(jnp.float32).max)
BISECT_UNROLL = 4
SNAP_FROM = 4
BISECT_CAP = 1024
DSA_KEY_TILE = 256
DSA_KEY_EXTENT = 256
IN_PROJ_ROWS = 1024
IDX_BATCH = 4
DSA_SAMPLE_BATCH = 2
DSA_HEAD_PAIR = 2
FFN_DOWN_TILE = 512
FFN_SUB_ROWS = 256

QKV_W = 2 * GDN_HEADS * GDN_DK + GDN_HEADS * GDN_DV
Z_W = GDN_HEADS * GDN_DV
DQ_W = DSA_HEADS * DSA_HEAD_DIM
DKV_W = DSA_KV_HEADS * DSA_HEAD_DIM
IQ_W = IDX_HEADS * IDX_DIM
OFF_Z = QKV_W
GDN_PROJ_W = QKV_W + Z_W
OFF_DQ = 0
OFF_DK = OFF_DQ + DQ_W
OFF_DV = OFF_DK + DKV_W
OFF_IQ = OFF_DV + DKV_W
DSA_PROJ_W = OFF_IQ + IQ_W
SM_IK = 0
SM_A = IDX_DIM
SM_B = SM_A + GDN_HEADS
SM_IW = SM_B + GDN_HEADS


def _sigmoid(x):
    return 1.0 / (1.0 + jnp.exp(-x))


def _silu(x):
    return x * _sigmoid(x)


def _dot(a, b):
    return jnp.dot(a, b, preferred_element_type=f32)


def _dot_nt(a, b):
    return lax.dot_general(a, b, (((1,), (1,)), ((), ())), preferred_element_type=f32)


def _dot_tn(a, b):
    return lax.dot_general(a, b, (((0,), (0,)), ((), ())), preferred_element_type=f32)


def _split3(a):
    hi = a.astype(bf16)
    lo = (a - hi.astype(f32)).astype(bf16)
    return hi, lo


def _mm3(a, b):
    ah, al = a
    bh, bl = b
    return _dot(ah, bh) + (_dot(ah, bl) + _dot(al, bh))


def _dot_exact(a, b):
    return jnp.dot(a, b, preferred_element_type=f32, precision=lax.Precision.HIGHEST)


def _params(vmem_mb=None, n_axes=1):
    kw = dict(dimension_semantics=("arbitrary",) * n_axes)
    if vmem_mb is not None:
        kw["vmem_limit_bytes"] = vmem_mb * 1024 * 1024
    return pltpu.CompilerParams(**kw)


def _ada_kernel(c_ref, w_ref, b_ref, o_ref):
    s = _silu(c_ref[...]).astype(bf16)
    o_ref[...] = _dot(s, w_ref[...].astype(bf16)) + b_ref[...]


def _ada(c_all, w_ada, b_ada):
    m, d = c_all.shape
    n = w_ada.shape[1]
    tn = 1024
    return pl.pallas_call(
        _ada_kernel,
        grid=(n // tn,),
        in_specs=[pl.BlockSpec((m, d), lambda j: (0, 0)),
                  pl.BlockSpec((d, tn), lambda j: (0, j)),
                  pl.BlockSpec((1, tn), lambda j: (0, j))],
        out_specs=pl.BlockSpec((m, tn), lambda j: (0, j)),
        out_shape=jax.ShapeDtypeStruct((m, n), f32),
        compiler_params=_params(48),
        name="ada",
    )(c_all, w_ada, b_ada.reshape(1, n))


def _mod_spec(group, d, rows_per_batch, tm, col):
    if group == "prompt":
        return pl.BlockSpec((None, 1, d), lambda i, *_: ((i * tm) // rows_per_batch, 0, col))
    return pl.BlockSpec((rows_per_batch, d), lambda i, *_: (0, col))


def _prep_kernel(x_ref, g_ref, sc_ref, sh_ref, o_ref):
    x = x_ref[...]
    y = x * lax.rsqrt(jnp.mean(x * x, axis=-1, keepdims=True) + NORM_EPS)
    o_ref[...] = ((y * g_ref[...]) * (1.0 + sc_ref[...]) + sh_ref[...]).astype(bf16)


def _prep(x, g, ada, group, rows_per_batch, tm):
    m, d = x.shape
    return pl.pallas_call(
        _prep_kernel,
        grid=(m // tm,),
        in_specs=[pl.BlockSpec((tm, d), lambda i: (i, 0)),
                  pl.BlockSpec((1, d), lambda i: (0, 0)),
                  _mod_spec(group, d, rows_per_batch, tm, 1),
                  _mod_spec(group, d, rows_per_batch, tm, 0)],
        out_specs=pl.BlockSpec((tm, d), lambda i: (i, 0)),
        out_shape=jax.ShapeDtypeStruct((m, d), bf16),
        compiler_params=_params(),
        name="prep_" + group,
    )(x, g.reshape(1, d), ada, ada)


def _mm_kernel(a_ref, w_ref, o_ref, wb_s):
    @pl.when(pl.program_id(1) == 0)
    def _():
        wb_s[...] = w_ref[...].astype(bf16)

    o_ref[...] = _dot(a_ref[...], wb_s[...])


def _matmul(a, w, tm, tn, name, n=None):
    m, k = a.shape
    n = w.shape[1] if n is None else n
    return pl.pallas_call(
        _mm_kernel,
        grid=(n // tn, m // tm),
        in_specs=[pl.BlockSpec((tm, k), lambda j, i: (i, 0)),
                  pl.BlockSpec((k, tn), lambda j, i: (0, j))],
        out_specs=pl.BlockSpec((tm, tn), lambda j, i: (i, j)),
        out_shape=jax.ShapeDtypeStruct((m, n), f32),
        scratch_shapes=[pltpu.VMEM((k, tn), bf16)],
        compiler_params=_params(48, 2),
        name=name,
    )(a, w)


def _l2n(x):
    return x * lax.rsqrt(jnp.sum(x * x, axis=-1, keepdims=True) + NORM_EPS)


def _gdn_gates(sm, alog, dtb):
    xa = sm + dtb
    softplus = jnp.maximum(xa, 0.0) + jnp.log1p(jnp.exp(-jnp.abs(xa)))
    return -jnp.exp(alog) * softplus, _sigmoid(sm)


def _gated_norm(o, gn, z):
    y = o * lax.rsqrt(jnp.mean(o * o, axis=-1, keepdims=True) + NORM_EPS)
    return (y * gn) * _silu(z)


def _gdn_prompt_kernel(qkv_ref, halo_ref, z_ref, sm_ref, wc_ref, alog_ref, dtb_ref, gn_ref,
                       o_ref, sfin_ref, s_ref):
    c = pl.program_id(1)
    n_seq, C, _ = qkv_ref.shape

    @pl.when(c == 0)
    def _():
        s_ref[...] = jnp.zeros_like(s_ref)

    keep = (c > 0).astype(f32)
    ri = lax.broadcasted_iota(jnp.int32, (C, C), 0)
    ci = lax.broadcasted_iota(jnp.int32, (C, C), 1)
    tril = (ri >= ci).astype(f32)
    gates = [_gdn_gates(sm_ref[s], alog_ref[...], dtb_ref[...]) for s in range(n_seq)]
    beta_all = [g[1] for g in gates]
    gc_all = [_dot_exact(tril, g[0]) for g in gates]
    gc_t = [g.T for g in gc_all]

    def conv(s, col):
        xs = jnp.concatenate([halo_ref[s, :, col:col + LANES] * keep, qkv_ref[s, :, col:col + LANES]], axis=0)
        w = wc_ref[:, col:col + LANES]
        y = xs[SUBLANES:] * w[GDN_CONV - 1:GDN_CONV]
        for i in range(GDN_CONV - 1):
            sft = GDN_CONV - 1 - i
            y = y + xs[SUBLANES - sft:SUBLANES - sft + C] * w[i:i + 1]
        return _silu(y)

    N = GDN_GROUP * C
    rn = lax.broadcasted_iota(jnp.int32, (N, N), 0)
    cn = lax.broadcasted_iota(jnp.int32, (N, N), 1)
    same = (rn // C) == (cn // C)
    incl = jnp.logical_and(same, rn >= cn)
    strict = jnp.logical_and(same, rn > cn)
    eye_f = (rn == cn).astype(f32)
    n_sq = max(1, (C - 1).bit_length() - 1)
    groups = [(s, list(range(g0, g0 + GDN_GROUP))) for s in range(n_seq)
              for g0 in range(0, GDN_HEADS, GDN_GROUP)]
    stack = lambda xs: jnp.concatenate(xs, axis=0)

    qs = [stack([_l2n(conv(s, h * GDN_DK)) * (GDN_DK ** -0.5) for h in hs]) for s, hs in groups]
    ks = [stack([_l2n(conv(s, GDN_HEADS * GDN_DK + h * GDN_DK)) for h in hs]) for s, hs in groups]
    vs = [stack([conv(s, 2 * GDN_HEADS * GDN_DK + h * GDN_DV) for h in hs]) for s, hs in groups]
    gcs = [stack([gc_all[s][:, SM_A + h:SM_A + h + 1] for h in hs]) for s, hs in groups]
    betas = [stack([beta_all[s][:, SM_B + h:SM_B + h + 1] for h in hs]) for s, hs in groups]
    gc_rows = [jnp.concatenate([gc_t[s][SM_A + h:SM_A + h + 1, :] for h in hs], axis=1)
               for s, hs in groups]
    egs = [jnp.exp(gc) for gc in gcs]
    decays = [jnp.where(incl, jnp.exp(jnp.where(incl, gc - gr, 0.0)), 0.0) for gc, gr in zip(gcs, gc_rows)]
    kbs = [k.astype(bf16) for k in ks]
    a_s = [jnp.where(strict, b * _dot_nt(kb, kb) * dc, 0.0) for b, kb, dc in zip(betas, kbs, decays)]
    qks = [(_dot_nt(q.astype(bf16), kb) * dc).astype(bf16) for q, kb, dc in zip(qs, kbs, decays)]

    invs = [eye_f - a for a in a_s]
    pws = [a.astype(bf16) for a in a_s]
    for _ in range(n_sq):
        pws = [_dot(p, p).astype(bf16) for p in pws]
        invs = [inv + _dot(inv.astype(bf16), p) for inv, p in zip(invs, pws)]
    inv_s = [_split3(inv) for inv in invs]
    res = [eye_f - inv - _mm3(_split3(a), sp) for a, inv, sp in zip(a_s, invs, inv_s)]
    invs = [inv + _dot(sp[0], r.astype(bf16)) for inv, sp, r in zip(invs, inv_s, res)]
    rhs = [jnp.concatenate([b * v, (b * eg) * k], axis=1) for b, v, eg, k in zip(betas, vs, egs, ks)]
    sols = [_mm3(_split3(inv), _split3(r)) for inv, r in zip(invs, rhs)]

    for gi, (s, hs) in enumerate(groups):
        sol, q, k, gc, eg = sols[gi], qs[gi], ks[gi], gcs[gi], egs[gi]
        q_dec = (q * eg).astype(bf16)
        w_k = sol[:, GDN_DV:].astype(bf16)
        rows = [slice(j * C, (j + 1) * C) for j in range(GDN_GROUP)]
        s_old = [s_ref[s, h] for h in hs]
        ws = [_dot(jnp.concatenate([w_k[r], q_dec[r]], axis=0), so.astype(bf16)) for r, so in zip(rows, s_old)]
        u = stack([sol[r, :GDN_DV] - w[:C] for r, w in zip(rows, ws)])
        ub = u.astype(bf16)
        o_intra = _dot(qks[gi], ub)
        for j, h in enumerate(hs):
            r = rows[j]
            gl = gc[r][C - 1:C, :]
            k_end = (k[r] * jnp.exp(gl - gc[r])).astype(bf16)
            s_ref[s, h] = s_old[j] * jnp.exp(gl) + _dot_tn(k_end, ub[r])
            o = ws[j][C:] + o_intra[r]
            zh = z_ref[s, :, h * GDN_DV:(h + 1) * GDN_DV]
            o_ref[s, :, h * GDN_DV:(h + 1) * GDN_DV] = _gated_norm(o, gn_ref[...], zh).astype(bf16)

    @pl.when(c == pl.num_programs(1) - 1)
    def _():
        sfin_ref[...] = s_ref[...]


def _gdn_vecs(a_log, dt_bias, g_gdn_norm):
    alog = jnp.zeros((1, LANES), f32).at[0, SM_A:SM_A + GDN_HEADS].set(a_log)
    dtb = jnp.zeros((1, LANES), f32).at[0, SM_A:SM_A + GDN_HEADS].set(dt_bias)
    return alog, dtb, g_gdn_norm.reshape(1, GDN_DV)


def _gdn_prompt(y, y_small, w_conv, a_log, dt_bias, g_gdn_norm, nb, t):
    C = GDN_CHUNK
    nc = t // C
    ns = GDN_SEQS if nb % GDN_SEQS == 0 else 1
    alog, dtb, gn = _gdn_vecs(a_log, dt_bias, g_gdn_norm)
    cst = lambda b, c: (0, 0)
    y3 = y.reshape(nb, t, -1)
    ysm3 = y_small.reshape(nb, t, LANES)
    o, s_fin = pl.pallas_call(
        _gdn_prompt_kernel,
        grid=(nb // ns, nc),
        in_specs=[pl.BlockSpec((ns, C, QKV_W), lambda b, c: (b, c, 0)),
                  pl.BlockSpec((ns, SUBLANES, QKV_W),
                               lambda b, c: (b, jnp.maximum(c * (C // SUBLANES) - 1, 0), 0)),
                  pl.BlockSpec((ns, C, Z_W), lambda b, c: (b, c, OFF_Z // Z_W)),
                  pl.BlockSpec((ns, C, LANES), lambda b, c: (b, c, 0)),
                  pl.BlockSpec((GDN_CONV, QKV_W), cst),
                  pl.BlockSpec((1, LANES), cst),
                  pl.BlockSpec((1, LANES), cst),
                  pl.BlockSpec((1, GDN_DV), cst)],
        out_specs=[pl.BlockSpec((ns, C, Z_W), lambda b, c: (b, c, 0)),
                   pl.BlockSpec((ns, GDN_HEADS, GDN_DK, GDN_DV), lambda b, c: (b, 0, 0, 0))],
        out_shape=[jax.ShapeDtypeStruct((nb, t, Z_W), bf16),
                   jax.ShapeDtypeStruct((nb, GDN_HEADS, GDN_DK, GDN_DV), f32)],
        scratch_shapes=[pltpu.VMEM((ns, GDN_HEADS, GDN_DK, GDN_DV), f32)],
        compiler_params=_params(None, 2),
        name="gdn_prompt",
    )(y3, y3, y3, ysm3, w_conv, alog, dtb, gn)
    return o.reshape(nb * t, Z_W), s_fin


def _gdn_sample_kernel(qkv_ref, buf_ref, z_ref, sm_ref, sin_ref, wc_ref, alog_ref, dtb_ref, gn_ref,
                       o_ref, sout_ref, q_s, k_s, v_s, a_s, b_s, o_s):
    ts, G, _ = qkv_ref.shape
    nbuf = GDN_CONV - 1

    for h in range(GDN_HEADS):
        for part, dst in ((0, q_s), (1, k_s), (2, v_s)):
            col = part * GDN_HEADS * GDN_DK + h * GDN_DK
            w = wc_ref[:, col:col + LANES]
            rows = [buf_ref[i, :, col:col + LANES] for i in range(nbuf)]
            rows += [qkv_ref[t, :, col:col + LANES] for t in range(ts)]
            for t in range(ts):
                y = rows[t] * w[0:1]
                for i in range(1, GDN_CONV):
                    y = y + rows[t + i] * w[i:i + 1]
                y = _silu(y)
                if part == 0:
                    y = _l2n(y) * (GDN_DK ** -0.5)
                elif part == 1:
                    y = _l2n(y)
                dst[t, :, h * LANES:(h + 1) * LANES] = y

    for t in range(ts):
        g_all, beta_all = _gdn_gates(sm_ref[t], alog_ref[...], dtb_ref[...])
        a_all = jnp.exp(g_all)
        for h in range(GDN_HEADS):
            a_s[t, :, h * LANES:(h + 1) * LANES] = jnp.broadcast_to(a_all[:, SM_A + h:SM_A + h + 1], (G, LANES))
            b_s[t, :, h * LANES:(h + 1) * LANES] = jnp.broadcast_to(beta_all[:, SM_B + h:SM_B + h + 1], (G, LANES))

    nv = 2 * ts
    sr = lax.broadcasted_iota(jnp.int32, (3 * nv, nv * LANES), 0)
    sc = lax.broadcasted_iota(jnp.int32, (3 * nv, nv * LANES), 1)
    spread = (sr % nv == sc // LANES).astype(bf16)

    def head(h, carry):
        cols = pl.ds(pl.multiple_of(h * LANES, LANES), LANES)
        for i in range(G):
            row = slice(i, i + 1)
            kq = jnp.concatenate([k_s[t, row, cols] for t in range(ts)]
                                 + [q_s[t, row, cols] for t in range(ts)], axis=0)
            hi = kq.astype(bf16).astype(f32)
            mid = (kq - hi).astype(bf16).astype(f32)
            lo = (kq - hi) - mid
            kq_b = _dot(jnp.concatenate([hi, mid, lo], axis=0).T.astype(bf16), spread)
            s = sin_ref[i, h]
            for t in range(ts):
                kc = kq_b[:, t * LANES:(t + 1) * LANES]
                qc = kq_b[:, (ts + t) * LANES:(ts + t + 1) * LANES]
                a = a_s[t, row, cols]
                b = b_s[t, row, cols]
                ks = jnp.sum(s * kc, axis=0, keepdims=True)
                r = b * (v_s[t, row, cols] - a * ks)
                s = a * s + kc * r
                o_s[t, row, cols] = jnp.sum(s * qc, axis=0, keepdims=True)
            sout_ref[i, h] = s
        return carry

    lax.fori_loop(0, GDN_HEADS, head, 0)

    for t in range(ts):
        for h in range(GDN_HEADS):
            sl = slice(h * GDN_DV, (h + 1) * GDN_DV)
            o_ref[t, :, sl] = _gated_norm(o_s[t, :, sl], gn_ref[...], z_ref[t, :, sl]).astype(bf16)


def _gdn_sample(y3, y3_small, buf3, state, w_conv, a_log, dt_bias, g_gdn_norm):
    ts, db, _ = y3.shape
    G = SUBLANES
    alog, dtb, gn = _gdn_vecs(a_log, dt_bias, g_gdn_norm)
    cst = lambda g: (0, 0)
    st_spec = pl.BlockSpec((G, GDN_HEADS, GDN_DK, GDN_DV), lambda g: (g, 0, 0, 0))
    scr = pltpu.VMEM((ts, G, Z_W), f32)
    return pl.pallas_call(
        _gdn_sample_kernel,
        grid=(db // G,),
        in_specs=[pl.BlockSpec((ts, G, QKV_W), lambda g: (0, g, 0)),
                  pl.BlockSpec((GDN_CONV - 1, G, QKV_W), lambda g: (0, g, 0)),
                  pl.BlockSpec((ts, G, Z_W), lambda g: (0, g, OFF_Z // Z_W)),
                  pl.BlockSpec((ts, G, LANES), lambda g: (0, g, 0)),
                  st_spec,
                  pl.BlockSpec((GDN_CONV, QKV_W), cst),
                  pl.BlockSpec((1, LANES), cst),
                  pl.BlockSpec((1, LANES), cst),
                  pl.BlockSpec((1, GDN_DV), cst)],
        out_specs=[pl.BlockSpec((ts, G, Z_W), lambda g: (0, g, 0)), st_spec],
        out_shape=[jax.ShapeDtypeStruct((ts, db, Z_W), bf16),
                   jax.ShapeDtypeStruct(state.shape, f32)],
        scratch_shapes=[scr, scr, scr, scr, scr, scr],
        compiler_params=_params(48),
        name="gdn_sample",
    )(y3, buf3, y3, y3_small, state, w_conv, alog, dtb, gn)


def _rope_tables(pos, rot, width):
    half = rot // 2
    inv_freq = ROPE_THETA ** (-jnp.arange(half, dtype=f32) * (2.0 / rot))
    ang = pos.astype(f32)[:, None] * inv_freq[None, :]
    cos, sin = jnp.cos(ang), jnp.sin(ang)
    n = pos.shape[0]
    z = lambda w: jnp.zeros((n, w), f32)
    cosw = jnp.concatenate([cos, cos, jnp.ones((n, width - rot), f32)], axis=1)
    sina = jnp.concatenate([-sin, z(width - half)], axis=1)
    sinb = jnp.concatenate([z(half), sin, z(width - rot)], axis=1)
    reps = LANES // width
    return tuple(jnp.tile(a, (1, reps)) for a in (cosw, sina, sinb))


def _rope_kernel(dq_ref, dk_ref, iq_ref, sm_ref, c1, sa1, sb1, c2, sa2, sb2,
                 dq_o, dk_o, iq_o, ik_o, sm_o):
    h1 = DSA_HEAD_DIM // ROPE_FRACTION // 2
    h2 = IDX_DIM // ROPE_FRACTION // 2

    def rot(x, c, sa, sb, half):
        return x * c[...] + pltpu.roll(x, LANES - half, 1) * sa[...] + pltpu.roll(x, half, 1) * sb[...]

    for j in range(DQ_W // LANES):
        sl = slice(j * LANES, (j + 1) * LANES)
        dq_o[:, sl] = rot(dq_ref[:, sl], c1, sa1, sb1, h1).astype(bf16)
    for j in range(DKV_W // LANES):
        sl = slice(j * LANES, (j + 1) * LANES)
        dk_o[:, sl] = rot(dk_ref[:, sl], c1, sa1, sb1, h1)
    for j in range(IQ_W // LANES):
        sl = slice(j * LANES, (j + 1) * LANES)
        iq_o[:, sl] = rot(iq_ref[:, sl], c2, sa2, sb2, h2).astype(bf16)
    sm = sm_ref[...]
    ik_o[...] = rot(sm, c2, sa2, sb2, h2)[:, :IDX_DIM]
    sm_o[...] = sm


def _rope(y, y_small, pos, tm, group, db=None):
    m = y.shape[0]
    t1 =_rope_tables(pos, DSA_HEAD_DIM // ROPE_FRACTION, DSA_HEAD_DIM)
    t2 = _rope_tables(pos, IDX_DIM // ROPE_FRACTION, IDX_DIM)
    tab_blocks = pos.shape[0] // tm
    tab = pl.BlockSpec((tm, LANES), lambda i: (i % tab_blocks, 0))
    if group == "prompt":
        omap = lambda i: (i, 0)
        rows = lambda w: m
        cols = lambda w: w
    else:
        ts = m // db
        omap = lambda i: (0, i)
        rows = lambda w: db
        cols = lambda w: ts * w
    out = lambda w, dt: jax.ShapeDtypeStruct((rows(w), cols(w)), dt)
    return pl.pallas_call(
        _rope_kernel,
        grid=(m // tm,),
        in_specs=[pl.BlockSpec((tm, DQ_W), lambda i: (i, OFF_DQ // DQ_W)),
                  pl.BlockSpec((tm, DKV_W), lambda i: (i, OFF_DK // DKV_W)),
                  pl.BlockSpec((tm, IQ_W), lambda i: (i, OFF_IQ // IQ_W)),
                  pl.BlockSpec((tm, LANES), lambda i: (i, 0)),
                  tab, tab, tab, tab, tab, tab],
        out_specs=[pl.BlockSpec((tm, DQ_W), omap),
                   pl.BlockSpec((tm, DKV_W), omap),
                   pl.BlockSpec((tm, IQ_W), omap),
                   pl.BlockSpec((tm, IDX_DIM), lambda i: (i, 0)),
                   pl.BlockSpec((tm, LANES), omap)],
        out_shape=[out(DQ_W, bf16), out(DKV_W, f32), out(IQ_W, bf16),
                   jax.ShapeDtypeStruct((m, IDX_DIM), f32), out(LANES, f32)],
        compiler_params=_params(),
        name="rope_" + group,
    )(y, y, y, y_small, *t1, *t2)


def _topk_bias(x_s, bias_s, lo0, mx, few, nkt, kt_w, k):
    R = x_s.shape[0]
    kf = float(k)
    tiles = [slice(kt * kt_w, (kt + 1) * kt_w) for kt in range(nkt)]
    n_grp = 2 if R % (2 * SUBLANES) == 0 else 1
    grps = [slice(g * (R // n_grp), (g + 1) * (R // n_grp)) for g in range(n_grp)]

    def count_ge(rows, th):
        acc = jnp.where(x_s[rows, tiles[0]] >= th, 1.0, 0.0)
        for t in tiles[1:]:
            acc = acc + jnp.where(x_s[rows, t] >= th, 1.0, 0.0)
        return jnp.sum(acc, axis=1, keepdims=True)

    def bisect(rows, lo, hi, done):
        mid = 0.5 * lo + 0.5 * hi
        cnt = count_ge(rows, mid)
        collapsed = jnp.logical_or(mid <= lo, mid >= hi)
        live = jnp.logical_and(done < 0.5, jnp.logical_not(collapsed))
        lo = jnp.where(jnp.logical_and(live, cnt >= kf), mid, lo)
        hi = jnp.where(jnp.logical_and(live, cnt <= kf), mid, hi)
        done = jnp.where(jnp.logical_or(collapsed, cnt == kf), 1.0, done)
        return lo, hi, done

    def snap(rows, lo, hi, done):
        a = b = None
        for t in tiles:
            x = x_s[rows, t]
            at = jnp.where(x >= lo, x, POS)
            bt = jnp.where(x < hi, x, NEG)
            a = at if a is None else jnp.minimum(a, at)
            b = bt if b is None else jnp.maximum(b, bt)
        a = jnp.min(a, axis=1, keepdims=True)
        b = jnp.max(b, axis=1, keepdims=True)
        live = done < 0.5
        return jnp.where(live, a, lo), jnp.where(jnp.logical_and(live, a >= b), 1.0, done)

    def cond(c):
        it, st = c
        left = st[0][2]
        for s in st[1:]:
            left = jnp.minimum(left, s[2])
        return jnp.logical_and(it < BISECT_CAP, jnp.min(left) < 0.5)

    def body(c):
        it, st = c
        for _ in range(BISECT_UNROLL):
            st = [bisect(rows, *s) for rows, s in zip(grps, st)]

        def snapped():
            out = []
            for rows, (lo, hi, done) in zip(grps, st):
                lo2, done2 = snap(rows, lo, hi, done)
                out.append((lo2, hi, done2))
            return out

        st = lax.cond(it >= SNAP_FROM, snapped, lambda: st)
        return it + 1, st

    hi0 = jnp.where(few, lo0, mx + (jnp.abs(mx) + 1.0))
    st0 = [(lo0[rows], hi0[rows], few[rows].astype(f32)) for rows in grps]
    _, st = lax.while_loop(cond, body, (jnp.int32(0), st0))
    lo = jnp.concatenate([s[0] for s in st], axis=0)
    hi = jnp.concatenate([s[1] for s in st], axis=0)

    has_run = jnp.max(jnp.where(lo < hi, 1.0, 0.0)) > 0.5
    rows_all = slice(0, R)

    @pl.when(jnp.logical_not(has_run))
    def _():
        for t in tiles:
            bias_s[:, t] = jnp.where(x_s[:, t] >= hi, 0.0, NEG)

    @pl.when(has_run)
    def _():
        need = kf - count_ge(rows_all, hi)
        ai = lax.broadcasted_iota(jnp.int32, (kt_w, kt_w), 0)
        bi = lax.broadcasted_iota(jnp.int32, (kt_w, kt_w), 1)
        before = (ai < bi).astype(bf16)
        seen = jnp.zeros((R, 1), f32)
        for t in tiles:
            x = x_s[:, t]
            run = jnp.logical_and(x >= lo, x < hi)
            runf = run.astype(f32)
            rank = seen + _dot(runf.astype(bf16), before)
            take = jnp.logical_or(x >= hi, jnp.logical_and(run, rank < need))
            bias_s[:, t] = jnp.where(take, 0.0, NEG)
            seen = seen + jnp.sum(runf, axis=1, keepdims=True)


def _dsa_prompt_kernel(iq_ref, sm_ref, ik_ref, q_ref, k_ref, v_ref, o_ref,
                       ikb, kb, vb, x_s, bias_s, s_s, *, topk, kt_w, ext_w):
    qi = pl.program_id(1)
    R = iq_ref.shape[0]
    T = ik_ref.shape[0]

    @pl.when(qi == 0)
    def _():
        ikb[...] = ik_ref[...].astype(bf16)
        kb[...] = k_ref[...].astype(bf16)
        vb[...] = v_ref[...].astype(bf16)

    def block(ext):
        tiles = [slice(j * kt_w, (j + 1) * kt_w) for j in range(ext // kt_w)]
        wgt = sm_ref[...] * IDX_SCALE
        qpos = qi * R + lax.broadcasted_iota(jnp.int32, (R, kt_w), 0)
        key0 = lax.broadcasted_iota(jnp.int32, (R, kt_w), 1)
        mn = mx = None
        for j, t in enumerate(tiles):
            keys = ikb[t, :]
            sc = jnp.zeros((R, kt_w), f32)
            for h in range(IDX_HEADS):
                rel = jnp.maximum(_dot_nt(iq_ref[:, h * IDX_DIM:(h + 1) * IDX_DIM], keys), 0.0)
                sc = sc + rel * wgt[:, SM_IW + h:SM_IW + h + 1]
            causal = key0 + j * kt_w <= qpos
            x_s[:, t] = jnp.where(causal, sc, NEG)
            lo_t = jnp.where(causal, sc, POS)
            hi_t = jnp.where(causal, sc, NEG)
            mn = lo_t if mn is None else jnp.minimum(mn, lo_t)
            mx = hi_t if mx is None else jnp.maximum(mx, hi_t)
        n_causal = qi * R + lax.broadcasted_iota(jnp.int32, (R, 1), 0) + 1
        _topk_bias(x_s, bias_s, jnp.min(mn, axis=1, keepdims=True), jnp.max(mx, axis=1, keepdims=True),
                   n_causal <= topk, len(tiles), kt_w, topk)

        def heads(hp, carry):
            h0 = hp * DSA_HEAD_PAIR
            g = h0 // (DSA_HEADS // DSA_KV_HEADS)
            gc = pl.ds(pl.multiple_of(g * DSA_HEAD_DIM, DSA_HEAD_DIM), DSA_HEAD_DIM)
            hcs = [pl.ds(pl.multiple_of((h0 + i) * DSA_HEAD_DIM, DSA_HEAD_DIM), DSA_HEAD_DIM)
                   for i in range(DSA_HEAD_PAIR)]
            qs = [q_ref[:, hc] for hc in hcs]
            mxa = [None] * DSA_HEAD_PAIR
            for t in tiles:
                kt = kb[t, gc]
                bt = bias_s[:, t]
                for i in range(DSA_HEAD_PAIR):
                    s = _dot_nt(qs[i], kt) * (DSA_HEAD_DIM ** -0.5) + bt
                    s_s[i, :, t] = s
                    mxa[i] = s if mxa[i] is None else jnp.maximum(mxa[i], s)
            ms = [jnp.max(a, axis=1, keepdims=True) for a in mxa]
            la = [jnp.zeros((R, kt_w), f32)] * DSA_HEAD_PAIR
            acc = [jnp.zeros((R, DSA_HEAD_DIM), f32)] * DSA_HEAD_PAIR
            for t in tiles:
                vt = vb[t, gc]
                for i in range(DSA_HEAD_PAIR):
                    p = jnp.exp(s_s[i, :, t] - ms[i])
                    la[i] = la[i] + p
                    acc[i] = acc[i] + _dot(p.astype(bf16), vt)
            for i in range(DSA_HEAD_PAIR):
                o_ref[:, hcs[i]] = (acc[i] / jnp.sum(la[i], axis=1, keepdims=True)).astype(bf16)
            return carry

        lax.fori_loop(0, DSA_HEADS // DSA_HEAD_PAIR, heads, 0)

    for e in range(T // ext_w):
        pl.when(qi // (ext_w // R) == e)(functools.partial(block, (e + 1) * ext_w))


def _dsa_prompt(iq, sm, ik, dq, dk, y, nb, t, topk):
    R = LANES
    nq = t // R
    kt_w = min(DSA_KEY_TILE, t)
    ext_w = min(DSA_KEY_EXTENT, t)
    kern = functools.partial(_dsa_prompt_kernel, topk=topk, kt_w=kt_w, ext_w=ext_w)
    return pl.pallas_call(
        kern,
        grid=(nb, nq),
        in_specs=[pl.BlockSpec((R, IQ_W), lambda b, i: (b * nq + i, 0)),
                  pl.BlockSpec((R, LANES), lambda b, i: (b * nq + i, 0)),
                  pl.BlockSpec((t, IDX_DIM), lambda b, i: (b, 0)),
                  pl.BlockSpec((R, DQ_W), lambda b, i: (b * nq + i, 0)),
                  pl.BlockSpec((t, DKV_W), lambda b, i: (b, 0)),
                  pl.BlockSpec((t, DKV_W), lambda b, i: (b, OFF_DV // DKV_W))],
        out_specs=pl.BlockSpec((R, DQ_W), lambda b, i: (b * nq + i, 0)),
        out_shape=jax.ShapeDtypeStruct((nb * t, DQ_W), bf16),
        scratch_shapes=[pltpu.VMEM((t, IDX_DIM), bf16), pltpu.VMEM((t, DKV_W), bf16),
                        pltpu.VMEM((t, DKV_W), bf16), pltpu.VMEM((R, t), f32), pltpu.VMEM((R, t), f32),
                        pltpu.VMEM((DSA_HEAD_PAIR, R, t), f32)],
        compiler_params=_params(48, 2),
        name="dsa_prompt",
    )(iq, sm, ik, dq, dk, y)


def _idx_score_kernel(pt_ref, iq_ref, sm_ref, ikn_ref, *rest, n_pages, page):
    eb, tp, _ = iq_ref.shape
    o_ref = rest[eb * n_pages]
    ts = o_ref.shape[1]
    past = n_pages * page
    lp = o_ref.shape[2]
    for e in range(eb):
        pages = rest[e * n_pages:(e + 1) * n_pages]
        keys_t = jnp.concatenate([p[...] for p in pages], axis=1).astype(bf16)
        keys_n = jnp.concatenate([ikn_ref[e], jnp.zeros((lp - past - tp, IDX_DIM), f32)], axis=0).astype(bf16)
        wgt = sm_ref[e] * IDX_SCALE
        iq = iq_ref[e].astype(bf16)
        score = jnp.zeros((tp, lp), f32)
        for h in range(IDX_HEADS):
            iqh = iq[:, h * IDX_DIM:(h + 1) * IDX_DIM]
            rel = jnp.maximum(jnp.concatenate([_dot(iqh, keys_t), _dot_nt(iqh, keys_n)], axis=1), 0.0)
            score = score + rel * wgt[:, SM_IW + h:SM_IW + h + 1]
        o_ref[e] = score[:ts]


def _idx_scores(page_table, iq, sm, ikn, cache_idx, lp, ts):
    db, tp, _ = iq.shape
    n_pages = page_table.shape[1]
    page = cache_idx.shape[2]
    eb = IDX_BATCH
    kern = functools.partial(_idx_score_kernel, n_pages=n_pages, page=page)
    page_specs = [pl.BlockSpec((None, IDX_DIM, page),
                               functools.partial(lambda b, pt, e, p: (pt[b * eb + e, p], 0, 0), e=e, p=p))
                  for e in range(eb) for p in range(n_pages)]
    grid_spec = pltpu.PrefetchScalarGridSpec(
        num_scalar_prefetch=1,
        grid=(db // eb,),
        in_specs=[pl.BlockSpec((eb, tp, IQ_W), lambda b, pt: (b, 0, 0)),
                  pl.BlockSpec((eb, tp, LANES), lambda b, pt: (b, 0, 0)),
                  pl.BlockSpec((eb, tp, IDX_DIM), lambda b, pt: (b, 0, 0))] + page_specs,
        out_specs=pl.BlockSpec((eb, ts, lp), lambda b, pt: (b, 0, 0)),
    )
    return pl.pallas_call(
        kern, grid_spec=grid_spec,
        out_shape=jax.ShapeDtypeStruct((db, ts, lp), f32),
        compiler_params=_params(),
        name="idx_scores_sample",
    )(page_table, iq, sm, ikn, *([cache_idx] * (eb * n_pages)))


def _select_sample_kernel(x_ref, bias_ref, x_s, *, topk, past, ts):
    R, lp = x_ref.shape
    key = lax.broadcasted_iota(jnp.int32, (R, lp), 1)
    t = lax.broadcasted_iota(jnp.int32, (R, lp), 0) % ts
    causal = key <= past + t
    x = x_ref[...]
    x_s[...] = jnp.where(causal, x, NEG)
    lo0 = jnp.min(jnp.where(causal, x, POS), axis=1, keepdims=True)
    mx = jnp.max(jnp.where(causal, x, NEG), axis=1, keepdims=True)
    n_causal = past + lax.broadcasted_iota(jnp.int32, (R, 1), 0) % ts + 1
    _topk_bias(x_s, bias_ref, lo0, mx, n_causal <= topk, lp // LANES, LANES, topk)


def _select_sample(scores, topk, past, ts):
    m, lp = scores.shape
    R = LANES
    kern = functools.partial(_select_sample_kernel, topk=topk, past=past, ts=ts)
    return pl.pallas_call(
        kern, grid=(m // R,),
        in_specs=[pl.BlockSpec((R, lp), lambda i: (i, 0))],
        out_specs=pl.BlockSpec((R, lp), lambda i: (i, 0)),
        out_shape=jax.ShapeDtypeStruct((m, lp), f32),
        scratch_shapes=[pltpu.VMEM((R, lp), f32)],
        compiler_params=_params(),
        name="select_sample",
    )(scores)


def _dsa_sample_kernel(pt_ref, q_ref, sel_ref, kn_ref, vn_ref, *rest, n_pages, page):
    eb, tp, _ = q_ref.shape
    o_ref = rest[2 * eb * n_pages]
    ts = sel_ref.shape[1]
    past = n_pages * page
    lp = sel_ref.shape[2]
    hpg = DSA_HEADS // DSA_KV_HEADS
    padn = jnp.zeros((lp - past - tp, DSA_HEAD_DIM), f32)
    for e in range(eb):
        kp = rest[e * n_pages:(e + 1) * n_pages]
        vp = rest[(eb + e) * n_pages:(eb + e + 1) * n_pages]
        bias = jnp.concatenate([sel_ref[e], jnp.zeros((tp - ts, lp), f32)], axis=0)
        bias = jnp.concatenate([bias] * hpg, axis=0)
        for g in range(DSA_KV_HEADS):
            gsl = slice(g * DSA_HEAD_DIM, (g + 1) * DSA_HEAD_DIM)
            kg = jnp.concatenate([p[pl.ds(g, page, stride=DSA_KV_HEADS), :] for p in kp]
                                 + [kn_ref[e, :, gsl], padn], axis=0).astype(bf16)
            vg = jnp.concatenate([p[pl.ds(g, page, stride=DSA_KV_HEADS), :] for p in vp]
                                 + [vn_ref[e, :, gsl], padn], axis=0).astype(bf16)
            qg = jnp.concatenate([q_ref[e, :, (g * hpg + j) * DSA_HEAD_DIM:(g * hpg + j + 1) * DSA_HEAD_DIM]
                                  for j in range(hpg)], axis=0).astype(bf16)
            s = _dot_nt(qg, kg) * (DSA_HEAD_DIM ** -0.5) + bias
            m = jnp.max(s, axis=1, keepdims=True)
            p = jnp.exp(s - m)
            l = jnp.sum(p, axis=1, keepdims=True)
            o = _dot(p.astype(bf16), vg) / l
            for j in range(hpg):
                hsl = slice((g * hpg + j) * DSA_HEAD_DIM, (g * hpg + j + 1) * DSA_HEAD_DIM)
                o_ref[e, :, hsl] = o[j * tp:j * tp + ts]


def _dsa_sample(page_table, dq, sel, kn, vn, ck, cv):
    db, tp, _ = dq.shape
    ts = sel.shape[1]
    lp = sel.shape[2]
    n_pages = page_table.shape[1]
    rows = ck.shape[1]
    page = rows // DSA_KV_HEADS
    eb = DSA_SAMPLE_BATCH
    kern = functools.partial(_dsa_sample_kernel, n_pages=n_pages, page=page)
    pspec = [pl.BlockSpec((None, rows, DSA_HEAD_DIM),
                          functools.partial(lambda b, pt, e, p: (pt[b * eb + e, p], 0, 0), e=e, p=p))
             for e in range(eb) for p in range(n_pages)]
    per_b = lambda r, w: pl.BlockSpec((eb, r, w), lambda b, pt: (b, 0, 0))
    grid_spec = pltpu.PrefetchScalarGridSpec(
        num_scalar_prefetch=1,
        grid=(db // eb,),
        in_specs=[per_b(tp, DQ_W), per_b(ts, lp), per_b(tp, DKV_W), per_b(tp, DKV_W)] + pspec + pspec,
        out_specs=per_b(ts, DQ_W),
    )
    return pl.pallas_call(
        kern, grid_spec=grid_spec,
        out_shape=jax.ShapeDtypeStruct((db, ts, DQ_W), f32),
        compiler_params=_params(48),
        name="dsa_sample",
    )(page_table, dq, sel, kn, vn, *([ck] * (eb * n_pages)), *([cv] * (eb * n_pages)))


def _merge_kernel(og_ref, od_ref, gg_ref, gd_ref, x_ref, gt_ref, sc_ref, sh_ref, g2_ref,
                  wg_ref, wd_ref, wo_ref, x1_ref, h2_ref):
    mix = (_sigmoid(gg_ref[...]) * _dot(og_ref[...], wg_ref[...])
           + _sigmoid(gd_ref[...]) * _dot(od_ref[...], wd_ref[...]))
    x1 = x_ref[...] + gt_ref[...] * _dot(mix.astype(bf16), wo_ref[...])
    x1_ref[...] = x1
    y = x1 * lax.rsqrt(jnp.mean(x1 * x1, axis=-1, keepdims=True) + NORM_EPS)
    h2_ref[...] = ((y * g2_ref[...]) * (1.0 + sc_ref[...]) + sh_ref[...]).astype(bf16)


def _merge(o_gdn, o_dsa, y, x, ada, g2, wg, wd, wo, group, rows_per_batch, tm):
    m, d = x.shape
    row = lambda w, blk=0: pl.BlockSpec((tm, w), lambda i: (i, blk))
    res = lambda a: pl.BlockSpec(a.shape, lambda i: (0, 0), pipeline_mode=pl.Buffered(1))
    return pl.pallas_call(
        _merge_kernel,
        grid=(m // tm,),
        in_specs=[row(Z_W), row(DQ_W), row(d, 0), row(d, 1), row(d),
                  _mod_spec(group, d, rows_per_batch, tm, 2),
                  _mod_spec(group, d, rows_per_batch, tm, 4),
                  _mod_spec(group, d, rows_per_batch, tm, 3),
                  pl.BlockSpec((1, d), lambda i: (0, 0)),
                  res(wg), res(wd), res(wo)],
        out_specs=[row(d), row(d)],
        out_shape=[jax.ShapeDtypeStruct((m, d), f32), jax.ShapeDtypeStruct((m, d), bf16)],
        compiler_params=_params(48),
        name="merge_" + group,
    )(o_gdn, o_dsa, y, y, x, ada, ada, ada, g2.reshape(1, d), wg, wd, wo)


def _ffn_epilogue(acc_ref, x1_ref, gt_ref, gf_ref, y_ref):
    tm = acc_ref.shape[0]
    r = gt_ref.shape[0] if gt_ref.shape[0] > 1 else tm
    for s0 in range(0, tm, r):
        sl = slice(s0, s0 + r)
        x2 = x1_ref[sl, :] + gt_ref[...] * acc_ref[sl, :]
        y_ref[sl, :] = (x2 * lax.rsqrt(jnp.mean(x2 * x2, axis=-1, keepdims=True) + NORM_EPS)) * gf_ref[...]


def _ffn_prompt_kernel(h_ref, wug_ref, wuv_ref, wcg_ref, wcv_ref, bg_ref, bv_ref, wd_ref,
                       x1_ref, gt_ref, gf_ref, y_ref, ug_ref, uv_ref, acc_ref, act_s, prev_s,
                       *, blocks_per_seq, n_up):
    i = pl.program_id(0)
    j = pl.program_id(1)
    tm = h_ref.shape[0]
    keep = (i % blocks_per_seq) > 0

    sr = min(FFN_SUB_ROWS, tm)

    def conv(u, prev, wc_ref, b_ref):
        ux = jnp.concatenate([prev, u], axis=0)
        w = wc_ref[...]
        y = u * w[FFN_CONV - 1:FFN_CONV] + b_ref[...]
        for t in range(FFN_CONV - 1):
            sft = FFN_CONV - 1 - t
            y = y + ux[SUBLANES - sft:SUBLANES - sft + sr] * w[t:t + 1]
        return y

    @pl.when(jnp.logical_and(i == 0, j == 0))
    def _():
        prev_s[...] = jnp.zeros_like(prev_s)

    def up(store):
        pg = jnp.where(keep, prev_s[j, 0], 0.0)
        pv = jnp.where(keep, prev_s[j, 1], 0.0)
        for s in range(tm // sr):
            rows = slice(s * sr, (s + 1) * sr)
            hs = h_ref[rows, :]
            ug = _dot(hs, wug_ref[...])
            uv = _dot(hs, wuv_ref[...])
            store(rows, _silu(conv(ug, pg, wcg_ref, bg_ref)) * conv(uv, pv, wcv_ref, bv_ref))
            pg = ug[sr - SUBLANES:]
            pv = uv[sr - SUBLANES:]
        ug_ref[...] = pg
        uv_ref[...] = pv
        prev_s[j, 0] = pg
        prev_s[j, 1] = pv

    _ffn_phases(j, n_up, up, act_s, wd_ref, acc_ref, x1_ref, gt_ref, gf_ref, y_ref)


def _ffn_phases(j, n_up, up, act_s, wd_ref, acc_ref, x1_ref, gt_ref, gf_ref, y_ref):
    tf = act_s.shape[1] // n_up
    tn = wd_ref.shape[1]

    @pl.when(j < n_up)
    def _():
        cols = pl.ds(pl.multiple_of(j * tf, tf), tf)

        def store(rows, act):
            act_s[rows, cols] = act.astype(bf16)
        up(store)

    @pl.when(j >= n_up)
    def _():
        acc_ref[:, pl.ds(pl.multiple_of((j - n_up) * tn, tn), tn)] = _dot(act_s[...], wd_ref[...])

    @pl.when(j == pl.num_programs(1) - 1)
    def _():
        _ffn_epilogue(acc_ref, x1_ref, gt_ref, gf_ref, y_ref)


def _ffn_sample_kernel(h_ref, buf_ref, wug_ref, wuv_ref, wcg_ref, wcv_ref, bg_ref, bv_ref, wd_ref,
                       x1_ref, gt_ref, gf_ref, y_ref, ug_ref, uv_ref, acc_ref, act_s, *, ts, n_up):
    j = pl.program_id(1)
    db = h_ref.shape[0] // ts
    nbuf = FFN_CONV - 1
    tf = wug_ref.shape[1]

    def branch(wu_ref, wc_ref, b_ref, ubuf_ref, half):
        u = _dot(h_ref[...], wu_ref[...])
        rows = [buf_ref[r, :, half * tf:(half + 1) * tf] for r in range(nbuf)]
        rows += [u[t * db:(t + 1) * db] for t in range(ts)]
        for r in range(nbuf):
            ubuf_ref[r] = rows[ts + r]
        w = wc_ref[...]
        outs = []
        for t in range(ts):
            y = rows[t] * w[0:1] + b_ref[...]
            for r in range(1, FFN_CONV):
                y = y + rows[t + r] * w[r:r + 1]
            outs.append(y)
        return jnp.concatenate(outs, axis=0)

    def up(store):
        store(slice(0, ts * db),
              _silu(branch(wug_ref, wcg_ref, bg_ref, ug_ref, 0)) * branch(wuv_ref, wcv_ref, bv_ref, uv_ref, 1))

    _ffn_phases(j, n_up, up, act_s, wd_ref, acc_ref, x1_ref, gt_ref, gf_ref, y_ref)


def _ffn(h2, x1, ada, g_final, w_up, w_conv, b_conv, w_down, group, rows_per_batch, tm, tf, buf=None):
    m, d = h2.shape
    dff = w_down.shape[0]
    nj = dff // tf
    tn = min(FFN_DOWN_TILE, d)
    b2 = b_conv.reshape(1, 2 * dff)
    up = lambda j: jnp.minimum(j, nj - 1)
    common_w = [pl.BlockSpec((d, tf), lambda i, j: (0, up(j))),
                pl.BlockSpec((d, tf), lambda i, j: (0, nj + up(j))),
                pl.BlockSpec((FFN_CONV, tf), lambda i, j: (0, up(j))),
                pl.BlockSpec((FFN_CONV, tf), lambda i, j: (0, nj + up(j))),
                pl.BlockSpec((1, tf), lambda i, j: (0, up(j))),
                pl.BlockSpec((1, tf), lambda i, j: (0, nj + up(j))),
                pl.BlockSpec((dff, tn), lambda i, j: (0, jnp.maximum(j - nj, 0)))]
    tail = [pl.BlockSpec((tm, d), lambda i, j: (i, 0)),
            _mod_spec(group, d, rows_per_batch, tm, 5),
            pl.BlockSpec((1, d), lambda i, j: (0, 0))]
    y_spec = pl.BlockSpec((tm, d), lambda i, j: (i, 0))
    if group == "prompt":
        bps = rows_per_batch // tm
        nb = m // rows_per_batch
        kern = functools.partial(_ffn_prompt_kernel, blocks_per_seq=bps, n_up=nj)
        first = [pl.BlockSpec((tm, d), lambda i, j: (i, 0))]
        ubuf_spec = pl.BlockSpec((None, SUBLANES, tf), lambda i, j: (i, 0, up(j)))
        ubuf_shape = jax.ShapeDtypeStruct((m // tm, SUBLANES, dff), f32)
        args = (h2,)
        extra_scratch = [pltpu.VMEM((nj, 2, SUBLANES, tf), f32)]
    else:
        extra_scratch = []
        ts = m // rows_per_batch
        kern = functools.partial(_ffn_sample_kernel, ts=ts, n_up=nj)
        first = [pl.BlockSpec((tm, d), lambda i, j: (i, 0)),
                 pl.BlockSpec((FFN_CONV - 1, rows_per_batch, 2 * tf), lambda i, j: (0, 0, up(j)))]
        ubuf_spec = pl.BlockSpec((FFN_CONV - 1, rows_per_batch, tf), lambda i, j: (0, 0, up(j)))
        ubuf_shape = jax.ShapeDtypeStruct((FFN_CONV - 1, rows_per_batch, dff), f32)
        args = (h2, buf)
    return pl.pallas_call(
        kern,
        grid=(m // tm, nj + d // tn),
        in_specs=first + common_w + tail,
        out_specs=[y_spec, ubuf_spec, ubuf_spec],
        out_shape=[jax.ShapeDtypeStruct((m, d), f32), ubuf_shape, ubuf_shape],
        scratch_shapes=[pltpu.VMEM((tm, d), f32), pltpu.VMEM((tm, dff), bf16)] + extra_scratch,
        compiler_params=_params(56, 2),
        name="ffn_" + group,
    )(*args, w_up, w_up, w_conv, w_conv, b2, b2, w_down, x1, ada, g_final.reshape(1, d))


def _split_w_in(w_in, d):
    sizes = (QKV_W, Z_W, GDN_HEADS, GDN_HEADS, DQ_W, DKV_W, DKV_W, IQ_W, IDX_DIM, IDX_HEADS, d, d)
    offs = [0]
    for s in sizes:
        offs.append(offs[-1] + s)
    seg = lambda i, j: w_in[:, offs[i]:offs[j]]
    pad = jnp.zeros((w_in.shape[0], LANES - (IDX_DIM + 2 * GDN_HEADS + IDX_HEADS)), w_in.dtype)
    small = jnp.concatenate([seg(8, 9), seg(2, 4), seg(9, 10), pad], axis=1)
    return ((w_in, GDN_PROJ_W), (seg(4, 8), DSA_PROJ_W), (seg(10, 12), 2 * d), (small, LANES))


def _in_proj(h, w_parts, tm, name):
    return tuple(_matmul(h, w, tm, _mm_tile(n), name + "_" + part, n)
                 for (w, n), part in zip(w_parts, ("gdn", "dsa", "gate", "small")))


def _ffn_tile(dff):
    for tf in (512, 256, 128):
        if dff % tf == 0:
            return tf
    raise ValueError("d_ff must be a multiple of 128")


def _mm_tile(n):
    for tn in (1024, 512, 256, 128):
        if n % tn == 0:
            return tn
    raise ValueError("projection width must be a multiple of 128")


def kernel(x_prompt, x_sample, c_prompt, c_sample, cache_k, cache_v, cache_idx_k, page_table, state_gdn, state_gdn_conv, state_ffn_conv, w_ada, b_ada, g_norm1, w_in, w_gdn_conv, a_log, dt_bias, g_gdn_norm, w_gdn_out, w_dsa_out, w_o, g_norm2, w_up, w_ffn_conv, b_ffn_conv, w_down, g_final):
    nb, t, d = x_prompt.shape
    db, ts, _ = x_sample.shape
    depth = w_ada.shape[0]
    assert depth == 1 and db == LANES and ts >= GDN_CONV - 1
    n_pages = page_table.shape[1]
    page = cache_k.shape[2]
    past = n_pages * page
    dff = w_down.shape[1]
    l = 0

    n_c = nb + db
    pad_c = (-n_c) % SUBLANES
    c_all = jnp.concatenate([c_prompt, c_sample, jnp.zeros((pad_c, d), f32)], axis=0)
    ada = _ada(c_all, w_ada[l], b_ada[l])
    ada_p = ada[:nb].reshape(nb, 1, 6 * d)
    ada_s = ada[nb:nb + db]

    w_in_parts = _split_w_in(w_in[l], d)
    wg = w_gdn_out[l].astype(bf16)
    wd = w_dsa_out[l].astype(bf16)
    wo = w_o[l].astype(bf16)
    wup = w_up[l].astype(bf16)
    wdn = w_down[l].astype(bf16)
    tf = _ffn_tile(dff)

    xp = x_prompt.reshape(nb * t, d)
    tm_p = min(512, t)
    h1 = _prep(xp, g_norm1[l], ada_p, "prompt", t, tm_p)
    yp_gdn, yp_dsa, yp_gate, yp_small = _in_proj(h1, w_in_parts, min(IN_PROJ_ROWS, nb * t), "in_proj_prompt")
    og_p, s_p = _gdn_prompt(yp_gdn, yp_small, w_gdn_conv[l], a_log[l], dt_bias[l], g_gdn_norm[l], nb, t)
    dq_p, dk_p, iq_p, ik_p, sm_p = _rope(yp_dsa, yp_small, jnp.arange(t), tm_p, "prompt")
    od_p = _dsa_prompt(iq_p, sm_p, ik_p, dq_p, dk_p, yp_dsa, nb, t, min(DSA_TOPK, t // 4))
    x1_p, h2_p = _merge(og_p, od_p, yp_gate, xp, ada_p, g_norm2[l], wg, wd, wo, "prompt", t, min(256, t))
    y_p, ug_p, uv_p = _ffn(h2_p, x1_p, ada_p, g_final, wup, w_ffn_conv[l], b_ffn_conv[l], wdn,
                           "prompt", t, tm_p, tf)

    nfb = FFN_CONV - 1
    bps_p = t // tm_p
    out_p = (
        y_p.reshape(nb, t, d),
        dk_p.reshape(1, nb, t, DSA_KV_HEADS, DSA_HEAD_DIM),
        yp_dsa[:, OFF_DV:OFF_DV + DKV_W].reshape(1, nb, t, DSA_KV_HEADS, DSA_HEAD_DIM),
        ik_p.reshape(1, nb, t, IDX_DIM),
        s_p[None],
        yp_gdn.reshape(nb, t, -1)[:, t - (GDN_CONV - 1):, :QKV_W][None],
        jnp.concatenate([ug_p[bps_p - 1::bps_p, SUBLANES - nfb:], uv_p[bps_p - 1::bps_p, SUBLANES - nfb:]],
                        axis=-1)[None],
    )

    xs = x_sample.transpose(1, 0, 2).reshape(ts * db, d)
    h1s = _prep(xs, g_norm1[l], ada_s, "sample", db, db)
    ys_gdn, ys_dsa, ys_gate, ys_small = _in_proj(h1s, w_in_parts, ts * db, "in_proj_sample")
    ys_gdn3 = ys_gdn.reshape(ts, db, -1)
    og_s, s_s = _gdn_sample(ys_gdn3, ys_small.reshape(ts, db, LANES), state_gdn_conv[l].transpose(1, 0, 2),
                            state_gdn[l], w_gdn_conv[l], a_log[l], dt_bias[l], g_gdn_norm[l])
    pos_s = jnp.repeat(past + jnp.arange(ts), db)
    dq_s, dk_s, iq_s, ik_s, sm_s = _rope(ys_dsa, ys_small, pos_s, db, "sample", db)
    ik_s_b = ik_s.reshape(ts, db, IDX_DIM).transpose(1, 0, 2)
    lp = past + LANES
    tp = -(-ts // SUBLANES) * SUBLANES
    pad_t = lambda a: jnp.pad(a.astype(f32), ((0, 0), (0, tp - ts), (0, 0)))
    scores = _idx_scores(page_table, pad_t(iq_s.reshape(db, ts, IQ_W)), pad_t(sm_s.reshape(db, ts, LANES)),
                         pad_t(ik_s_b), jnp.swapaxes(cache_idx_k[l], 1, 2), lp, ts)
    sel = _select_sample(scores.reshape(db * ts, lp), min(DSA_TOPK, (past + ts) // 4), past, ts)
    dv_s_b = ys_dsa.reshape(ts, db, -1)[:, :, OFF_DV:OFF_DV + DKV_W].transpose(1, 0, 2)
    ck = cache_k[l].reshape(cache_k.shape[1], page * DSA_KV_HEADS, DSA_HEAD_DIM)
    cv = cache_v[l].reshape(cache_v.shape[1], page * DSA_KV_HEADS, DSA_HEAD_DIM)
    od_s = _dsa_sample(page_table, pad_t(dq_s.reshape(db, ts, DQ_W)), sel.reshape(db, ts, lp),
                       pad_t(dk_s.reshape(db, ts, DKV_W)), pad_t(dv_s_b), ck, cv)
    od_s = od_s.astype(bf16).transpose(1, 0, 2).reshape(ts * db, DQ_W)
    x1_s, h2_s = _merge(og_s.reshape(ts * db, Z_W), od_s, ys_gate, xs, ada_s, g_norm2[l], wg, wd, wo,
                        "sample", db, db)
    fb = state_ffn_conv[l].transpose(1, 0, 2)
    nj = dff // tf
    fb = jnp.concatenate([fb[:, :, :dff].reshape(nfb, db, nj, tf), fb[:, :, dff:].reshape(nfb, db, nj, tf)],
                         axis=-1).reshape(nfb, db, 2 * dff)
    y_s, ug_s, uv_s = _ffn(h2_s, x1_s, ada_s, g_final, wup, w_ffn_conv[l], b_ffn_conv[l], wdn,
                           "sample", db, ts * db, tf, buf=fb)

    out_s = (
        y_s.reshape(ts, db, d).transpose(1, 0, 2),
        dk_s.reshape(1, db, ts, DSA_KV_HEADS, DSA_HEAD_DIM),
        dv_s_b.reshape(1, db, ts, DSA_KV_HEADS, DSA_HEAD_DIM),
        ik_s_b[None],
        s_s[None],
        ys_gdn3[ts - (GDN_CONV - 1):, :, :QKV_W].transpose(1, 0, 2)[None],
        jnp.concatenate([ug_s, uv_s], axis=-1).transpose(1, 0, 2)[None],
    )
    return (out_p[0], out_s[0]) + out_p[1:] + out_s[1:]
```

```python
import functools

import jax
import jax.numpy as jnp
from jax import lax
from jax.experimental import pallas as pl
from jax.experimental.pallas import tpu as pltpu

f32 = jnp.float32
bf16 = jnp.bfloat16

GDN_HEADS = 8
GDN_DK = 128
GDN_DV = 128
GDN_CONV = 4
GDN_CHUNK = 64
GDN_GROUP = 4
GDN_SEQS = 4
DSA_HEADS = 8
DSA_KV_HEADS = 2
DSA_HEAD_DIM = 128
IDX_HEADS = 8
IDX_DIM = 64
IDX_SCALE = IDX_HEADS ** -0.5 * IDX_DIM ** -0.5
DSA_TOPK = 256
ROPE_THETA = 500000.0
ROPE_FRACTION = 4
FFN_CONV = 3
NORM_EPS = 1e-6

LANES = 128
SUBLANES = 8
NEG = float(jnp.finfo(jnp.float32).min)
POS = float(jnp.finfo(jnp.float32).max)
BISECT_UNROLL = 4
SNAP_FROM = 4
BISECT_CAP = 1024
DSA_KEY_TILE = 256
DSA_KEY_EXTENT = 256
IN_PROJ_ROWS = 1024
IDX_BATCH = 8
DSA_SAMPLE_BATCH = 2
DSA_HEAD_PAIR = 4
FFN_DOWN_TILE = 512
FFN_SUB_ROWS = 256

QKV_W = 2 * GDN_HEADS * GDN_DK + GDN_HEADS * GDN_DV
Z_W = GDN_HEADS * GDN_DV
DQ_W = DSA_HEADS * DSA_HEAD_DIM
DKV_W = DSA_KV_HEADS * DSA_HEAD_DIM
IQ_W = IDX_HEADS * IDX_DIM
OFF_Z = QKV_W
GDN_PROJ_W = QKV_W + Z_W
OFF_DQ = 0
OFF_DK = OFF_DQ + DQ_W
OFF_DV = OFF_DK + DKV_W
OFF_IQ = OFF_DV + DKV_W
DSA_PROJ_W = OFF_IQ + IQ_W
SM_IK = 0
SM_A = IDX_DIM
SM_B = SM_A + GDN_HEADS
SM_IW = SM_B + GDN_HEADS


def _sigmoid(x):
    return 1.0 / (1.0 + jnp.exp(-x))


def _silu(x):
    return x * _sigmoid(x)


def _dot(a, b):
    return jnp.dot(a, b, preferred_element_type=f32)


def _dot_nt(a, b):
    return lax.dot_general(a, b, (((1,), (1,)), ((), ())), preferred_element_type=f32)


def _dot_tn(a, b):
    return lax.dot_general(a, b, (((0,), (0,)), ((), ())), preferred_element_type=f32)


def _split3(a):
    hi = a.astype(bf16)
    lo = (a - hi.astype(f32)).astype(bf16)
    return hi, lo


def _mm3(a, b):
    ah, al = a
    bh, bl = b
    return _dot(ah, bh) + (_dot(ah, bl) + _dot(al, bh))


def _dot_exact(a, b):
    return jnp.dot(a, b, preferred_element_type=f32, precision=lax.Precision.HIGHEST)


def _params(vmem_mb=None, n_axes=1):
    kw = dict(dimension_semantics=("arbitrary",) * n_axes)
    if vmem_mb is not None:
        kw["vmem_limit_bytes"] = vmem_mb * 1024 * 1024
    return pltpu.CompilerParams(**kw)


def _ada_kernel(c_ref, w_ref, b_ref, o_ref):
    s = _silu(c_ref[...]).astype(bf16)
    o_ref[...] = _dot(s, w_ref[...].astype(bf16)) + b_ref[...]


def _ada(c_all, w_ada, b_ada):
    m, d = c_all.shape
    n = w_ada.shape[1]
    tn = 1024
    return pl.pallas_call(
        _ada_kernel,
        grid=(n // tn,),
        in_specs=[pl.BlockSpec((m, d), lambda j: (0, 0)),
                  pl.BlockSpec((d, tn), lambda j: (0, j)),
                  pl.BlockSpec((1, tn), lambda j: (0, j))],
        out_specs=pl.BlockSpec((m, tn), lambda j: (0, j)),
        out_shape=jax.ShapeDtypeStruct((m, n), f32),
        compiler_params=_params(48),
        name="ada",
    )(c_all, w_ada, b_ada.reshape(1, n))


def _mod_spec(group, d, rows_per_batch, tm, col):
    if group == "prompt":
        return pl.BlockSpec((None, 1, d), lambda i, *_: ((i * tm) // rows_per_batch, 0, col))
    return pl.BlockSpec((rows_per_batch, d), lambda i, *_: (0, col))


def _prep_kernel(x_ref, g_ref, sc_ref, sh_ref, o_ref):
    x = x_ref[...]
    y = x * lax.rsqrt(jnp.mean(x * x, axis=-1, keepdims=True) + NORM_EPS)
    o_ref[...] = ((y * g_ref[...]) * (1.0 + sc_ref[...]) + sh_ref[...]).astype(bf16)


def _prep(x, g, ada, group, rows_per_batch, tm):
    m, d = x.shape
    return pl.pallas_call(
        _prep_kernel,
        grid=(m // tm,),
        in_specs=[pl.BlockSpec((tm, d), lambda i: (i, 0)),
                  pl.BlockSpec((1, d), lambda i: (0, 0)),
                  _mod_spec(group, d, rows_per_batch, tm, 1),
                  _mod_spec(group, d, rows_per_batch, tm, 0)],
        out_specs=pl.BlockSpec((tm, d), lambda i: (i, 0)),
        out_shape=jax.ShapeDtypeStruct((m, d), bf16),
        compiler_params=_params(),
        name="prep_" + group,
    )(x, g.reshape(1, d), ada, ada)


def _mm_kernel(a_ref, w_ref, o_ref):
    o_ref[...] = _dot(a_ref[...], w_ref[...])


def _matmul(a, w, tm, tn, name):
    m, k = a.shape
    n = w.shape[1]
    return pl.pallas_call(
        _mm_kernel,
        grid=(n // tn, m // tm),
        in_specs=[pl.BlockSpec((tm, k), lambda j, i: (i, 0)),
                  pl.BlockSpec((k, tn), lambda j, i: (0, j))],
        out_specs=pl.BlockSpec((tm, tn), lambda j, i: (i, j)),
        out_shape=jax.ShapeDtypeStruct((m, n), f32),
        compiler_params=_params(48, 2),
        name=name,
    )(a, w)


def _l2n(x):
    return x * lax.rsqrt(jnp.sum(x * x, axis=-1, keepdims=True) + NORM_EPS)


def _gdn_gates(sm, alog, dtb):
    xa = sm + dtb
    softplus = jnp.maximum(xa, 0.0) + jnp.log1p(jnp.exp(-jnp.abs(xa)))
    return -jnp.exp(alog) * softplus, _sigmoid(sm)


def _gated_norm(o, gn, z):
    y = o * lax.rsqrt(jnp.mean(o * o, axis=-1, keepdims=True) + NORM_EPS)
    return (y * gn) * _silu(z)


def _gdn_prompt_kernel(qkv_ref, halo_ref, z_ref, sm_ref, wc_ref, alog_ref, dtb_ref, gn_ref,
                       o_ref, sfin_ref, s_ref):
    c = pl.program_id(1)
    n_seq, C, _ = qkv_ref.shape

    @pl.when(c == 0)
    def _():
        s_ref[...] = jnp.zeros_like(s_ref)

    keep = (c > 0).astype(f32)
    ri = lax.broadcasted_iota(jnp.int32, (C, C), 0)
    ci = lax.broadcasted_iota(jnp.int32, (C, C), 1)
    tril = (ri >= ci).astype(f32)
    gates = [_gdn_gates(sm_ref[s], alog_ref[...], dtb_ref[...]) for s in range(n_seq)]
    beta_all = [g[1] for g in gates]
    gc_all = [_dot_exact(tril, g[0]) for g in gates]
    gc_t = [g.T for g in gc_all]

    def conv(s, col):
        xs = jnp.concatenate([halo_ref[s, :, col:col + LANES] * keep, qkv_ref[s, :, col:col + LANES]], axis=0)
        w = wc_ref[:, col:col + LANES]
        y = xs[SUBLANES:] * w[GDN_CONV - 1:GDN_CONV]
        for i in range(GDN_CONV - 1):
            sft = GDN_CONV - 1 - i
            y = y + xs[SUBLANES - sft:SUBLANES - sft + C] * w[i:i + 1]
        return _silu(y)

    N = GDN_GROUP * C
    rn = lax.broadcasted_iota(jnp.int32, (N, N), 0)
    cn = lax.broadcasted_iota(jnp.int32, (N, N), 1)
    same = (rn // C) == (cn // C)
    incl = jnp.logical_and(same, rn >= cn)
    strict = jnp.logical_and(same, rn > cn)
    eye_f = (rn == cn).astype(f32)
    n_sq = max(1, (C - 1).bit_length() - 1)
    groups = [(s, list(range(g0, g0 + GDN_GROUP))) for s in range(n_seq)
              for g0 in range(0, GDN_HEADS, GDN_GROUP)]
    stack = lambda xs: jnp.concatenate(xs, axis=0)

    qs = [stack([_l2n(conv(s, h * GDN_DK)) * (GDN_DK ** -0.5) for h in hs]) for s, hs in groups]
    ks = [stack([_l2n(conv(s, GDN_HEADS * GDN_DK + h * GDN_DK)) for h in hs]) for s, hs in groups]
    vs = [stack([conv(s, 2 * GDN_HEADS * GDN_DK + h * GDN_DV) for h in hs]) for s, hs in groups]
    gcs = [stack([gc_all[s][:, SM_A + h:SM_A + h + 1] for h in hs]) for s, hs in groups]
    betas = [stack([beta_all[s][:, SM_B + h:SM_B + h + 1] for h in hs]) for s, hs in groups]
    gc_rows = [jnp.concatenate([gc_t[s][SM_A + h:SM_A + h + 1, :] for h in hs], axis=1)
               for s, hs in groups]
    egs = [jnp.exp(gc) for gc in gcs]
    decays = [jnp.where(incl, jnp.exp(jnp.where(incl, gc - gr, 0.0)), 0.0) for gc, gr in zip(gcs, gc_rows)]
    kbs = [k.astype(bf16) for k in ks]
    a_s = [jnp.where(strict, b * _dot_nt(kb, kb) * dc, 0.0) for b, kb, dc in zip(betas, kbs, decays)]
    qks = [(_dot_nt(q.astype(bf16), kb) * dc).astype(bf16) for q, kb, dc in zip(qs, kbs, decays)]

    invs = [eye_f - a for a in a_s]
    pws = [a.astype(bf16) for a in a_s]
    for _ in range(n_sq):
        pws = [_dot(p, p).astype(bf16) for p in pws]
        invs = [inv + _dot(inv.astype(bf16), p) for inv, p in zip(invs, pws)]
    inv_s = [_split3(inv) for inv in invs]
    res = [eye_f - inv - _mm3(_split3(a), sp) for a, inv, sp in zip(a_s, invs, inv_s)]
    invs = [inv + _dot(sp[0], r.astype(bf16)) for inv, sp, r in zip(invs, inv_s, res)]
    rhs = [jnp.concatenate([b * v, (b * eg) * k], axis=1) for b, v, eg, k in zip(betas, vs, egs, ks)]
    sols = [_mm3(_split3(inv), _split3(r)) for inv, r in zip(invs, rhs)]

    for gi, (s, hs) in enumerate(groups):
        sol, q, k, gc, eg = sols[gi], qs[gi], ks[gi], gcs[gi], egs[gi]
        q_dec = (q * eg).astype(bf16)
        w_k = sol[:, GDN_DV:].astype(bf16)
        rows = [slice(j * C, (j + 1) * C) for j in range(GDN_GROUP)]
        s_old = [s_ref[s, h] for h in hs]
        ws = [_dot(jnp.concatenate([w_k[r], q_dec[r]], axis=0), so.astype(bf16)) for r, so in zip(rows, s_old)]
        u = stack([sol[r, :GDN_DV] - w[:C] for r, w in zip(rows, ws)])
        ub = u.astype(bf16)
        o_intra = _dot(qks[gi], ub)
        for j, h in enumerate(hs):
            r = rows[j]
            gl = gc[r][C - 1:C, :]
            k_end = (k[r] * jnp.exp(gl - gc[r])).astype(bf16)
            s_ref[s, h] = s_old[j] * jnp.exp(gl) + _dot_tn(k_end, ub[r])
            o = ws[j][C:] + o_intra[r]
            zh = z_ref[s, :, h * GDN_DV:(h + 1) * GDN_DV]
            o_ref[s, :, h * GDN_DV:(h + 1) * GDN_DV] = _gated_norm(o, gn_ref[...], zh).astype(bf16)

    @pl.when(c == pl.num_programs(1) - 1)
    def _():
        sfin_ref[...] = s_ref[...]


def _gdn_vecs(a_log, dt_bias, g_gdn_norm):
    alog = jnp.zeros((1, LANES), f32).at[0, SM_A:SM_A + GDN_HEADS].set(a_log)
    dtb = jnp.zeros((1, LANES), f32).at[0, SM_A:SM_A + GDN_HEADS].set(dt_bias)
    return alog, dtb, g_gdn_norm.reshape(1, GDN_DV)


def _gdn_prompt(y, y_small, w_conv, a_log, dt_bias, g_gdn_norm, nb, t):
    C = GDN_CHUNK
    nc = t // C
    ns = GDN_SEQS if nb % GDN_SEQS == 0 else 1
    alog, dtb, gn = _gdn_vecs(a_log, dt_bias, g_gdn_norm)
    cst = lambda b, c: (0, 0)
    y3 = y.reshape(nb, t, -1)
    ysm3 = y_small.reshape(nb, t, LANES)
    o, s_fin = pl.pallas_call(
        _gdn_prompt_kernel,
        grid=(nb // ns, nc),
        in_specs=[pl.BlockSpec((ns, C, QKV_W), lambda b, c: (b, c, 0)),
                  pl.BlockSpec((ns, SUBLANES, QKV_W),
                               lambda b, c: (b, jnp.maximum(c * (C // SUBLANES) - 1, 0), 0)),
                  pl.BlockSpec((ns, C, Z_W), lambda b, c: (b, c, OFF_Z // Z_W)),
                  pl.BlockSpec((ns, C, LANES), lambda b, c: (b, c, 0)),
                  pl.BlockSpec((GDN_CONV, QKV_W), cst),
                  pl.BlockSpec((1, LANES), cst),
                  pl.BlockSpec((1, LANES), cst),
                  pl.BlockSpec((1, GDN_DV), cst)],
        out_specs=[pl.BlockSpec((ns, C, Z_W), lambda b, c: (b, c, 0)),
                   pl.BlockSpec((ns, GDN_HEADS, GDN_DK, GDN_DV), lambda b, c: (b, 0, 0, 0))],
        out_shape=[jax.ShapeDtypeStruct((nb, t, Z_W), bf16),
                   jax.ShapeDtypeStruct((nb, GDN_HEADS, GDN_DK, GDN_DV), f32)],
        scratch_shapes=[pltpu.VMEM((ns, GDN_HEADS, GDN_DK, GDN_DV), f32)],
        compiler_params=_params(None, 2),
        name="gdn_prompt",
    )(y3, y3, y3, ysm3, w_conv, alog, dtb, gn)
    return o.reshape(nb * t, Z_W), s_fin


def _gdn_sample_kernel(qkv_ref, buf_ref, z_ref, sm_ref, sin_ref, wc_ref, alog_ref, dtb_ref, gn_ref,
                       o_ref, sout_ref, q_s, k_s, v_s, a_s, b_s, o_s):
    ts, G, _ = qkv_ref.shape
    nbuf = GDN_CONV - 1

    for h in range(GDN_HEADS):
        for part, dst in ((0, q_s), (1, k_s), (2, v_s)):
            col = part * GDN_HEADS * GDN_DK + h * GDN_DK
            w = wc_ref[:, col:col + LANES]
            rows = [buf_ref[i, :, col:col + LANES] for i in range(nbuf)]
            rows += [qkv_ref[t, :, col:col + LANES] for t in range(ts)]
            for t in range(ts):
                y = rows[t] * w[0:1]
                for i in range(1, GDN_CONV):
                    y = y + rows[t + i] * w[i:i + 1]
                y = _silu(y)
                if part == 0:
                    y = _l2n(y) * (GDN_DK ** -0.5)
                elif part == 1:
                    y = _l2n(y)
                dst[t, :, h * LANES:(h + 1) * LANES] = y

    for t in range(ts):
        g_all, beta_all = _gdn_gates(sm_ref[t], alog_ref[...], dtb_ref[...])
        a_all = jnp.exp(g_all)
        for h in range(GDN_HEADS):
            a_s[t, :, h * LANES:(h + 1) * LANES] = jnp.broadcast_to(a_all[:, SM_A + h:SM_A + h + 1], (G, LANES))
            b_s[t, :, h * LANES:(h + 1) * LANES] = jnp.broadcast_to(beta_all[:, SM_B + h:SM_B + h + 1], (G, LANES))

    nv = 2 * ts
    sr = lax.broadcasted_iota(jnp.int32, (3 * nv, nv * LANES), 0)
    sc = lax.broadcasted_iota(jnp.int32, (3 * nv, nv * LANES), 1)
    spread = (sr % nv == sc // LANES).astype(bf16)

    def head(h, carry):
        cols = pl.ds(pl.multiple_of(h * LANES, LANES), LANES)
        for i in range(G):
            row = slice(i, i + 1)
            kq = jnp.concatenate([k_s[t, row, cols] for t in range(ts)]
                                 + [q_s[t, row, cols] for t in range(ts)], axis=0)
            hi = kq.astype(bf16).astype(f32)
            mid = (kq - hi).astype(bf16).astype(f32)
            lo = (kq - hi) - mid
            kq_b = _dot(jnp.concatenate([hi, mid, lo], axis=0).T.astype(bf16), spread)
            s = sin_ref[i, h]
            for t in range(ts):
                kc = kq_b[:, t * LANES:(t + 1) * LANES]
                qc = kq_b[:, (ts + t) * LANES:(ts + t + 1) * LANES]
                a = a_s[t, row, cols]
                b = b_s[t, row, cols]
                ks = jnp.sum(s * kc, axis=0, keepdims=True)
                r = b * (v_s[t, row, cols] - a * ks)
                s = a * s + kc * r
                o_s[t, row, cols] = jnp.sum(s * qc, axis=0, keepdims=True)
            sout_ref[i, h] = s
        return carry

    lax.fori_loop(0, GDN_HEADS, head, 0)

    for t in range(ts):
        for h in range(GDN_HEADS):
            sl = slice(h * GDN_DV, (h + 1) * GDN_DV)
            o_ref[t, :, sl] = _gated_norm(o_s[t, :, sl], gn_ref[...], z_ref[t, :, sl]).astype(bf16)


def _gdn_sample(y3, y3_small, buf3, state, w_conv, a_log, dt_bias, g_gdn_norm):
    ts, db, _ = y3.shape
    G = SUBLANES
    alog, dtb, gn = _gdn_vecs(a_log, dt_bias, g_gdn_norm)
    cst = lambda g: (0, 0)
    st_spec = pl.BlockSpec((G, GDN_HEADS, GDN_DK, GDN_DV), lambda g: (g, 0, 0, 0))
    scr = pltpu.VMEM((ts, G, Z_W), f32)
    return pl.pallas_call(
        _gdn_sample_kernel,
        grid=(db // G,),
        in_specs=[pl.BlockSpec((ts, G, QKV_W), lambda g: (0, g, 0)),
                  pl.BlockSpec((GDN_CONV - 1, G, QKV_W), lambda g: (0, g, 0)),
                  pl.BlockSpec((ts, G, Z_W), lambda g: (0, g, OFF_Z // Z_W)),
                  pl.BlockSpec((ts, G, LANES), lambda g: (0, g, 0)),
                  st_spec,
                  pl.BlockSpec((GDN_CONV, QKV_W), cst),
                  pl.BlockSpec((1, LANES), cst),
                  pl.BlockSpec((1, LANES), cst),
                  pl.BlockSpec((1, GDN_DV), cst)],
        out_specs=[pl.BlockSpec((ts, G, Z_W), lambda g: (0, g, 0)), st_spec],
        out_shape=[jax.ShapeDtypeStruct((ts, db, Z_W), bf16),
                   jax.ShapeDtypeStruct(state.shape, f32)],
        scratch_shapes=[scr, scr, scr, scr, scr, scr],
        compiler_params=_params(48),
        name="gdn_sample",
    )(y3, buf3, y3, y3_small, state, w_conv, alog, dtb, gn)


def _rope_tables(pos, rot, width):
    half = rot // 2
    inv_freq = ROPE_THETA ** (-jnp.arange(half, dtype=f32) * (2.0 / rot))
    ang = pos.astype(f32)[:, None] * inv_freq[None, :]
    cos, sin = jnp.cos(ang), jnp.sin(ang)
    n = pos.shape[0]
    z = lambda w: jnp.zeros((n, w), f32)
    cosw = jnp.concatenate([cos, cos, jnp.ones((n, width - rot), f32)], axis=1)
    sina = jnp.concatenate([-sin, z(width - half)], axis=1)
    sinb = jnp.concatenate([z(half), sin, z(width - rot)], axis=1)
    reps = LANES // width
    return tuple(jnp.tile(a, (1, reps)) for a in (cosw, sina, sinb))


def _rope_kernel(dq_ref, dk_ref, iq_ref, sm_ref, c1, sa1, sb1, c2, sa2, sb2,
                 dq_o, dk_o, iq_o, ik_o, sm_o):
    h1 = DSA_HEAD_DIM // ROPE_FRACTION // 2
    h2 = IDX_DIM // ROPE_FRACTION // 2

    def rot(x, c, sa, sb, half):
        return x * c[...] + pltpu.roll(x, LANES - half, 1) * sa[...] + pltpu.roll(x, half, 1) * sb[...]

    for j in range(DQ_W // LANES):
        sl = slice(j * LANES, (j + 1) * LANES)
        dq_o[:, sl] = rot(dq_ref[:, sl], c1, sa1, sb1, h1).astype(bf16)
    for j in range(DKV_W // LANES):
        sl = slice(j * LANES, (j + 1) * LANES)
        dk_o[:, sl] = rot(dk_ref[:, sl], c1, sa1, sb1, h1)
    for j in range(IQ_W // LANES):
        sl = slice(j * LANES, (j + 1) * LANES)
        iq_o[:, sl] = rot(iq_ref[:, sl], c2, sa2, sb2, h2).astype(bf16)
    sm = sm_ref[...]
    ik_o[...] = rot(sm, c2, sa2, sb2, h2)[:, :IDX_DIM]
    sm_o[...] = sm


def _rope(y, y_small, pos, tm, group, db=None):
    m = y.shape[0]
    t1 =_rope_tables(pos, DSA_HEAD_DIM // ROPE_FRACTION, DSA_HEAD_DIM)
    t2 = _rope_tables(pos, IDX_DIM // ROPE_FRACTION, IDX_DIM)
    tab_blocks = pos.shape[0] // tm
    tab = pl.BlockSpec((tm, LANES), lambda i: (i % tab_blocks, 0))
    if group == "prompt":
        omap = lambda i: (i, 0)
        rows = lambda w: m
        cols = lambda w: w
    else:
        ts = m // db
        omap = lambda i: (0, i)
        rows = lambda w: db
        cols = lambda w: ts * w
    out = lambda w, dt: jax.ShapeDtypeStruct((rows(w), cols(w)), dt)
    return pl.pallas_call(
        _rope_kernel,
        grid=(m // tm,),
        in_specs=[pl.BlockSpec((tm, DQ_W), lambda i: (i, OFF_DQ // DQ_W)),
                  pl.BlockSpec((tm, DKV_W), lambda i: (i, OFF_DK // DKV_W)),
                  pl.BlockSpec((tm, IQ_W), lambda i: (i, OFF_IQ // IQ_W)),
                  pl.BlockSpec((tm, LANES), lambda i: (i, 0)),
                  tab, tab, tab, tab, tab, tab],
        out_specs=[pl.BlockSpec((tm, DQ_W), omap),
                   pl.BlockSpec((tm, DKV_W), omap),
                   pl.BlockSpec((tm, IQ_W), omap),
                   pl.BlockSpec((tm, IDX_DIM), lambda i: (i, 0)),
                   pl.BlockSpec((tm, LANES), omap)],
        out_shape=[out(DQ_W, bf16), out(DKV_W, f32), out(IQ_W, bf16),
                   jax.ShapeDtypeStruct((m, IDX_DIM), f32), out(LANES, f32)],
        compiler_params=_params(),
        name="rope_" + group,
    )(y, y, y, y_small, *t1, *t2)


def _topk_bias(x_s, bias_s, lo0, mx, few, nkt, kt_w, k):
    R = x_s.shape[0]
    kf = float(k)
    tiles = [slice(kt * kt_w, (kt + 1) * kt_w) for kt in range(nkt)]
    n_grp = 2 if R % (2 * SUBLANES) == 0 else 1
    grps = [slice(g * (R // n_grp), (g + 1) * (R // n_grp)) for g in range(n_grp)]

    def count_ge(rows, th):
        acc = jnp.where(x_s[rows, tiles[0]] >= th, 1.0, 0.0)
        for t in tiles[1:]:
            acc = acc + jnp.where(x_s[rows, t] >= th, 1.0, 0.0)
        return jnp.sum(acc, axis=1, keepdims=True)

    def bisect(rows, lo, hi, done):
        mid = 0.5 * lo + 0.5 * hi
        cnt = count_ge(rows, mid)
        collapsed = jnp.logical_or(mid <= lo, mid >= hi)
        live = jnp.logical_and(done < 0.5, jnp.logical_not(collapsed))
        lo = jnp.where(jnp.logical_and(live, cnt >= kf), mid, lo)
        hi = jnp.where(jnp.logical_and(live, cnt <= kf), mid, hi)
        done = jnp.where(jnp.logical_or(collapsed, cnt == kf), 1.0, done)
        return lo, hi, done

    def snap(rows, lo, hi, done):
        a = b = None
        for t in tiles:
            x = x_s[rows, t]
            at = jnp.where(x >= lo, x, POS)
            bt = jnp.where(x < hi, x, NEG)
            a = at if a is None else jnp.minimum(a, at)
            b = bt if b is None else jnp.maximum(b, bt)
        a = jnp.min(a, axis=1, keepdims=True)
        b = jnp.max(b, axis=1, keepdims=True)
        live = done < 0.5
        return jnp.where(live, a, lo), jnp.where(jnp.logical_and(live, a >= b), 1.0, done)

    def cond(c):
        it, st = c
        left = st[0][2]
        for s in st[1:]:
            left = jnp.minimum(left, s[2])
        return jnp.logical_and(it < BISECT_CAP, jnp.min(left) < 0.5)

    def body(c):
        it, st = c
        for _ in range(BISECT_UNROLL):
            st = [bisect(rows, *s) for rows, s in zip(grps, st)]

        def snapped():
            out = []
            for rows, (lo, hi, done) in zip(grps, st):
                lo2, done2 = snap(rows, lo, hi, done)
                out.append((lo2, hi, done2))
            return out

        st = lax.cond(it >= SNAP_FROM, snapped, lambda: st)
        return it + 1, st

    hi0 = jnp.where(few, lo0, mx + (jnp.abs(mx) + 1.0))
    st0 = [(lo0[rows], hi0[rows], few[rows].astype(f32)) for rows in grps]
    _, st = lax.while_loop(cond, body, (jnp.int32(0), st0))
    lo = jnp.concatenate([s[0] for s in st], axis=0)
    hi = jnp.concatenate([s[1] for s in st], axis=0)

    has_run = jnp.max(jnp.where(lo < hi, 1.0, 0.0)) > 0.5
    rows_all = slice(0, R)

    @pl.when(jnp.logical_not(has_run))
    def _():
        for t in tiles:
            bias_s[:, t] = jnp.where(x_s[:, t] >= hi, 0.0, NEG)

    @pl.when(has_run)
    def _():
        need = kf - count_ge(rows_all, hi)
        ai = lax.broadcasted_iota(jnp.int32, (kt_w, kt_w), 0)
        bi = lax.broadcasted_iota(jnp.int32, (kt_w, kt_w), 1)
        before = (ai < bi).astype(bf16)
        seen = jnp.zeros((R, 1), f32)
        for t in tiles:
            x = x_s[:, t]
            run = jnp.logical_and(x >= lo, x < hi)
            runf = run.astype(f32)
            rank = seen + _dot(runf.astype(bf16), before)
            take = jnp.logical_or(x >= hi, jnp.logical_and(run, rank < need))
            bias_s[:, t] = jnp.where(take, 0.0, NEG)
            seen = seen + jnp.sum(runf, axis=1, keepdims=True)


def _dsa_prompt_kernel(iq_ref, sm_ref, ik_ref, q_ref, k_ref, v_ref, o_ref,
                       ikb, kb, vb, x_s, bias_s, s_s, *, topk, kt_w, ext_w):
    qi = pl.program_id(1)
    R = iq_ref.shape[0]
    T = ik_ref.shape[0]

    @pl.when(qi == 0)
    def _():
        ikb[...] = ik_ref[...].astype(bf16)
        kb[...] = k_ref[...].astype(bf16)
        vb[...] = v_ref[...].astype(bf16)

    def block(ext):
        tiles = [slice(j * kt_w, (j + 1) * kt_w) for j in range(ext // kt_w)]
        wgt = sm_ref[...] * IDX_SCALE
        qpos = qi * R + lax.broadcasted_iota(jnp.int32, (R, kt_w), 0)
        key0 = lax.broadcasted_iota(jnp.int32, (R, kt_w), 1)
        mn = mx = None
        for j, t in enumerate(tiles):
            keys = ikb[t, :]
            sc = jnp.zeros((R, kt_w), f32)
            for h in range(IDX_HEADS):
                rel = jnp.maximum(_dot_nt(iq_ref[:, h * IDX_DIM:(h + 1) * IDX_DIM], keys), 0.0)
                sc = sc + rel * wgt[:, SM_IW + h:SM_IW + h + 1]
            causal = key0 + j * kt_w <= qpos
            x_s[:, t] = jnp.where(causal, sc, NEG)
            lo_t = jnp.where(causal, sc, POS)
            hi_t = jnp.where(causal, sc, NEG)
            mn = lo_t if mn is None else jnp.minimum(mn, lo_t)
            mx = hi_t if mx is None else jnp.maximum(mx, hi_t)
        n_causal = qi * R + lax.broadcasted_iota(jnp.int32, (R, 1), 0) + 1
        _topk_bias(x_s, bias_s, jnp.min(mn, axis=1, keepdims=True), jnp.max(mx, axis=1, keepdims=True),
                   n_causal <= topk, len(tiles), kt_w, topk)

        def heads(hp, carry):
            h0 = hp * DSA_HEAD_PAIR
            g = h0 // (DSA_HEADS // DSA_KV_HEADS)
            gc = pl.ds(pl.multiple_of(g * DSA_HEAD_DIM, DSA_HEAD_DIM), DSA_HEAD_DIM)
            hcs = [pl.ds(pl.multiple_of((h0 + i) * DSA_HEAD_DIM, DSA_HEAD_DIM), DSA_HEAD_DIM)
                   for i in range(DSA_HEAD_PAIR)]
            qs = [q_ref[:, hc] for hc in hcs]
            mxa = [None] * DSA_HEAD_PAIR
            for t in tiles:
                kt = kb[t, gc]
                bt = bias_s[:, t]
                for i in range(DSA_HEAD_PAIR):
                    s = _dot_nt(qs[i], kt) * (DSA_HEAD_DIM ** -0.5) + bt
                    s_s[i, :, t] = s
                    mxa[i] = s if mxa[i] is None else jnp.maximum(mxa[i], s)
            ms = [jnp.max(a, axis=1, keepdims=True) for a in mxa]
            la = [jnp.zeros((R, kt_w), f32)] * DSA_HEAD_PAIR
            acc = [jnp.zeros((R, DSA_HEAD_DIM), f32)] * DSA_HEAD_PAIR
            for t in tiles:
                vt = vb[t, gc]
                for i in range(DSA_HEAD_PAIR):
                    p = jnp.exp(s_s[i, :, t] - ms[i])
                    la[i] = la[i] + p
                    acc[i] = acc[i] + _dot(p.astype(bf16), vt)
            for i in range(DSA_HEAD_PAIR):
                o_ref[:, hcs[i]] = (acc[i] / jnp.sum(la[i], axis=1, keepdims=True)).astype(bf16)
            return carry

        lax.fori_loop(0, DSA_HEADS // DSA_HEAD_PAIR, heads, 0)

    for e in range(T // ext_w):
        pl.when(qi // (ext_w // R) == e)(functools.partial(block, (e + 1) * ext_w))


def _dsa_prompt(iq, sm, ik, dq, dk, y, nb, t, topk):
    R = LANES
    nq = t // R
    kt_w = min(DSA_KEY_TILE, t)
    ext_w = min(DSA_KEY_EXTENT, t)
    kern = functools.partial(_dsa_prompt_kernel, topk=topk, kt_w=kt_w, ext_w=ext_w)
    return pl.pallas_call(
        kern,
        grid=(nb, nq),
        in_specs=[pl.BlockSpec((R, IQ_W), lambda b, i: (b * nq + i, 0)),
                  pl.BlockSpec((R, LANES), lambda b, i: (b * nq + i, 0)),
                  pl.BlockSpec((t, IDX_DIM), lambda b, i: (b, 0)),
                  pl.BlockSpec((R, DQ_W), lambda b, i: (b * nq + i, 0)),
                  pl.BlockSpec((t, DKV_W), lambda b, i: (b, 0)),
                  pl.BlockSpec((t, DKV_W), lambda b, i: (b, OFF_DV // DKV_W))],
        out_specs=pl.BlockSpec((R, DQ_W), lambda b, i: (b * nq + i, 0)),
        out_shape=jax.ShapeDtypeStruct((nb * t, DQ_W), bf16),
        scratch_shapes=[pltpu.VMEM((t, IDX_DIM), bf16), pltpu.VMEM((t, DKV_W), bf16),
                        pltpu.VMEM((t, DKV_W), bf16), pltpu.VMEM((R, t), f32), pltpu.VMEM((R, t), f32),
                        pltpu.VMEM((DSA_HEAD_PAIR, R, t), f32)],
        compiler_params=_params(48, 2),
        name="dsa_prompt",
    )(iq, sm, ik, dq, dk, y)


def _idx_score_kernel(pt_ref, iq_ref, sm_ref, ikn_ref, *rest, n_pages, page):
    eb, tp, _ = iq_ref.shape
    o_ref = rest[eb * n_pages]
    ts = o_ref.shape[1]
    past = n_pages * page
    lp = o_ref.shape[2]
    for e in range(eb):
        pages = rest[e * n_pages:(e + 1) * n_pages]
        keys_t = jnp.concatenate([p[...] for p in pages], axis=1).astype(bf16)
        keys_n = jnp.concatenate([ikn_ref[e], jnp.zeros((lp - past - tp, IDX_DIM), f32)], axis=0).astype(bf16)
        wgt = sm_ref[e] * IDX_SCALE
        iq = iq_ref[e].astype(bf16)
        score = jnp.zeros((tp, lp), f32)
        for h in range(IDX_HEADS):
            iqh = iq[:, h * IDX_DIM:(h + 1) * IDX_DIM]
            rel = jnp.maximum(jnp.concatenate([_dot(iqh, keys_t), _dot_nt(iqh, keys_n)], axis=1), 0.0)
            score = score + rel * wgt[:, SM_IW + h:SM_IW + h + 1]
        o_ref[e] = score[:ts]


def _idx_scores(page_table, iq, sm, ikn, cache_idx, lp, ts):
    db, tp, _ = iq.shape
    n_pages = page_table.shape[1]
    page = cache_idx.shape[2]
    eb = IDX_BATCH
    kern = functools.partial(_idx_score_kernel, n_pages=n_pages, page=page)
    page_specs = [pl.BlockSpec((None, IDX_DIM, page),
                               functools.partial(lambda b, pt, e, p: (pt[b * eb + e, p], 0, 0), e=e, p=p))
                  for e in range(eb) for p in range(n_pages)]
    grid_spec = pltpu.PrefetchScalarGridSpec(
        num_scalar_prefetch=1,
        grid=(db // eb,),
        in_specs=[pl.BlockSpec((eb, tp, IQ_W), lambda b, pt: (b, 0, 0)),
                  pl.BlockSpec((eb, tp, LANES), lambda b, pt: (b, 0, 0)),
                  pl.BlockSpec((eb, tp, IDX_DIM), lambda b, pt: (b, 0, 0))] + page_specs,
        out_specs=pl.BlockSpec((eb, ts, lp), lambda b, pt: (b, 0, 0)),
    )
    return pl.pallas_call(
        kern, grid_spec=grid_spec,
        out_shape=jax.ShapeDtypeStruct((db, ts, lp), f32),
        compiler_params=_params(),
        name="idx_scores_sample",
    )(page_table, iq, sm, ikn, *([cache_idx] * (eb * n_pages)))


def _select_sample_kernel(x_ref, bias_ref, x_s, *, topk, past, ts):
    R, lp = x_ref.shape
    key = lax.broadcasted_iota(jnp.int32, (R, lp), 1)
    t = lax.broadcasted_iota(jnp.int32, (R, lp), 0) % ts
    causal = key <= past + t
    x = x_ref[...]
    x_s[...] = jnp.where(causal, x, NEG)
    lo0 = jnp.min(jnp.where(causal, x, POS), axis=1, keepdims=True)
    mx = jnp.max(jnp.where(causal, x, NEG), axis=1, keepdims=True)
    n_causal = past + lax.broadcasted_iota(jnp.int32, (R, 1), 0) % ts + 1
    _topk_bias(x_s, bias_ref, lo0, mx, n_causal <= topk, lp // LANES, LANES, topk)


def _select_sample(scores, topk, past, ts):
    m, lp = scores.shape
    R = LANES
    kern = functools.partial(_select_sample_kernel, topk=topk, past=past, ts=ts)
    return pl.pallas_call(
        kern, grid=(m // R,),
        in_specs=[pl.BlockSpec((R, lp), lambda i: (i, 0))],
        out_specs=pl.BlockSpec((R, lp), lambda i: (i, 0)),
        out_shape=jax.ShapeDtypeStruct((m, lp), f32),
        scratch_shapes=[pltpu.VMEM((R, lp), f32)],
        compiler_params=_params(),
        name="select_sample",
    )(scores)


def _dsa_sample_kernel(pt_ref, q_ref, sel_ref, kn_ref, vn_ref, *rest, n_pages, page):
    eb, tp, _ = q_ref.shape
    o_ref = rest[2 * eb * n_pages]
    ts = sel_ref.shape[1]
    past = n_pages * page
    lp = sel_ref.shape[2]
    hpg = DSA_HEADS // DSA_KV_HEADS
    padn = jnp.zeros((lp - past - tp, DSA_HEAD_DIM), f32)
    for e in range(eb):
        kp = rest[e * n_pages:(e + 1) * n_pages]
        vp = rest[(eb + e) * n_pages:(eb + e + 1) * n_pages]
        bias = jnp.concatenate([sel_ref[e], jnp.zeros((tp - ts, lp), f32)], axis=0)
        bias = jnp.concatenate([bias] * hpg, axis=0)
        for g in range(DSA_KV_HEADS):
            gsl = slice(g * DSA_HEAD_DIM, (g + 1) * DSA_HEAD_DIM)
            kg = jnp.concatenate([p[pl.ds(g, page, stride=DSA_KV_HEADS), :] for p in kp]
                                 + [kn_ref[e, :, gsl], padn], axis=0).astype(bf16)
            vg = jnp.concatenate([p[pl.ds(g, page, stride=DSA_KV_HEADS), :] for p in vp]
                                 + [vn_ref[e, :, gsl], padn], axis=0).astype(bf16)
            qg = jnp.concatenate([q_ref[e, :, (g * hpg + j) * DSA_HEAD_DIM:(g * hpg + j + 1) * DSA_HEAD_DIM]
                                  for j in range(hpg)], axis=0).astype(bf16)
            s = _dot_nt(qg, kg) * (DSA_HEAD_DIM ** -0.5) + bias
            m = jnp.max(s, axis=1, keepdims=True)
            p = jnp.exp(s - m)
            l = jnp.sum(p, axis=1, keepdims=True)
            o = _dot(p.astype(bf16), vg) / l
            for j in range(hpg):
                hsl = slice((g * hpg + j) * DSA_HEAD_DIM, (g * hpg + j + 1) * DSA_HEAD_DIM)
                o_ref[e, :, hsl] = o[j * tp:j * tp + ts]


def _dsa_sample(page_table, dq, sel, kn, vn, ck, cv):
    db, tp, _ = dq.shape
    ts = sel.shape[1]
    lp = sel.shape[2]
    n_pages = page_table.shape[1]
    rows = ck.shape[1]
    page = rows // DSA_KV_HEADS
    eb = DSA_SAMPLE_BATCH
    kern = functools.partial(_dsa_sample_kernel, n_pages=n_pages, page=page)
    pspec = [pl.BlockSpec((None, rows, DSA_HEAD_DIM),
                          functools.partial(lambda b, pt, e, p: (pt[b * eb + e, p], 0, 0), e=e, p=p))
             for e in range(eb) for p in range(n_pages)]
    per_b = lambda r, w: pl.BlockSpec((eb, r, w), lambda b, pt: (b, 0, 0))
    grid_spec = pltpu.PrefetchScalarGridSpec(
        num_scalar_prefetch=1,
        grid=(db // eb,),
        in_specs=[per_b(tp, DQ_W), per_b(ts, lp), per_b(tp, DKV_W), per_b(tp, DKV_W)] + pspec + pspec,
        out_specs=per_b(ts, DQ_W),
    )
    return pl.pallas_call(
        kern, grid_spec=grid_spec,
        out_shape=jax.ShapeDtypeStruct((db, ts, DQ_W), f32),
        compiler_params=_params(48),
        name="dsa_sample",
    )(page_table, dq, sel, kn, vn, *([ck] * (eb * n_pages)), *([cv] * (eb * n_pages)))


def _merge_kernel(og_ref, od_ref, gg_ref, gd_ref, x_ref, gt_ref, sc_ref, sh_ref, g2_ref,
                  wg_ref, wd_ref, wo_ref, x1_ref, h2_ref):
    mix = (_sigmoid(gg_ref[...]) * _dot(og_ref[...], wg_ref[...])
           + _sigmoid(gd_ref[...]) * _dot(od_ref[...], wd_ref[...]))
    x1 = x_ref[...] + gt_ref[...] * _dot(mix.astype(bf16), wo_ref[...])
    x1_ref[...] = x1
    y = x1 * lax.rsqrt(jnp.mean(x1 * x1, axis=-1, keepdims=True) + NORM_EPS)
    h2_ref[...] = ((y * g2_ref[...]) * (1.0 + sc_ref[...]) + sh_ref[...]).astype(bf16)


def _merge(o_gdn, o_dsa, y, x, ada, g2, wg, wd, wo, group, rows_per_batch, tm):
    m, d = x.shape
    row = lambda w, blk=0: pl.BlockSpec((tm, w), lambda i: (i, blk))
    res = lambda a: pl.BlockSpec(a.shape, lambda i: (0, 0), pipeline_mode=pl.Buffered(1))
    return pl.pallas_call(
        _merge_kernel,
        grid=(m // tm,),
        in_specs=[row(Z_W), row(DQ_W), row(d, 0), row(d, 1), row(d),
                  _mod_spec(group, d, rows_per_batch, tm, 2),
                  _mod_spec(group, d, rows_per_batch, tm, 4),
                  _mod_spec(group, d, rows_per_batch, tm, 3),
                  pl.BlockSpec((1, d), lambda i: (0, 0)),
                  res(wg), res(wd), res(wo)],
        out_specs=[row(d), row(d)],
        out_shape=[jax.ShapeDtypeStruct((m, d), f32), jax.ShapeDtypeStruct((m, d), bf16)],
        compiler_params=_params(48),
        name="merge_" + group,
    )(o_gdn, o_dsa, y, y, x, ada, ada, ada, g2.reshape(1, d), wg, wd, wo)


def _ffn_epilogue(acc_ref, x1_ref, gt_ref, gf_ref, y_ref):
    tm = acc_ref.shape[0]
    r = gt_ref.shape[0] if gt_ref.shape[0] > 1 else tm
    for s0 in range(0, tm, r):
        sl = slice(s0, s0 + r)
        x2 = x1_ref[sl, :] + gt_ref[...] * acc_ref[sl, :]
        y_ref[sl, :] = (x2 * lax.rsqrt(jnp.mean(x2 * x2, axis=-1, keepdims=True) + NORM_EPS)) * gf_ref[...]


def _ffn_prompt_kernel(h_ref, wug_ref, wuv_ref, wcg_ref, wcv_ref, bg_ref, bv_ref, wd_ref,
                       x1_ref, gt_ref, gf_ref, y_ref, ug_ref, uv_ref, acc_ref, act_s, prev_s,
                       *, blocks_per_seq, n_up):
    i = pl.program_id(0)
    j = pl.program_id(1)
    tm = h_ref.shape[0]
    keep = (i % blocks_per_seq) > 0
    sr = min(FFN_SUB_ROWS, tm)

    @pl.when(jnp.logical_and(i == 0, j == 0))
    def _():
        prev_s[...] = jnp.zeros_like(prev_s)

    def conv(u, prev, wc_ref, b_ref):
        ux = jnp.concatenate([prev, u], axis=0)
        w = wc_ref[...]
        y = u * w[FFN_CONV - 1:FFN_CONV] + b_ref[...]
        for t in range(FFN_CONV - 1):
            sft = FFN_CONV - 1 - t
            y = y + ux[SUBLANES - sft:SUBLANES - sft + sr] * w[t:t + 1]
        return y

    def up(store):
        pg = jnp.where(keep, prev_s[j, 0], 0.0)
        pv = jnp.where(keep, prev_s[j, 1], 0.0)
        for s in range(tm // sr):
            rows = slice(s * sr, (s + 1) * sr)
            hs = h_ref[rows, :]
            ug = _dot(hs, wug_ref[...])
            uv = _dot(hs, wuv_ref[...])
            store(rows, _silu(conv(ug, pg, wcg_ref, bg_ref)) * conv(uv, pv, wcv_ref, bv_ref))
            pg = ug[sr - SUBLANES:]
            pv = uv[sr - SUBLANES:]
        ug_ref[...] = pg
        uv_ref[...] = pv
        prev_s[j, 0] = pg
        prev_s[j, 1] = pv

    _ffn_phases(j, n_up, up, act_s, wd_ref, acc_ref, x1_ref, gt_ref, gf_ref, y_ref)


def _ffn_phases(j, n_up, up, act_s, wd_ref, acc_ref, x1_ref, gt_ref, gf_ref, y_ref):
    tf = act_s.shape[1] // n_up
    tn = wd_ref.shape[1]

    @pl.when(j < n_up)
    def _():
        cols = pl.ds(pl.multiple_of(j * tf, tf), tf)

        def store(rows, act):
            act_s[rows, cols] = act.astype(bf16)
        up(store)

    @pl.when(j >= n_up)
    def _():
        acc_ref[:, pl.ds(pl.multiple_of((j - n_up) * tn, tn), tn)] = _dot(act_s[...], wd_ref[...])

    @pl.when(j == pl.num_programs(1) - 1)
    def _():
        _ffn_epilogue(acc_ref, x1_ref, gt_ref, gf_ref, y_ref)


def _ffn_sample_kernel(h_ref, buf_ref, wug_ref, wuv_ref, wcg_ref, wcv_ref, bg_ref, bv_ref, wd_ref,
                       x1_ref, gt_ref, gf_ref, y_ref, ug_ref, uv_ref, acc_ref, act_s, *, ts, n_up):
    j = pl.program_id(1)
    db = h_ref.shape[0] // ts
    nbuf = FFN_CONV - 1
    tf = wug_ref.shape[1]

    def branch(wu_ref, wc_ref, b_ref, ubuf_ref, half):
        u = _dot(h_ref[...], wu_ref[...])
        rows = [buf_ref[r, :, half * tf:(half + 1) * tf] for r in range(nbuf)]
        rows += [u[t * db:(t + 1) * db] for t in range(ts)]
        for r in range(nbuf):
            ubuf_ref[r] = rows[ts + r]
        w = wc_ref[...]
        outs = []
        for t in range(ts):
            y = rows[t] * w[0:1] + b_ref[...]
            for r in range(1, FFN_CONV):
                y = y + rows[t + r] * w[r:r + 1]
            outs.append(y)
        return jnp.concatenate(outs, axis=0)

    def up(store):
        store(slice(0, ts * db),
              _silu(branch(wug_ref, wcg_ref, bg_ref, ug_ref, 0)) * branch(wuv_ref, wcv_ref, bv_ref, uv_ref, 1))

    _ffn_phases(j, n_up, up, act_s, wd_ref, acc_ref, x1_ref, gt_ref, gf_ref, y_ref)


def _ffn(h2, x1, ada, g_final, w_up, w_conv, b_conv, w_down, group, rows_per_batch, tm, tf, buf=None):
    m, d = h2.shape
    dff = w_down.shape[0]
    nj = dff // tf
    tn = min(FFN_DOWN_TILE, d)
    b2 = b_conv.reshape(1, 2 * dff)
    up = lambda j: jnp.minimum(j, nj - 1)
    common_w = [pl.BlockSpec((d, tf), lambda i, j: (0, up(j))),
                pl.BlockSpec((d, tf), lambda i, j: (0, nj + up(j))),
                pl.BlockSpec((FFN_CONV, tf), lambda i, j: (0, up(j))),
                pl.BlockSpec((FFN_CONV, tf), lambda i, j: (0, nj + up(j))),
                pl.BlockSpec((1, tf), lambda i, j: (0, up(j))),
                pl.BlockSpec((1, tf), lambda i, j: (0, nj + up(j))),
                pl.BlockSpec((dff, tn), lambda i, j: (0, jnp.maximum(j - nj, 0)))]
    tail = [pl.BlockSpec((tm, d), lambda i, j: (i, 0)),
            _mod_spec(group, d, rows_per_batch, tm, 5),
            pl.BlockSpec((1, d), lambda i, j: (0, 0))]
    y_spec = pl.BlockSpec((tm, d), lambda i, j: (i, 0))
    if group == "prompt":
        bps = rows_per_batch // tm
        nb = m // rows_per_batch
        kern = functools.partial(_ffn_prompt_kernel, blocks_per_seq=bps, n_up=nj)
        first = [pl.BlockSpec((tm, d), lambda i, j: (i, 0))]
        ubuf_spec = pl.BlockSpec((None, SUBLANES, tf), lambda i, j: (i, 0, up(j)))
        ubuf_shape = jax.ShapeDtypeStruct((m // tm, SUBLANES, dff), f32)
        args = (h2,)
        extra_scratch = [pltpu.VMEM((nj, 2, SUBLANES, tf), f32)]
    else:
        extra_scratch = []
        ts = m // rows_per_batch
        kern = functools.partial(_ffn_sample_kernel, ts=ts, n_up=nj)
        first = [pl.BlockSpec((tm, d), lambda i, j: (i, 0)),
                 pl.BlockSpec((FFN_CONV - 1, rows_per_batch, 2 * tf), lambda i, j: (0, 0, up(j)))]
        ubuf_spec = pl.BlockSpec((FFN_CONV - 1, rows_per_batch, tf), lambda i, j: (0, 0, up(j)))
        ubuf_shape = jax.ShapeDtypeStruct((FFN_CONV - 1, rows_per_batch, dff), f32)
        args = (h2, buf)
    return pl.pallas_call(
        kern,
        grid=(m // tm, nj + d // tn),
        in_specs=first + common_w + tail,
        out_specs=[y_spec, ubuf_spec, ubuf_spec],
        out_shape=[jax.ShapeDtypeStruct((m, d), f32), ubuf_shape, ubuf_shape],
        scratch_shapes=[pltpu.VMEM((tm, d), f32), pltpu.VMEM((tm, dff), bf16)] + extra_scratch,
        compiler_params=_params(56, 2),
        name="ffn_" + group,
    )(*args, w_up, w_up, w_conv, w_conv, b2, b2, w_down, x1, ada, g_final.reshape(1, d))


def _split_w_in(w_in, d):
    sizes = (QKV_W, Z_W, GDN_HEADS, GDN_HEADS, DQ_W, DKV_W, DKV_W, IQ_W, IDX_DIM, IDX_HEADS, d, d)
    offs = [0]
    for s in sizes:
        offs.append(offs[-1] + s)
    seg = lambda i, j: w_in[:, offs[i]:offs[j]]
    pad = jnp.zeros((w_in.shape[0], LANES - (IDX_DIM + 2 * GDN_HEADS + IDX_HEADS)), w_in.dtype)
    small = jnp.concatenate([seg(8, 9), seg(2, 4), seg(9, 10), pad], axis=1)
    return tuple(w.astype(bf16) for w in (seg(0, 2), seg(4, 8), seg(10, 12), small))


def _in_proj(h, w_parts, tm, name):
    return tuple(_matmul(h, w, tm, _mm_tile(w.shape[1]), name + "_" + part)
                 for w, part in zip(w_parts, ("gdn", "dsa", "gate", "small")))


def _ffn_tile(dff):
    for tf in (512, 256, 128):
        if dff % tf == 0:
            return tf
    raise ValueError("d_ff must be a multiple of 128")


def _mm_tile(n):
    for tn in (1024, 512, 256, 128):
        if n % tn == 0:
            return tn
    raise ValueError("projection width must be a multiple of 128")


def kernel(x_prompt, x_sample, c_prompt, c_sample, cache_k, cache_v, cache_idx_k, page_table, state_gdn, state_gdn_conv, state_ffn_conv, w_ada, b_ada, g_norm1, w_in, w_gdn_conv, a_log, dt_bias, g_gdn_norm, w_gdn_out, w_dsa_out, w_o, g_norm2, w_up, w_ffn_conv, b_ffn_conv, w_down, g_final):
    nb, t, d = x_prompt.shape
    db, ts, _ = x_sample.shape
    depth = w_ada.shape[0]
    assert depth == 1 and db == LANES and ts >= GDN_CONV - 1
    n_pages = page_table.shape[1]
    page = cache_k.shape[2]
    past = n_pages * page
    dff = w_down.shape[1]
    l = 0

    n_c = nb + db
    pad_c = (-n_c) % SUBLANES
    c_all = jnp.concatenate([c_prompt, c_sample, jnp.zeros((pad_c, d), f32)], axis=0)
    ada = _ada(c_all, w_ada[l], b_ada[l])
    ada_p = ada[:nb].reshape(nb, 1, 6 * d)
    ada_s = ada[nb:nb + db]

    w_in_parts = _split_w_in(w_in[l], d)
    wg = w_gdn_out[l].astype(bf16)
    wd = w_dsa_out[l].astype(bf16)
    wo = w_o[l].astype(bf16)
    wup = w_up[l].astype(bf16)
    wdn = w_down[l].astype(bf16)
    tf = _ffn_tile(dff)

    xp = x_prompt.reshape(nb * t, d)
    tm_p = min(512, t)
    h1 = _prep(xp, g_norm1[l], ada_p, "prompt", t, tm_p)
    yp_gdn, yp_dsa, yp_gate, yp_small = _in_proj(h1, w_in_parts, min(IN_PROJ_ROWS, nb * t), "in_proj_prompt")
    og_p, s_p = _gdn_prompt(yp_gdn, yp_small, w_gdn_conv[l], a_log[l], dt_bias[l], g_gdn_norm[l], nb, t)
    dq_p, dk_p, iq_p, ik_p, sm_p = _rope(yp_dsa, yp_small, jnp.arange(t), tm_p, "prompt")
    od_p = _dsa_prompt(iq_p, sm_p, ik_p, dq_p, dk_p, yp_dsa, nb, t, min(DSA_TOPK, t // 4))
    x1_p, h2_p = _merge(og_p, od_p, yp_gate, xp, ada_p, g_norm2[l], wg, wd, wo, "prompt", t, min(256, t))
    y_p, ug_p, uv_p = _ffn(h2_p, x1_p, ada_p, g_final, wup, w_ffn_conv[l], b_ffn_conv[l], wdn,
                           "prompt", t, tm_p, tf)

    nfb = FFN_CONV - 1
    bps_p = t // tm_p
    out_p = (
        y_p.reshape(nb, t, d),
        dk_p.reshape(1, nb, t, DSA_KV_HEADS, DSA_HEAD_DIM),
        yp_dsa[:, OFF_DV:OFF_DV + DKV_W].reshape(1, nb, t, DSA_KV_HEADS, DSA_HEAD_DIM),
        ik_p.reshape(1, nb, t, IDX_DIM),
        s_p[None],
        yp_gdn.reshape(nb, t, -1)[:, t - (GDN_CONV - 1):, :QKV_W][None],
        jnp.concatenate([ug_p[bps_p - 1::bps_p, SUBLANES - nfb:], uv_p[bps_p - 1::bps_p, SUBLANES - nfb:]],
                        axis=-1)[None],
    )

    xs = x_sample.transpose(1, 0, 2).reshape(ts * db, d)
    h1s = _prep(xs, g_norm1[l], ada_s, "sample", db, db)
    ys_gdn, ys_dsa, ys_gate, ys_small = _in_proj(h1s, w_in_parts, ts * db, "in_proj_sample")
    ys_gdn3 = ys_gdn.reshape(ts, db, -1)
    og_s, s_s = _gdn_sample(ys_gdn3, ys_small.reshape(ts, db, LANES), state_gdn_conv[l].transpose(1, 0, 2),
                            state_gdn[l], w_gdn_conv[l], a_log[l], dt_bias[l], g_gdn_norm[l])
    pos_s = jnp.repeat(past + jnp.arange(ts), db)
    dq_s, dk_s, iq_s, ik_s, sm_s = _rope(ys_dsa, ys_small, pos_s, db, "sample", db)
    ik_s_b = ik_s.reshape(ts, db, IDX_DIM).transpose(1, 0, 2)
    lp = past + LANES
    tp = -(-ts // SUBLANES) * SUBLANES
    pad_t = lambda a: jnp.pad(a.astype(f32), ((0, 0), (0, tp - ts), (0, 0)))
    scores = _idx_scores(page_table, pad_t(iq_s.reshape(db, ts, IQ_W)), pad_t(sm_s.reshape(db, ts, LANES)),
                         pad_t(ik_s_b), jnp.swapaxes(cache_idx_k[l], 1, 2), lp, ts)
    sel = _select_sample(scores.reshape(db * ts, lp), min(DSA_TOPK, (past + ts) // 4), past, ts)
    dv_s_b = ys_dsa.reshape(ts, db, -1)[:, :, OFF_DV:OFF_DV + DKV_W].transpose(1, 0, 2)
    ck = cache_k[l].reshape(cache_k.shape[1], page * DSA_KV_HEADS, DSA_HEAD_DIM)
    cv = cache_v[l].reshape(cache_v.shape[1], page * DSA_KV_HEADS, DSA_HEAD_DIM)
    od_s = _dsa_sample(page_table, pad_t(dq_s.reshape(db, ts, DQ_W)), sel.reshape(db, ts, lp),
                       pad_t(dk_s.reshape(db, ts, DKV_W)), pad_t(dv_s_b), ck, cv)
    od_s = od_s.astype(bf16).transpose(1, 0, 2).reshape(ts * db, DQ_W)
    x1_s, h2_s = _merge(og_s.reshape(ts * db, Z_W), od_s, ys_gate, xs, ada_s, g_norm2[l], wg, wd, wo,
                        "sample", db, db)
    fb = state_ffn_conv[l].transpose(1, 0, 2)
    nj = dff // tf
    fb = jnp.concatenate([fb[:, :, :dff].reshape(nfb, db, nj, tf), fb[:, :, dff:].reshape(nfb, db, nj, tf)],
                         axis=-1).reshape(nfb, db, 2 * dff)
    y_s, ug_s, uv_s = _ffn(h2_s, x1_s, ada_s, g_final, wup, w_ffn_conv[l], b_ffn_conv[l], wdn,
                           "sample", db, ts * db, tf, buf=fb)

    out_s = (
        y_s.reshape(ts, db, d).transpose(1, 0, 2),
        dk_s.reshape(1, db, ts, DSA_KV_HEADS, DSA_HEAD_DIM),
        dv_s_b.reshape(1, db, ts, DSA_KV_HEADS, DSA_HEAD_DIM),
        ik_s_b[None],
        s_s[None],
        ys_gdn3[ts - (GDN_CONV - 1):, :, :QKV_W].transpose(1, 0, 2)[None],
        jnp.concatenate([ug_s, uv_s], axis=-1).transpose(1, 0, 2)[None],
    )
    return (out_p[0], out_s[0]) + out_p[1:] + out_s[1:]
```

```python
import functools

import jax
import jax.numpy as jnp
from jax import lax
from jax.experimental import pallas as pl
from jax.experimental.pallas import tpu as pltpu

f32 = jnp.float32
bf16 = jnp.bfloat16

GDN_HEADS = 8
GDN_DK = 128
GDN_DV = 128
GDN_CONV = 4
GDN_CHUNK = 64
GDN_GROUP = 4
GDN_SEQS = 4
DSA_HEADS = 8
DSA_KV_HEADS = 2
DSA_HEAD_DIM = 128
IDX_HEADS = 8
IDX_DIM = 64
IDX_SCALE = IDX_HEADS ** -0.5 * IDX_DIM ** -0.5
DSA_TOPK = 256
ROPE_THETA = 500000.0
ROPE_FRACTION = 4
FFN_CONV = 3
NORM_EPS = 1e-6

LANES = 128
SUBLANES = 8
NEG = float(jnp.finfo(jnp.float32).min)
POS = float(jnp.finfo(jnp.float32).max)
BISECT_UNROLL = 4
SNAP_FROM = 4
BISECT_CAP = 1024
DSA_KEY_TILE = 256
DSA_KEY_EXTENT = 256
IN_PROJ_ROWS = 1024
IDX_BATCH = 8
DSA_SAMPLE_BATCH = 4
DSA_HEAD_GROUP = 4
FFN_DOWN_TILE = 512
FFN_SUB_ROWS = 256

QKV_W =2 * GDN_HEADS * GDN_DK + GDN_HEADS * GDN_DV
Z_W = GDN_HEADS * GDN_DV
DQ_W = DSA_HEADS * DSA_HEAD_DIM
DKV_W = DSA_KV_HEADS * DSA_HEAD_DIM
IQ_W = IDX_HEADS * IDX_DIM
OFF_Z = QKV_W
GDN_PROJ_W = QKV_W + Z_W
OFF_DQ = 0
OFF_DK = OFF_DQ + DQ_W
OFF_DV = OFF_DK + DKV_W
OFF_IQ = OFF_DV + DKV_W
DSA_PROJ_W = OFF_IQ + IQ_W
SM_IK = 0
SM_A = IDX_DIM
SM_B = SM_A + GDN_HEADS
SM_IW = SM_B + GDN_HEADS


def _sigmoid(x):
    return 1.0 / (1.0 + jnp.exp(-x))


def _silu(x):
    return x * _sigmoid(x)


def _dot(a, b):
    return jnp.dot(a, b, preferred_element_type=f32)


def _dot_nt(a, b):
    return lax.dot_general(a, b, (((1,), (1,)), ((), ())), preferred_element_type=f32)


def _dot_tn(a, b):
    return lax.dot_general(a, b, (((0,), (0,)), ((), ())), preferred_element_type=f32)


def _split3(a):
    hi = a.astype(bf16)
    lo = (a - hi.astype(f32)).astype(bf16)
    return hi, lo


def _mm3(a, b):
    ah, al = a
    bh, bl = b
    return _dot(ah, bh) + (_dot(ah, bl) + _dot(al, bh))


def _dot_exact(a, b):
    return jnp.dot(a, b, preferred_element_type=f32, precision=lax.Precision.HIGHEST)


def _params(vmem_mb=None, n_axes=1):
    kw = dict(dimension_semantics=("arbitrary",) * n_axes)
    if vmem_mb is not None:
        kw["vmem_limit_bytes"] = vmem_mb * 1024 * 1024
    return pltpu.CompilerParams(**kw)


def _ada_kernel(c_ref, w_ref, b_ref, o_ref):
    s = _silu(c_ref[...]).astype(bf16)
    o_ref[...] = _dot(s, w_ref[...].astype(bf16)) + b_ref[...]


def _ada(c_all, w_ada, b_ada):
    m, d = c_all.shape
    n = w_ada.shape[1]
    tn = 1024
    return pl.pallas_call(
        _ada_kernel,
        grid=(n // tn,),
        in_specs=[pl.BlockSpec((m, d), lambda j: (0, 0)),
                  pl.BlockSpec((d, tn), lambda j: (0, j)),
                  pl.BlockSpec((1, tn), lambda j: (0, j))],
        out_specs=pl.BlockSpec((m, tn), lambda j: (0, j)),
        out_shape=jax.ShapeDtypeStruct((m, n), f32),
        compiler_params=_params(48),
        name="ada",
    )(c_all, w_ada, b_ada.reshape(1, n))


def _mod_spec(group, d, rows_per_batch, tm, col):
    if group == "prompt":
        return pl.BlockSpec((None, 1, d), lambda i, *_: ((i * tm) // rows_per_batch, 0, col))
    return pl.BlockSpec((rows_per_batch, d), lambda i, *_: (0, col))


def _prep_kernel(x_ref, g_ref, sc_ref, sh_ref, o_ref):
    x = x_ref[...]
    y = x * lax.rsqrt(jnp.mean(x * x, axis=-1, keepdims=True) + NORM_EPS)
    o_ref[...] = ((y * g_ref[...]) * (1.0 + sc_ref[...]) + sh_ref[...]).astype(bf16)


def _prep(x, g, ada, group, rows_per_batch, tm):
    m, d = x.shape
    return pl.pallas_call(
        _prep_kernel,
        grid=(m // tm,),
        in_specs=[pl.BlockSpec((tm, d), lambda i: (i, 0)),
                  pl.BlockSpec((1, d), lambda i: (0, 0)),
                  _mod_spec(group, d, rows_per_batch, tm, 1),
                  _mod_spec(group, d, rows_per_batch, tm, 0)],
        out_specs=pl.BlockSpec((tm, d), lambda i: (i, 0)),
        out_shape=jax.ShapeDtypeStruct((m, d), bf16),
        compiler_params=_params(),
        name="prep_" + group,
    )(x, g.reshape(1, d), ada, ada)


def _mm_kernel(a_ref, w_ref, o_ref):
    o_ref[...] = _dot(a_ref[...], w_ref[...])


def _matmul(a, w, tm, tn, name):
    m, k = a.shape
    n = w.shape[1]
    return pl.pallas_call(
        _mm_kernel,
        grid=(n // tn, m // tm),
        in_specs=[pl.BlockSpec((tm, k), lambda j, i: (i, 0)),
                  pl.BlockSpec((k, tn), lambda j, i: (0, j))],
        out_specs=pl.BlockSpec((tm, tn), lambda j, i: (i, j)),
        out_shape=jax.ShapeDtypeStruct((m, n), f32),
        compiler_params=_params(48, 2),
        name=name,
    )(a, w)


def _l2n(x):
    return x * lax.rsqrt(jnp.sum(x * x, axis=-1, keepdims=True) + NORM_EPS)


def _gdn_gates(sm, alog, dtb):
    xa = sm + dtb
    softplus = jnp.maximum(xa, 0.0) + jnp.log1p(jnp.exp(-jnp.abs(xa)))
    return -jnp.exp(alog) * softplus, _sigmoid(sm)


def _gated_norm(o, gn, z):
    y = o * lax.rsqrt(jnp.mean(o * o, axis=-1, keepdims=True) + NORM_EPS)
    return (y * gn) * _silu(z)


def _gdn_prompt_kernel(qkv_ref, halo_ref, z_ref, sm_ref, wc_ref, alog_ref, dtb_ref, gn_ref,
                       o_ref, sfin_ref, s_ref):
    c = pl.program_id(1)
    n_seq, C, _ = qkv_ref.shape

    @pl.when(c == 0)
    def _():
        s_ref[...] = jnp.zeros_like(s_ref)

    keep = (c > 0).astype(f32)
    ri = lax.broadcasted_iota(jnp.int32, (C, C), 0)
    ci = lax.broadcasted_iota(jnp.int32, (C, C), 1)
    tril = (ri >= ci).astype(f32)
    gates = [_gdn_gates(sm_ref[s], alog_ref[...], dtb_ref[...]) for s in range(n_seq)]
    beta_all = [g[1] for g in gates]
    gc_all = [_dot_exact(tril, g[0]) for g in gates]
    gc_t = [g.T for g in gc_all]

    def conv(s, col):
        xs = jnp.concatenate([halo_ref[s, :, col:col + LANES] * keep, qkv_ref[s, :, col:col + LANES]], axis=0)
        w = wc_ref[:, col:col + LANES]
        y = xs[SUBLANES:] * w[GDN_CONV - 1:GDN_CONV]
        for i in range(GDN_CONV - 1):
            sft = GDN_CONV - 1 - i
            y = y + xs[SUBLANES - sft:SUBLANES - sft + C] * w[i:i + 1]
        return _silu(y)

    N = GDN_GROUP * C
    rn = lax.broadcasted_iota(jnp.int32, (N, N), 0)
    cn = lax.broadcasted_iota(jnp.int32, (N, N), 1)
    same = (rn // C) == (cn // C)
    incl = jnp.logical_and(same, rn >= cn)
    strict = jnp.logical_and(same, rn > cn)
    eye_f = (rn == cn).astype(f32)
    n_sq = max(1, (C - 1).bit_length() - 1)
    groups = [(s, list(range(g0, g0 + GDN_GROUP))) for s in range(n_seq)
              for g0 in range(0, GDN_HEADS, GDN_GROUP)]
    stack = lambda xs: jnp.concatenate(xs, axis=0)

    qs = [stack([_l2n(conv(s, h * GDN_DK)) * (GDN_DK ** -0.5) for h in hs]) for s, hs in groups]
    ks = [stack([_l2n(conv(s, GDN_HEADS * GDN_DK + h * GDN_DK)) for h in hs]) for s, hs in groups]
    vs = [stack([conv(s, 2 * GDN_HEADS * GDN_DK + h * GDN_DV) for h in hs]) for s, hs in groups]
    gcs = [stack([gc_all[s][:, SM_A + h:SM_A + h + 1] for h in hs]) for s, hs in groups]
    betas = [stack([beta_all[s][:, SM_B + h:SM_B + h + 1] for h in hs]) for s, hs in groups]
    gc_rows = [jnp.concatenate([gc_t[s][SM_A + h:SM_A + h + 1, :] for h in hs], axis=1)
               for s, hs in groups]
    egs = [jnp.exp(gc) for gc in gcs]
    decays = [jnp.where(incl, jnp.exp(jnp.where(incl, gc - gr, 0.0)), 0.0) for gc, gr in zip(gcs, gc_rows)]
    kbs = [k.astype(bf16) for k in ks]
    a_s = [jnp.where(strict, b * _dot_nt(kb, kb) * dc, 0.0) for b, kb, dc in zip(betas, kbs, decays)]
    qks = [(_dot_nt(q.astype(bf16), kb) * dc).astype(bf16) for q, kb, dc in zip(qs, kbs, decays)]

    invs = [eye_f - a for a in a_s]
    pws = [a.astype(bf16) for a in a_s]
    for _ in range(n_sq):
        pws = [_dot(p, p).astype(bf16) for p in pws]
        invs = [inv + _dot(inv.astype(bf16), p) for inv, p in zip(invs, pws)]
    inv_s = [_split3(inv) for inv in invs]
    res = [eye_f - inv - _mm3(_split3(a), sp) for a, inv, sp in zip(a_s, invs, inv_s)]
    invs = [inv + _dot(sp[0], r.astype(bf16)) for inv, sp, r in zip(invs, inv_s, res)]
    rhs = [jnp.concatenate([b * v, (b * eg) * k], axis=1) for b, v, eg, k in zip(betas, vs, egs, ks)]
    sols = [_mm3(_split3(inv), _split3(r)) for inv, r in zip(invs, rhs)]

    for gi, (s, hs) in enumerate(groups):
        sol, q, k, gc, eg = sols[gi], qs[gi], ks[gi], gcs[gi], egs[gi]
        q_dec = (q * eg).astype(bf16)
        w_k = sol[:, GDN_DV:].astype(bf16)
        rows = [slice(j * C, (j + 1) * C) for j in range(GDN_GROUP)]
        s_old = [s_ref[s, h] for h in hs]
        ws = [_dot(jnp.concatenate([w_k[r], q_dec[r]], axis=0), so.astype(bf16)) for r, so in zip(rows, s_old)]
        u = stack([sol[r, :GDN_DV] - w[:C] for r, w in zip(rows, ws)])
        ub = u.astype(bf16)
        o_intra = _dot(qks[gi], ub)
        for j, h in enumerate(hs):
            r = rows[j]
            gl = gc[r][C - 1:C, :]
            k_end = (k[r] * jnp.exp(gl - gc[r])).astype(bf16)
            s_ref[s, h] = s_old[j] * jnp.exp(gl) + _dot_tn(k_end, ub[r])
            o = ws[j][C:] + o_intra[r]
            zh = z_ref[s, :, h * GDN_DV:(h + 1) * GDN_DV]
            o_ref[s, :, h * GDN_DV:(h + 1) * GDN_DV] = _gated_norm(o, gn_ref[...], zh).astype(bf16)

    @pl.when(c == pl.num_programs(1) - 1)
    def _():
        sfin_ref[...] = s_ref[...]


def _gdn_vecs(a_log, dt_bias, g_gdn_norm):
    alog = jnp.zeros((1, LANES), f32).at[0, SM_A:SM_A + GDN_HEADS].set(a_log)
    dtb = jnp.zeros((1, LANES), f32).at[0, SM_A:SM_A + GDN_HEADS].set(dt_bias)
    return alog, dtb, g_gdn_norm.reshape(1, GDN_DV)


def _gdn_prompt(y, y_small, w_conv, a_log, dt_bias, g_gdn_norm, nb, t):
    C = GDN_CHUNK
    nc = t // C
    ns = GDN_SEQS if nb % GDN_SEQS == 0 else 1
    alog, dtb, gn = _gdn_vecs(a_log, dt_bias, g_gdn_norm)
    cst = lambda b, c: (0, 0)
    y3 = y.reshape(nb, t, -1)
    ysm3 = y_small.reshape(nb, t, LANES)
    o, s_fin = pl.pallas_call(
        _gdn_prompt_kernel,
        grid=(nb // ns, nc),
        in_specs=[pl.BlockSpec((ns, C, QKV_W), lambda b, c: (b, c, 0)),
                  pl.BlockSpec((ns, SUBLANES, QKV_W),
                               lambda b, c: (b, jnp.maximum(c * (C // SUBLANES) - 1, 0), 0)),
                  pl.BlockSpec((ns, C, Z_W), lambda b, c: (b, c, OFF_Z // Z_W)),
                  pl.BlockSpec((ns, C, LANES), lambda b, c: (b, c, 0)),
                  pl.BlockSpec((GDN_CONV, QKV_W), cst),
                  pl.BlockSpec((1, LANES), cst),
                  pl.BlockSpec((1, LANES), cst),
                  pl.BlockSpec((1, GDN_DV), cst)],
        out_specs=[pl.BlockSpec((ns, C, Z_W), lambda b, c: (b, c, 0)),
                   pl.BlockSpec((ns, GDN_HEADS, GDN_DK, GDN_DV), lambda b, c: (b, 0, 0, 0))],
        out_shape=[jax.ShapeDtypeStruct((nb, t, Z_W), bf16),
                   jax.ShapeDtypeStruct((nb, GDN_HEADS, GDN_DK, GDN_DV), f32)],
        scratch_shapes=[pltpu.VMEM((ns, GDN_HEADS, GDN_DK, GDN_DV), f32)],
        compiler_params=_params(None, 2),
        name="gdn_prompt",
    )(y3, y3, y3, ysm3, w_conv, alog, dtb, gn)
    return o.reshape(nb * t, Z_W), s_fin


def _gdn_sample_kernel(qkv_ref, buf_ref, z_ref, sm_ref, sin_ref, wc_ref, alog_ref, dtb_ref, gn_ref,
                       o_ref, sout_ref, q_s, k_s, v_s, a_s, b_s, o_s):
    ts, G, _ = qkv_ref.shape
    nbuf = GDN_CONV - 1

    for h in range(GDN_HEADS):
        for part, dst in ((0, q_s), (1, k_s), (2, v_s)):
            col = part * GDN_HEADS * GDN_DK + h * GDN_DK
            w = wc_ref[:, col:col + LANES]
            rows = [buf_ref[i, :, col:col + LANES] for i in range(nbuf)]
            rows += [qkv_ref[t, :, col:col + LANES] for t in range(ts)]
            for t in range(ts):
                y = rows[t] * w[0:1]
                for i in range(1, GDN_CONV):
                    y = y + rows[t + i] * w[i:i + 1]
                y = _silu(y)
                if part == 0:
                    y = _l2n(y) * (GDN_DK ** -0.5)
                elif part == 1:
                    y = _l2n(y)
                dst[t, :, h * LANES:(h + 1) * LANES] = y

    for t in range(ts):
        g_all, beta_all = _gdn_gates(sm_ref[t], alog_ref[...], dtb_ref[...])
        a_all = jnp.exp(g_all)
        for h in range(GDN_HEADS):
            a_s[t, :, h * LANES:(h + 1) * LANES] = jnp.broadcast_to(a_all[:, SM_A + h:SM_A + h + 1], (G, LANES))
            b_s[t, :, h * LANES:(h + 1) * LANES] = jnp.broadcast_to(beta_all[:, SM_B + h:SM_B + h + 1], (G, LANES))

    nv = 2 * ts
    sr = lax.broadcasted_iota(jnp.int32, (3 * nv, nv * LANES), 0)
    sc = lax.broadcasted_iota(jnp.int32, (3 * nv, nv * LANES), 1)
    spread = (sr % nv == sc // LANES).astype(bf16)

    def head(h, carry):
        cols = pl.ds(pl.multiple_of(h * LANES, LANES), LANES)
        for i in range(G):
            row = slice(i, i + 1)
            kq = jnp.concatenate([k_s[t, row, cols] for t in range(ts)]
                                 + [q_s[t, row, cols] for t in range(ts)], axis=0)
            hi = kq.astype(bf16).astype(f32)
            mid = (kq - hi).astype(bf16).astype(f32)
            lo = (kq - hi) - mid
            kq_b = _dot(jnp.concatenate([hi, mid, lo], axis=0).T.astype(bf16), spread)
            s = sin_ref[i, h]
            for t in range(ts):
                kc = kq_b[:, t * LANES:(t + 1) * LANES]
                qc = kq_b[:, (ts + t) * LANES:(ts + t + 1) * LANES]
                a = a_s[t, row, cols]
                b = b_s[t, row, cols]
                ks = jnp.sum(s * kc, axis=0, keepdims=True)
                r = b * (v_s[t, row, cols] - a * ks)
                s = a * s + kc * r
                o_s[t, row, cols] = jnp.sum(s * qc, axis=0, keepdims=True)
            sout_ref[i, h] = s
        return carry

    lax.fori_loop(0, GDN_HEADS, head, 0)

    for t in range(ts):
        for h in range(GDN_HEADS):
            sl = slice(h * GDN_DV, (h + 1) * GDN_DV)
            o_ref[t, :, sl] = _gated_norm(o_s[t, :, sl], gn_ref[...], z_ref[t, :, sl]).astype(bf16)


def _gdn_sample(y3, y3_small, buf3, state, w_conv, a_log, dt_bias, g_gdn_norm):
    ts, db, _ = y3.shape
    G = SUBLANES
    alog, dtb, gn = _gdn_vecs(a_log, dt_bias, g_gdn_norm)
    cst = lambda g: (0, 0)
    st_spec = pl.BlockSpec((G, GDN_HEADS, GDN_DK, GDN_DV), lambda g: (g, 0, 0, 0))
    scr = pltpu.VMEM((ts, G, Z_W), f32)
    return pl.pallas_call(
        _gdn_sample_kernel,
        grid=(db // G,),
        in_specs=[pl.BlockSpec((ts, G, QKV_W), lambda g: (0, g, 0)),
                  pl.BlockSpec((GDN_CONV - 1, G, QKV_W), lambda g: (0, g, 0)),
                  pl.BlockSpec((ts, G, Z_W), lambda g: (0, g, OFF_Z // Z_W)),
                  pl.BlockSpec((ts, G, LANES), lambda g: (0, g, 0)),
                  st_spec,
                  pl.BlockSpec((GDN_CONV, QKV_W), cst),
                  pl.BlockSpec((1, LANES), cst),
                  pl.BlockSpec((1, LANES), cst),
                  pl.BlockSpec((1, GDN_DV), cst)],
        out_specs=[pl.BlockSpec((ts, G, Z_W), lambda g: (0, g, 0)), st_spec],
        out_shape=[jax.ShapeDtypeStruct((ts, db, Z_W), bf16),
                   jax.ShapeDtypeStruct(state.shape, f32)],
        scratch_shapes=[scr, scr, scr, scr, scr, scr],
        compiler_params=_params(48),
        name="gdn_sample",
    )(y3, buf3, y3, y3_small, state, w_conv, alog, dtb, gn)


def _rope_tables(pos, rot, width):
    half = rot // 2
    inv_freq = ROPE_THETA ** (-jnp.arange(half, dtype=f32) * (2.0 / rot))
    ang = pos.astype(f32)[:, None] * inv_freq[None, :]
    cos, sin = jnp.cos(ang), jnp.sin(ang)
    n = pos.shape[0]
    z = lambda w: jnp.zeros((n, w), f32)
    cosw = jnp.concatenate([cos, cos, jnp.ones((n, width - rot), f32)], axis=1)
    sina = jnp.concatenate([-sin, z(width - half)], axis=1)
    sinb = jnp.concatenate([z(half), sin, z(width - rot)], axis=1)
    reps = LANES // width
    return tuple(jnp.tile(a, (1, reps)) for a in (cosw, sina, sinb))


def _rope_kernel(dq_ref, dk_ref, iq_ref, sm_ref, c1, sa1, sb1, c2, sa2, sb2,
                 dq_o, dk_o, iq_o, ik_o, sm_o):
    h1 = DSA_HEAD_DIM // ROPE_FRACTION // 2
    h2 = IDX_DIM // ROPE_FRACTION // 2

    def rot(x, c, sa, sb, half):
        return x * c[...] + pltpu.roll(x, LANES - half, 1) * sa[...] + pltpu.roll(x, half, 1) * sb[...]

    for j in range(DQ_W // LANES):
        sl = slice(j * LANES, (j + 1) * LANES)
        dq_o[:, sl] = rot(dq_ref[:, sl], c1, sa1, sb1, h1).astype(bf16)
    for j in range(DKV_W // LANES):
        sl = slice(j * LANES, (j + 1) * LANES)
        dk_o[:, sl] = rot(dk_ref[:, sl], c1, sa1, sb1, h1)
    for j in range(IQ_W // LANES):
        sl = slice(j * LANES, (j + 1) * LANES)
        iq_o[:, sl] = rot(iq_ref[:, sl], c2, sa2, sb2, h2).astype(bf16)
    sm = sm_ref[...]
    ik_o[...] = rot(sm, c2, sa2, sb2, h2)[:, :IDX_DIM]
    sm_o[...] = sm


def _rope(y, y_small, pos, tm, group, db=None):
    m = y.shape[0]
    t1 =_rope_tables(pos, DSA_HEAD_DIM // ROPE_FRACTION, DSA_HEAD_DIM)
    t2 = _rope_tables(pos, IDX_DIM // ROPE_FRACTION, IDX_DIM)
    tab_blocks = pos.shape[0] // tm
    tab = pl.BlockSpec((tm, LANES), lambda i: (i % tab_blocks, 0))
    if group == "prompt":
        omap = lambda i: (i, 0)
        rows = lambda w: m
        cols = lambda w: w
    else:
        ts = m // db
        omap = lambda i: (0, i)
        rows = lambda w: db
        cols = lambda w: ts * w
    out = lambda w, dt: jax.ShapeDtypeStruct((rows(w), cols(w)), dt)
    return pl.pallas_call(
        _rope_kernel,
        grid=(m // tm,),
        in_specs=[pl.BlockSpec((tm, DQ_W), lambda i: (i, OFF_DQ // DQ_W)),
                  pl.BlockSpec((tm, DKV_W), lambda i: (i, OFF_DK // DKV_W)),
                  pl.BlockSpec((tm, IQ_W), lambda i: (i, OFF_IQ // IQ_W)),
                  pl.BlockSpec((tm, LANES), lambda i: (i, 0)),
                  tab, tab, tab, tab, tab, tab],
        out_specs=[pl.BlockSpec((tm, DQ_W), omap),
                   pl.BlockSpec((tm, DKV_W), omap),
                   pl.BlockSpec((tm, IQ_W), omap),
                   pl.BlockSpec((tm, IDX_DIM), lambda i: (i, 0)),
                   pl.BlockSpec((tm, LANES), omap)],
        out_shape=[out(DQ_W, bf16), out(DKV_W, f32), out(IQ_W, bf16),
                   jax.ShapeDtypeStruct((m, IDX_DIM), f32), out(LANES, f32)],
        compiler_params=_params(),
        name="rope_" + group,
    )(y, y, y, y_small, *t1, *t2)


def _topk_bias(x_s, bias_s, lo0, mx, few, nkt, kt_w, k):
    R = x_s.shape[0]
    kf = float(k)
    tiles = [slice(kt * kt_w, (kt + 1) * kt_w) for kt in range(nkt)]
    n_grp = 2 if R % (2 * SUBLANES) == 0 else 1
    grps = [slice(g * (R // n_grp), (g + 1) * (R // n_grp)) for g in range(n_grp)]

    def count_ge(rows, th):
        acc = jnp.where(x_s[rows, tiles[0]] >= th, 1.0, 0.0)
        for t in tiles[1:]:
            acc = acc + jnp.where(x_s[rows, t] >= th, 1.0, 0.0)
        return jnp.sum(acc, axis=1, keepdims=True)

    def bisect(rows, lo, hi, done):
        mid = 0.5 * lo + 0.5 * hi
        cnt = count_ge(rows, mid)
        collapsed = jnp.logical_or(mid <= lo, mid >= hi)
        live = jnp.logical_and(done < 0.5, jnp.logical_not(collapsed))
        lo = jnp.where(jnp.logical_and(live, cnt >= kf), mid, lo)
        hi = jnp.where(jnp.logical_and(live, cnt <= kf), mid, hi)
        done = jnp.where(jnp.logical_or(collapsed, cnt == kf), 1.0, done)
        return lo, hi, done

    def snap(rows, lo, hi, done):
        a = b = None
        for t in tiles:
            x = x_s[rows, t]
            at = jnp.where(x >= lo, x, POS)
            bt = jnp.where(x < hi, x, NEG)
            a = at if a is None else jnp.minimum(a, at)
            b = bt if b is None else jnp.maximum(b, bt)
        a = jnp.min(a, axis=1, keepdims=True)
        b = jnp.max(b, axis=1, keepdims=True)
        live = done < 0.5
        return jnp.where(live, a, lo), jnp.where(jnp.logical_and(live, a >= b), 1.0, done)

    def cond(c):
        it, st = c
        left = st[0][2]
        for s in st[1:]:
            left = jnp.minimum(left, s[2])
        return jnp.logical_and(it < BISECT_CAP, jnp.min(left) < 0.5)

    def body(c):
        it, st = c
        for _ in range(BISECT_UNROLL):
            st = [bisect(rows, *s) for rows, s in zip(grps, st)]

        def snapped():
            out = []
            for rows, (lo, hi, done) in zip(grps, st):
                lo2, done2 = snap(rows, lo, hi, done)
                out.append((lo2, hi, done2))
            return out

        st = lax.cond(it >= SNAP_FROM, snapped, lambda: st)
        return it + 1, st

    hi0 = jnp.where(few, lo0, mx + (jnp.abs(mx) + 1.0))
    st0 = [(lo0[rows], hi0[rows], few[rows].astype(f32)) for rows in grps]
    _, st = lax.while_loop(cond, body, (jnp.int32(0), st0))
    lo = jnp.concatenate([s[0] for s in st], axis=0)
    hi = jnp.concatenate([s[1] for s in st], axis=0)

    has_run = jnp.max(jnp.where(lo < hi, 1.0, 0.0)) > 0.5
    rows_all = slice(0, R)

    @pl.when(jnp.logical_not(has_run))
    def _():
        for t in tiles:
            bias_s[:, t] = jnp.where(x_s[:, t] >= hi, 0.0, NEG)

    @pl.when(has_run)
    def _():
        need = kf - count_ge(rows_all, hi)
        ai = lax.broadcasted_iota(jnp.int32, (kt_w, kt_w), 0)
        bi = lax.broadcasted_iota(jnp.int32, (kt_w, kt_w), 1)
        before = (ai < bi).astype(bf16)
        seen = jnp.zeros((R, 1), f32)
        for t in tiles:
            x = x_s[:, t]
            run = jnp.logical_and(x >= lo, x < hi)
            runf = run.astype(f32)
            rank = seen + _dot(runf.astype(bf16), before)
            take = jnp.logical_or(x >= hi, jnp.logical_and(run, rank < need))
            bias_s[:, t] = jnp.where(take, 0.0, NEG)
            seen = seen + jnp.sum(runf, axis=1, keepdims=True)


def _dsa_prompt_kernel(iq_ref, sm_ref, ik_ref, q_ref, k_ref, v_ref, o_ref,
                       ikb, kb, vb, x_s, bias_s, s_s, *, topk, kt_w, ext_w):
    qi = pl.program_id(1)
    R = iq_ref.shape[0]
    T = ik_ref.shape[0]

    @pl.when(qi == 0)
    def _():
        ikb[...] = ik_ref[...].astype(bf16)
        kb[...] = k_ref[...].astype(bf16)
        vb[...] = v_ref[...].astype(bf16)

    def block(ext):
        tiles = [slice(j * kt_w, (j + 1) * kt_w) for j in range(ext // kt_w)]
        wgt = sm_ref[...] * IDX_SCALE
        qpos = qi * R + lax.broadcasted_iota(jnp.int32, (R, kt_w), 0)
        key0 = lax.broadcasted_iota(jnp.int32, (R, kt_w), 1)
        mn = mx = None
        for j, t in enumerate(tiles):
            keys = ikb[t, :]
            sc = jnp.zeros((R, kt_w), f32)
            for h in range(IDX_HEADS):
                rel = jnp.maximum(_dot_nt(iq_ref[:, h * IDX_DIM:(h + 1) * IDX_DIM], keys), 0.0)
                sc = sc + rel * wgt[:, SM_IW + h:SM_IW + h + 1]
            causal = key0 + j * kt_w <= qpos
            x_s[:, t] = jnp.where(causal, sc, NEG)
            lo_t = jnp.where(causal, sc, POS)
            hi_t = jnp.where(causal, sc, NEG)
            mn = lo_t if mn is None else jnp.minimum(mn, lo_t)
            mx = hi_t if mx is None else jnp.maximum(mx, hi_t)
        n_causal = qi * R + lax.broadcasted_iota(jnp.int32, (R, 1), 0) + 1
        _topk_bias(x_s, bias_s, jnp.min(mn, axis=1, keepdims=True), jnp.max(mx, axis=1, keepdims=True),
                   n_causal <= topk, len(tiles), kt_w, topk)

        def heads(hp, carry):
            h0 = hp * DSA_HEAD_GROUP
            g = h0 // (DSA_HEADS // DSA_KV_HEADS)
            gc = pl.ds(pl.multiple_of(g * DSA_HEAD_DIM, DSA_HEAD_DIM), DSA_HEAD_DIM)
            hcs = [pl.ds(pl.multiple_of((h0 + i) * DSA_HEAD_DIM, DSA_HEAD_DIM), DSA_HEAD_DIM)
                   for i in range(DSA_HEAD_GROUP)]
            qs = [q_ref[:, hc] for hc in hcs]
            mxa = [None] * DSA_HEAD_GROUP
            for t in tiles:
                kt = kb[t, gc]
                bt = bias_s[:, t]
                for i in range(DSA_HEAD_GROUP):
                    s = _dot_nt(qs[i], kt) * (DSA_HEAD_DIM ** -0.5) + bt
                    s_s[i, :, t] = s
                    mxa[i] = s if mxa[i] is None else jnp.maximum(mxa[i], s)
            ms = [jnp.max(a, axis=1, keepdims=True) for a in mxa]
            la = [jnp.zeros((R, kt_w), f32)] * DSA_HEAD_GROUP
            acc = [jnp.zeros((R, DSA_HEAD_DIM), f32)] * DSA_HEAD_GROUP
            for t in tiles:
                vt = vb[t, gc]
                for i in range(DSA_HEAD_GROUP):
                    p = jnp.exp(s_s[i, :, t] - ms[i])
                    la[i] = la[i] + p
                    acc[i] = acc[i] + _dot(p.astype(bf16), vt)
            for i in range(DSA_HEAD_GROUP):
                o_ref[:, hcs[i]] = (acc[i] / jnp.sum(la[i], axis=1, keepdims=True)).astype(bf16)
            return carry

        lax.fori_loop(0, DSA_HEADS // DSA_HEAD_GROUP, heads, 0)

    for e in range(T // ext_w):
        pl.when(qi // (ext_w // R) == e)(functools.partial(block, (e + 1) * ext_w))


def _dsa_prompt(iq, sm, ik, dq, dk, y, nb, t, topk):
    R = LANES
    nq = t // R
    kt_w = min(DSA_KEY_TILE, t)
    ext_w = min(DSA_KEY_EXTENT, t)
    kern = functools.partial(_dsa_prompt_kernel, topk=topk, kt_w=kt_w, ext_w=ext_w)
    return pl.pallas_call(
        kern,
        grid=(nb, nq),
        in_specs=[pl.BlockSpec((R, IQ_W), lambda b, i: (b * nq + i, 0)),
                  pl.BlockSpec((R, LANES), lambda b, i: (b * nq + i, 0)),
                  pl.BlockSpec((t, IDX_DIM), lambda b, i: (b, 0)),
                  pl.BlockSpec((R, DQ_W), lambda b, i: (b * nq + i, 0)),
                  pl.BlockSpec((t, DKV_W), lambda b, i: (b, 0)),
                  pl.BlockSpec((t, DKV_W), lambda b, i: (b, OFF_DV // DKV_W))],
        out_specs=pl.BlockSpec((R, DQ_W), lambda b, i: (b * nq + i, 0)),
        out_shape=jax.ShapeDtypeStruct((nb * t, DQ_W), bf16),
        scratch_shapes=[pltpu.VMEM((t, IDX_DIM), bf16), pltpu.VMEM((t, DKV_W), bf16),
                        pltpu.VMEM((t, DKV_W), bf16), pltpu.VMEM((R, t), f32), pltpu.VMEM((R, t), f32),
                        pltpu.VMEM((DSA_HEAD_GROUP, R, t), f32)],
        compiler_params=_params(48, 2),
        name="dsa_prompt",
    )(iq, sm, ik, dq, dk, y)


def _idx_score_kernel(pt_ref, iq_ref, sm_ref, ikn_ref, *rest, n_pages, page):
    eb, tp, _ = iq_ref.shape
    o_ref = rest[eb * n_pages]
    ts = o_ref.shape[1]
    past = n_pages * page
    lp = o_ref.shape[2]
    for e in range(eb):
        pages = rest[e * n_pages:(e + 1) * n_pages]
        keys_t = jnp.concatenate([p[...] for p in pages], axis=1).astype(bf16)
        keys_n = jnp.concatenate([ikn_ref[e], jnp.zeros((lp - past - tp, IDX_DIM), f32)], axis=0).astype(bf16)
        wgt = sm_ref[e] * IDX_SCALE
        iq = iq_ref[e].astype(bf16)
        score = jnp.zeros((tp, lp), f32)
        for h in range(IDX_HEADS):
            iqh = iq[:, h * IDX_DIM:(h + 1) * IDX_DIM]
            rel = jnp.maximum(jnp.concatenate([_dot(iqh, keys_t), _dot_nt(iqh, keys_n)], axis=1), 0.0)
            score = score + rel * wgt[:, SM_IW + h:SM_IW + h + 1]
        o_ref[e] = score[:ts]


def _idx_scores(page_table, iq, sm, ikn, cache_idx, lp, ts):
    db, tp, _ = iq.shape
    n_pages = page_table.shape[1]
    page = cache_idx.shape[2]
    eb = IDX_BATCH
    kern = functools.partial(_idx_score_kernel, n_pages=n_pages, page=page)
    page_specs = [pl.BlockSpec((None, IDX_DIM, page),
                               functools.partial(lambda b, pt, e, p: (pt[b * eb + e, p], 0, 0), e=e, p=p))
                  for e in range(eb) for p in range(n_pages)]
    grid_spec = pltpu.PrefetchScalarGridSpec(
        num_scalar_prefetch=1,
        grid=(db // eb,),
        in_specs=[pl.BlockSpec((eb, tp, IQ_W), lambda b, pt: (b, 0, 0)),
                  pl.BlockSpec((eb, tp, LANES), lambda b, pt: (b, 0, 0)),
                  pl.BlockSpec((eb, tp, IDX_DIM), lambda b, pt: (b, 0, 0))] + page_specs,
        out_specs=pl.BlockSpec((eb, ts, lp), lambda b, pt: (b, 0, 0)),
    )
    return pl.pallas_call(
        kern, grid_spec=grid_spec,
        out_shape=jax.ShapeDtypeStruct((db, ts, lp), f32),
        compiler_params=_params(),
        name="idx_scores_sample",
    )(page_table, iq, sm, ikn, *([cache_idx] * (eb * n_pages)))


def _select_sample_kernel(x_ref, bias_ref, x_s, *, topk, past, ts):
    R, lp = x_ref.shape
    key = lax.broadcasted_iota(jnp.int32, (R, lp), 1)
    t = lax.broadcasted_iota(jnp.int32, (R, lp), 0) % ts
    causal = key <= past + t
    x = x_ref[...]
    x_s[...] = jnp.where(causal, x, NEG)
    lo0 = jnp.min(jnp.where(causal, x, POS), axis=1, keepdims=True)
    mx = jnp.max(jnp.where(causal, x, NEG), axis=1, keepdims=True)
    n_causal = past + lax.broadcasted_iota(jnp.int32, (R, 1), 0) % ts + 1
    _topk_bias(x_s, bias_ref, lo0, mx, n_causal <= topk, lp // LANES, LANES, topk)


def _select_sample(scores, topk, past, ts):
    m, lp = scores.shape
    R = LANES
    kern = functools.partial(_select_sample_kernel, topk=topk, past=past, ts=ts)
    return pl.pallas_call(
        kern, grid=(m // R,),
        in_specs=[pl.BlockSpec((R, lp), lambda i: (i, 0))],
        out_specs=pl.BlockSpec((R, lp), lambda i: (i, 0)),
        out_shape=jax.ShapeDtypeStruct((m, lp), f32),
        scratch_shapes=[pltpu.VMEM((R, lp), f32)],
        compiler_params=_params(),
        name="select_sample",
    )(scores)


def _dsa_sample_kernel(pt_ref, q_ref, sel_ref, kn_ref, vn_ref, *rest, n_pages, page):
    eb, tp, _ = q_ref.shape
    o_ref = rest[2 * eb * n_pages]
    ts = sel_ref.shape[1]
    past = n_pages * page
    lp = sel_ref.shape[2]
    hpg = DSA_HEADS // DSA_KV_HEADS
    padn = jnp.zeros((lp - past - tp, DSA_HEAD_DIM), f32)
    for e in range(eb):
        kp = rest[e * n_pages:(e + 1) * n_pages]
        vp = rest[(eb + e) * n_pages:(eb + e + 1) * n_pages]
        bias = jnp.concatenate([sel_ref[e], jnp.zeros((tp - ts, lp), f32)], axis=0)
        bias = jnp.concatenate([bias] * hpg, axis=0)
        for g in range(DSA_KV_HEADS):
            gsl = slice(g * DSA_HEAD_DIM, (g + 1) * DSA_HEAD_DIM)
            kg = jnp.concatenate([p[pl.ds(g, page, stride=DSA_KV_HEADS), :] for p in kp]
                                 + [kn_ref[e, :, gsl], padn], axis=0).astype(bf16)
            vg = jnp.concatenate([p[pl.ds(g, page, stride=DSA_KV_HEADS), :] for p in vp]
                                 + [vn_ref[e, :, gsl], padn], axis=0).astype(bf16)
            qg = jnp.concatenate([q_ref[e, :, (g * hpg + j) * DSA_HEAD_DIM:(g * hpg + j + 1) * DSA_HEAD_DIM]
                                  for j in range(hpg)], axis=0).astype(bf16)
            s = _dot_nt(qg, kg) * (DSA_HEAD_DIM ** -0.5) + bias
            m = jnp.max(s, axis=1, keepdims=True)
            p = jnp.exp(s - m)
            l = jnp.sum(p, axis=1, keepdims=True)
            o = _dot(p.astype(bf16), vg) / l
            for j in range(hpg):
                hsl = slice((g * hpg + j) * DSA_HEAD_DIM, (g * hpg + j + 1) * DSA_HEAD_DIM)
                o_ref[e, :, hsl] = o[j * tp:j * tp + ts]


def _dsa_sample(page_table, dq, sel, kn, vn, ck, cv):
    db, tp, _ = dq.shape
    ts = sel.shape[1]
    lp = sel.shape[2]
    n_pages = page_table.shape[1]
    rows = ck.shape[1]
    page = rows // DSA_KV_HEADS
    eb = DSA_SAMPLE_BATCH
    kern = functools.partial(_dsa_sample_kernel, n_pages=n_pages, page=page)
    pspec = [pl.BlockSpec((None, rows, DSA_HEAD_DIM),
                          functools.partial(lambda b, pt, e, p: (pt[b * eb + e, p], 0, 0), e=e, p=p))
             for e in range(eb) for p in range(n_pages)]
    per_b = lambda r, w: pl.BlockSpec((eb, r, w), lambda b, pt: (b, 0, 0))
    grid_spec = pltpu.PrefetchScalarGridSpec(
        num_scalar_prefetch=1,
        grid=(db // eb,),
        in_specs=[per_b(tp, DQ_W), per_b(ts, lp), per_b(tp, DKV_W), per_b(tp, DKV_W)] + pspec + pspec,
        out_specs=per_b(ts, DQ_W),
    )
    return pl.pallas_call(
        kern, grid_spec=grid_spec,
        out_shape=jax.ShapeDtypeStruct((db, ts, DQ_W), f32),
        compiler_params=_params(48),
        name="dsa_sample",
    )(page_table, dq, sel, kn, vn, *([ck] * (eb * n_pages)), *([cv] * (eb * n_pages)))


def _merge_kernel(og_ref, od_ref, gg_ref, gd_ref, x_ref, gt_ref, sc_ref, sh_ref, g2_ref,
                  wg_ref, wd_ref, wo_ref, x1_ref, h2_ref):
    mix = (_sigmoid(gg_ref[...]) * _dot(og_ref[...], wg_ref[...])
           + _sigmoid(gd_ref[...]) * _dot(od_ref[...], wd_ref[...]))
    x1 = x_ref[...] + gt_ref[...] * _dot(mix.astype(bf16), wo_ref[...])
    x1_ref[...] = x1
    y = x1 * lax.rsqrt(jnp.mean(x1 * x1, axis=-1, keepdims=True) + NORM_EPS)
    h2_ref[...] = ((y * g2_ref[...]) * (1.0 + sc_ref[...]) + sh_ref[...]).astype(bf16)


def _merge(o_gdn, o_dsa, y, x, ada, g2, wg, wd, wo, group, rows_per_batch, tm):
    m, d = x.shape
    row = lambda w, blk=0: pl.BlockSpec((tm, w), lambda i: (i, blk))
    res = lambda a: pl.BlockSpec(a.shape, lambda i: (0, 0), pipeline_mode=pl.Buffered(1))
    return pl.pallas_call(
        _merge_kernel,
        grid=(m // tm,),
        in_specs=[row(Z_W), row(DQ_W), row(d, 0), row(d, 1), row(d),
                  _mod_spec(group, d, rows_per_batch, tm, 2),
                  _mod_spec(group, d, rows_per_batch, tm, 4),
                  _mod_spec(group, d, rows_per_batch, tm, 3),
                  pl.BlockSpec((1, d), lambda i: (0, 0)),
                  res(wg), res(wd), res(wo)],
        out_specs=[row(d), row(d)],
        out_shape=[jax.ShapeDtypeStruct((m, d), f32), jax.ShapeDtypeStruct((m, d), bf16)],
        compiler_params=_params(48),
        name="merge_" + group,
    )(o_gdn, o_dsa, y, y, x, ada, ada, ada, g2.reshape(1, d), wg, wd, wo)


def _ffn_epilogue(acc_ref, x1_ref, gt_ref, gf_ref, y_ref):
    tm = acc_ref.shape[0]
    r = gt_ref.shape[0] if gt_ref.shape[0] > 1 else tm
    for s0 in range(0, tm, r):
        sl = slice(s0, s0 + r)
        x2 = x1_ref[sl, :] + gt_ref[...] * acc_ref[sl, :]
        y_ref[sl, :] = (x2 * lax.rsqrt(jnp.mean(x2 * x2, axis=-1, keepdims=True) + NORM_EPS)) * gf_ref[...]


def _ffn_prompt_kernel(h_ref, wug_ref, wuv_ref, wcg_ref, wcv_ref, bg_ref, bv_ref, wd_ref,
                       x1_ref, gt_ref, gf_ref, y_ref, ug_ref, uv_ref, acc_ref, act_s, prev_s,
                       *, blocks_per_seq, n_up):
    i = pl.program_id(0)
    j = pl.program_id(1)
    tm = h_ref.shape[0]
    keep = (i % blocks_per_seq) > 0
    sr = min(FFN_SUB_ROWS, tm)

    @pl.when(jnp.logical_and(i == 0, j == 0))
    def _():
        prev_s[...] = jnp.zeros_like(prev_s)

    def conv(u, prev, wc_ref, b_ref):
        ux = jnp.concatenate([prev, u], axis=0)
        w = wc_ref[...]
        y = u * w[FFN_CONV - 1:FFN_CONV] + b_ref[...]
        for t in range(FFN_CONV - 1):
            sft = FFN_CONV - 1 - t
            y = y + ux[SUBLANES - sft:SUBLANES - sft + sr] * w[t:t + 1]
        return y

    def up(store):
        pg = jnp.where(keep, prev_s[j, 0], 0.0)
        pv = jnp.where(keep, prev_s[j, 1], 0.0)
        for s in range(tm // sr):
            rows = slice(s * sr, (s + 1) * sr)
            hs = h_ref[rows, :]
            ug = _dot(hs, wug_ref[...])
            uv = _dot(hs, wuv_ref[...])
            store(rows, _silu(conv(ug, pg, wcg_ref, bg_ref)) * conv(uv, pv, wcv_ref, bv_ref))
            pg = ug[sr - SUBLANES:]
            pv = uv[sr - SUBLANES:]
        ug_ref[...] = pg
        uv_ref[...] = pv
        prev_s[j, 0] = pg
        prev_s[j, 1] = pv

    _ffn_phases(j, n_up, up, act_s, wd_ref, acc_ref, x1_ref, gt_ref, gf_ref, y_ref)


def _ffn_phases(j, n_up, up, act_s, wd_ref, acc_ref, x1_ref, gt_ref, gf_ref, y_ref):
    tf = act_s.shape[1] // n_up
    tn = wd_ref.shape[1]

    @pl.when(j < n_up)
    def _():
        cols = pl.ds(pl.multiple_of(j * tf, tf), tf)

        def store(rows, act):
            act_s[rows, cols] = act.astype(bf16)
        up(store)

    @pl.when(j >= n_up)
    def _():
        acc_ref[:, pl.ds(pl.multiple_of((j - n_up) * tn, tn), tn)] = _dot(act_s[...], wd_ref[...])

    @pl.when(j == pl.num_programs(1) - 1)
    def _():
        _ffn_epilogue(acc_ref, x1_ref, gt_ref, gf_ref, y_ref)


def _ffn_sample_kernel(h_ref, bufg_ref, bufv_ref, wug_ref, wuv_ref, wcg_ref, wcv_ref, bg_ref, bv_ref, wd_ref,
                       x1_ref, gt_ref, gf_ref, y_ref, ug_ref, uv_ref, acc_ref, act_s, *, ts, n_up):
    j = pl.program_id(1)
    db = h_ref.shape[0] // ts
    nbuf = FFN_CONV - 1

    def branch(wu_ref, wc_ref, b_ref, buf_ref, ubuf_ref):
        u = _dot(h_ref[...], wu_ref[...])
        rows = [buf_ref[:, r, :] for r in range(nbuf)]
        rows += [u[t * db:(t + 1) * db] for t in range(ts)]
        for r in range(nbuf):
            ubuf_ref[:, r, :] = rows[ts + r]
        w = wc_ref[...]
        outs = []
        for t in range(ts):
            y = rows[t] * w[0:1] + b_ref[...]
            for r in range(1, FFN_CONV):
                y = y + rows[t + r] * w[r:r + 1]
            outs.append(y)
        return jnp.concatenate(outs, axis=0)

    def up(store):
        store(slice(0, ts * db),
              _silu(branch(wug_ref, wcg_ref, bg_ref, bufg_ref, ug_ref))
              * branch(wuv_ref, wcv_ref, bv_ref, bufv_ref, uv_ref))

    _ffn_phases(j, n_up, up, act_s, wd_ref, acc_ref, x1_ref, gt_ref, gf_ref, y_ref)


def _ffn(h2, x1, ada, g_final, w_up, w_conv, b_conv, w_down, group, rows_per_batch, tm, tf, buf=None):
    m, d = h2.shape
    dff = w_down.shape[0]
    nj = dff // tf
    tn = min(FFN_DOWN_TILE, d)
    b2 = b_conv.reshape(1, 2 * dff)
    up = lambda j: jnp.minimum(j, nj - 1)
    common_w = [pl.BlockSpec((d, tf), lambda i, j: (0, up(j))),
                pl.BlockSpec((d, tf), lambda i, j: (0, nj + up(j))),
                pl.BlockSpec((FFN_CONV, tf), lambda i, j: (0, up(j))),
                pl.BlockSpec((FFN_CONV, tf), lambda i, j: (0, nj + up(j))),
                pl.BlockSpec((1, tf), lambda i, j: (0, up(j))),
                pl.BlockSpec((1, tf), lambda i, j: (0, nj + up(j))),
                pl.BlockSpec((dff, tn), lambda i, j: (0, jnp.maximum(j - nj, 0)))]
    tail = [pl.BlockSpec((tm, d), lambda i, j: (i, 0)),
            _mod_spec(group, d, rows_per_batch, tm, 5),
            pl.BlockSpec((1, d), lambda i, j: (0, 0))]
    y_spec = pl.BlockSpec((tm, d), lambda i, j: (i, 0))
    if group == "prompt":
        bps = rows_per_batch // tm
        nb = m // rows_per_batch
        kern = functools.partial(_ffn_prompt_kernel, blocks_per_seq=bps, n_up=nj)
        first = [pl.BlockSpec((tm, d), lambda i, j: (i, 0))]
        ubuf_spec = pl.BlockSpec((None, SUBLANES, tf), lambda i, j: (i, 0, up(j)))
        ubuf_shape = jax.ShapeDtypeStruct((m // tm, SUBLANES, dff), f32)
        args = (h2,)
        extra_scratch = [pltpu.VMEM((nj, 2, SUBLANES, tf), f32)]
    else:
        extra_scratch = []
        ts = m // rows_per_batch
        kern = functools.partial(_ffn_sample_kernel, ts=ts, n_up=nj)
        first = [pl.BlockSpec((tm, d), lambda i, j: (i, 0)),
                 pl.BlockSpec((rows_per_batch, FFN_CONV - 1, tf), lambda i, j: (0, 0, up(j))),
                 pl.BlockSpec((rows_per_batch, FFN_CONV - 1, tf), lambda i, j: (0, 0, nj + up(j)))]
        ubuf_spec = pl.BlockSpec((rows_per_batch, FFN_CONV - 1, tf), lambda i, j: (0, 0, up(j)))
        ubuf_shape = jax.ShapeDtypeStruct((rows_per_batch, FFN_CONV - 1, dff), f32)
        args = (h2, buf, buf)
    return pl.pallas_call(
        kern,
        grid=(m // tm, nj + d // tn),
        in_specs=first + common_w + tail,
        out_specs=[y_spec, ubuf_spec, ubuf_spec],
        out_shape=[jax.ShapeDtypeStruct((m, d), f32), ubuf_shape, ubuf_shape],
        scratch_shapes=[pltpu.VMEM((tm, d), f32), pltpu.VMEM((tm, dff), bf16)] + extra_scratch,
        compiler_params=_params(56, 2),
        name="ffn_" + group,
    )(*args, w_up, w_up, w_conv, w_conv, b2, b2, w_down, x1, ada, g_final.reshape(1, d))


def _split_w_in(w_in, d):
    sizes = (QKV_W, Z_W, GDN_HEADS, GDN_HEADS, DQ_W, DKV_W, DKV_W, IQ_W, IDX_DIM, IDX_HEADS, d, d)
    offs = [0]
    for s in sizes:
        offs.append(offs[-1] + s)
    seg = lambda i, j: w_in[:, offs[i]:offs[j]]
    pad = jnp.zeros((w_in.shape[0], LANES - (IDX_DIM + 2 * GDN_HEADS + IDX_HEADS)), w_in.dtype)
    small = jnp.concatenate([seg(8, 9), seg(2, 4), seg(9, 10), pad], axis=1)
    return tuple(w.astype(bf16) for w in (seg(0, 2), seg(4, 8), seg(10, 12), small))


def _in_proj(h, w_parts, tm, name):
    return tuple(_matmul(h, w, tm, _mm_tile(w.shape[1]), name + "_" + part)
                 for w, part in zip(w_parts, ("gdn", "dsa", "gate", "small")))


def _ffn_tile(dff):
    for tf in (512, 256, 128):
        if dff % tf == 0:
            return tf
    raise ValueError("d_ff must be a multiple of 128")


def _mm_tile(n):
    for tn in (1024, 512, 256, 128):
        if n % tn == 0:
            return tn
    raise ValueError("projection width must be a multiple of 128")


def kernel(x_prompt, x_sample, c_prompt, c_sample, cache_k, cache_v, cache_idx_k, page_table, state_gdn, state_gdn_conv, state_ffn_conv, w_ada, b_ada, g_norm1, w_in, w_gdn_conv, a_log, dt_bias, g_gdn_norm, w_gdn_out, w_dsa_out, w_o, g_norm2, w_up, w_ffn_conv, b_ffn_conv, w_down, g_final):
    nb, t, d = x_prompt.shape
    db, ts, _ = x_sample.shape
    depth = w_ada.shape[0]
    assert depth == 1 and db == LANES and ts >= GDN_CONV - 1
    n_pages = page_table.shape[1]
    page = cache_k.shape[2]
    past = n_pages * page
    dff = w_down.shape[1]
    l = 0

    n_c = nb + db
    pad_c = (-n_c) % SUBLANES
    c_all = jnp.concatenate([c_prompt, c_sample, jnp.zeros((pad_c, d), f32)], axis=0)
    ada = _ada(c_all, w_ada[l], b_ada[l])
    ada_p = ada[:nb].reshape(nb, 1, 6 * d)
    ada_s = ada[nb:nb + db]

    w_in_parts = _split_w_in(w_in[l], d)
    wg = w_gdn_out[l].astype(bf16)
    wd = w_dsa_out[l].astype(bf16)
    wo = w_o[l].astype(bf16)
    wup = w_up[l].astype(bf16)
    wdn = w_down[l].astype(bf16)
    tf = _ffn_tile(dff)

    xp = x_prompt.reshape(nb * t, d)
    tm_p = min(512, t)
    h1 = _prep(xp, g_norm1[l], ada_p, "prompt", t, tm_p)
    yp_gdn, yp_dsa, yp_gate, yp_small = _in_proj(h1, w_in_parts, min(IN_PROJ_ROWS, nb * t), "in_proj_prompt")
    og_p, s_p = _gdn_prompt(yp_gdn, yp_small, w_gdn_conv[l], a_log[l], dt_bias[l], g_gdn_norm[l], nb, t)
    dq_p, dk_p, iq_p, ik_p, sm_p = _rope(yp_dsa, yp_small, jnp.arange(t), tm_p, "prompt")
    od_p = _dsa_prompt(iq_p, sm_p, ik_p, dq_p, dk_p, yp_dsa, nb, t, min(DSA_TOPK, t // 4))
    x1_p, h2_p = _merge(og_p, od_p, yp_gate, xp, ada_p, g_norm2[l], wg, wd, wo, "prompt", t, min(256, t))
    y_p, ug_p, uv_p = _ffn(h2_p, x1_p, ada_p, g_final, wup, w_ffn_conv[l], b_ffn_conv[l], wdn,
                           "prompt", t, tm_p, tf)

    nfb = FFN_CONV - 1
    bps_p = t // tm_p
    out_p = (
        y_p.reshape(nb, t, d),
        dk_p.reshape(1, nb, t, DSA_KV_HEADS, DSA_HEAD_DIM),
        yp_dsa[:, OFF_DV:OFF_DV + DKV_W].reshape(1, nb, t, DSA_KV_HEADS, DSA_HEAD_DIM),
        ik_p.reshape(1, nb, t, IDX_DIM),
        s_p[None],
        yp_gdn.reshape(nb, t, -1)[:, t - (GDN_CONV - 1):, :QKV_W][None],
        jnp.concatenate([ug_p[bps_p - 1::bps_p, SUBLANES - nfb:], uv_p[bps_p - 1::bps_p, SUBLANES - nfb:]],
                        axis=-1)[None],
    )

    xs = x_sample.transpose(1, 0, 2).reshape(ts * db, d)
    h1s = _prep(xs, g_norm1[l], ada_s, "sample", db, db)
    ys_gdn, ys_dsa, ys_gate, ys_small = _in_proj(h1s, w_in_parts, ts * db, "in_proj_sample")
    ys_gdn3 = ys_gdn.reshape(ts, db, -1)
    og_s, s_s = _gdn_sample(ys_gdn3, ys_small.reshape(ts, db, LANES), state_gdn_conv[l].transpose(1, 0, 2),
                            state_gdn[l], w_gdn_conv[l], a_log[l], dt_bias[l], g_gdn_norm[l])
    pos_s = jnp.repeat(past + jnp.arange(ts), db)
    dq_s, dk_s, iq_s, ik_s, sm_s = _rope(ys_dsa, ys_small, pos_s, db, "sample", db)
    ik_s_b = ik_s.reshape(ts, db, IDX_DIM).transpose(1, 0, 2)
    lp = past + LANES
    tp = -(-ts // SUBLANES) * SUBLANES
    pad_t = lambda a: jnp.pad(a.astype(f32), ((0, 0), (0, tp - ts), (0, 0)))
    scores = _idx_scores(page_table, pad_t(iq_s.reshape(db, ts, IQ_W)), pad_t(sm_s.reshape(db, ts, LANES)),
                         pad_t(ik_s_b), jnp.swapaxes(cache_idx_k[l], 1, 2), lp, ts)
    sel = _select_sample(scores.reshape(db * ts, lp), min(DSA_TOPK, (past + ts) // 4), past, ts)
    dv_s_b = ys_dsa.reshape(ts, db, -1)[:, :, OFF_DV:OFF_DV + DKV_W].transpose(1, 0, 2)
    ck = cache_k[l].reshape(cache_k.shape[1], page * DSA_KV_HEADS, DSA_HEAD_DIM)
    cv = cache_v[l].reshape(cache_v.shape[1], page * DSA_KV_HEADS, DSA_HEAD_DIM)
    od_s = _dsa_sample(page_table, pad_t(dq_s.reshape(db, ts, DQ_W)), sel.reshape(db, ts, lp),
                       pad_t(dk_s.reshape(db, ts, DKV_W)), pad_t(dv_s_b), ck, cv)
    od_s = od_s.astype(bf16).transpose(1, 0, 2).reshape(ts * db, DQ_W)
    x1_s, h2_s = _merge(og_s.reshape(ts * db, Z_W), od_s, ys_gate, xs, ada_s, g_norm2[l], wg, wd, wo,
                        "sample", db, db)
    y_s, ug_s, uv_s = _ffn(h2_s, x1_s, ada_s, g_final, wup, w_ffn_conv[l], b_ffn_conv[l], wdn,
                           "sample", db, ts * db, tf, buf=state_ffn_conv[l])

    out_s = (
        y_s.reshape(ts, db, d).transpose(1, 0, 2),
        dk_s.reshape(1, db, ts, DSA_KV_HEADS, DSA_HEAD_DIM),
        dv_s_b.reshape(1, db, ts, DSA_KV_HEADS, DSA_HEAD_DIM),
        ik_s_b[None],
        s_s[None],
        ys_gdn3[ts - (GDN_CONV - 1):, :, :QKV_W].transpose(1, 0, 2)[None],
        jnp.concatenate([ug_s, uv_s], axis=-1)[None],
    )
    return (out_p[0], out_s[0]) + out_p[1:] + out_s[1:]
```

```python
import functools

import jax
import jax.numpy as jnp
from jax import lax
from jax.experimental import pallas as pl
from jax.experimental.pallas import tpu as pltpu

f32 = jnp.float32
bf16 = jnp.bfloat16

GDN_HEADS = 8
GDN_DK = 128
GDN_DV = 128
GDN_CONV = 4
GDN_CHUNK = 64
GDN_GROUP = 4
GDN_SEQS = 4
DSA_HEADS = 8
DSA_KV_HEADS = 2
DSA_HEAD_DIM = 128
IDX_HEADS = 8
IDX_DIM = 64
IDX_SCALE = IDX_HEADS ** -0.5 * IDX_DIM ** -0.5
DSA_TOPK = 256
ROPE_THETA = 500000.0
ROPE_FRACTION = 4
FFN_CONV = 3
NORM_EPS = 1e-6

LANES = 128
SUBLANES = 8
NEG = float(jnp.finfo(jnp.float32).min)
POS = float(jnp.finfo(jnp.float32).max)
BISECT_UNROLL = 4
SNAP_FROM = 4
BISECT_CAP = 1024
DSA_KEY_TILE = 256
DSA_KEY_EXTENT = 256
IN_PROJ_ROWS = 1024
IDX_BATCH = 8
DSA_SAMPLE_BATCH = 4
DSA_HEAD_GROUP = 4
FFN_DOWN_TILE = 512
FFN_SUB_ROWS = 256

QKV_W =2 * GDN_HEADS * GDN_DK + GDN_HEADS * GDN_DV
Z_W = GDN_HEADS * GDN_DV
DQ_W = DSA_HEADS * DSA_HEAD_DIM
DKV_W = DSA_KV_HEADS * DSA_HEAD_DIM
IQ_W = IDX_HEADS * IDX_DIM
OFF_Z = QKV_W
GDN_PROJ_W = QKV_W + Z_W
OFF_DQ = 0
OFF_DK = OFF_DQ + DQ_W
OFF_DV = OFF_DK + DKV_W
OFF_IQ = OFF_DV + DKV_W
DSA_PROJ_W = OFF_IQ + IQ_W
SM_IK = 0
SM_A = IDX_DIM
SM_B = SM_A + GDN_HEADS
SM_IW = SM_B + GDN_HEADS


def _sigmoid(x):
    return 1.0 / (1.0 + jnp.exp(-x))


def _silu(x):
    return x * _sigmoid(x)


def _dot(a, b):
    return jnp.dot(a, b, preferred_element_type=f32)


def _dot_nt(a, b):
    return lax.dot_general(a, b, (((1,), (1,)), ((), ())), preferred_element_type=f32)


def _dot_tn(a, b):
    return lax.dot_general(a, b, (((0,), (0,)), ((), ())), preferred_element_type=f32)


def _split3(a):
    hi = a.astype(bf16)
    lo = (a - hi.astype(f32)).astype(bf16)
    return hi, lo


def _mm3(a, b):
    ah, al = a
    bh, bl = b
    return _dot(ah, bh) + (_dot(ah, bl) + _dot(al, bh))


def _dot_exact(a, b):
    return jnp.dot(a, b, preferred_element_type=f32, precision=lax.Precision.HIGHEST)


def _params(vmem_mb=None, n_axes=1):
    kw = dict(dimension_semantics=("arbitrary",) * n_axes)
    if vmem_mb is not None:
        kw["vmem_limit_bytes"] = vmem_mb * 1024 * 1024
    return pltpu.CompilerParams(**kw)


def _ada_kernel(c_ref, w_ref, b_ref, o_ref):
    s = _silu(c_ref[...]).astype(bf16)
    o_ref[...] = _dot(s, w_ref[...].astype(bf16)) + b_ref[...]


def _ada(c_all, w_ada, b_ada):
    m, d = c_all.shape
    n = w_ada.shape[1]
    tn = 1024
    return pl.pallas_call(
        _ada_kernel,
        grid=(n // tn,),
        in_specs=[pl.BlockSpec((m, d), lambda j: (0, 0)),
                  pl.BlockSpec((d, tn), lambda j: (0, j)),
                  pl.BlockSpec((1, tn), lambda j: (0, j))],
        out_specs=pl.BlockSpec((m, tn), lambda j: (0, j)),
        out_shape=jax.ShapeDtypeStruct((m, n), f32),
        compiler_params=_params(48),
        name="ada",
    )(c_all, w_ada, b_ada.reshape(1, n))


def _mod_spec(group, d, rows_per_batch, tm, col):
    if group == "prompt":
        return pl.BlockSpec((None, 1, d), lambda i, *_: ((i * tm) // rows_per_batch, 0, col))
    return pl.BlockSpec((rows_per_batch, d), lambda i, *_: (0, col))


def _prep_kernel(x_ref, g_ref, sc_ref, sh_ref, o_ref):
    x = x_ref[...]
    y = x * lax.rsqrt(jnp.mean(x * x, axis=-1, keepdims=True) + NORM_EPS)
    o_ref[...] = ((y * g_ref[...]) * (1.0 + sc_ref[...]) + sh_ref[...]).astype(bf16)


def _prep(x, g, ada, group, rows_per_batch, tm):
    m, d = x.shape
    return pl.pallas_call(
        _prep_kernel,
        grid=(m // tm,),
        in_specs=[pl.BlockSpec((tm, d), lambda i: (i, 0)),
                  pl.BlockSpec((1, d), lambda i: (0, 0)),
                  _mod_spec(group, d, rows_per_batch, tm, 1),
                  _mod_spec(group, d, rows_per_batch, tm, 0)],
        out_specs=pl.BlockSpec((tm, d), lambda i: (i, 0)),
        out_shape=jax.ShapeDtypeStruct((m, d), bf16),
        compiler_params=_params(),
        name="prep_" + group,
    )(x, g.reshape(1, d), ada, ada)


def _mm_kernel(a_ref, w_ref, o_ref):
    o_ref[...] = _dot(a_ref[...], w_ref[...])


def _matmul(a, w, tm, tn, name):
    m, k = a.shape
    n = w.shape[1]
    return pl.pallas_call(
        _mm_kernel,
        grid=(n // tn, m // tm),
        in_specs=[pl.BlockSpec((tm, k), lambda j, i: (i, 0)),
                  pl.BlockSpec((k, tn), lambda j, i: (0, j))],
        out_specs=pl.BlockSpec((tm, tn), lambda j, i: (i, j)),
        out_shape=jax.ShapeDtypeStruct((m, n), f32),
        compiler_params=_params(48, 2),
        name=name,
    )(a, w)


def _l2n(x):
    return x * lax.rsqrt(jnp.sum(x * x, axis=-1, keepdims=True) + NORM_EPS)


def _gdn_gates(sm, alog, dtb):
    xa = sm + dtb
    softplus = jnp.maximum(xa, 0.0) + jnp.log1p(jnp.exp(-jnp.abs(xa)))
    return -jnp.exp(alog) * softplus, _sigmoid(sm)


def _gated_norm(o, gn, z):
    y = o * lax.rsqrt(jnp.mean(o * o, axis=-1, keepdims=True) + NORM_EPS)
    return (y * gn) * _silu(z)


def _gdn_prompt_kernel(qkv_ref, halo_ref, z_ref, sm_ref, wc_ref, alog_ref, dtb_ref, gn_ref,
                       o_ref, sfin_ref, s_ref):
    c = pl.program_id(1)
    n_seq, C, _ = qkv_ref.shape

    @pl.when(c == 0)
    def _():
        s_ref[...] = jnp.zeros_like(s_ref)

    keep = (c > 0).astype(f32)
    ri = lax.broadcasted_iota(jnp.int32, (C, C), 0)
    ci = lax.broadcasted_iota(jnp.int32, (C, C), 1)
    tril = (ri >= ci).astype(f32)
    gates = [_gdn_gates(sm_ref[s], alog_ref[...], dtb_ref[...]) for s in range(n_seq)]
    beta_all = [g[1] for g in gates]
    gc_all = [_dot_exact(tril, g[0]) for g in gates]
    gc_t = [g.T for g in gc_all]

    def conv(s, col):
        xs = jnp.concatenate([halo_ref[s, :, col:col + LANES] * keep, qkv_ref[s, :, col:col + LANES]], axis=0)
        w = wc_ref[:, col:col + LANES]
        y = xs[SUBLANES:] * w[GDN_CONV - 1:GDN_CONV]
        for i in range(GDN_CONV - 1):
            sft = GDN_CONV - 1 - i
            y = y + xs[SUBLANES - sft:SUBLANES - sft + C] * w[i:i + 1]
        return _silu(y)

    N = GDN_GROUP * C
    rn = lax.broadcasted_iota(jnp.int32, (N, N), 0)
    cn = lax.broadcasted_iota(jnp.int32, (N, N), 1)
    same = (rn // C) == (cn // C)
    incl = jnp.logical_and(same, rn >= cn)
    strict = jnp.logical_and(same, rn > cn)
    eye_f = (rn == cn).astype(f32)
    n_sq = max(1, (C - 1).bit_length() - 1)
    groups = [(s, list(range(g0, g0 + GDN_GROUP))) for s in range(n_seq)
              for g0 in range(0, GDN_HEADS, GDN_GROUP)]
    stack = lambda xs: jnp.concatenate(xs, axis=0)

    qs = [stack([_l2n(conv(s, h * GDN_DK)) * (GDN_DK ** -0.5) for h in hs]) for s, hs in groups]
    ks = [stack([_l2n(conv(s, GDN_HEADS * GDN_DK + h * GDN_DK)) for h in hs]) for s, hs in groups]
    vs = [stack([conv(s, 2 * GDN_HEADS * GDN_DK + h * GDN_DV) for h in hs]) for s, hs in groups]
    gcs = [stack([gc_all[s][:, SM_A + h:SM_A + h + 1] for h in hs]) for s, hs in groups]
    betas = [stack([beta_all[s][:, SM_B + h:SM_B + h + 1] for h in hs]) for s, hs in groups]
    gc_rows = [jnp.concatenate([gc_t[s][SM_A + h:SM_A + h + 1, :] for h in hs], axis=1)
               for s, hs in groups]
    egs = [jnp.exp(gc) for gc in gcs]
    decays = [jnp.where(incl, jnp.exp(jnp.where(incl, gc - gr, 0.0)), 0.0) for gc, gr in zip(gcs, gc_rows)]
    kbs = [k.astype(bf16) for k in ks]
    a_s = [jnp.where(strict, b * _dot_nt(kb, kb) * dc, 0.0) for b, kb, dc in zip(betas, kbs, decays)]
    qks = [(_dot_nt(q.astype(bf16), kb) * dc).astype(bf16) for q, kb, dc in zip(qs, kbs, decays)]

    invs = [eye_f - a for a in a_s]
    pws = [a.astype(bf16) for a in a_s]
    for _ in range(n_sq):
        pws = [_dot(p, p).astype(bf16) for p in pws]
        invs = [inv + _dot(inv.astype(bf16), p) for inv, p in zip(invs, pws)]
    inv_s = [_split3(inv) for inv in invs]
    res = [eye_f - inv - _mm3(_split3(a), sp) for a, inv, sp in zip(a_s, invs, inv_s)]
    invs = [inv + _dot(sp[0], r.astype(bf16)) for inv, sp, r in zip(invs, inv_s, res)]
    rhs = [jnp.concatenate([b * v, (b * eg) * k], axis=1) for b, v, eg, k in zip(betas, vs, egs, ks)]
    sols = [_mm3(_split3(inv), _split3(r)) for inv, r in zip(invs, rhs)]

    for gi, (s, hs) in enumerate(groups):
        sol, q, k, gc, eg = sols[gi], qs[gi], ks[gi], gcs[gi], egs[gi]
        q_dec = (q * eg).astype(bf16)
        w_k = sol[:, GDN_DV:].astype(bf16)
        rows = [slice(j * C, (j + 1) * C) for j in range(GDN_GROUP)]
        s_old = [s_ref[s, h] for h in hs]
        ws = [_dot(jnp.concatenate([w_k[r], q_dec[r]], axis=0), so.astype(bf16)) for r, so in zip(rows, s_old)]
        u = stack([sol[r, :GDN_DV] - w[:C] for r, w in zip(rows, ws)])
        ub = u.astype(bf16)
        o_intra = _dot(qks[gi], ub)
        for j, h in enumerate(hs):
            r = rows[j]
            gl = gc[r][C - 1:C, :]
            k_end = (k[r] * jnp.exp(gl - gc[r])).astype(bf16)
            s_ref[s, h] = s_old[j] * jnp.exp(gl) + _dot_tn(k_end, ub[r])
            o = ws[j][C:] + o_intra[r]
            zh = z_ref[s, :, h * GDN_DV:(h + 1) * GDN_DV]
            o_ref[s, :, h * GDN_DV:(h + 1) * GDN_DV] = _gated_norm(o, gn_ref[...], zh).astype(bf16)

    @pl.when(c == pl.num_programs(1) - 1)
    def _():
        sfin_ref[...] = s_ref[...]


def _gdn_vecs(a_log, dt_bias, g_gdn_norm):
    alog = jnp.zeros((1, LANES), f32).at[0, SM_A:SM_A + GDN_HEADS].set(a_log)
    dtb = jnp.zeros((1, LANES), f32).at[0, SM_A:SM_A + GDN_HEADS].set(dt_bias)
    return alog, dtb, g_gdn_norm.reshape(1, GDN_DV)


def _gdn_prompt(y, y_small, w_conv, a_log, dt_bias, g_gdn_norm, nb, t):
    C = GDN_CHUNK
    nc = t // C
    ns = GDN_SEQS if nb % GDN_SEQS == 0 else 1
    alog, dtb, gn = _gdn_vecs(a_log, dt_bias, g_gdn_norm)
    cst = lambda b, c: (0, 0)
    y3 = y.reshape(nb, t, -1)
    ysm3 = y_small.reshape(nb, t, LANES)
    o, s_fin = pl.pallas_call(
        _gdn_prompt_kernel,
        grid=(nb // ns, nc),
        in_specs=[pl.BlockSpec((ns, C, QKV_W), lambda b, c: (b, c, 0)),
                  pl.BlockSpec((ns, SUBLANES, QKV_W),
                               lambda b, c: (b, jnp.maximum(c * (C // SUBLANES) - 1, 0), 0)),
                  pl.BlockSpec((ns, C, Z_W), lambda b, c: (b, c, OFF_Z // Z_W)),
                  pl.BlockSpec((ns, C, LANES), lambda b, c: (b, c, 0)),
                  pl.BlockSpec((GDN_CONV, QKV_W), cst),
                  pl.BlockSpec((1, LANES), cst),
                  pl.BlockSpec((1, LANES), cst),
                  pl.BlockSpec((1, GDN_DV), cst)],
        out_specs=[pl.BlockSpec((ns, C, Z_W), lambda b, c: (b, c, 0)),
                   pl.BlockSpec((ns, GDN_HEADS, GDN_DK, GDN_DV), lambda b, c: (b, 0, 0, 0))],
        out_shape=[jax.ShapeDtypeStruct((nb, t, Z_W), bf16),
                   jax.ShapeDtypeStruct((nb, GDN_HEADS, GDN_DK, GDN_DV), f32)],
        scratch_shapes=[pltpu.VMEM((ns, GDN_HEADS, GDN_DK, GDN_DV), f32)],
        compiler_params=_params(None, 2),
        name="gdn_prompt",
    )(y3, y3, y3, ysm3, w_conv, alog, dtb, gn)
    return o.reshape(nb * t, Z_W), s_fin


def _gdn_sample_kernel(qkv_ref, buf_ref, z_ref, sm_ref, sin_ref, wc_ref, alog_ref, dtb_ref, gn_ref,
                       o_ref, sout_ref, q_s, k_s, v_s, a_s, b_s, o_s):
    ts, G, _ = qkv_ref.shape
    nbuf = GDN_CONV - 1

    for h in range(GDN_HEADS):
        for part, dst in ((0, q_s), (1, k_s), (2, v_s)):
            col = part * GDN_HEADS * GDN_DK + h * GDN_DK
            w = wc_ref[:, col:col + LANES]
            rows = [buf_ref[i, :, col:col + LANES] for i in range(nbuf)]
            rows += [qkv_ref[t, :, col:col + LANES] for t in range(ts)]
            for t in range(ts):
                y = rows[t] * w[0:1]
                for i in range(1, GDN_CONV):
                    y = y + rows[t + i] * w[i:i + 1]
                y = _silu(y)
                if part == 0:
                    y = _l2n(y) * (GDN_DK ** -0.5)
                elif part == 1:
                    y = _l2n(y)
                dst[t, :, h * LANES:(h + 1) * LANES] = y

    for t in range(ts):
        g_all, beta_all = _gdn_gates(sm_ref[t], alog_ref[...], dtb_ref[...])
        a_all = jnp.exp(g_all)
        for h in range(GDN_HEADS):
            a_s[t, :, h * LANES:(h + 1) * LANES] = jnp.broadcast_to(a_all[:, SM_A + h:SM_A + h + 1], (G, LANES))
            b_s[t, :, h * LANES:(h + 1) * LANES] = jnp.broadcast_to(beta_all[:, SM_B + h:SM_B + h + 1], (G, LANES))

    nv = 2 * ts
    sr = lax.broadcasted_iota(jnp.int32, (3 * nv, nv * LANES), 0)
    sc = lax.broadcasted_iota(jnp.int32, (3 * nv, nv * LANES), 1)
    spread = (sr % nv == sc // LANES).astype(bf16)

    def head(h, carry):
        cols = pl.ds(pl.multiple_of(h * LANES, LANES), LANES)
        for i in range(G):
            row = slice(i, i + 1)
            kq = jnp.concatenate([k_s[t, row, cols] for t in range(ts)]
                                 + [q_s[t, row, cols] for t in range(ts)], axis=0)
            hi = kq.astype(bf16).astype(f32)
            mid = (kq - hi).astype(bf16).astype(f32)
            lo = (kq - hi) - mid
            kq_b = _dot(jnp.concatenate([hi, mid, lo], axis=0).T.astype(bf16), spread)
            s = sin_ref[i, h]
            for t in range(ts):
                kc = kq_b[:, t * LANES:(t + 1) * LANES]
                qc = kq_b[:, (ts + t) * LANES:(ts + t + 1) * LANES]
                a = a_s[t, row, cols]
                b = b_s[t, row, cols]
                ks = jnp.sum(s * kc, axis=0, keepdims=True)
                r = b * (v_s[t, row, cols] - a * ks)
                s = a * s + kc * r
                o_s[t, row, cols] = jnp.sum(s * qc, axis=0, keepdims=True)
            sout_ref[i, h] = s
        return carry

    lax.fori_loop(0, GDN_HEADS, head, 0)

    for t in range(ts):
        for h in range(GDN_HEADS):
            sl = slice(h * GDN_DV, (h + 1) * GDN_DV)
            o_ref[t, :, sl] = _gated_norm(o_s[t, :, sl], gn_ref[...], z_ref[t, :, sl]).astype(bf16)


def _gdn_sample(y3, y3_small, buf3, state, w_conv, a_log, dt_bias, g_gdn_norm):
    ts, db, _ = y3.shape
    G = SUBLANES
    alog, dtb, gn = _gdn_vecs(a_log, dt_bias, g_gdn_norm)
    cst = lambda g: (0, 0)
    st_spec = pl.BlockSpec((G, GDN_HEADS, GDN_DK, GDN_DV), lambda g: (g, 0, 0, 0))
    scr = pltpu.VMEM((ts, G, Z_W), f32)
    return pl.pallas_call(
        _gdn_sample_kernel,
        grid=(db // G,),
        in_specs=[pl.BlockSpec((ts, G, QKV_W), lambda g: (0, g, 0)),
                  pl.BlockSpec((GDN_CONV - 1, G, QKV_W), lambda g: (0, g, 0)),
                  pl.BlockSpec((ts, G, Z_W), lambda g: (0, g, OFF_Z // Z_W)),
                  pl.BlockSpec((ts, G, LANES), lambda g: (0, g, 0)),
                  st_spec,
                  pl.BlockSpec((GDN_CONV, QKV_W), cst),
                  pl.BlockSpec((1, LANES), cst),
                  pl.BlockSpec((1, LANES), cst),
                  pl.BlockSpec((1, GDN_DV), cst)],
        out_specs=[pl.BlockSpec((ts, G, Z_W), lambda g: (0, g, 0)), st_spec],
        out_shape=[jax.ShapeDtypeStruct((ts, db, Z_W), bf16),
                   jax.ShapeDtypeStruct(state.shape, f32)],
        scratch_shapes=[scr, scr, scr, scr, scr, scr],
        compiler_params=_params(48),
        name="gdn_sample",
    )(y3, buf3, y3, y3_small, state, w_conv, alog, dtb, gn)


def _rope_tables(pos, rot, width):
    half = rot // 2
    inv_freq = ROPE_THETA ** (-jnp.arange(half, dtype=f32) * (2.0 / rot))
    ang = pos.astype(f32)[:, None] * inv_freq[None, :]
    cos, sin = jnp.cos(ang), jnp.sin(ang)
    n = pos.shape[0]
    z = lambda w: jnp.zeros((n, w), f32)
    cosw = jnp.concatenate([cos, cos, jnp.ones((n, width - rot), f32)], axis=1)
    sina = jnp.concatenate([-sin, z(width - half)], axis=1)
    sinb = jnp.concatenate([z(half), sin, z(width - rot)], axis=1)
    reps = LANES // width
    return tuple(jnp.tile(a, (1, reps)) for a in (cosw, sina, sinb))


def _rope_kernel(dq_ref, dk_ref, dv_ref, iq_ref, sm_ref, c1, sa1, sb1, c2, sa2, sb2,
                 dq_o, dk_o, iq_o, ik_o, sm_o, kc_o, vc_o):
    tm = dq_ref.shape[0]
    h1 = DSA_HEAD_DIM // ROPE_FRACTION // 2
    h2 = IDX_DIM // ROPE_FRACTION // 2

    def rot(x, c, sa, sb, half):
        return x * c[...] + pltpu.roll(x, LANES - half, 1) * sa[...] + pltpu.roll(x, half, 1) * sb[...]

    for j in range(DQ_W // LANES):
        sl = slice(j * LANES, (j + 1) * LANES)
        dq_o[:, sl] = rot(dq_ref[:, sl], c1, sa1, sb1, h1).astype(bf16)
    for g in range(DSA_KV_HEADS):
        sl = slice(g * DSA_HEAD_DIM, (g + 1) * DSA_HEAD_DIM)
        kg = rot(dk_ref[:, sl], c1, sa1, sb1, h1)
        dk_o[:, sl] = kg
        kc_o[pl.ds(g, tm, stride=DSA_KV_HEADS), :] = kg
        vc_o[pl.ds(g, tm, stride=DSA_KV_HEADS), :] = dv_ref[:, sl]
    for j in range(IQ_W // LANES):
        sl = slice(j * LANES, (j + 1) * LANES)
        iq_o[:, sl] = rot(iq_ref[:, sl], c2, sa2, sb2, h2).astype(bf16)
    sm = sm_ref[...]
    ik_o[...] = rot(sm, c2, sa2, sb2, h2)[:, :IDX_DIM]
    sm_o[...] = sm


def _rope(y, y_small, pos, tm, group, db=None):
    m = y.shape[0]
    t1 =_rope_tables(pos, DSA_HEAD_DIM // ROPE_FRACTION, DSA_HEAD_DIM)
    t2 = _rope_tables(pos, IDX_DIM // ROPE_FRACTION, IDX_DIM)
    tab_blocks = pos.shape[0] // tm
    tab = pl.BlockSpec((tm, LANES), lambda i: (i % tab_blocks, 0))
    if group == "prompt":
        omap = lambda i: (i, 0)
        rows = lambda w: m
        cols = lambda w: w
    else:
        ts = m // db
        omap = lambda i: (0, i)
        rows = lambda w: db
        cols = lambda w: ts * w
    out = lambda w, dt: jax.ShapeDtypeStruct((rows(w), cols(w)), dt)
    return pl.pallas_call(
        _rope_kernel,
        grid=(m // tm,),
        in_specs=[pl.BlockSpec((tm, DQ_W), lambda i: (i, OFF_DQ // DQ_W)),
                  pl.BlockSpec((tm, DKV_W), lambda i: (i, OFF_DK // DKV_W)),
                  pl.BlockSpec((tm, DKV_W), lambda i: (i, OFF_DV // DKV_W)),
                  pl.BlockSpec((tm, IQ_W), lambda i: (i, OFF_IQ // IQ_W)),
                  pl.BlockSpec((tm, LANES), lambda i: (i, 0)),
                  tab, tab, tab, tab, tab, tab],
        out_specs=[pl.BlockSpec((tm, DQ_W), omap),
                   pl.BlockSpec((tm, DKV_W), omap),
                   pl.BlockSpec((tm, IQ_W), omap),
                   pl.BlockSpec((tm, IDX_DIM), lambda i: (i, 0)),
                   pl.BlockSpec((tm, LANES), omap),
                   pl.BlockSpec((DSA_KV_HEADS * tm, DSA_HEAD_DIM), lambda i: (i, 0)),
                   pl.BlockSpec((DSA_KV_HEADS * tm, DSA_HEAD_DIM), lambda i: (i, 0))],
        out_shape=[out(DQ_W, bf16), out(DKV_W, f32), out(IQ_W, bf16),
                   jax.ShapeDtypeStruct((m, IDX_DIM), f32), out(LANES, f32),
                   jax.ShapeDtypeStruct((DSA_KV_HEADS * m, DSA_HEAD_DIM), f32),
                   jax.ShapeDtypeStruct((DSA_KV_HEADS * m, DSA_HEAD_DIM), f32)],
        compiler_params=_params(),
        name="rope_" + group,
    )(y, y, y, y, y_small, *t1, *t2)


def _topk_bias(x_s, bias_s, lo0, mx, few, nkt, kt_w, k):
    R = x_s.shape[0]
    kf = float(k)
    tiles = [slice(kt * kt_w, (kt + 1) * kt_w) for kt in range(nkt)]
    n_grp = 2 if R % (2 * SUBLANES) == 0 else 1
    grps = [slice(g * (R // n_grp), (g + 1) * (R // n_grp)) for g in range(n_grp)]

    def count_ge(rows, th):
        acc = jnp.where(x_s[rows, tiles[0]] >= th, 1.0, 0.0)
        for t in tiles[1:]:
            acc = acc + jnp.where(x_s[rows, t] >= th, 1.0, 0.0)
        return jnp.sum(acc, axis=1, keepdims=True)

    def bisect(rows, lo, hi, done):
        mid = 0.5 * lo + 0.5 * hi
        cnt = count_ge(rows, mid)
        collapsed = jnp.logical_or(mid <= lo, mid >= hi)
        live = jnp.logical_and(done < 0.5, jnp.logical_not(collapsed))
        lo = jnp.where(jnp.logical_and(live, cnt >= kf), mid, lo)
        hi = jnp.where(jnp.logical_and(live, cnt <= kf), mid, hi)
        done = jnp.where(jnp.logical_or(collapsed, cnt == kf), 1.0, done)
        return lo, hi, done

    def snap(rows, lo, hi, done):
        a = b = None
        for t in tiles:
            x = x_s[rows, t]
            at = jnp.where(x >= lo, x, POS)
            bt = jnp.where(x < hi, x, NEG)
            a = at if a is None else jnp.minimum(a, at)
            b = bt if b is None else jnp.maximum(b, bt)
        a = jnp.min(a, axis=1, keepdims=True)
        b = jnp.max(b, axis=1, keepdims=True)
        live = done < 0.5
        return jnp.where(live, a, lo), jnp.where(jnp.logical_and(live, a >= b), 1.0, done)

    def cond(c):
        it, st = c
        left = st[0][2]
        for s in st[1:]:
            left = jnp.minimum(left, s[2])
        return jnp.logical_and(it < BISECT_CAP, jnp.min(left) < 0.5)

    def body(c):
        it, st = c
        for _ in range(BISECT_UNROLL):
            st = [bisect(rows, *s) for rows, s in zip(grps, st)]

        def snapped():
            out = []
            for rows, (lo, hi, done) in zip(grps, st):
                lo2, done2 = snap(rows, lo, hi, done)
                out.append((lo2, hi, done2))
            return out

        st = lax.cond(it >= SNAP_FROM, snapped, lambda: st)
        return it + 1, st

    hi0 = jnp.where(few, lo0, mx + (jnp.abs(mx) + 1.0))
    st0 = [(lo0[rows], hi0[rows], few[rows].astype(f32)) for rows in grps]
    _, st = lax.while_loop(cond, body, (jnp.int32(0), st0))
    lo = jnp.concatenate([s[0] for s in st], axis=0)
    hi = jnp.concatenate([s[1] for s in st], axis=0)

    has_run = jnp.max(jnp.where(lo < hi, 1.0, 0.0)) > 0.5
    rows_all = slice(0, R)

    @pl.when(jnp.logical_not(has_run))
    def _():
        for t in tiles:
            bias_s[:, t] = jnp.where(x_s[:, t] >= hi, 0.0, NEG)

    @pl.when(has_run)
    def _():
        need = kf - count_ge(rows_all, hi)
        ai = lax.broadcasted_iota(jnp.int32, (kt_w, kt_w), 0)
        bi = lax.broadcasted_iota(jnp.int32, (kt_w, kt_w), 1)
        before = (ai < bi).astype(bf16)
        seen = jnp.zeros((R, 1), f32)
        for t in tiles:
            x = x_s[:, t]
            run = jnp.logical_and(x >= lo, x < hi)
            runf = run.astype(f32)
            rank = seen + _dot(runf.astype(bf16), before)
            take = jnp.logical_or(x >= hi, jnp.logical_and(run, rank < need))
            bias_s[:, t] = jnp.where(take, 0.0, NEG)
            seen = seen + jnp.sum(runf, axis=1, keepdims=True)


def _dsa_prompt_kernel(iq_ref, sm_ref, ik_ref, q_ref, k_ref, v_ref, o_ref,
                       ikb, kb, vb, x_s, bias_s, s_s, *, topk, kt_w, ext_w):
    qi = pl.program_id(1)
    R = iq_ref.shape[0]
    T = ik_ref.shape[0]

    @pl.when(qi == 0)
    def _():
        ikb[...] = ik_ref[...].astype(bf16)
        kb[...] = k_ref[...].astype(bf16)
        vb[...] = v_ref[...].astype(bf16)

    def block(ext):
        tiles = [slice(j * kt_w, (j + 1) * kt_w) for j in range(ext // kt_w)]
        wgt = sm_ref[...] * IDX_SCALE
        qpos = qi * R + lax.broadcasted_iota(jnp.int32, (R, kt_w), 0)
        key0 = lax.broadcasted_iota(jnp.int32, (R, kt_w), 1)
        mn = mx = None
        for j, t in enumerate(tiles):
            keys = ikb[t, :]
            sc = jnp.zeros((R, kt_w), f32)
            for h in range(IDX_HEADS):
                rel = jnp.maximum(_dot_nt(iq_ref[:, h * IDX_DIM:(h + 1) * IDX_DIM], keys), 0.0)
                sc = sc + rel * wgt[:, SM_IW + h:SM_IW + h + 1]
            causal = key0 + j * kt_w <= qpos
            x_s[:, t] = jnp.where(causal, sc, NEG)
            lo_t = jnp.where(causal, sc, POS)
            hi_t = jnp.where(causal, sc, NEG)
            mn = lo_t if mn is None else jnp.minimum(mn, lo_t)
            mx = hi_t if mx is None else jnp.maximum(mx, hi_t)
        n_causal = qi * R + lax.broadcasted_iota(jnp.int32, (R, 1), 0) + 1
        _topk_bias(x_s, bias_s, jnp.min(mn, axis=1, keepdims=True), jnp.max(mx, axis=1, keepdims=True),
                   n_causal <= topk, len(tiles), kt_w, topk)

        def heads(hp, carry):
            h0 = hp * DSA_HEAD_GROUP
            g = h0 // (DSA_HEADS // DSA_KV_HEADS)
            gc = pl.ds(pl.multiple_of(g * DSA_HEAD_DIM, DSA_HEAD_DIM), DSA_HEAD_DIM)
            hcs = [pl.ds(pl.multiple_of((h0 + i) * DSA_HEAD_DIM, DSA_HEAD_DIM), DSA_HEAD_DIM)
                   for i in range(DSA_HEAD_GROUP)]
            qs = [q_ref[:, hc] for hc in hcs]
            mxa = [None] * DSA_HEAD_GROUP
            for t in tiles:
                kt = kb[t, gc]
                bt = bias_s[:, t]
                for i in range(DSA_HEAD_GROUP):
                    s = _dot_nt(qs[i], kt) * (DSA_HEAD_DIM ** -0.5) + bt
                    s_s[i, :, t] = s
                    mxa[i] = s if mxa[i] is None else jnp.maximum(mxa[i], s)
            ms = [jnp.max(a, axis=1, keepdims=True) for a in mxa]
            la = [jnp.zeros((R, kt_w), f32)] * DSA_HEAD_GROUP
            acc = [jnp.zeros((R, DSA_HEAD_DIM), f32)] * DSA_HEAD_GROUP
            for t in tiles:
                vt = vb[t, gc]
                for i in range(DSA_HEAD_GROUP):
                    p = jnp.exp(s_s[i, :, t] - ms[i])
                    la[i] = la[i] + p
                    acc[i] = acc[i] + _dot(p.astype(bf16), vt)
            for i in range(DSA_HEAD_GROUP):
                o_ref[:, hcs[i]] = (acc[i] / jnp.sum(la[i], axis=1, keepdims=True)).astype(bf16)
            return carry

        lax.fori_loop(0, DSA_HEADS // DSA_HEAD_GROUP, heads, 0)

    for e in range(T // ext_w):
        pl.when(qi // (ext_w // R) == e)(functools.partial(block, (e + 1) * ext_w))


def _dsa_prompt(iq, sm, ik, dq, dk, y, nb, t, topk):
    R = LANES
    nq = t // R
    kt_w = min(DSA_KEY_TILE, t)
    ext_w = min(DSA_KEY_EXTENT, t)
    kern = functools.partial(_dsa_prompt_kernel, topk=topk, kt_w=kt_w, ext_w=ext_w)
    return pl.pallas_call(
        kern,
        grid=(nb, nq),
        in_specs=[pl.BlockSpec((R, IQ_W), lambda b, i: (b * nq + i, 0)),
                  pl.BlockSpec((R, LANES), lambda b, i: (b * nq + i, 0)),
                  pl.BlockSpec((t, IDX_DIM), lambda b, i: (b, 0)),
                  pl.BlockSpec((R, DQ_W), lambda b, i: (b * nq + i, 0)),
                  pl.BlockSpec((t, DKV_W), lambda b, i: (b, 0)),
                  pl.BlockSpec((t, DKV_W), lambda b, i: (b, OFF_DV // DKV_W))],
        out_specs=pl.BlockSpec((R, DQ_W), lambda b, i: (b * nq + i, 0)),
        out_shape=jax.ShapeDtypeStruct((nb * t, DQ_W), bf16),
        scratch_shapes=[pltpu.VMEM((t, IDX_DIM), bf16), pltpu.VMEM((t, DKV_W), bf16),
                        pltpu.VMEM((t, DKV_W), bf16), pltpu.VMEM((R, t), f32), pltpu.VMEM((R, t), f32),
                        pltpu.VMEM((DSA_HEAD_GROUP, R, t), f32)],
        compiler_params=_params(48, 2),
        name="dsa_prompt",
    )(iq, sm, ik, dq, dk, y)


def _idx_score_kernel(pt_ref, iq_ref, sm_ref, ikn_ref, *rest, n_pages, page):
    eb, tp, _ = iq_ref.shape
    o_ref = rest[eb * n_pages]
    ts = o_ref.shape[1]
    past = n_pages * page
    lp = o_ref.shape[2]
    for e in range(eb):
        pages = rest[e * n_pages:(e + 1) * n_pages]
        keys_t = jnp.concatenate([p[...] for p in pages], axis=1).astype(bf16)
        keys_n = jnp.concatenate([ikn_ref[e], jnp.zeros((lp - past - tp, IDX_DIM), f32)], axis=0).astype(bf16)
        wgt = sm_ref[e] * IDX_SCALE
        iq = iq_ref[e].astype(bf16)
        score = jnp.zeros((tp, lp), f32)
        for h in range(IDX_HEADS):
            iqh = iq[:, h * IDX_DIM:(h + 1) * IDX_DIM]
            rel = jnp.maximum(jnp.concatenate([_dot(iqh, keys_t), _dot_nt(iqh, keys_n)], axis=1), 0.0)
            score = score + rel * wgt[:, SM_IW + h:SM_IW + h + 1]
        o_ref[e] = score[:ts]


def _idx_scores(page_table, iq, sm, ikn, cache_idx, lp, ts):
    db, tp, _ = iq.shape
    n_pages = page_table.shape[1]
    page = cache_idx.shape[2]
    eb = IDX_BATCH
    kern = functools.partial(_idx_score_kernel, n_pages=n_pages, page=page)
    page_specs = [pl.BlockSpec((None, IDX_DIM, page),
                               functools.partial(lambda b, pt, e, p: (pt[b * eb + e, p], 0, 0), e=e, p=p))
                  for e in range(eb) for p in range(n_pages)]
    grid_spec = pltpu.PrefetchScalarGridSpec(
        num_scalar_prefetch=1,
        grid=(db // eb,),
        in_specs=[pl.BlockSpec((eb, tp, IQ_W), lambda b, pt: (b, 0, 0)),
                  pl.BlockSpec((eb, tp, LANES), lambda b, pt: (b, 0, 0)),
                  pl.BlockSpec((eb, tp, IDX_DIM), lambda b, pt: (b, 0, 0))] + page_specs,
        out_specs=pl.BlockSpec((eb, ts, lp), lambda b, pt: (b, 0, 0)),
    )
    return pl.pallas_call(
        kern, grid_spec=grid_spec,
        out_shape=jax.ShapeDtypeStruct((db, ts, lp), f32),
        compiler_params=_params(),
        name="idx_scores_sample",
    )(page_table, iq, sm, ikn, *([cache_idx] * (eb * n_pages)))


def _select_sample_kernel(x_ref, bias_ref, x_s, *, topk, past, ts):
    R, lp = x_ref.shape
    key = lax.broadcasted_iota(jnp.int32, (R, lp), 1)
    t = lax.broadcasted_iota(jnp.int32, (R, lp), 0) % ts
    causal = key <= past + t
    x = x_ref[...]
    x_s[...] = jnp.where(causal, x, NEG)
    lo0 = jnp.min(jnp.where(causal, x, POS), axis=1, keepdims=True)
    mx = jnp.max(jnp.where(causal, x, NEG), axis=1, keepdims=True)
    n_causal = past + lax.broadcasted_iota(jnp.int32, (R, 1), 0) % ts + 1
    _topk_bias(x_s, bias_ref, lo0, mx, n_causal <= topk, lp // LANES, LANES, topk)


def _select_sample(scores, topk, past, ts):
    m, lp = scores.shape
    R = LANES
    kern = functools.partial(_select_sample_kernel, topk=topk, past=past, ts=ts)
    return pl.pallas_call(
        kern, grid=(m // R,),
        in_specs=[pl.BlockSpec((R, lp), lambda i: (i, 0))],
        out_specs=pl.BlockSpec((R, lp), lambda i: (i, 0)),
        out_shape=jax.ShapeDtypeStruct((m, lp), f32),
        scratch_shapes=[pltpu.VMEM((R, lp), f32)],
        compiler_params=_params(),
        name="select_sample",
    )(scores)


def _dsa_sample_kernel(pt_ref, q_ref, sel_ref, kn_ref, vn_ref, *rest, n_pages, page):
    eb, tp, _ = q_ref.shape
    o_ref = rest[2 * eb * n_pages]
    ts = sel_ref.shape[1]
    past = n_pages * page
    lp = sel_ref.shape[2]
    hpg = DSA_HEADS // DSA_KV_HEADS
    padn = jnp.zeros((lp - past - tp, DSA_HEAD_DIM), f32)
    for e in range(eb):
        kp = rest[e * n_pages:(e + 1) * n_pages]
        vp = rest[(eb + e) * n_pages:(eb + e + 1) * n_pages]
        bias = jnp.concatenate([sel_ref[e], jnp.zeros((tp - ts, lp), f32)], axis=0)
        bias = jnp.concatenate([bias] * hpg, axis=0)
        for g in range(DSA_KV_HEADS):
            gsl = slice(g * DSA_HEAD_DIM, (g + 1) * DSA_HEAD_DIM)
            kg = jnp.concatenate([p[pl.ds(g, page, stride=DSA_KV_HEADS), :] for p in kp]
                                 + [kn_ref[e, :, gsl], padn], axis=0).astype(bf16)
            vg = jnp.concatenate([p[pl.ds(g, page, stride=DSA_KV_HEADS), :] for p in vp]
                                 + [vn_ref[e, :, gsl], padn], axis=0).astype(bf16)
            qg = jnp.concatenate([q_ref[e, :, (g * hpg + j) * DSA_HEAD_DIM:(g * hpg + j + 1) * DSA_HEAD_DIM]
                                  for j in range(hpg)], axis=0).astype(bf16)
            s = _dot_nt(qg, kg) * (DSA_HEAD_DIM ** -0.5) + bias
            m = jnp.max(s, axis=1, keepdims=True)
            p = jnp.exp(s - m)
            l = jnp.sum(p, axis=1, keepdims=True)
            o = _dot(p.astype(bf16), vg) / l
            for j in range(hpg):
                hsl = slice((g * hpg + j) * DSA_HEAD_DIM, (g * hpg + j + 1) * DSA_HEAD_DIM)
                o_ref[e, :, hsl] = o[j * tp:j * tp + ts]


def _dsa_sample(page_table, dq, sel, kn, vn, ck, cv):
    db, tp, _ = dq.shape
    ts = sel.shape[1]
    lp = sel.shape[2]
    n_pages = page_table.shape[1]
    rows = ck.shape[1]
    page = rows // DSA_KV_HEADS
    eb = DSA_SAMPLE_BATCH
    kern = functools.partial(_dsa_sample_kernel, n_pages=n_pages, page=page)
    pspec = [pl.BlockSpec((None, rows, DSA_HEAD_DIM),
                          functools.partial(lambda b, pt, e, p: (pt[b * eb + e, p], 0, 0), e=e, p=p))
             for e in range(eb) for p in range(n_pages)]
    per_b = lambda r, w: pl.BlockSpec((eb, r, w), lambda b, pt: (b, 0, 0))
    grid_spec = pltpu.PrefetchScalarGridSpec(
        num_scalar_prefetch=1,
        grid=(db // eb,),
        in_specs=[per_b(tp, DQ_W), per_b(ts, lp), per_b(tp, DKV_W), per_b(tp, DKV_W)] + pspec + pspec,
        out_specs=per_b(ts, DQ_W),
    )
    return pl.pallas_call(
        kern, grid_spec=grid_spec,
        out_shape=jax.ShapeDtypeStruct((db, ts, DQ_W), f32),
        compiler_params=_params(48),
        name="dsa_sample",
    )(page_table, dq, sel, kn, vn, *([ck] * (eb * n_pages)), *([cv] * (eb * n_pages)))


def _merge_kernel(og_ref, od_ref, gg_ref, gd_ref, x_ref, gt_ref, sc_ref, sh_ref, g2_ref,
                  wg_ref, wd_ref, wo_ref, x1_ref, h2_ref):
    mix = (_sigmoid(gg_ref[...]) * _dot(og_ref[...], wg_ref[...])
           + _sigmoid(gd_ref[...]) * _dot(od_ref[...], wd_ref[...]))
    x1 = x_ref[...] + gt_ref[...] * _dot(mix.astype(bf16), wo_ref[...])
    x1_ref[...] = x1
    y = x1 * lax.rsqrt(jnp.mean(x1 * x1, axis=-1, keepdims=True) + NORM_EPS)
    h2_ref[...] = ((y * g2_ref[...]) * (1.0 + sc_ref[...]) + sh_ref[...]).astype(bf16)


def _merge(o_gdn, o_dsa, y, x, ada, g2, wg, wd, wo, group, rows_per_batch, tm):
    m, d = x.shape
    row = lambda w, blk=0: pl.BlockSpec((tm, w), lambda i: (i, blk))
    res = lambda a: pl.BlockSpec(a.shape, lambda i: (0, 0), pipeline_mode=pl.Buffered(1))
    return pl.pallas_call(
        _merge_kernel,
        grid=(m // tm,),
        in_specs=[row(Z_W), row(DQ_W), row(d, 0), row(d, 1), row(d),
                  _mod_spec(group, d, rows_per_batch, tm, 2),
                  _mod_spec(group, d, rows_per_batch, tm, 4),
                  _mod_spec(group, d, rows_per_batch, tm, 3),
                  pl.BlockSpec((1, d), lambda i: (0, 0)),
                  res(wg), res(wd), res(wo)],
        out_specs=[row(d), row(d)],
        out_shape=[jax.ShapeDtypeStruct((m, d), f32), jax.ShapeDtypeStruct((m, d), bf16)],
        compiler_params=_params(48),
        name="merge_" + group,
    )(o_gdn, o_dsa, y, y, x, ada, ada, ada, g2.reshape(1, d), wg, wd, wo)


def _ffn_epilogue(acc_ref, x1_ref, gt_ref, gf_ref, y_ref):
    tm = acc_ref.shape[0]
    r = gt_ref.shape[0] if gt_ref.shape[0] > 1 else tm
    for s0 in range(0, tm, r):
        sl = slice(s0, s0 + r)
        x2 = x1_ref[sl, :] + gt_ref[...] * acc_ref[sl, :]
        y_ref[sl, :] = (x2 * lax.rsqrt(jnp.mean(x2 * x2, axis=-1, keepdims=True) + NORM_EPS)) * gf_ref[...]


def _ffn_prompt_kernel(h_ref, wug_ref, wuv_ref, wcg_ref, wcv_ref, bg_ref, bv_ref, wd_ref,
                       x1_ref, gt_ref, gf_ref, y_ref, ug_ref, uv_ref, acc_ref, act_s, prev_s,
                       *, blocks_per_seq, n_up):
    i = pl.program_id(0)
    j = pl.program_id(1)
    tm = h_ref.shape[0]
    keep = (i % blocks_per_seq) > 0
    sr = min(FFN_SUB_ROWS, tm)

    @pl.when(jnp.logical_and(i == 0, j == 0))
    def _():
        prev_s[...] = jnp.zeros_like(prev_s)

    def conv(u, prev, wc_ref, b_ref):
        ux = jnp.concatenate([prev, u], axis=0)
        w = wc_ref[...]
        y = u * w[FFN_CONV - 1:FFN_CONV] + b_ref[...]
        for t in range(FFN_CONV - 1):
            sft = FFN_CONV - 1 - t
            y = y + ux[SUBLANES - sft:SUBLANES - sft + sr] * w[t:t + 1]
        return y

    def up(store):
        pg = jnp.where(keep, prev_s[j, 0], 0.0)
        pv = jnp.where(keep, prev_s[j, 1], 0.0)
        for s in range(tm // sr):
            rows = slice(s * sr, (s + 1) * sr)
            hs = h_ref[rows, :]
            ug = _dot(hs, wug_ref[...])
            uv = _dot(hs, wuv_ref[...])
            store(rows, _silu(conv(ug, pg, wcg_ref, bg_ref)) * conv(uv, pv, wcv_ref, bv_ref))
            pg = ug[sr - SUBLANES:]
            pv = uv[sr - SUBLANES:]
        ug_ref[...] = pg
        uv_ref[...] = pv
        prev_s[j, 0] = pg
        prev_s[j, 1] = pv

    _ffn_phases(j, n_up, up, act_s, wd_ref, acc_ref, x1_ref, gt_ref, gf_ref, y_ref)


def _ffn_phases(j, n_up, up, act_s, wd_ref, acc_ref, x1_ref, gt_ref, gf_ref, y_ref):
    tf = act_s.shape[1] // n_up
    tn = wd_ref.shape[1]

    @pl.when(j < n_up)
    def _():
        cols = pl.ds(pl.multiple_of(j * tf, tf), tf)

        def store(rows, act):
            act_s[rows, cols] = act.astype(bf16)
        up(store)

    @pl.when(j >= n_up)
    def _():
        acc_ref[:, pl.ds(pl.multiple_of((j - n_up) * tn, tn), tn)] = _dot(act_s[...], wd_ref[...])

    @pl.when(j == pl.num_programs(1) - 1)
    def _():
        _ffn_epilogue(acc_ref, x1_ref, gt_ref, gf_ref, y_ref)


def _ffn_sample_kernel(h_ref, bufg_ref, bufv_ref, wug_ref, wuv_ref, wcg_ref, wcv_ref, bg_ref, bv_ref, wd_ref,
                       x1_ref, gt_ref, gf_ref, y_ref, ug_ref, uv_ref, acc_ref, act_s, *, ts, n_up):
    j = pl.program_id(1)
    db = h_ref.shape[0] // ts
    nbuf = FFN_CONV - 1

    def branch(wu_ref, wc_ref, b_ref, buf_ref, ubuf_ref):
        u = _dot(h_ref[...], wu_ref[...])
        rows = [buf_ref[:, r, :] for r in range(nbuf)]
        rows += [u[t * db:(t + 1) * db] for t in range(ts)]
        for r in range(nbuf):
            ubuf_ref[:, r, :] = rows[ts + r]
        w = wc_ref[...]
        outs = []
        for t in range(ts):
            y = rows[t] * w[0:1] + b_ref[...]
            for r in range(1, FFN_CONV):
                y = y + rows[t + r] * w[r:r + 1]
            outs.append(y)
        return jnp.concatenate(outs, axis=0)

    def up(store):
        store(slice(0, ts * db),
              _silu(branch(wug_ref, wcg_ref, bg_ref, bufg_ref, ug_ref))
              * branch(wuv_ref, wcv_ref, bv_ref, bufv_ref, uv_ref))

    _ffn_phases(j, n_up, up, act_s, wd_ref, acc_ref, x1_ref, gt_ref, gf_ref, y_ref)


def _ffn(h2, x1, ada, g_final, w_up, w_conv, b_conv, w_down, group, rows_per_batch, tm, tf, buf=None):
    m, d = h2.shape
    dff = w_down.shape[0]
    nj = dff // tf
    tn = min(FFN_DOWN_TILE, d)
    b2 = b_conv.reshape(1, 2 * dff)
    up = lambda j: jnp.minimum(j, nj - 1)
    common_w = [pl.BlockSpec((d, tf), lambda i, j: (0, up(j))),
                pl.BlockSpec((d, tf), lambda i, j: (0, nj + up(j))),
                pl.BlockSpec((FFN_CONV, tf), lambda i, j: (0, up(j))),
                pl.BlockSpec((FFN_CONV, tf), lambda i, j: (0, nj + up(j))),
                pl.BlockSpec((1, tf), lambda i, j: (0, up(j))),
                pl.BlockSpec((1, tf), lambda i, j: (0, nj + up(j))),
                pl.BlockSpec((dff, tn), lambda i, j: (0, jnp.maximum(j - nj, 0)))]
    tail = [pl.BlockSpec((tm, d), lambda i, j: (i, 0)),
            _mod_spec(group, d, rows_per_batch, tm, 5),
            pl.BlockSpec((1, d), lambda i, j: (0, 0))]
    y_spec = pl.BlockSpec((tm, d), lambda i, j: (i, 0))
    if group == "prompt":
        bps = rows_per_batch // tm
        nb = m // rows_per_batch
        kern = functools.partial(_ffn_prompt_kernel, blocks_per_seq=bps, n_up=nj)
        first = [pl.BlockSpec((tm, d), lambda i, j: (i, 0))]
        ubuf_spec = pl.BlockSpec((None, SUBLANES, tf), lambda i, j: (i, 0, up(j)))
        ubuf_shape = jax.ShapeDtypeStruct((m // tm, SUBLANES, dff), f32)
        args = (h2,)
        extra_scratch = [pltpu.VMEM((nj, 2, SUBLANES, tf), f32)]
    else:
        extra_scratch = []
        ts = m // rows_per_batch
        kern = functools.partial(_ffn_sample_kernel, ts=ts, n_up=nj)
        first = [pl.BlockSpec((tm, d), lambda i, j: (i, 0)),
                 pl.BlockSpec((rows_per_batch, FFN_CONV - 1, tf), lambda i, j: (0, 0, up(j))),
                 pl.BlockSpec((rows_per_batch, FFN_CONV - 1, tf), lambda i, j: (0, 0, nj + up(j)))]
        ubuf_spec = pl.BlockSpec((rows_per_batch, FFN_CONV - 1, tf), lambda i, j: (0, 0, up(j)))
        ubuf_shape = jax.ShapeDtypeStruct((rows_per_batch, FFN_CONV - 1, dff), f32)
        args = (h2, buf, buf)
    return pl.pallas_call(
        kern,
        grid=(m // tm, nj + d // tn),
        in_specs=first + common_w + tail,
        out_specs=[y_spec, ubuf_spec, ubuf_spec],
        out_shape=[jax.ShapeDtypeStruct((m, d), f32), ubuf_shape, ubuf_shape],
        scratch_shapes=[pltpu.VMEM((tm, d), f32), pltpu.VMEM((tm, dff), bf16)] + extra_scratch,
        compiler_params=_params(56, 2),
        name="ffn_" + group,
    )(*args, w_up, w_up, w_conv, w_conv, b2, b2, w_down, x1, ada, g_final.reshape(1, d))


def _split_w_in(w_in, d):
    sizes = (QKV_W, Z_W, GDN_HEADS, GDN_HEADS, DQ_W, DKV_W, DKV_W, IQ_W, IDX_DIM, IDX_HEADS, d, d)
    offs = [0]
    for s in sizes:
        offs.append(offs[-1] + s)
    seg = lambda i, j: w_in[:, offs[i]:offs[j]]
    pad = jnp.zeros((w_in.shape[0], LANES - (IDX_DIM + 2 * GDN_HEADS + IDX_HEADS)), w_in.dtype)
    small = jnp.concatenate([seg(8, 9), seg(2, 4), seg(9, 10), pad], axis=1)
    return tuple(w.astype(bf16) for w in (seg(0, 2), seg(4, 8), seg(10, 12), small))


def _in_proj(h, w_parts, tm, name):
    return tuple(_matmul(h, w, tm, _mm_tile(w.shape[1]), name + "_" + part)
                 for w, part in zip(w_parts, ("gdn", "dsa", "gate", "small")))


def _ffn_tile(dff):
    for tf in (512, 256, 128):
        if dff % tf == 0:
            return tf
    raise ValueError("d_ff must be a multiple of 128")


def _mm_tile(n):
    for tn in (1024, 512, 256, 128):
        if n % tn == 0:
            return tn
    raise ValueError("projection width must be a multiple of 128")


def kernel(x_prompt, x_sample, c_prompt, c_sample, cache_k, cache_v, cache_idx_k, page_table, state_gdn, state_gdn_conv, state_ffn_conv, w_ada, b_ada, g_norm1, w_in, w_gdn_conv, a_log, dt_bias, g_gdn_norm, w_gdn_out, w_dsa_out, w_o, g_norm2, w_up, w_ffn_conv, b_ffn_conv, w_down, g_final):
    nb, t, d = x_prompt.shape
    db, ts, _ = x_sample.shape
    depth = w_ada.shape[0]
    assert depth == 1 and db == LANES and ts >= GDN_CONV - 1
    n_pages = page_table.shape[1]
    page = cache_k.shape[2]
    past = n_pages * page
    dff = w_down.shape[1]
    l = 0

    n_c = nb + db
    pad_c = (-n_c) % SUBLANES
    c_all = jnp.concatenate([c_prompt, c_sample, jnp.zeros((pad_c, d), f32)], axis=0)
    ada = _ada(c_all, w_ada[l], b_ada[l])
    ada_p = ada[:nb].reshape(nb, 1, 6 * d)
    ada_s = ada[nb:nb + db]

    w_in_parts = _split_w_in(w_in[l], d)
    wg = w_gdn_out[l].astype(bf16)
    wd = w_dsa_out[l].astype(bf16)
    wo = w_o[l].astype(bf16)
    wup = w_up[l].astype(bf16)
    wdn = w_down[l].astype(bf16)
    tf = _ffn_tile(dff)

    xp = x_prompt.reshape(nb * t, d)
    tm_p = min(512, t)
    h1 = _prep(xp, g_norm1[l], ada_p, "prompt", t, tm_p)
    yp_gdn, yp_dsa, yp_gate, yp_small = _in_proj(h1, w_in_parts, min(IN_PROJ_ROWS, nb * t), "in_proj_prompt")
    og_p, s_p = _gdn_prompt(yp_gdn, yp_small, w_gdn_conv[l], a_log[l], dt_bias[l], g_gdn_norm[l], nb, t)
    dq_p, dk_p, iq_p, ik_p, sm_p, kc_p, vc_p = _rope(yp_dsa, yp_small, jnp.arange(t), tm_p, "prompt")
    od_p = _dsa_prompt(iq_p, sm_p, ik_p, dq_p, dk_p, yp_dsa, nb, t, min(DSA_TOPK, t // 4))
    x1_p, h2_p = _merge(og_p, od_p, yp_gate, xp, ada_p, g_norm2[l], wg, wd, wo, "prompt", t, min(256, t))
    y_p, ug_p, uv_p = _ffn(h2_p, x1_p, ada_p, g_final, wup, w_ffn_conv[l], b_ffn_conv[l], wdn,
                           "prompt", t, tm_p, tf)

    nfb = FFN_CONV - 1
    bps_p = t // tm_p
    out_p = (
        y_p.reshape(nb, t, d),
        kc_p.reshape(1, nb, t, DSA_KV_HEADS, DSA_HEAD_DIM),
        vc_p.reshape(1, nb, t, DSA_KV_HEADS, DSA_HEAD_DIM),
        ik_p.reshape(1, nb, t, IDX_DIM),
        s_p[None],
        yp_gdn.reshape(nb, t, -1)[:, t - (GDN_CONV - 1):, :QKV_W][None],
        jnp.concatenate([ug_p[bps_p - 1::bps_p, SUBLANES - nfb:], uv_p[bps_p - 1::bps_p, SUBLANES - nfb:]],
                        axis=-1)[None],
    )

    xs = x_sample.transpose(1, 0, 2).reshape(ts * db, d)
    h1s = _prep(xs, g_norm1[l], ada_s, "sample", db, db)
    ys_gdn, ys_dsa, ys_gate, ys_small = _in_proj(h1s, w_in_parts, ts * db, "in_proj_sample")
    ys_gdn3 = ys_gdn.reshape(ts, db, -1)
    og_s, s_s = _gdn_sample(ys_gdn3, ys_small.reshape(ts, db, LANES), state_gdn_conv[l].transpose(1, 0, 2),
                            state_gdn[l], w_gdn_conv[l], a_log[l], dt_bias[l], g_gdn_norm[l])
    pos_s = jnp.repeat(past + jnp.arange(ts), db)
    dq_s, dk_s, iq_s, ik_s, sm_s, _, _ = _rope(ys_dsa, ys_small, pos_s, db, "sample", db)
    ik_s_b = ik_s.reshape(ts, db, IDX_DIM).transpose(1, 0, 2)
    lp = past + LANES
    tp = -(-ts // SUBLANES) * SUBLANES
    pad_t = lambda a: jnp.pad(a.astype(f32), ((0, 0), (0, tp - ts), (0, 0)))
    scores = _idx_scores(page_table, pad_t(iq_s.reshape(db, ts, IQ_W)), pad_t(sm_s.reshape(db, ts, LANES)),
                         pad_t(ik_s_b), jnp.swapaxes(cache_idx_k[l], 1, 2), lp, ts)
    sel = _select_sample(scores.reshape(db * ts, lp), min(DSA_TOPK, (past + ts) // 4), past, ts)
    dv_s_b = ys_dsa.reshape(ts, db, -1)[:, :, OFF_DV:OFF_DV + DKV_W].transpose(1, 0, 2)
    ck = cache_k[l].reshape(cache_k.shape[1], page * DSA_KV_HEADS, DSA_HEAD_DIM)
    cv = cache_v[l].reshape(cache_v.shape[1], page * DSA_KV_HEADS, DSA_HEAD_DIM)
    od_s = _dsa_sample(page_table, pad_t(dq_s.reshape(db, ts, DQ_W)), sel.reshape(db, ts, lp),
                       pad_t(dk_s.reshape(db, ts, DKV_W)), pad_t(dv_s_b), ck, cv)
    od_s = od_s.astype(bf16).transpose(1, 0, 2).reshape(ts * db, DQ_W)
    x1_s, h2_s = _merge(og_s.reshape(ts * db, Z_W), od_s, ys_gate, xs, ada_s, g_norm2[l], wg, wd, wo,
                        "sample", db, db)
    y_s, ug_s, uv_s = _ffn(h2_s, x1_s, ada_s, g_final, wup, w_ffn_conv[l], b_ffn_conv[l], wdn,
                           "sample", db, ts * db, tf, buf=state_ffn_conv[l])

    out_s = (
        y_s.reshape(ts, db, d).transpose(1, 0, 2),
        dk_s.reshape(1, db, ts, DSA_KV_HEADS, DSA_HEAD_DIM),
        dv_s_b.reshape(1, db, ts, DSA_KV_HEADS, DSA_HEAD_DIM),
        ik_s_b[None],
        s_s[None],
        ys_gdn3[ts - (GDN_CONV - 1):, :, :QKV_W].transpose(1, 0, 2)[None],
        jnp.concatenate([ug_s, uv_s], axis=-1)[None],
    )
    return (out_p[0], out_s[0]) + out_p[1:] + out_s[1:]
```

```python
import functools

import jax
import jax.numpy as jnp
from jax import lax
from jax.experimental import pallas as pl
from jax.experimental.pallas import tpu as pltpu

f32 = jnp.float32
bf16 = jnp.bfloat16

GDN_HEADS = 8
GDN_DK = 128
GDN_DV = 128
GDN_CONV = 4
GDN_CHUNK = 64
GDN_GROUP = 4
GDN_SEQS = 4
DSA_HEADS = 8
DSA_KV_HEADS = 2
DSA_HEAD_DIM = 128
IDX_HEADS = 8
IDX_DIM = 64
IDX_SCALE = IDX_HEADS ** -0.5 * IDX_DIM ** -0.5
DSA_TOPK = 256
ROPE_THETA = 500000.0
ROPE_FRACTION = 4
FFN_CONV = 3
NORM_EPS = 1e-6

LANES = 128
SUBLANES = 8
NEG = float(jnp.finfo(jnp.float32).min)
POS = float(jnp.finfo(jnp.float32).max)
BISECT_UNROLL = 4
SNAP_FROM = 4
BISECT_CAP = 1024
DSA_KEY_TILE = 256
DSA_KEY_EXTENT = 256
IN_PROJ_ROWS = 1024
IDX_BATCH = 8
DSA_SAMPLE_BATCH = 4
DSA_HEAD_GROUP = 4
FFN_DOWN_TILE = 512
FFN_SUB_ROWS = 256

QKV_W =2 * GDN_HEADS * GDN_DK + GDN_HEADS * GDN_DV
Z_W = GDN_HEADS * GDN_DV
DQ_W = DSA_HEADS * DSA_HEAD_DIM
DKV_W = DSA_KV_HEADS * DSA_HEAD_DIM
IQ_W = IDX_HEADS * IDX_DIM
OFF_Z = QKV_W
GDN_PROJ_W = QKV_W + Z_W
OFF_DQ = 0
OFF_DK = OFF_DQ + DQ_W
OFF_DV = OFF_DK + DKV_W
OFF_IQ = OFF_DV + DKV_W
DSA_PROJ_W = OFF_IQ + IQ_W
SM_IK = 0
SM_A = IDX_DIM
SM_B = SM_A + GDN_HEADS
SM_IW = SM_B + GDN_HEADS


def _sigmoid(x):
    return 1.0 / (1.0 + jnp.exp(-x))


def _silu(x):
    return x * _sigmoid(x)


def _dot(a, b):
    return jnp.dot(a, b, preferred_element_type=f32)


def _dot_nt(a, b):
    return lax.dot_general(a, b, (((1,), (1,)), ((), ())), preferred_element_type=f32)


def _dot_tn(a, b):
    return lax.dot_general(a, b, (((0,), (0,)), ((), ())), preferred_element_type=f32)


def _split3(a):
    hi = a.astype(bf16)
    lo = (a - hi.astype(f32)).astype(bf16)
    return hi, lo


def _mm3(a, b):
    ah, al = a
    bh, bl = b
    return _dot(ah, bh) + (_dot(ah, bl) + _dot(al, bh))


def _dot_exact(a, b):
    return jnp.dot(a, b, preferred_element_type=f32, precision=lax.Precision.HIGHEST)


def _params(vmem_mb=None, n_axes=1):
    kw = dict(dimension_semantics=("arbitrary",) * n_axes)
    if vmem_mb is not None:
        kw["vmem_limit_bytes"] = vmem_mb * 1024 * 1024
    return pltpu.CompilerParams(**kw)


def _ada_kernel(c_ref, w_ref, b_ref, o_ref):
    s = _silu(c_ref[...]).astype(bf16)
    o_ref[...] = _dot(s, w_ref[...].astype(bf16)) + b_ref[...]


def _ada(c_all, w_ada, b_ada):
    m, d = c_all.shape
    n = w_ada.shape[1]
    tn = 1024
    return pl.pallas_call(
        _ada_kernel,
        grid=(n // tn,),
        in_specs=[pl.BlockSpec((m, d), lambda j: (0, 0)),
                  pl.BlockSpec((d, tn), lambda j: (0, j)),
                  pl.BlockSpec((1, tn), lambda j: (0, j))],
        out_specs=pl.BlockSpec((m, tn), lambda j: (0, j)),
        out_shape=jax.ShapeDtypeStruct((m, n), f32),
        compiler_params=_params(48),
        name="ada",
    )(c_all, w_ada, b_ada.reshape(1, n))


def _mod_spec(group, d, rows_per_batch, tm, col):
    if group == "prompt":
        return pl.BlockSpec((None, 1, d), lambda i, *_: ((i * tm) // rows_per_batch, 0, col))
    return pl.BlockSpec((rows_per_batch, d), lambda i, *_: (0, col))


def _prep_kernel(x_ref, g_ref, sc_ref, sh_ref, o_ref):
    x = x_ref[...]
    y = x * lax.rsqrt(jnp.mean(x * x, axis=-1, keepdims=True) + NORM_EPS)
    o_ref[...] = ((y * g_ref[...]) * (1.0 + sc_ref[...]) + sh_ref[...]).astype(bf16)


def _prep(x, g, ada, group, rows_per_batch, tm):
    m, d = x.shape
    return pl.pallas_call(
        _prep_kernel,
        grid=(m // tm,),
        in_specs=[pl.BlockSpec((tm, d), lambda i: (i, 0)),
                  pl.BlockSpec((1, d), lambda i: (0, 0)),
                  _mod_spec(group, d, rows_per_batch, tm, 1),
                  _mod_spec(group, d, rows_per_batch, tm, 0)],
        out_specs=pl.BlockSpec((tm, d), lambda i: (i, 0)),
        out_shape=jax.ShapeDtypeStruct((m, d), bf16),
        compiler_params=_params(),
        name="prep_" + group,
    )(x, g.reshape(1, d), ada, ada)


def _mm_kernel(a_ref, w_ref, o_ref):
    o_ref[...] = _dot(a_ref[...], w_ref[...])


def _matmul(a, w, tm, tn, name):
    m, k = a.shape
    n = w.shape[1]
    return pl.pallas_call(
        _mm_kernel,
        grid=(n // tn, m // tm),
        in_specs=[pl.BlockSpec((tm, k), lambda j, i: (i, 0)),
                  pl.BlockSpec((k, tn), lambda j, i: (0, j))],
        out_specs=pl.BlockSpec((tm, tn), lambda j, i: (i, j)),
        out_shape=jax.ShapeDtypeStruct((m, n), f32),
        compiler_params=_params(48, 2),
        name=name,
    )(a, w)


def _l2n(x):
    return x * lax.rsqrt(jnp.sum(x * x, axis=-1, keepdims=True) + NORM_EPS)


def _gdn_gates(sm, alog, dtb):
    xa = sm + dtb
    softplus = jnp.maximum(xa, 0.0) + jnp.log1p(jnp.exp(-jnp.abs(xa)))
    return -jnp.exp(alog) * softplus, _sigmoid(sm)


def _gated_norm(o, gn, z):
    y = o * lax.rsqrt(jnp.mean(o * o, axis=-1, keepdims=True) + NORM_EPS)
    return (y * gn) * _silu(z)


def _gdn_prompt_kernel(qkv_ref, halo_ref, z_ref, sm_ref, wc_ref, alog_ref, dtb_ref, gn_ref,
                       o_ref, sfin_ref, s_ref):
    c = pl.program_id(1)
    n_seq, C, _ = qkv_ref.shape

    @pl.when(c == 0)
    def _():
        s_ref[...] = jnp.zeros_like(s_ref)

    keep = (c > 0).astype(f32)
    ri = lax.broadcasted_iota(jnp.int32, (C, C), 0)
    ci = lax.broadcasted_iota(jnp.int32, (C, C), 1)
    tril = (ri >= ci).astype(f32)
    gates = [_gdn_gates(sm_ref[s], alog_ref[...], dtb_ref[...]) for s in range(n_seq)]
    beta_all = [g[1] for g in gates]
    gc_all = [_dot_exact(tril, g[0]) for g in gates]
    gc_t = [g.T for g in gc_all]

    def conv(s, col):
        xs = jnp.concatenate([halo_ref[s, :, col:col + LANES] * keep, qkv_ref[s, :, col:col + LANES]], axis=0)
        w = wc_ref[:, col:col + LANES]
        y = xs[SUBLANES:] * w[GDN_CONV - 1:GDN_CONV]
        for i in range(GDN_CONV - 1):
            sft = GDN_CONV - 1 - i
            y = y + xs[SUBLANES - sft:SUBLANES - sft + C] * w[i:i + 1]
        return _silu(y)

    N = GDN_GROUP * C
    rn = lax.broadcasted_iota(jnp.int32, (N, N), 0)
    cn = lax.broadcasted_iota(jnp.int32, (N, N), 1)
    same = (rn // C) == (cn // C)
    incl = jnp.logical_and(same, rn >= cn)
    strict = jnp.logical_and(same, rn > cn)
    eye_f = (rn == cn).astype(f32)
    n_sq = max(1, (C - 1).bit_length() - 1)
    groups = [(s, list(range(g0, g0 + GDN_GROUP))) for s in range(n_seq)
              for g0 in range(0, GDN_HEADS, GDN_GROUP)]
    stack = lambda xs: jnp.concatenate(xs, axis=0)

    qs = [stack([_l2n(conv(s, h * GDN_DK)) * (GDN_DK ** -0.5) for h in hs]) for s, hs in groups]
    ks = [stack([_l2n(conv(s, GDN_HEADS * GDN_DK + h * GDN_DK)) for h in hs]) for s, hs in groups]
    vs = [stack([conv(s, 2 * GDN_HEADS * GDN_DK + h * GDN_DV) for h in hs]) for s, hs in groups]
    gcs = [stack([gc_all[s][:, SM_A + h:SM_A + h + 1] for h in hs]) for s, hs in groups]
    betas = [stack([beta_all[s][:, SM_B + h:SM_B + h + 1] for h in hs]) for s, hs in groups]
    gc_rows = [jnp.concatenate([gc_t[s][SM_A + h:SM_A + h + 1, :] for h in hs], axis=1)
               for s, hs in groups]
    egs = [jnp.exp(gc) for gc in gcs]
    decays = [jnp.where(incl, jnp.exp(jnp.where(incl, gc - gr, 0.0)), 0.0) for gc, gr in zip(gcs, gc_rows)]
    kbs = [k.astype(bf16) for k in ks]
    a_s = [jnp.where(strict, b * _dot_nt(kb, kb) * dc, 0.0) for b, kb, dc in zip(betas, kbs, decays)]
    qks = [(_dot_nt(q.astype(bf16), kb) * dc).astype(bf16) for q, kb, dc in zip(qs, kbs, decays)]

    invs = [eye_f - a for a in a_s]
    pws = [a.astype(bf16) for a in a_s]
    for _ in range(n_sq):
        pws = [_dot(p, p).astype(bf16) for p in pws]
        invs = [inv + _dot(inv.astype(bf16), p) for inv, p in zip(invs, pws)]
    inv_s = [_split3(inv) for inv in invs]
    res = [eye_f - inv - _mm3(_split3(a), sp) for a, inv, sp in zip(a_s, invs, inv_s)]
    invs = [inv + _dot(sp[0], r.astype(bf16)) for inv, sp, r in zip(invs, inv_s, res)]
    rhs = [jnp.concatenate([b * v, (b * eg) * k], axis=1) for b, v, eg, k in zip(betas, vs, egs, ks)]
    sols = [_mm3(_split3(inv), _split3(r)) for inv, r in zip(invs, rhs)]

    for gi, (s, hs) in enumerate(groups):
        sol, q, k, gc, eg = sols[gi], qs[gi], ks[gi], gcs[gi], egs[gi]
        q_dec = (q * eg).astype(bf16)
        w_k = sol[:, GDN_DV:].astype(bf16)
        rows = [slice(j * C, (j + 1) * C) for j in range(GDN_GROUP)]
        s_old = [s_ref[s, h] for h in hs]
        ws = [_dot(jnp.concatenate([w_k[r], q_dec[r]], axis=0), so.astype(bf16)) for r, so in zip(rows, s_old)]
        u = stack([sol[r, :GDN_DV] - w[:C] for r, w in zip(rows, ws)])
        ub = u.astype(bf16)
        o_intra = _dot(qks[gi], ub)
        for j, h in enumerate(hs):
            r = rows[j]
            gl = gc[r][C - 1:C, :]
            k_end = (k[r] * jnp.exp(gl - gc[r])).astype(bf16)
            s_ref[s, h] = s_old[j] * jnp.exp(gl) + _dot_tn(k_end, ub[r])
            o = ws[j][C:] + o_intra[r]
            zh = z_ref[s, :, h * GDN_DV:(h + 1) * GDN_DV]
            o_ref[s, :, h * GDN_DV:(h + 1) * GDN_DV] = _gated_norm(o, gn_ref[...], zh).astype(bf16)

    @pl.when(c == pl.num_programs(1) - 1)
    def _():
        sfin_ref[...] = s_ref[...]


def _gdn_vecs(a_log, dt_bias, g_gdn_norm):
    alog = jnp.zeros((1, LANES), f32).at[0, SM_A:SM_A + GDN_HEADS].set(a_log)
    dtb = jnp.zeros((1, LANES), f32).at[0, SM_A:SM_A + GDN_HEADS].set(dt_bias)
    return alog, dtb, g_gdn_norm.reshape(1, GDN_DV)


def _gdn_prompt(y, y_small, w_conv, a_log, dt_bias, g_gdn_norm, nb, t):
    C = GDN_CHUNK
    nc = t // C
    ns = GDN_SEQS if nb % GDN_SEQS == 0 else 1
    alog, dtb, gn = _gdn_vecs(a_log, dt_bias, g_gdn_norm)
    cst = lambda b, c: (0, 0)
    y3 = y.reshape(nb, t, -1)
    ysm3 = y_small.reshape(nb, t, LANES)
    o, s_fin = pl.pallas_call(
        _gdn_prompt_kernel,
        grid=(nb // ns, nc),
        in_specs=[pl.BlockSpec((ns, C, QKV_W), lambda b, c: (b, c, 0)),
                  pl.BlockSpec((ns, SUBLANES, QKV_W),
                               lambda b, c: (b, jnp.maximum(c * (C // SUBLANES) - 1, 0), 0)),
                  pl.BlockSpec((ns, C, Z_W), lambda b, c: (b, c, OFF_Z // Z_W)),
                  pl.BlockSpec((ns, C, LANES), lambda b, c: (b, c, 0)),
                  pl.BlockSpec((GDN_CONV, QKV_W), cst),
                  pl.BlockSpec((1, LANES), cst),
                  pl.BlockSpec((1, LANES), cst),
                  pl.BlockSpec((1, GDN_DV), cst)],
        out_specs=[pl.BlockSpec((ns, C, Z_W), lambda b, c: (b, c, 0)),
                   pl.BlockSpec((ns, GDN_HEADS, GDN_DK, GDN_DV), lambda b, c: (b, 0, 0, 0))],
        out_shape=[jax.ShapeDtypeStruct((nb, t, Z_W), bf16),
                   jax.ShapeDtypeStruct((nb, GDN_HEADS, GDN_DK, GDN_DV), f32)],
        scratch_shapes=[pltpu.VMEM((ns, GDN_HEADS, GDN_DK, GDN_DV), f32)],
        compiler_params=_params(None, 2),
        name="gdn_prompt",
    )(y3, y3, y3, ysm3, w_conv, alog, dtb, gn)
    return o.reshape(nb * t, Z_W), s_fin


def _gdn_sample_kernel(qkv_ref, buf_ref, z_ref, sm_ref, sin_ref, wc_ref, alog_ref, dtb_ref, gn_ref,
                       o_ref, sout_ref, q_s, k_s, v_s, a_s, b_s, o_s):
    ts, G, _ = qkv_ref.shape
    nbuf = GDN_CONV - 1

    for h in range(GDN_HEADS):
        for part, dst in ((0, q_s), (1, k_s), (2, v_s)):
            col = part * GDN_HEADS * GDN_DK + h * GDN_DK
            w = wc_ref[:, col:col + LANES]
            rows = [buf_ref[i, :, col:col + LANES] for i in range(nbuf)]
            rows += [qkv_ref[t, :, col:col + LANES] for t in range(ts)]
            for t in range(ts):
                y = rows[t] * w[0:1]
                for i in range(1, GDN_CONV):
                    y = y + rows[t + i] * w[i:i + 1]
                y = _silu(y)
                if part == 0:
                    y = _l2n(y) * (GDN_DK ** -0.5)
                elif part == 1:
                    y = _l2n(y)
                dst[t, :, h * LANES:(h + 1) * LANES] = y

    for t in range(ts):
        g_all, beta_all = _gdn_gates(sm_ref[t], alog_ref[...], dtb_ref[...])
        a_all = jnp.exp(g_all)
        for h in range(GDN_HEADS):
            a_s[t, :, h * LANES:(h + 1) * LANES] = jnp.broadcast_to(a_all[:, SM_A + h:SM_A + h + 1], (G, LANES))
            b_s[t, :, h * LANES:(h + 1) * LANES] = jnp.broadcast_to(beta_all[:, SM_B + h:SM_B + h + 1], (G, LANES))

    nv = 2 * ts
    sr = lax.broadcasted_iota(jnp.int32, (3 * nv, nv * LANES), 0)
    sc = lax.broadcasted_iota(jnp.int32, (3 * nv, nv * LANES), 1)
    spread = (sr % nv == sc // LANES).astype(bf16)

    def head(h, carry):
        cols = pl.ds(pl.multiple_of(h * LANES, LANES), LANES)
        for i in range(G):
            row = slice(i, i + 1)
            kq = jnp.concatenate([k_s[t, row, cols] for t in range(ts)]
                                 + [q_s[t, row, cols] for t in range(ts)], axis=0)
            hi = kq.astype(bf16).astype(f32)
            mid = (kq - hi).astype(bf16).astype(f32)
            lo = (kq - hi) - mid
            kq_b = _dot(jnp.concatenate([hi, mid, lo], axis=0).T.astype(bf16), spread)
            s = sin_ref[i, h]
            for t in range(ts):
                kc = kq_b[:, t * LANES:(t + 1) * LANES]
                qc = kq_b[:, (ts + t) * LANES:(ts + t + 1) * LANES]
                a = a_s[t, row, cols]
                b = b_s[t, row, cols]
                ks = jnp.sum(s * kc, axis=0, keepdims=True)
                r = b * (v_s[t, row, cols] - a * ks)
                s = a * s + kc * r
                o_s[t, row, cols] = jnp.sum(s * qc, axis=0, keepdims=True)
            sout_ref[i, h] = s
        return carry

    lax.fori_loop(0, GDN_HEADS, head, 0)

    for t in range(ts):
        for h in range(GDN_HEADS):
            sl = slice(h * GDN_DV, (h + 1) * GDN_DV)
            o_ref[t, :, sl] = _gated_norm(o_s[t, :, sl], gn_ref[...], z_ref[t, :, sl]).astype(bf16)


def _gdn_sample(y3, y3_small, buf3, state, w_conv, a_log, dt_bias, g_gdn_norm):
    ts, db, _ = y3.shape
    G = SUBLANES
    alog, dtb, gn = _gdn_vecs(a_log, dt_bias, g_gdn_norm)
    cst = lambda g: (0, 0)
    st_spec = pl.BlockSpec((G, GDN_HEADS, GDN_DK, GDN_DV), lambda g: (g, 0, 0, 0))
    scr = pltpu.VMEM((ts, G, Z_W), f32)
    return pl.pallas_call(
        _gdn_sample_kernel,
        grid=(db // G,),
        in_specs=[pl.BlockSpec((ts, G, QKV_W), lambda g: (0, g, 0)),
                  pl.BlockSpec((GDN_CONV - 1, G, QKV_W), lambda g: (0, g, 0)),
                  pl.BlockSpec((ts, G, Z_W), lambda g: (0, g, OFF_Z // Z_W)),
                  pl.BlockSpec((ts, G, LANES), lambda g: (0, g, 0)),
                  st_spec,
                  pl.BlockSpec((GDN_CONV, QKV_W), cst),
                  pl.BlockSpec((1, LANES), cst),
                  pl.BlockSpec((1, LANES), cst),
                  pl.BlockSpec((1, GDN_DV), cst)],
        out_specs=[pl.BlockSpec((ts, G, Z_W), lambda g: (0, g, 0)), st_spec],
        out_shape=[jax.ShapeDtypeStruct((ts, db, Z_W), bf16),
                   jax.ShapeDtypeStruct(state.shape, f32)],
        scratch_shapes=[scr, scr, scr, scr, scr, scr],
        compiler_params=_params(48),
        name="gdn_sample",
    )(y3, buf3, y3, y3_small, state, w_conv, alog, dtb, gn)


def _rope_tables(pos, rot, width):
    half = rot // 2
    inv_freq = ROPE_THETA ** (-jnp.arange(half, dtype=f32) * (2.0 / rot))
    ang = pos.astype(f32)[:, None] * inv_freq[None, :]
    cos, sin = jnp.cos(ang), jnp.sin(ang)
    n = pos.shape[0]
    z = lambda w: jnp.zeros((n, w), f32)
    cosw = jnp.concatenate([cos, cos, jnp.ones((n, width - rot), f32)], axis=1)
    sina = jnp.concatenate([-sin, z(width - half)], axis=1)
    sinb = jnp.concatenate([z(half), sin, z(width - rot)], axis=1)
    reps = LANES // width
    return tuple(jnp.tile(a, (1, reps)) for a in (cosw, sina, sinb))


def _rope_kernel(dq_ref, dk_ref, dv_ref, iq_ref, sm_ref, c1, sa1, sb1, c2, sa2, sb2,
                 dq_o, dk_o, iq_o, ik_o, sm_o, kc_o, vc_o):
    tm = dq_ref.shape[0]
    h1 = DSA_HEAD_DIM // ROPE_FRACTION // 2
    h2 = IDX_DIM // ROPE_FRACTION // 2

    def rot(x, c, sa, sb, half):
        return x * c[...] + pltpu.roll(x, LANES - half, 1) * sa[...] + pltpu.roll(x, half, 1) * sb[...]

    for j in range(DQ_W // LANES):
        sl = slice(j * LANES, (j + 1) * LANES)
        dq_o[:, sl] = rot(dq_ref[:, sl], c1, sa1, sb1, h1).astype(bf16)
    for g in range(DSA_KV_HEADS):
        sl = slice(g * DSA_HEAD_DIM, (g + 1) * DSA_HEAD_DIM)
        kg = rot(dk_ref[:, sl], c1, sa1, sb1, h1)
        dk_o[:, sl] = kg
        kc_o[pl.ds(g, tm, stride=DSA_KV_HEADS), :] = kg
        vc_o[pl.ds(g, tm, stride=DSA_KV_HEADS), :] = dv_ref[:, sl]
    for j in range(IQ_W // LANES):
        sl = slice(j * LANES, (j + 1) * LANES)
        iq_o[:, sl] = rot(iq_ref[:, sl], c2, sa2, sb2, h2).astype(bf16)
    sm = sm_ref[...]
    ik_o[...] = rot(sm, c2, sa2, sb2, h2)[:, :IDX_DIM]
    sm_o[...] = sm


def _rope(y, y_small, pos, tm, group, db=None):
    m = y.shape[0]
    t1 =_rope_tables(pos, DSA_HEAD_DIM // ROPE_FRACTION, DSA_HEAD_DIM)
    t2 = _rope_tables(pos, IDX_DIM // ROPE_FRACTION, IDX_DIM)
    tab_blocks = pos.shape[0] // tm
    tab = pl.BlockSpec((tm, LANES), lambda i: (i % tab_blocks, 0))
    if group == "prompt":
        omap = lambda i: (i, 0)
        rows = lambda w: m
        cols = lambda w: w
    else:
        ts = m // db
        omap = lambda i: (0, i)
        rows = lambda w: db
        cols = lambda w: ts * w
    out = lambda w, dt: jax.ShapeDtypeStruct((rows(w), cols(w)), dt)
    return pl.pallas_call(
        _rope_kernel,
        grid=(m // tm,),
        in_specs=[pl.BlockSpec((tm, DQ_W), lambda i: (i, OFF_DQ // DQ_W)),
                  pl.BlockSpec((tm, DKV_W), lambda i: (i, OFF_DK // DKV_W)),
                  pl.BlockSpec((tm, DKV_W), lambda i: (i, OFF_DV // DKV_W)),
                  pl.BlockSpec((tm, IQ_W), lambda i: (i, OFF_IQ // IQ_W)),
                  pl.BlockSpec((tm, LANES), lambda i: (i, 0)),
                  tab, tab, tab, tab, tab, tab],
        out_specs=[pl.BlockSpec((tm, DQ_W), omap),
                   pl.BlockSpec((tm, DKV_W), omap),
                   pl.BlockSpec((tm, IQ_W), omap),
                   pl.BlockSpec((tm, IDX_DIM), lambda i: (i, 0)),
                   pl.BlockSpec((tm, LANES), omap),
                   pl.BlockSpec((DSA_KV_HEADS * tm, DSA_HEAD_DIM), lambda i: (i, 0)),
                   pl.BlockSpec((DSA_KV_HEADS * tm, DSA_HEAD_DIM), lambda i: (i, 0))],
        out_shape=[out(DQ_W, bf16), out(DKV_W, f32), out(IQ_W, bf16),
                   jax.ShapeDtypeStruct((m, IDX_DIM), f32), out(LANES, f32),
                   jax.ShapeDtypeStruct((DSA_KV_HEADS * m, DSA_HEAD_DIM), f32),
                   jax.ShapeDtypeStruct((DSA_KV_HEADS * m, DSA_HEAD_DIM), f32)],
        compiler_params=_params(),
        name="rope_" + group,
    )(y, y, y, y, y_small, *t1, *t2)


def _topk_bias(x_s, bias_s, lo0, mx, few, nkt, kt_w, k):
    R = x_s.shape[0]
    kf = float(k)
    tiles = [slice(kt * kt_w, (kt + 1) * kt_w) for kt in range(nkt)]
    n_grp = 2 if R % (2 * SUBLANES) == 0 else 1
    grps = [slice(g * (R // n_grp), (g + 1) * (R // n_grp)) for g in range(n_grp)]

    def count_ge(rows, th):
        acc = jnp.where(x_s[rows, tiles[0]] >= th, 1.0, 0.0)
        for t in tiles[1:]:
            acc = acc + jnp.where(x_s[rows, t] >= th, 1.0, 0.0)
        return jnp.sum(acc, axis=1, keepdims=True)

    def bisect(rows, lo, hi, done, last):
        mid = 0.5 * lo + 0.5 * hi
        cnt = count_ge(rows, mid)
        if last:
            stuck = jnp.logical_or(mid <= lo, mid >= hi)
            done = jnp.where(jnp.logical_or(stuck, cnt == kf), 1.0, done)
        return jnp.where(cnt >= kf, mid, lo), jnp.where(cnt <= kf, mid, hi), done

    def snap(rows, lo, hi, done):
        a = b = None
        for t in tiles:
            x = x_s[rows, t]
            at = jnp.where(x >= lo, x, POS)
            bt = jnp.where(x < hi, x, NEG)
            a = at if a is None else jnp.minimum(a, at)
            b = bt if b is None else jnp.maximum(b, bt)
        a = jnp.min(a, axis=1, keepdims=True)
        b = jnp.max(b, axis=1, keepdims=True)
        live = done < 0.5
        return jnp.where(live, a, lo), jnp.where(jnp.logical_and(live, a >= b), 1.0, done)

    def cond(c):
        it, st = c
        left = st[0][2]
        for s in st[1:]:
            left = jnp.minimum(left, s[2])
        return jnp.logical_and(it < BISECT_CAP, jnp.min(left) < 0.5)

    def body(c):
        it, st = c
        for u in range(BISECT_UNROLL):
            st = [bisect(rows, *s, last=u == BISECT_UNROLL - 1) for rows, s in zip(grps, st)]

        def snapped():
            out = []
            for rows, (lo, hi, done) in zip(grps, st):
                lo2, done2 = snap(rows, lo, hi, done)
                out.append((lo2, hi, done2))
            return out

        st = lax.cond(it >= SNAP_FROM, snapped, lambda: st)
        return it + 1, st

    hi0 = jnp.where(few, lo0, mx + (jnp.abs(mx) + 1.0))
    st0 = [(lo0[rows], hi0[rows], few[rows].astype(f32)) for rows in grps]
    _, st = lax.while_loop(cond, body, (jnp.int32(0), st0))
    lo = jnp.concatenate([s[0] for s in st], axis=0)
    hi = jnp.concatenate([s[1] for s in st], axis=0)

    has_run = jnp.max(jnp.where(lo < hi, 1.0, 0.0)) > 0.5
    rows_all = slice(0, R)

    @pl.when(jnp.logical_not(has_run))
    def _():
        for t in tiles:
            bias_s[:, t] = jnp.where(x_s[:, t] >= hi, 0.0, NEG)

    @pl.when(has_run)
    def _():
        need = kf - count_ge(rows_all, hi)
        ai = lax.broadcasted_iota(jnp.int32, (kt_w, kt_w), 0)
        bi = lax.broadcasted_iota(jnp.int32, (kt_w, kt_w), 1)
        before = (ai < bi).astype(bf16)
        seen = jnp.zeros((R, 1), f32)
        for t in tiles:
            x = x_s[:, t]
            run = jnp.logical_and(x >= lo, x < hi)
            runf = run.astype(f32)
            rank = seen + _dot(runf.astype(bf16), before)
            take = jnp.logical_or(x >= hi, jnp.logical_and(run, rank < need))
            bias_s[:, t] = jnp.where(take, 0.0, NEG)
            seen = seen + jnp.sum(runf, axis=1, keepdims=True)


def _dsa_prompt_kernel(iq_ref, sm_ref, ik_ref, q_ref, k_ref, v_ref, o_ref,
                       ikb, kb, vb, x_s, bias_s, s_s, *, topk, kt_w, ext_w):
    qi = pl.program_id(1)
    R = iq_ref.shape[0]
    T = ik_ref.shape[0]

    @pl.when(qi == 0)
    def _():
        ikb[...] = ik_ref[...].astype(bf16)
        kb[...] = k_ref[...].astype(bf16)
        vb[...] = v_ref[...].astype(bf16)

    def block(ext):
        tiles = [slice(j * kt_w, (j + 1) * kt_w) for j in range(ext // kt_w)]
        wgt = sm_ref[...] * IDX_SCALE
        qpos = qi * R + lax.broadcasted_iota(jnp.int32, (R, kt_w), 0)
        key0 = lax.broadcasted_iota(jnp.int32, (R, kt_w), 1)
        mn = mx = None
        for j, t in enumerate(tiles):
            keys = ikb[t, :]
            sc = jnp.zeros((R, kt_w), f32)
            for h in range(IDX_HEADS):
                rel = jnp.maximum(_dot_nt(iq_ref[:, h * IDX_DIM:(h + 1) * IDX_DIM], keys), 0.0)
                sc = sc + rel * wgt[:, SM_IW + h:SM_IW + h + 1]
            causal = key0 + j * kt_w <= qpos
            x_s[:, t] = jnp.where(causal, sc, NEG)
            lo_t = jnp.where(causal, sc, POS)
            hi_t = jnp.where(causal, sc, NEG)
            mn = lo_t if mn is None else jnp.minimum(mn, lo_t)
            mx = hi_t if mx is None else jnp.maximum(mx, hi_t)
        n_causal = qi * R + lax.broadcasted_iota(jnp.int32, (R, 1), 0) + 1
        _topk_bias(x_s, bias_s, jnp.min(mn, axis=1, keepdims=True), jnp.max(mx, axis=1, keepdims=True),
                   n_causal <= topk, len(tiles), kt_w, topk)

        def heads(hp, carry):
            h0 = hp * DSA_HEAD_GROUP
            g = h0 // (DSA_HEADS // DSA_KV_HEADS)
            gc = pl.ds(pl.multiple_of(g * DSA_HEAD_DIM, DSA_HEAD_DIM), DSA_HEAD_DIM)
            hcs = [pl.ds(pl.multiple_of((h0 + i) * DSA_HEAD_DIM, DSA_HEAD_DIM), DSA_HEAD_DIM)
                   for i in range(DSA_HEAD_GROUP)]
            qs = [q_ref[:, hc] for hc in hcs]
            mxa = [None] * DSA_HEAD_GROUP
            for t in tiles:
                kt = kb[t, gc]
                bt = bias_s[:, t]
                for i in range(DSA_HEAD_GROUP):
                    s = _dot_nt(qs[i], kt) * (DSA_HEAD_DIM ** -0.5) + bt
                    s_s[i, :, t] = s
                    mxa[i] = s if mxa[i] is None else jnp.maximum(mxa[i], s)
            ms = [jnp.max(a, axis=1, keepdims=True) for a in mxa]
            la = [jnp.zeros((R, kt_w), f32)] * DSA_HEAD_GROUP
            acc = [jnp.zeros((R, DSA_HEAD_DIM), f32)] * DSA_HEAD_GROUP
            for t in tiles:
                vt = vb[t, gc]
                for i in range(DSA_HEAD_GROUP):
                    p = jnp.exp(s_s[i, :, t] - ms[i])
                    la[i] = la[i] + p
                    acc[i] = acc[i] + _dot(p.astype(bf16), vt)
            for i in range(DSA_HEAD_GROUP):
                o_ref[:, hcs[i]] = (acc[i] / jnp.sum(la[i], axis=1, keepdims=True)).astype(bf16)
            return carry

        lax.fori_loop(0, DSA_HEADS // DSA_HEAD_GROUP, heads, 0)

    for e in range(T // ext_w):
        pl.when(qi // (ext_w // R) == e)(functools.partial(block, (e + 1) * ext_w))


def _dsa_prompt(iq, sm, ik, dq, dk, y, nb, t, topk):
    R = LANES
    nq = t // R
    kt_w = min(DSA_KEY_TILE, t)
    ext_w = min(DSA_KEY_EXTENT, t)
    kern = functools.partial(_dsa_prompt_kernel, topk=topk, kt_w=kt_w, ext_w=ext_w)
    return pl.pallas_call(
        kern,
        grid=(nb, nq),
        in_specs=[pl.BlockSpec((R, IQ_W), lambda b, i: (b * nq + i, 0)),
                  pl.BlockSpec((R, LANES), lambda b, i: (b * nq + i, 0)),
                  pl.BlockSpec((t, IDX_DIM), lambda b, i: (b, 0)),
                  pl.BlockSpec((R, DQ_W), lambda b, i: (b * nq + i, 0)),
                  pl.BlockSpec((t, DKV_W), lambda b, i: (b, 0)),
                  pl.BlockSpec((t, DKV_W), lambda b, i: (b, OFF_DV // DKV_W))],
        out_specs=pl.BlockSpec((R, DQ_W), lambda b, i: (b * nq + i, 0)),
        out_shape=jax.ShapeDtypeStruct((nb * t, DQ_W), bf16),
        scratch_shapes=[pltpu.VMEM((t, IDX_DIM), bf16), pltpu.VMEM((t, DKV_W), bf16),
                        pltpu.VMEM((t, DKV_W), bf16), pltpu.VMEM((R, t), f32), pltpu.VMEM((R, t), f32),
                        pltpu.VMEM((DSA_HEAD_GROUP, R, t), f32)],
        compiler_params=_params(48, 2),
        name="dsa_prompt",
    )(iq, sm, ik, dq, dk, y)


def _idx_score_kernel(pt_ref, iq_ref, sm_ref, ikn_ref, *rest, n_pages, page):
    eb, tp, _ = iq_ref.shape
    o_ref = rest[eb * n_pages]
    ts = o_ref.shape[1]
    past = n_pages * page
    lp = o_ref.shape[2]
    for e in range(eb):
        pages = rest[e * n_pages:(e + 1) * n_pages]
        keys_t = jnp.concatenate([p[...] for p in pages], axis=1).astype(bf16)
        keys_n = jnp.concatenate([ikn_ref[e], jnp.zeros((lp - past - tp, IDX_DIM), f32)], axis=0).astype(bf16)
        wgt = sm_ref[e] * IDX_SCALE
        iq = iq_ref[e].astype(bf16)
        score = jnp.zeros((tp, lp), f32)
        for h in range(IDX_HEADS):
            iqh = iq[:, h * IDX_DIM:(h + 1) * IDX_DIM]
            rel = jnp.maximum(jnp.concatenate([_dot(iqh, keys_t), _dot_nt(iqh, keys_n)], axis=1), 0.0)
            score = score + rel * wgt[:, SM_IW + h:SM_IW + h + 1]
        o_ref[e] = score[:ts]


def _idx_scores(page_table, iq, sm, ikn, cache_idx, lp, ts):
    db, tp, _ = iq.shape
    n_pages = page_table.shape[1]
    page = cache_idx.shape[2]
    eb = IDX_BATCH
    kern = functools.partial(_idx_score_kernel, n_pages=n_pages, page=page)
    page_specs = [pl.BlockSpec((None, IDX_DIM, page),
                               functools.partial(lambda b, pt, e, p: (pt[b * eb + e, p], 0, 0), e=e, p=p))
                  for e in range(eb) for p in range(n_pages)]
    grid_spec = pltpu.PrefetchScalarGridSpec(
        num_scalar_prefetch=1,
        grid=(db // eb,),
        in_specs=[pl.BlockSpec((eb, tp, IQ_W), lambda b, pt: (b, 0, 0)),
                  pl.BlockSpec((eb, tp, LANES), lambda b, pt: (b, 0, 0)),
                  pl.BlockSpec((eb, tp, IDX_DIM), lambda b, pt: (b, 0, 0))] + page_specs,
        out_specs=pl.BlockSpec((eb, ts, lp), lambda b, pt: (b, 0, 0)),
    )
    return pl.pallas_call(
        kern, grid_spec=grid_spec,
        out_shape=jax.ShapeDtypeStruct((db, ts, lp), f32),
        compiler_params=_params(),
        name="idx_scores_sample",
    )(page_table, iq, sm, ikn, *([cache_idx] * (eb * n_pages)))


def _select_sample_kernel(x_ref, bias_ref, x_s, *, topk, past, ts):
    R, lp = x_ref.shape
    key = lax.broadcasted_iota(jnp.int32, (R, lp), 1)
    t = lax.broadcasted_iota(jnp.int32, (R, lp), 0) % ts
    causal = key <= past + t
    x = x_ref[...]
    x_s[...] = jnp.where(causal, x, NEG)
    lo0 = jnp.min(jnp.where(causal, x, POS), axis=1, keepdims=True)
    mx = jnp.max(jnp.where(causal, x, NEG), axis=1, keepdims=True)
    n_causal = past + lax.broadcasted_iota(jnp.int32, (R, 1), 0) % ts + 1
    _topk_bias(x_s, bias_ref, lo0, mx, n_causal <= topk, lp // LANES, LANES, topk)


def _select_sample(scores, topk, past, ts):
    m, lp = scores.shape
    R = LANES
    kern = functools.partial(_select_sample_kernel, topk=topk, past=past, ts=ts)
    return pl.pallas_call(
        kern, grid=(m // R,),
        in_specs=[pl.BlockSpec((R, lp), lambda i: (i, 0))],
        out_specs=pl.BlockSpec((R, lp), lambda i: (i, 0)),
        out_shape=jax.ShapeDtypeStruct((m, lp), f32),
        scratch_shapes=[pltpu.VMEM((R, lp), f32)],
        compiler_params=_params(),
        name="select_sample",
    )(scores)


def _dsa_sample_kernel(pt_ref, q_ref, sel_ref, kn_ref, vn_ref, *rest, n_pages, page):
    eb, tp, _ = q_ref.shape
    o_ref = rest[2 * eb * n_pages]
    ts = sel_ref.shape[1]
    past = n_pages * page
    lp = sel_ref.shape[2]
    hpg = DSA_HEADS // DSA_KV_HEADS
    padn = jnp.zeros((lp - past - tp, DSA_HEAD_DIM), f32)
    for e in range(eb):
        kp = rest[e * n_pages:(e + 1) * n_pages]
        vp = rest[(eb + e) * n_pages:(eb + e + 1) * n_pages]
        bias = jnp.concatenate([sel_ref[e], jnp.zeros((tp - ts, lp), f32)], axis=0)
        bias = jnp.concatenate([bias] * hpg, axis=0)
        for g in range(DSA_KV_HEADS):
            gsl = slice(g * DSA_HEAD_DIM, (g + 1) * DSA_HEAD_DIM)
            kg = jnp.concatenate([p[pl.ds(g, page, stride=DSA_KV_HEADS), :] for p in kp]
                                 + [kn_ref[e, :, gsl], padn], axis=0).astype(bf16)
            vg = jnp.concatenate([p[pl.ds(g, page, stride=DSA_KV_HEADS), :] for p in vp]
                                 + [vn_ref[e, :, gsl], padn], axis=0).astype(bf16)
            qg = jnp.concatenate([q_ref[e, :, (g * hpg + j) * DSA_HEAD_DIM:(g * hpg + j + 1) * DSA_HEAD_DIM]
                                  for j in range(hpg)], axis=0).astype(bf16)
            s = _dot_nt(qg, kg) * (DSA_HEAD_DIM ** -0.5) + bias
            m = jnp.max(s, axis=1, keepdims=True)
            p = jnp.exp(s - m)
            l = jnp.sum(p, axis=1, keepdims=True)
            o = _dot(p.astype(bf16), vg) / l
            for j in range(hpg):
                hsl = slice((g * hpg + j) * DSA_HEAD_DIM, (g * hpg + j + 1) * DSA_HEAD_DIM)
                o_ref[e, :, hsl] = o[j * tp:j * tp + ts]


def _dsa_sample(page_table, dq, sel, kn, vn, ck, cv):
    db, tp, _ = dq.shape
    ts = sel.shape[1]
    lp = sel.shape[2]
    n_pages = page_table.shape[1]
    rows = ck.shape[1]
    page = rows // DSA_KV_HEADS
    eb = DSA_SAMPLE_BATCH
    kern = functools.partial(_dsa_sample_kernel, n_pages=n_pages, page=page)
    pspec = [pl.BlockSpec((None, rows, DSA_HEAD_DIM),
                          functools.partial(lambda b, pt, e, p: (pt[b * eb + e, p], 0, 0), e=e, p=p))
             for e in range(eb) for p in range(n_pages)]
    per_b = lambda r, w: pl.BlockSpec((eb, r, w), lambda b, pt: (b, 0, 0))
    grid_spec = pltpu.PrefetchScalarGridSpec(
        num_scalar_prefetch=1,
        grid=(db // eb,),
        in_specs=[per_b(tp, DQ_W), per_b(ts, lp), per_b(tp, DKV_W), per_b(tp, DKV_W)] + pspec + pspec,
        out_specs=per_b(ts, DQ_W),
    )
    return pl.pallas_call(
        kern, grid_spec=grid_spec,
        out_shape=jax.ShapeDtypeStruct((db, ts, DQ_W), f32),
        compiler_params=_params(48),
        name="dsa_sample",
    )(page_table, dq, sel, kn, vn, *([ck] * (eb * n_pages)), *([cv] * (eb * n_pages)))


def _merge_kernel(og_ref, od_ref, gg_ref, gd_ref, x_ref, gt_ref, sc_ref, sh_ref, g2_ref,
                  wg_ref, wd_ref, wo_ref, x1_ref, h2_ref):
    mix = (_sigmoid(gg_ref[...]) * _dot(og_ref[...], wg_ref[...])
           + _sigmoid(gd_ref[...]) * _dot(od_ref[...], wd_ref[...]))
    x1 = x_ref[...] + gt_ref[...] * _dot(mix.astype(bf16), wo_ref[...])
    x1_ref[...] = x1
    y = x1 * lax.rsqrt(jnp.mean(x1 * x1, axis=-1, keepdims=True) + NORM_EPS)
    h2_ref[...] = ((y * g2_ref[...]) * (1.0 + sc_ref[...]) + sh_ref[...]).astype(bf16)


def _merge(o_gdn, o_dsa, y, x, ada, g2, wg, wd, wo, group, rows_per_batch, tm):
    m, d = x.shape
    row = lambda w, blk=0: pl.BlockSpec((tm, w), lambda i: (i, blk))
    res = lambda a: pl.BlockSpec(a.shape, lambda i: (0, 0), pipeline_mode=pl.Buffered(1))
    return pl.pallas_call(
        _merge_kernel,
        grid=(m // tm,),
        in_specs=[row(Z_W), row(DQ_W), row(d, 0), row(d, 1), row(d),
                  _mod_spec(group, d, rows_per_batch, tm, 2),
                  _mod_spec(group, d, rows_per_batch, tm, 4),
                  _mod_spec(group, d, rows_per_batch, tm, 3),
                  pl.BlockSpec((1, d), lambda i: (0, 0)),
                  res(wg), res(wd), res(wo)],
        out_specs=[row(d), row(d)],
        out_shape=[jax.ShapeDtypeStruct((m, d), f32), jax.ShapeDtypeStruct((m, d), bf16)],
        compiler_params=_params(48),
        name="merge_" + group,
    )(o_gdn, o_dsa, y, y, x, ada, ada, ada, g2.reshape(1, d), wg, wd, wo)


def _ffn_epilogue(acc_ref, x1_ref, gt_ref, gf_ref, y_ref):
    tm = acc_ref.shape[0]
    r = gt_ref.shape[0] if gt_ref.shape[0] > 1 else tm
    for s0 in range(0, tm, r):
        sl = slice(s0, s0 + r)
        x2 = x1_ref[sl, :] + gt_ref[...] * acc_ref[sl, :]
        y_ref[sl, :] = (x2 * lax.rsqrt(jnp.mean(x2 * x2, axis=-1, keepdims=True) + NORM_EPS)) * gf_ref[...]


def _ffn_prompt_kernel(h_ref, wug_ref, wuv_ref, wcg_ref, wcv_ref, bg_ref, bv_ref, wd_ref,
                       x1_ref, gt_ref, gf_ref, y_ref, ug_ref, uv_ref, acc_ref, act_s, prev_s,
                       *, blocks_per_seq, n_up):
    i = pl.program_id(0)
    j = pl.program_id(1)
    tm = h_ref.shape[0]
    keep = (i % blocks_per_seq) > 0
    sr = min(FFN_SUB_ROWS, tm)

    @pl.when(jnp.logical_and(i == 0, j == 0))
    def _():
        prev_s[...] = jnp.zeros_like(prev_s)

    def conv(u, prev, wc_ref, b_ref):
        ux = jnp.concatenate([prev, u], axis=0)
        w = wc_ref[...]
        y = u * w[FFN_CONV - 1:FFN_CONV] + b_ref[...]
        for t in range(FFN_CONV - 1):
            sft = FFN_CONV - 1 - t
            y = y + ux[SUBLANES - sft:SUBLANES - sft + sr] * w[t:t + 1]
        return y

    def up(store):
        pg = jnp.where(keep, prev_s[j, 0], 0.0)
        pv = jnp.where(keep, prev_s[j, 1], 0.0)
        for s in range(tm // sr):
            rows = slice(s * sr, (s + 1) * sr)
            hs = h_ref[rows, :]
            ug = _dot(hs, wug_ref[...])
            uv = _dot(hs, wuv_ref[...])
            store(rows, _silu(conv(ug, pg, wcg_ref, bg_ref)) * conv(uv, pv, wcv_ref, bv_ref))
            pg = ug[sr - SUBLANES:]
            pv = uv[sr - SUBLANES:]
        ug_ref[...] = pg
        uv_ref[...] = pv
        prev_s[j, 0] = pg
        prev_s[j, 1] = pv

    _ffn_phases(j, n_up, up, act_s, wd_ref, acc_ref, x1_ref, gt_ref, gf_ref, y_ref)


def _ffn_phases(j, n_up, up, act_s, wd_ref, acc_ref, x1_ref, gt_ref, gf_ref, y_ref):
    tf = act_s.shape[1] // n_up
    tn = wd_ref.shape[1]

    @pl.when(j < n_up)
    def _():
        cols = pl.ds(pl.multiple_of(j * tf, tf), tf)

        def store(rows, act):
            act_s[rows, cols] = act.astype(bf16)
        up(store)

    @pl.when(j >= n_up)
    def _():
        acc_ref[:, pl.ds(pl.multiple_of((j - n_up) * tn, tn), tn)] = _dot(act_s[...], wd_ref[...])

    @pl.when(j == pl.num_programs(1) - 1)
    def _():
        _ffn_epilogue(acc_ref, x1_ref, gt_ref, gf_ref, y_ref)


def _ffn_sample_kernel(h_ref, bufg_ref, bufv_ref, wug_ref, wuv_ref, wcg_ref, wcv_ref, bg_ref, bv_ref, wd_ref,
                       x1_ref, gt_ref, gf_ref, y_ref, ug_ref, uv_ref, acc_ref, act_s, *, ts, n_up):
    j = pl.program_id(1)
    db = h_ref.shape[0] // ts
    nbuf = FFN_CONV - 1

    def branch(wu_ref, wc_ref, b_ref, buf_ref, ubuf_ref):
        u = _dot(h_ref[...], wu_ref[...])
        rows = [buf_ref[:, r, :] for r in range(nbuf)]
        rows += [u[t * db:(t + 1) * db] for t in range(ts)]
        for r in range(nbuf):
            ubuf_ref[:, r, :] = rows[ts + r]
        w = wc_ref[...]
        outs = []
        for t in range(ts):
            y = rows[t] * w[0:1] + b_ref[...]
            for r in range(1, FFN_CONV):
                y = y + rows[t + r] * w[r:r + 1]
            outs.append(y)
        return jnp.concatenate(outs, axis=0)

    def up(store):
        store(slice(0, ts * db),
              _silu(branch(wug_ref, wcg_ref, bg_ref, bufg_ref, ug_ref))
              * branch(wuv_ref, wcv_ref, bv_ref, bufv_ref, uv_ref))

    _ffn_phases(j, n_up, up, act_s, wd_ref, acc_ref, x1_ref, gt_ref, gf_ref, y_ref)


def _ffn(h2, x1, ada, g_final, w_up, w_conv, b_conv, w_down, group, rows_per_batch, tm, tf, buf=None):
    m, d = h2.shape
    dff = w_down.shape[0]
    nj = dff // tf
    tn = min(FFN_DOWN_TILE, d)
    b2 = b_conv.reshape(1, 2 * dff)
    up = lambda j: jnp.minimum(j, nj - 1)
    common_w = [pl.BlockSpec((d, tf), lambda i, j: (0, up(j))),
                pl.BlockSpec((d, tf), lambda i, j: (0, nj + up(j))),
                pl.BlockSpec((FFN_CONV, tf), lambda i, j: (0, up(j))),
                pl.BlockSpec((FFN_CONV, tf), lambda i, j: (0, nj + up(j))),
                pl.BlockSpec((1, tf), lambda i, j: (0, up(j))),
                pl.BlockSpec((1, tf), lambda i, j: (0, nj + up(j))),
                pl.BlockSpec((dff, tn), lambda i, j: (0, jnp.maximum(j - nj, 0)))]
    tail = [pl.BlockSpec((tm, d), lambda i, j: (i, 0)),
            _mod_spec(group, d, rows_per_batch, tm, 5),
            pl.BlockSpec((1, d), lambda i, j: (0, 0))]
    y_spec = pl.BlockSpec((tm, d), lambda i, j: (i, 0))
    if group == "prompt":
        bps = rows_per_batch // tm
        nb = m // rows_per_batch
        kern = functools.partial(_ffn_prompt_kernel, blocks_per_seq=bps, n_up=nj)
        first = [pl.BlockSpec((tm, d), lambda i, j: (i, 0))]
        ubuf_spec = pl.BlockSpec((None, SUBLANES, tf), lambda i, j: (i, 0, up(j)))
        ubuf_shape = jax.ShapeDtypeStruct((m // tm, SUBLANES, dff), f32)
        args = (h2,)
        extra_scratch = [pltpu.VMEM((nj, 2, SUBLANES, tf), f32)]
    else:
        extra_scratch = []
        ts = m // rows_per_batch
        kern = functools.partial(_ffn_sample_kernel, ts=ts, n_up=nj)
        first = [pl.BlockSpec((tm, d), lambda i, j: (i, 0)),
                 pl.BlockSpec((rows_per_batch, FFN_CONV - 1, tf), lambda i, j: (0, 0, up(j))),
                 pl.BlockSpec((rows_per_batch, FFN_CONV - 1, tf), lambda i, j: (0, 0, nj + up(j)))]
        ubuf_spec = pl.BlockSpec((rows_per_batch, FFN_CONV - 1, tf), lambda i, j: (0, 0, up(j)))
        ubuf_shape = jax.ShapeDtypeStruct((rows_per_batch, FFN_CONV - 1, dff), f32)
        args = (h2, buf, buf)
    return pl.pallas_call(
        kern,
        grid=(m // tm, nj + d // tn),
        in_specs=first + common_w + tail,
        out_specs=[y_spec, ubuf_spec, ubuf_spec],
        out_shape=[jax.ShapeDtypeStruct((m, d), f32), ubuf_shape, ubuf_shape],
        scratch_shapes=[pltpu.VMEM((tm, d), f32), pltpu.VMEM((tm, dff), bf16)] + extra_scratch,
        compiler_params=_params(56, 2),
        name="ffn_" + group,
    )(*args, w_up, w_up, w_conv, w_conv, b2, b2, w_down, x1, ada, g_final.reshape(1, d))


def _split_w_in(w_in, d):
    sizes = (QKV_W, Z_W, GDN_HEADS, GDN_HEADS, DQ_W, DKV_W, DKV_W, IQ_W, IDX_DIM, IDX_HEADS, d, d)
    offs = [0]
    for s in sizes:
        offs.append(offs[-1] + s)
    seg = lambda i, j: w_in[:, offs[i]:offs[j]]
    pad = jnp.zeros((w_in.shape[0], LANES - (IDX_DIM + 2 * GDN_HEADS + IDX_HEADS)), w_in.dtype)
    small = jnp.concatenate([seg(8, 9), seg(2, 4), seg(9, 10), pad], axis=1)
    return tuple(w.astype(bf16) for w in (seg(0, 2), seg(4, 8), seg(10, 12), small))


def _in_proj(h, w_parts, tm, name):
    return tuple(_matmul(h, w, tm, _mm_tile(w.shape[1]), name + "_" + part)
                 for w, part in zip(w_parts, ("gdn", "dsa", "gate", "small")))


def _ffn_tile(dff):
    for tf in (512, 256, 128):
        if dff % tf == 0:
            return tf
    raise ValueError("d_ff must be a multiple of 128")


def _mm_tile(n):
    for tn in (1024, 512, 256, 128):
        if n % tn == 0:
            return tn
    raise ValueError("projection width must be a multiple of 128")


def kernel(x_prompt, x_sample, c_prompt, c_sample, cache_k, cache_v, cache_idx_k, page_table, state_gdn, state_gdn_conv, state_ffn_conv, w_ada, b_ada, g_norm1, w_in, w_gdn_conv, a_log, dt_bias, g_gdn_norm, w_gdn_out, w_dsa_out, w_o, g_norm2, w_up, w_ffn_conv, b_ffn_conv, w_down, g_final):
    nb, t, d = x_prompt.shape
    db, ts, _ = x_sample.shape
    depth = w_ada.shape[0]
    assert depth == 1 and db == LANES and ts >= GDN_CONV - 1
    n_pages = page_table.shape[1]
    page = cache_k.shape[2]
    past = n_pages * page
    dff = w_down.shape[1]
    l = 0

    n_c = nb + db
    pad_c = (-n_c) % SUBLANES
    c_all = jnp.concatenate([c_prompt, c_sample, jnp.zeros((pad_c, d), f32)], axis=0)
    ada = _ada(c_all, w_ada[l], b_ada[l])
    ada_p = ada[:nb].reshape(nb, 1, 6 * d)
    ada_s = ada[nb:nb + db]

    w_in_parts = _split_w_in(w_in[l], d)
    wg = w_gdn_out[l].astype(bf16)
    wd = w_dsa_out[l].astype(bf16)
    wo = w_o[l].astype(bf16)
    wup = w_up[l].astype(bf16)
    wdn = w_down[l].astype(bf16)
    tf = _ffn_tile(dff)

    xp = x_prompt.reshape(nb * t, d)
    tm_p = min(512, t)
    h1 = _prep(xp, g_norm1[l], ada_p, "prompt", t, tm_p)
    yp_gdn, yp_dsa, yp_gate, yp_small = _in_proj(h1, w_in_parts, min(IN_PROJ_ROWS, nb * t), "in_proj_prompt")
    og_p, s_p = _gdn_prompt(yp_gdn, yp_small, w_gdn_conv[l], a_log[l], dt_bias[l], g_gdn_norm[l], nb, t)
    dq_p, dk_p, iq_p, ik_p, sm_p, kc_p, vc_p = _rope(yp_dsa, yp_small, jnp.arange(t), tm_p, "prompt")
    od_p = _dsa_prompt(iq_p, sm_p, ik_p, dq_p, dk_p, yp_dsa, nb, t, min(DSA_TOPK, t // 4))
    x1_p, h2_p = _merge(og_p, od_p, yp_gate, xp, ada_p, g_norm2[l], wg, wd, wo, "prompt", t, min(256, t))
    y_p, ug_p, uv_p = _ffn(h2_p, x1_p, ada_p, g_final, wup, w_ffn_conv[l], b_ffn_conv[l], wdn,
                           "prompt", t, tm_p, tf)

    nfb = FFN_CONV - 1
    bps_p = t // tm_p
    out_p = (
        y_p.reshape(nb, t, d),
        kc_p.reshape(1, nb, t, DSA_KV_HEADS, DSA_HEAD_DIM),
        vc_p.reshape(1, nb, t, DSA_KV_HEADS, DSA_HEAD_DIM),
        ik_p.reshape(1, nb, t, IDX_DIM),
        s_p[None],
        yp_gdn.reshape(nb, t, -1)[:, t - (GDN_CONV - 1):, :QKV_W][None],
        jnp.concatenate([ug_p[bps_p - 1::bps_p, SUBLANES - nfb:], uv_p[bps_p - 1::bps_p, SUBLANES - nfb:]],
                        axis=-1)[None],
    )

    xs = x_sample.transpose(1, 0, 2).reshape(ts * db, d)
    h1s = _prep(xs, g_norm1[l], ada_s, "sample", db, db)
    ys_gdn, ys_dsa, ys_gate, ys_small = _in_proj(h1s, w_in_parts, ts * db, "in_proj_sample")
    ys_gdn3 = ys_gdn.reshape(ts, db, -1)
    og_s, s_s = _gdn_sample(ys_gdn3, ys_small.reshape(ts, db, LANES), state_gdn_conv[l].transpose(1, 0, 2),
                            state_gdn[l], w_gdn_conv[l], a_log[l], dt_bias[l], g_gdn_norm[l])
    pos_s = jnp.repeat(past + jnp.arange(ts), db)
    dq_s, dk_s, iq_s, ik_s, sm_s, _, _ = _rope(ys_dsa, ys_small, pos_s, db, "sample", db)
    ik_s_b = ik_s.reshape(ts, db, IDX_DIM).transpose(1, 0, 2)
    lp = past + LANES
    tp = -(-ts // SUBLANES) * SUBLANES
    pad_t = lambda a: jnp.pad(a.astype(f32), ((0, 0), (0, tp - ts), (0, 0)))
    scores = _idx_scores(page_table, pad_t(iq_s.reshape(db, ts, IQ_W)), pad_t(sm_s.reshape(db, ts, LANES)),
                         pad_t(ik_s_b), jnp.swapaxes(cache_idx_k[l], 1, 2), lp, ts)
    sel = _select_sample(scores.reshape(db * ts, lp), min(DSA_TOPK, (past + ts) // 4), past, ts)
    dv_s_b = ys_dsa.reshape(ts, db, -1)[:, :, OFF_DV:OFF_DV + DKV_W].transpose(1, 0, 2)
    ck = cache_k[l].reshape(cache_k.shape[1], page * DSA_KV_HEADS, DSA_HEAD_DIM)
    cv = cache_v[l].reshape(cache_v.shape[1], page * DSA_KV_HEADS, DSA_HEAD_DIM)
    od_s = _dsa_sample(page_table, pad_t(dq_s.reshape(db, ts, DQ_W)), sel.reshape(db, ts, lp),
                       pad_t(dk_s.reshape(db, ts, DKV_W)), pad_t(dv_s_b), ck, cv)
    od_s = od_s.astype(bf16).transpose(1, 0, 2).reshape(ts * db, DQ_W)
    x1_s, h2_s = _merge(og_s.reshape(ts * db, Z_W), od_s, ys_gate, xs, ada_s, g_norm2[l], wg, wd, wo,
                        "sample", db, db)
    y_s, ug_s, uv_s = _ffn(h2_s, x1_s, ada_s, g_final, wup, w_ffn_conv[l], b_ffn_conv[l], wdn,
                           "sample", db, ts * db, tf, buf=state_ffn_conv[l])

    out_s = (
        y_s.reshape(ts, db, d).transpose(1, 0, 2),
        dk_s.reshape(1, db, ts, DSA_KV_HEADS, DSA_HEAD_DIM),
        dv_s_b.reshape(1, db, ts, DSA_KV_HEADS, DSA_HEAD_DIM),
        ik_s_b[None],
        s_s[None],
        ys_gdn3[ts - (GDN_CONV - 1):, :, :QKV_W].transpose(1, 0, 2)[None],
        jnp.concatenate([ug_s, uv_s], axis=-1)[None],
    )
    return (out_p[0], out_s[0]) + out_p[1:] + out_s[1:]
```

```python
import functools

import jax
import jax.numpy as jnp
from jax import lax
from jax.experimental import pallas as pl
from jax.experimental.pallas import tpu as pltpu

f32 = jnp.float32
bf16 = jnp.bfloat16

GDN_HEADS = 8
GDN_DK = 128
GDN_DV = 128
GDN_CONV = 4
GDN_CHUNK = 64
GDN_GROUP = 4
GDN_SEQS = 4
DSA_HEADS = 8
DSA_KV_HEADS = 2
DSA_HEAD_DIM = 128
IDX_HEADS = 8
IDX_DIM = 64
IDX_SCALE = IDX_HEADS ** -0.5 * IDX_DIM ** -0.5
DSA_TOPK = 256
ROPE_THETA = 500000.0
ROPE_FRACTION = 4
FFN_CONV = 3
NORM_EPS = 1e-6

LANES = 128
SUBLANES = 8
NEG = float(jnp.finfo(jnp.float32).min)
POS = float(jnp.finfo(jnp.float32).max)
BISECT_UNROLL = 4
SNAP_FROM = 4
BISECT_CAP = 1024
DSA_KEY_TILE = 256
DSA_KEY_EXTENT = 256
IN_PROJ_ROWS = 1024
IDX_BATCH = 8
DSA_SAMPLE_BATCH = 4
DSA_RING = 3
DSA_RING_BATCH = 2
DSA_HEAD_GROUP = 4
FFN_DOWN_TILE = 512
FFN_SUB_ROWS = 256

QKV_W =2 * GDN_HEADS * GDN_DK + GDN_HEADS * GDN_DV
Z_W = GDN_HEADS * GDN_DV
DQ_W = DSA_HEADS * DSA_HEAD_DIM
DKV_W = DSA_KV_HEADS * DSA_HEAD_DIM
IQ_W = IDX_HEADS * IDX_DIM
OFF_Z = QKV_W
GDN_PROJ_W = QKV_W + Z_W
OFF_DQ = 0
OFF_DK = OFF_DQ + DQ_W
OFF_DV = OFF_DK + DKV_W
OFF_IQ = OFF_DV + DKV_W
DSA_PROJ_W = OFF_IQ + IQ_W
SM_IK = 0
SM_A = IDX_DIM
SM_B = SM_A + GDN_HEADS
SM_IW = SM_B + GDN_HEADS


def _sigmoid(x):
    return 1.0 / (1.0 + jnp.exp(-x))


def _silu(x):
    return x * _sigmoid(x)


def _dot(a, b):
    return jnp.dot(a, b, preferred_element_type=f32)


def _dot_nt(a, b):
    return lax.dot_general(a, b, (((1,), (1,)), ((), ())), preferred_element_type=f32)


def _dot_tn(a, b):
    return lax.dot_general(a, b, (((0,), (0,)), ((), ())), preferred_element_type=f32)


def _split3(a):
    hi = a.astype(bf16)
    lo = (a - hi.astype(f32)).astype(bf16)
    return hi, lo


def _mm3(a, b):
    ah, al = a
    bh, bl = b
    return _dot(ah, bh) + (_dot(ah, bl) + _dot(al, bh))


def _dot_exact(a, b):
    return jnp.dot(a, b, preferred_element_type=f32, precision=lax.Precision.HIGHEST)


def _params(vmem_mb=None, n_axes=1):
    kw = dict(dimension_semantics=("arbitrary",) * n_axes)
    if vmem_mb is not None:
        kw["vmem_limit_bytes"] = vmem_mb * 1024 * 1024
    return pltpu.CompilerParams(**kw)


def _ada_kernel(c_ref, w_ref, b_ref, o_ref):
    s = _silu(c_ref[...]).astype(bf16)
    o_ref[...] = _dot(s, w_ref[...].astype(bf16)) + b_ref[...]


def _ada(c_all, w_ada, b_ada):
    m, d = c_all.shape
    n = w_ada.shape[1]
    tn = 1024
    return pl.pallas_call(
        _ada_kernel,
        grid=(n // tn,),
        in_specs=[pl.BlockSpec((m, d), lambda j: (0, 0)),
                  pl.BlockSpec((d, tn), lambda j: (0, j)),
                  pl.BlockSpec((1, tn), lambda j: (0, j))],
        out_specs=pl.BlockSpec((m, tn), lambda j: (0, j)),
        out_shape=jax.ShapeDtypeStruct((m, n), f32),
        compiler_params=_params(48),
        name="ada",
    )(c_all, w_ada, b_ada.reshape(1, n))


def _mod_spec(group, d, rows_per_batch, tm, col):
    if group == "prompt":
        return pl.BlockSpec((None, 1, d), lambda i, *_: ((i * tm) // rows_per_batch, 0, col))
    return pl.BlockSpec((rows_per_batch, d), lambda i, *_: (0, col))


def _prep_kernel(x_ref, g_ref, sc_ref, sh_ref, o_ref):
    x = x_ref[...]
    y = x * lax.rsqrt(jnp.mean(x * x, axis=-1, keepdims=True) + NORM_EPS)
    o_ref[...] = ((y * g_ref[...]) * (1.0 + sc_ref[...]) + sh_ref[...]).astype(bf16)


def _prep(x, g, ada, group, rows_per_batch, tm):
    m, d = x.shape
    return pl.pallas_call(
        _prep_kernel,
        grid=(m // tm,),
        in_specs=[pl.BlockSpec((tm, d), lambda i: (i, 0)),
                  pl.BlockSpec((1, d), lambda i: (0, 0)),
                  _mod_spec(group, d, rows_per_batch, tm, 1),
                  _mod_spec(group, d, rows_per_batch, tm, 0)],
        out_specs=pl.BlockSpec((tm, d), lambda i: (i, 0)),
        out_shape=jax.ShapeDtypeStruct((m, d), bf16),
        compiler_params=_params(),
        name="prep_" + group,
    )(x, g.reshape(1, d), ada, ada)


def _mm_kernel(a_ref, w_ref, o_ref):
    o_ref[...] = _dot(a_ref[...], w_ref[...])


def _matmul(a, w, tm, tn, name):
    m, k = a.shape
    n = w.shape[1]
    return pl.pallas_call(
        _mm_kernel,
        grid=(n // tn, m // tm),
        in_specs=[pl.BlockSpec((tm, k), lambda j, i: (i, 0)),
                  pl.BlockSpec((k, tn), lambda j, i: (0, j))],
        out_specs=pl.BlockSpec((tm, tn), lambda j, i: (i, j)),
        out_shape=jax.ShapeDtypeStruct((m, n), f32),
        compiler_params=_params(48, 2),
        name=name,
    )(a, w)


def _l2n(x):
    return x * lax.rsqrt(jnp.sum(x * x, axis=-1, keepdims=True) + NORM_EPS)


def _gdn_gates(sm, alog, dtb):
    xa = sm + dtb
    softplus = jnp.maximum(xa, 0.0) + jnp.log1p(jnp.exp(-jnp.abs(xa)))
    return -jnp.exp(alog) * softplus, _sigmoid(sm)


def _gated_norm(o, gn, z):
    y = o * lax.rsqrt(jnp.mean(o * o, axis=-1, keepdims=True) + NORM_EPS)
    return (y * gn) * _silu(z)


def _gdn_prompt_kernel(qkv_ref, halo_ref, z_ref, sm_ref, wc_ref, alog_ref, dtb_ref, gn_ref,
                       o_ref, sfin_ref, s_ref):
    c = pl.program_id(1)
    n_seq, C, _ = qkv_ref.shape

    @pl.when(c == 0)
    def _():
        s_ref[...] = jnp.zeros_like(s_ref)

    keep = (c > 0).astype(f32)
    ri = lax.broadcasted_iota(jnp.int32, (C, C), 0)
    ci = lax.broadcasted_iota(jnp.int32, (C, C), 1)
    tril = (ri >= ci).astype(f32)
    gates = [_gdn_gates(sm_ref[s], alog_ref[...], dtb_ref[...]) for s in range(n_seq)]
    beta_all = [g[1] for g in gates]
    gc_all = [_dot_exact(tril, g[0]) for g in gates]
    gc_t = [g.T for g in gc_all]

    def conv(s, col):
        xs = jnp.concatenate([halo_ref[s, :, col:col + LANES] * keep, qkv_ref[s, :, col:col + LANES]], axis=0)
        w = wc_ref[:, col:col + LANES]
        y = xs[SUBLANES:] * w[GDN_CONV - 1:GDN_CONV]
        for i in range(GDN_CONV - 1):
            sft = GDN_CONV - 1 - i
            y = y + xs[SUBLANES - sft:SUBLANES - sft + C] * w[i:i + 1]
        return _silu(y)

    N = GDN_GROUP * C
    rn = lax.broadcasted_iota(jnp.int32, (N, N), 0)
    cn = lax.broadcasted_iota(jnp.int32, (N, N), 1)
    same = (rn // C) == (cn // C)
    incl = jnp.logical_and(same, rn >= cn)
    strict = jnp.logical_and(same, rn > cn)
    eye_f = (rn == cn).astype(f32)
    n_sq = max(1, (C - 1).bit_length() - 1)
    groups = [(s, list(range(g0, g0 + GDN_GROUP))) for s in range(n_seq)
              for g0 in range(0, GDN_HEADS, GDN_GROUP)]
    stack = lambda xs: jnp.concatenate(xs, axis=0)

    qs = [stack([_l2n(conv(s, h * GDN_DK)) * (GDN_DK ** -0.5) for h in hs]) for s, hs in groups]
    ks = [stack([_l2n(conv(s, GDN_HEADS * GDN_DK + h * GDN_DK)) for h in hs]) for s, hs in groups]
    vs = [stack([conv(s, 2 * GDN_HEADS * GDN_DK + h * GDN_DV) for h in hs]) for s, hs in groups]
    gcs = [stack([gc_all[s][:, SM_A + h:SM_A + h + 1] for h in hs]) for s, hs in groups]
    betas = [stack([beta_all[s][:, SM_B + h:SM_B + h + 1] for h in hs]) for s, hs in groups]
    gc_rows = [jnp.concatenate([gc_t[s][SM_A + h:SM_A + h + 1, :] for h in hs], axis=1)
               for s, hs in groups]
    egs = [jnp.exp(gc) for gc in gcs]
    decays = [jnp.where(incl, jnp.exp(jnp.where(incl, gc - gr, 0.0)), 0.0) for gc, gr in zip(gcs, gc_rows)]
    kbs = [k.astype(bf16) for k in ks]
    a_s = [jnp.where(strict, b * _dot_nt(kb, kb) * dc, 0.0) for b, kb, dc in zip(betas, kbs, decays)]
    qks = [(_dot_nt(q.astype(bf16), kb) * dc).astype(bf16) for q, kb, dc in zip(qs, kbs, decays)]

    invs = [eye_f - a for a in a_s]
    pws = [a.astype(bf16) for a in a_s]
    for _ in range(n_sq):
        pws = [_dot(p, p).astype(bf16) for p in pws]
        invs = [inv + _dot(inv.astype(bf16), p) for inv, p in zip(invs, pws)]
    inv_s = [_split3(inv) for inv in invs]
    res = [eye_f - inv - _mm3(_split3(a), sp) for a, inv, sp in zip(a_s, invs, inv_s)]
    invs = [inv + _dot(sp[0], r.astype(bf16)) for inv, sp, r in zip(invs, inv_s, res)]
    rhs = [jnp.concatenate([b * v, (b * eg) * k], axis=1) for b, v, eg, k in zip(betas, vs, egs, ks)]
    sols = [_mm3(_split3(inv), _split3(r)) for inv, r in zip(invs, rhs)]

    for gi, (s, hs) in enumerate(groups):
        sol, q, k, gc, eg = sols[gi], qs[gi], ks[gi], gcs[gi], egs[gi]
        q_dec = (q * eg).astype(bf16)
        w_k = sol[:, GDN_DV:].astype(bf16)
        rows = [slice(j * C, (j + 1) * C) for j in range(GDN_GROUP)]
        s_old = [s_ref[s, h] for h in hs]
        ws = [_dot(jnp.concatenate([w_k[r], q_dec[r]], axis=0), so.astype(bf16)) for r, so in zip(rows, s_old)]
        u = stack([sol[r, :GDN_DV] - w[:C] for r, w in zip(rows, ws)])
        ub = u.astype(bf16)
        o_intra = _dot(qks[gi], ub)
        for j, h in enumerate(hs):
            r = rows[j]
            gl = gc[r][C - 1:C, :]
            k_end = (k[r] * jnp.exp(gl - gc[r])).astype(bf16)
            s_ref[s, h] = s_old[j] * jnp.exp(gl) + _dot_tn(k_end, ub[r])
            o = ws[j][C:] + o_intra[r]
            zh = z_ref[s, :, h * GDN_DV:(h + 1) * GDN_DV]
            o_ref[s, :, h * GDN_DV:(h + 1) * GDN_DV] = _gated_norm(o, gn_ref[...], zh).astype(bf16)

    @pl.when(c == pl.num_programs(1) - 1)
    def _():
        sfin_ref[...] = s_ref[...]


def _gdn_vecs(a_log, dt_bias, g_gdn_norm):
    alog = jnp.zeros((1, LANES), f32).at[0, SM_A:SM_A + GDN_HEADS].set(a_log)
    dtb = jnp.zeros((1, LANES), f32).at[0, SM_A:SM_A + GDN_HEADS].set(dt_bias)
    return alog, dtb, g_gdn_norm.reshape(1, GDN_DV)


def _gdn_prompt(y, y_small, w_conv, a_log, dt_bias, g_gdn_norm, nb, t):
    C = GDN_CHUNK
    nc = t // C
    ns = GDN_SEQS if nb % GDN_SEQS == 0 else 1
    alog, dtb, gn = _gdn_vecs(a_log, dt_bias, g_gdn_norm)
    cst = lambda b, c: (0, 0)
    y3 = y.reshape(nb, t, -1)
    ysm3 = y_small.reshape(nb, t, LANES)
    o, s_fin = pl.pallas_call(
        _gdn_prompt_kernel,
        grid=(nb // ns, nc),
        in_specs=[pl.BlockSpec((ns, C, QKV_W), lambda b, c: (b, c, 0)),
                  pl.BlockSpec((ns, SUBLANES, QKV_W),
                               lambda b, c: (b, jnp.maximum(c * (C // SUBLANES) - 1, 0), 0)),
                  pl.BlockSpec((ns, C, Z_W), lambda b, c: (b, c, OFF_Z // Z_W)),
                  pl.BlockSpec((ns, C, LANES), lambda b, c: (b, c, 0)),
                  pl.BlockSpec((GDN_CONV, QKV_W), cst),
                  pl.BlockSpec((1, LANES), cst),
                  pl.BlockSpec((1, LANES), cst),
                  pl.BlockSpec((1, GDN_DV), cst)],
        out_specs=[pl.BlockSpec((ns, C, Z_W), lambda b, c: (b, c, 0)),
                   pl.BlockSpec((ns, GDN_HEADS, GDN_DK, GDN_DV), lambda b, c: (b, 0, 0, 0))],
        out_shape=[jax.ShapeDtypeStruct((nb, t, Z_W), bf16),
                   jax.ShapeDtypeStruct((nb, GDN_HEADS, GDN_DK, GDN_DV), f32)],
        scratch_shapes=[pltpu.VMEM((ns, GDN_HEADS, GDN_DK, GDN_DV), f32)],
        compiler_params=_params(None, 2),
        name="gdn_prompt",
    )(y3, y3, y3, ysm3, w_conv, alog, dtb, gn)
    return o.reshape(nb * t, Z_W), s_fin


def _gdn_sample_kernel(qkv_ref, buf_ref, z_ref, sm_ref, sin_ref, wc_ref, alog_ref, dtb_ref, gn_ref,
                       o_ref, sout_ref, q_s, k_s, v_s, a_s, b_s, o_s):
    ts, G, _ = qkv_ref.shape
    nbuf = GDN_CONV - 1

    for h in range(GDN_HEADS):
        for part, dst in ((0, q_s), (1, k_s), (2, v_s)):
            col = part * GDN_HEADS * GDN_DK + h * GDN_DK
            w = wc_ref[:, col:col + LANES]
            rows = [buf_ref[i, :, col:col + LANES] for i in range(nbuf)]
            rows += [qkv_ref[t, :, col:col + LANES] for t in range(ts)]
            for t in range(ts):
                y = rows[t] * w[0:1]
                for i in range(1, GDN_CONV):
                    y = y + rows[t + i] * w[i:i + 1]
                y = _silu(y)
                if part == 0:
                    y = _l2n(y) * (GDN_DK ** -0.5)
                elif part == 1:
                    y = _l2n(y)
                dst[t, :, h * LANES:(h + 1) * LANES] = y

    for t in range(ts):
        g_all, beta_all = _gdn_gates(sm_ref[t], alog_ref[...], dtb_ref[...])
        a_all = jnp.exp(g_all)
        for h in range(GDN_HEADS):
            a_s[t, :, h * LANES:(h + 1) * LANES] = jnp.broadcast_to(a_all[:, SM_A + h:SM_A + h + 1], (G, LANES))
            b_s[t, :, h * LANES:(h + 1) * LANES] = jnp.broadcast_to(beta_all[:, SM_B + h:SM_B + h + 1], (G, LANES))

    nv = 2 * ts
    sr = lax.broadcasted_iota(jnp.int32, (3 * nv, nv * LANES), 0)
    sc = lax.broadcasted_iota(jnp.int32, (3 * nv, nv * LANES), 1)
    spread = (sr % nv == sc // LANES).astype(bf16)

    def head(h, carry):
        cols = pl.ds(pl.multiple_of(h * LANES, LANES), LANES)
        for i in range(G):
            row = slice(i, i + 1)
            kq = jnp.concatenate([k_s[t, row, cols] for t in range(ts)]
                                 + [q_s[t, row, cols] for t in range(ts)], axis=0)
            hi = kq.astype(bf16).astype(f32)
            mid = (kq - hi).astype(bf16).astype(f32)
            lo = (kq - hi) - mid
            kq_b = _dot(jnp.concatenate([hi, mid, lo], axis=0).T.astype(bf16), spread)
            s = sin_ref[i, h]
            for t in range(ts):
                kc = kq_b[:, t * LANES:(t + 1) * LANES]
                qc = kq_b[:, (ts + t) * LANES:(ts + t + 1) * LANES]
                a = a_s[t, row, cols]
                b = b_s[t, row, cols]
                ks = jnp.sum(s * kc, axis=0, keepdims=True)
                r = b * (v_s[t, row, cols] - a * ks)
                s = a * s + kc * r
                o_s[t, row, cols] = jnp.sum(s * qc, axis=0, keepdims=True)
            sout_ref[i, h] = s
        return carry

    lax.fori_loop(0, GDN_HEADS, head, 0)

    for t in range(ts):
        for h in range(GDN_HEADS):
            sl = slice(h * GDN_DV, (h + 1) * GDN_DV)
            o_ref[t, :, sl] = _gated_norm(o_s[t, :, sl], gn_ref[...], z_ref[t, :, sl]).astype(bf16)


def _gdn_sample(y3, y3_small, buf3, state, w_conv, a_log, dt_bias, g_gdn_norm):
    ts, db, _ = y3.shape
    G = SUBLANES
    alog, dtb, gn = _gdn_vecs(a_log, dt_bias, g_gdn_norm)
    cst = lambda g: (0, 0)
    st_spec = pl.BlockSpec((G, GDN_HEADS, GDN_DK, GDN_DV), lambda g: (g, 0, 0, 0))
    scr = pltpu.VMEM((ts, G, Z_W), f32)
    return pl.pallas_call(
        _gdn_sample_kernel,
        grid=(db // G,),
        in_specs=[pl.BlockSpec((ts, G, QKV_W), lambda g: (0, g, 0)),
                  pl.BlockSpec((GDN_CONV - 1, G, QKV_W), lambda g: (0, g, 0)),
                  pl.BlockSpec((ts, G, Z_W), lambda g: (0, g, OFF_Z // Z_W)),
                  pl.BlockSpec((ts, G, LANES), lambda g: (0, g, 0)),
                  st_spec,
                  pl.BlockSpec((GDN_CONV, QKV_W), cst),
                  pl.BlockSpec((1, LANES), cst),
                  pl.BlockSpec((1, LANES), cst),
                  pl.BlockSpec((1, GDN_DV), cst)],
        out_specs=[pl.BlockSpec((ts, G, Z_W), lambda g: (0, g, 0)), st_spec],
        out_shape=[jax.ShapeDtypeStruct((ts, db, Z_W), bf16),
                   jax.ShapeDtypeStruct(state.shape, f32)],
        scratch_shapes=[scr, scr, scr, scr, scr, scr],
        compiler_params=_params(48),
        name="gdn_sample",
    )(y3, buf3, y3, y3_small, state, w_conv, alog, dtb, gn)


def _rope_tables(pos, rot, width):
    half = rot // 2
    inv_freq = ROPE_THETA ** (-jnp.arange(half, dtype=f32) * (2.0 / rot))
    ang = pos.astype(f32)[:, None] * inv_freq[None, :]
    cos, sin = jnp.cos(ang), jnp.sin(ang)
    n = pos.shape[0]
    z = lambda w: jnp.zeros((n, w), f32)
    cosw = jnp.concatenate([cos, cos, jnp.ones((n, width - rot), f32)], axis=1)
    sina = jnp.concatenate([-sin, z(width - half)], axis=1)
    sinb = jnp.concatenate([z(half), sin, z(width - rot)], axis=1)
    reps = LANES // width
    return tuple(jnp.tile(a, (1, reps)) for a in (cosw, sina, sinb))


def _rope_kernel(dq_ref, dk_ref, dv_ref, iq_ref, sm_ref, c1, sa1, sb1, c2, sa2, sb2,
                 dq_o, dk_o, iq_o, ik_o, sm_o, kc_o, vc_o):
    tm = dq_ref.shape[0]
    h1 = DSA_HEAD_DIM // ROPE_FRACTION // 2
    h2 = IDX_DIM // ROPE_FRACTION // 2

    def rot(x, c, sa, sb, half):
        return x * c[...] + pltpu.roll(x, LANES - half, 1) * sa[...] + pltpu.roll(x, half, 1) * sb[...]

    for j in range(DQ_W // LANES):
        sl = slice(j * LANES, (j + 1) * LANES)
        dq_o[:, sl] = rot(dq_ref[:, sl], c1, sa1, sb1, h1).astype(bf16)
    for g in range(DSA_KV_HEADS):
        sl = slice(g * DSA_HEAD_DIM, (g + 1) * DSA_HEAD_DIM)
        kg = rot(dk_ref[:, sl], c1, sa1, sb1, h1)
        dk_o[:, sl] = kg
        kc_o[pl.ds(g, tm, stride=DSA_KV_HEADS), :] = kg
        vc_o[pl.ds(g, tm, stride=DSA_KV_HEADS), :] = dv_ref[:, sl]
    for j in range(IQ_W // LANES):
        sl = slice(j * LANES, (j + 1) * LANES)
        iq_o[:, sl] = rot(iq_ref[:, sl], c2, sa2, sb2, h2).astype(bf16)
    sm = sm_ref[...]
    ik_o[...] = rot(sm, c2, sa2, sb2, h2)[:, :IDX_DIM]
    sm_o[...] = sm


def _rope(y, y_small, pos, tm, group, db=None):
    m = y.shape[0]
    t1 =_rope_tables(pos, DSA_HEAD_DIM // ROPE_FRACTION, DSA_HEAD_DIM)
    t2 = _rope_tables(pos, IDX_DIM // ROPE_FRACTION, IDX_DIM)
    tab_blocks = pos.shape[0] // tm
    tab = pl.BlockSpec((tm, LANES), lambda i: (i % tab_blocks, 0))
    if group == "prompt":
        omap = lambda i: (i, 0)
        rows = lambda w: m
        cols = lambda w: w
    else:
        ts = m // db
        omap = lambda i: (0, i)
        rows = lambda w: db
        cols = lambda w: ts * w
    out = lambda w, dt: jax.ShapeDtypeStruct((rows(w), cols(w)), dt)
    return pl.pallas_call(
        _rope_kernel,
        grid=(m // tm,),
        in_specs=[pl.BlockSpec((tm, DQ_W), lambda i: (i, OFF_DQ // DQ_W)),
                  pl.BlockSpec((tm, DKV_W), lambda i: (i, OFF_DK // DKV_W)),
                  pl.BlockSpec((tm, DKV_W), lambda i: (i, OFF_DV // DKV_W)),
                  pl.BlockSpec((tm, IQ_W), lambda i: (i, OFF_IQ // IQ_W)),
                  pl.BlockSpec((tm, LANES), lambda i: (i, 0)),
                  tab, tab, tab, tab, tab, tab],
        out_specs=[pl.BlockSpec((tm, DQ_W), omap),
                   pl.BlockSpec((tm, DKV_W), omap),
                   pl.BlockSpec((tm, IQ_W), omap),
                   pl.BlockSpec((tm, IDX_DIM), lambda i: (i, 0)),
                   pl.BlockSpec((tm, LANES), omap),
                   pl.BlockSpec((DSA_KV_HEADS * tm, DSA_HEAD_DIM), lambda i: (i, 0)),
                   pl.BlockSpec((DSA_KV_HEADS * tm, DSA_HEAD_DIM), lambda i: (i, 0))],
        out_shape=[out(DQ_W, bf16), out(DKV_W, f32), out(IQ_W, bf16),
                   jax.ShapeDtypeStruct((m, IDX_DIM), f32), out(LANES, f32),
                   jax.ShapeDtypeStruct((DSA_KV_HEADS * m, DSA_HEAD_DIM), f32),
                   jax.ShapeDtypeStruct((DSA_KV_HEADS * m, DSA_HEAD_DIM), f32)],
        compiler_params=_params(),
        name="rope_" + group,
    )(y, y, y, y, y_small, *t1, *t2)


def _topk_bias(x_s, bias_s, lo0, mx, few, nkt, kt_w, k):
    R = x_s.shape[0]
    kf = float(k)
    tiles = [slice(kt * kt_w, (kt + 1) * kt_w) for kt in range(nkt)]
    n_grp = 2 if R % (2 * SUBLANES) == 0 else 1
    grps = [slice(g * (R // n_grp), (g + 1) * (R // n_grp)) for g in range(n_grp)]

    def count_ge(rows, th):
        acc = jnp.where(x_s[rows, tiles[0]] >= th, 1.0, 0.0)
        for t in tiles[1:]:
            acc = acc + jnp.where(x_s[rows, t] >= th, 1.0, 0.0)
        return jnp.sum(acc, axis=1, keepdims=True)

    def bisect(rows, lo, hi, done, last):
        mid = 0.5 * lo + 0.5 * hi
        cnt = count_ge(rows, mid)
        if last:
            stuck = jnp.logical_or(mid <= lo, mid >= hi)
            done = jnp.where(jnp.logical_or(stuck, cnt == kf), 1.0, done)
        return jnp.where(cnt >= kf, mid, lo), jnp.where(cnt <= kf, mid, hi), done

    def snap(rows, lo, hi, done):
        a = b = None
        for t in tiles:
            x = x_s[rows, t]
            at = jnp.where(x >= lo, x, POS)
            bt = jnp.where(x < hi, x, NEG)
            a = at if a is None else jnp.minimum(a, at)
            b = bt if b is None else jnp.maximum(b, bt)
        a = jnp.min(a, axis=1, keepdims=True)
        b = jnp.max(b, axis=1, keepdims=True)
        live = done < 0.5
        return jnp.where(live, a, lo), jnp.where(jnp.logical_and(live, a >= b), 1.0, done)

    def cond(c):
        it, st = c
        left = st[0][2]
        for s in st[1:]:
            left = jnp.minimum(left, s[2])
        return jnp.logical_and(it < BISECT_CAP, jnp.min(left) < 0.5)

    def body(c):
        it, st = c
        for u in range(BISECT_UNROLL):
            st = [bisect(rows, *s, last=u == BISECT_UNROLL - 1) for rows, s in zip(grps, st)]

        def snapped():
            out = []
            for rows, (lo, hi, done) in zip(grps, st):
                lo2, done2 = snap(rows, lo, hi, done)
                out.append((lo2, hi, done2))
            return out

        st = lax.cond(it >= SNAP_FROM, snapped, lambda: st)
        return it + 1, st

    hi0 = jnp.where(few, lo0, mx + (jnp.abs(mx) + 1.0))
    st0 = [(lo0[rows], hi0[rows], few[rows].astype(f32)) for rows in grps]
    _, st = lax.while_loop(cond, body, (jnp.int32(0), st0))
    lo = jnp.concatenate([s[0] for s in st], axis=0)
    hi = jnp.concatenate([s[1] for s in st], axis=0)

    has_run = jnp.max(jnp.where(lo < hi, 1.0, 0.0)) > 0.5
    rows_all = slice(0, R)

    @pl.when(jnp.logical_not(has_run))
    def _():
        for t in tiles:
            bias_s[:, t] = jnp.where(x_s[:, t] >= hi, 0.0, NEG)

    @pl.when(has_run)
    def _():
        need = kf - count_ge(rows_all, hi)
        ai = lax.broadcasted_iota(jnp.int32, (kt_w, kt_w), 0)
        bi = lax.broadcasted_iota(jnp.int32, (kt_w, kt_w), 1)
        before = (ai < bi).astype(bf16)
        seen = jnp.zeros((R, 1), f32)
        for t in tiles:
            x = x_s[:, t]
            run = jnp.logical_and(x >= lo, x < hi)
            runf = run.astype(f32)
            rank = seen + _dot(runf.astype(bf16), before)
            take = jnp.logical_or(x >= hi, jnp.logical_and(run, rank < need))
            bias_s[:, t] = jnp.where(take, 0.0, NEG)
            seen = seen + jnp.sum(runf, axis=1, keepdims=True)


def _dsa_prompt_kernel(iq_ref, sm_ref, ik_ref, q_ref, k_ref, v_ref, o_ref,
                       ikb, kb, vb, x_s, bias_s, s_s, *, topk, kt_w, ext_w):
    qi = pl.program_id(1)
    R = iq_ref.shape[0]
    T = ik_ref.shape[0]

    @pl.when(qi == 0)
    def _():
        ikb[...] = ik_ref[...].astype(bf16)
        kb[...] = k_ref[...].astype(bf16)
        vb[...] = v_ref[...].astype(bf16)

    def block(ext):
        tiles = [slice(j * kt_w, (j + 1) * kt_w) for j in range(ext // kt_w)]
        wgt = sm_ref[...] * IDX_SCALE
        qpos = qi * R + lax.broadcasted_iota(jnp.int32, (R, kt_w), 0)
        key0 = lax.broadcasted_iota(jnp.int32, (R, kt_w), 1)
        mn = mx = None
        for j, t in enumerate(tiles):
            keys = ikb[t, :]
            sc = jnp.zeros((R, kt_w), f32)
            for h in range(IDX_HEADS):
                rel = jnp.maximum(_dot_nt(iq_ref[:, h * IDX_DIM:(h + 1) * IDX_DIM], keys), 0.0)
                sc = sc + rel * wgt[:, SM_IW + h:SM_IW + h + 1]
            causal = key0 + j * kt_w <= qpos
            x_s[:, t] = jnp.where(causal, sc, NEG)
            lo_t = jnp.where(causal, sc, POS)
            hi_t = jnp.where(causal, sc, NEG)
            mn = lo_t if mn is None else jnp.minimum(mn, lo_t)
            mx = hi_t if mx is None else jnp.maximum(mx, hi_t)
        n_causal = qi * R + lax.broadcasted_iota(jnp.int32, (R, 1), 0) + 1
        _topk_bias(x_s, bias_s, jnp.min(mn, axis=1, keepdims=True), jnp.max(mx, axis=1, keepdims=True),
                   n_causal <= topk, len(tiles), kt_w, topk)

        def heads(hp, carry):
            h0 = hp * DSA_HEAD_GROUP
            g = h0 // (DSA_HEADS // DSA_KV_HEADS)
            gc = pl.ds(pl.multiple_of(g * DSA_HEAD_DIM, DSA_HEAD_DIM), DSA_HEAD_DIM)
            hcs = [pl.ds(pl.multiple_of((h0 + i) * DSA_HEAD_DIM, DSA_HEAD_DIM), DSA_HEAD_DIM)
                   for i in range(DSA_HEAD_GROUP)]
            qs = [q_ref[:, hc] for hc in hcs]
            mxa = [None] * DSA_HEAD_GROUP
            for t in tiles:
                kt = kb[t, gc]
                bt = bias_s[:, t]
                for i in range(DSA_HEAD_GROUP):
                    s = _dot_nt(qs[i], kt) * (DSA_HEAD_DIM ** -0.5) + bt
                    s_s[i, :, t] = s
                    mxa[i] = s if mxa[i] is None else jnp.maximum(mxa[i], s)
            ms = [jnp.max(a, axis=1, keepdims=True) for a in mxa]
            la = [jnp.zeros((R, kt_w), f32)] * DSA_HEAD_GROUP
            acc = [jnp.zeros((R, DSA_HEAD_DIM), f32)] * DSA_HEAD_GROUP
            for t in tiles:
                vt = vb[t, gc]
                for i in range(DSA_HEAD_GROUP):
                    p = jnp.exp(s_s[i, :, t] - ms[i])
                    la[i] = la[i] + p
                    acc[i] = acc[i] + _dot(p.astype(bf16), vt)
            for i in range(DSA_HEAD_GROUP):
                o_ref[:, hcs[i]] = (acc[i] / jnp.sum(la[i], axis=1, keepdims=True)).astype(bf16)
            return carry

        lax.fori_loop(0, DSA_HEADS // DSA_HEAD_GROUP, heads, 0)

    for e in range(T // ext_w):
        pl.when(qi // (ext_w // R) == e)(functools.partial(block, (e + 1) * ext_w))


def _dsa_prompt(iq, sm, ik, dq, dk, y, nb, t, topk):
    R = LANES
    nq = t // R
    kt_w = min(DSA_KEY_TILE, t)
    ext_w = min(DSA_KEY_EXTENT, t)
    kern = functools.partial(_dsa_prompt_kernel, topk=topk, kt_w=kt_w, ext_w=ext_w)
    return pl.pallas_call(
        kern,
        grid=(nb, nq),
        in_specs=[pl.BlockSpec((R, IQ_W), lambda b, i: (b * nq + i, 0)),
                  pl.BlockSpec((R, LANES), lambda b, i: (b * nq + i, 0)),
                  pl.BlockSpec((t, IDX_DIM), lambda b, i: (b, 0)),
                  pl.BlockSpec((R, DQ_W), lambda b, i: (b * nq + i, 0)),
                  pl.BlockSpec((t, DKV_W), lambda b, i: (b, 0)),
                  pl.BlockSpec((t, DKV_W), lambda b, i: (b, OFF_DV // DKV_W))],
        out_specs=pl.BlockSpec((R, DQ_W), lambda b, i: (b * nq + i, 0)),
        out_shape=jax.ShapeDtypeStruct((nb * t, DQ_W), bf16),
        scratch_shapes=[pltpu.VMEM((t, IDX_DIM), bf16), pltpu.VMEM((t, DKV_W), bf16),
                        pltpu.VMEM((t, DKV_W), bf16), pltpu.VMEM((R, t), f32), pltpu.VMEM((R, t), f32),
                        pltpu.VMEM((DSA_HEAD_GROUP, R, t), f32)],
        compiler_params=_params(48, 2),
        name="dsa_prompt",
    )(iq, sm, ik, dq, dk, y)


def _idx_score_kernel(pt_ref, iq_ref, sm_ref, ikn_ref, *rest, n_pages, page):
    eb, tp, _ = iq_ref.shape
    o_ref = rest[eb * n_pages]
    ts = o_ref.shape[1]
    past = n_pages * page
    lp = o_ref.shape[2]
    for e in range(eb):
        pages = rest[e * n_pages:(e + 1) * n_pages]
        keys_t = jnp.concatenate([p[...] for p in pages], axis=1).astype(bf16)
        keys_n = jnp.concatenate([ikn_ref[e], jnp.zeros((lp - past - tp, IDX_DIM), f32)], axis=0).astype(bf16)
        wgt = sm_ref[e] * IDX_SCALE
        iq = iq_ref[e].astype(bf16)
        score = jnp.zeros((tp, lp), f32)
        for h in range(IDX_HEADS):
            iqh = iq[:, h * IDX_DIM:(h + 1) * IDX_DIM]
            rel = jnp.maximum(jnp.concatenate([_dot(iqh, keys_t), _dot_nt(iqh, keys_n)], axis=1), 0.0)
            score = score + rel * wgt[:, SM_IW + h:SM_IW + h + 1]
        o_ref[e] = score[:ts]


def _idx_scores(page_table, iq, sm, ikn, cache_idx, lp, ts):
    db, tp, _ = iq.shape
    n_pages = page_table.shape[1]
    page = cache_idx.shape[2]
    eb = IDX_BATCH
    kern = functools.partial(_idx_score_kernel, n_pages=n_pages, page=page)
    page_specs = [pl.BlockSpec((None, IDX_DIM, page),
                               functools.partial(lambda b, pt, e, p: (pt[b * eb + e, p], 0, 0), e=e, p=p))
                  for e in range(eb) for p in range(n_pages)]
    grid_spec = pltpu.PrefetchScalarGridSpec(
        num_scalar_prefetch=1,
        grid=(db // eb,),
        in_specs=[pl.BlockSpec((eb, tp, IQ_W), lambda b, pt: (b, 0, 0)),
                  pl.BlockSpec((eb, tp, LANES), lambda b, pt: (b, 0, 0)),
                  pl.BlockSpec((eb, tp, IDX_DIM), lambda b, pt: (b, 0, 0))] + page_specs,
        out_specs=pl.BlockSpec((eb, ts, lp), lambda b, pt: (b, 0, 0)),
    )
    return pl.pallas_call(
        kern, grid_spec=grid_spec,
        out_shape=jax.ShapeDtypeStruct((db, ts, lp), f32),
        compiler_params=_params(),
        name="idx_scores_sample",
    )(page_table, iq, sm, ikn, *([cache_idx] * (eb * n_pages)))


def _select_sample_kernel(x_ref, bias_ref, x_s, *, topk, past, ts):
    R, lp = x_ref.shape
    key = lax.broadcasted_iota(jnp.int32, (R, lp), 1)
    t = lax.broadcasted_iota(jnp.int32, (R, lp), 0) % ts
    causal = key <= past + t
    x = x_ref[...]
    x_s[...] = jnp.where(causal, x, NEG)
    lo0 = jnp.min(jnp.where(causal, x, POS), axis=1, keepdims=True)
    mx = jnp.max(jnp.where(causal, x, NEG), axis=1, keepdims=True)
    n_causal = past + lax.broadcasted_iota(jnp.int32, (R, 1), 0) % ts + 1
    _topk_bias(x_s, bias_ref, lo0, mx, n_causal <= topk, lp // LANES, LANES, topk)


def _select_sample(scores, topk, past, ts):
    m, lp = scores.shape
    R = LANES
    kern = functools.partial(_select_sample_kernel, topk=topk, past=past, ts=ts)
    return pl.pallas_call(
        kern, grid=(m // R,),
        in_specs=[pl.BlockSpec((R, lp), lambda i: (i, 0))],
        out_specs=pl.BlockSpec((R, lp), lambda i: (i, 0)),
        out_shape=jax.ShapeDtypeStruct((m, lp), f32),
        scratch_shapes=[pltpu.VMEM((R, lp), f32)],
        compiler_params=_params(),
        name="select_sample",
    )(scores)


def _dsa_sample_kernel(pt_ref, q_ref, sel_ref, kn_ref, vn_ref, *rest, n_pages, page):
    eb, tp, _ = q_ref.shape
    o_ref = rest[2 * eb * n_pages]
    ts = sel_ref.shape[1]
    past = n_pages * page
    lp = sel_ref.shape[2]
    hpg = DSA_HEADS // DSA_KV_HEADS
    padn = jnp.zeros((lp - past - tp, DSA_HEAD_DIM), f32)
    for e in range(eb):
        kp = rest[e * n_pages:(e + 1) * n_pages]
        vp = rest[(eb + e) * n_pages:(eb + e + 1) * n_pages]
        bias = jnp.concatenate([sel_ref[e], jnp.zeros((tp - ts, lp), f32)], axis=0)
        bias = jnp.concatenate([bias] * hpg, axis=0)
        for g in range(DSA_KV_HEADS):
            gsl = slice(g * DSA_HEAD_DIM, (g + 1) * DSA_HEAD_DIM)
            kg = jnp.concatenate([p[pl.ds(g, page, stride=DSA_KV_HEADS), :] for p in kp]
                                 + [kn_ref[e, :, gsl], padn], axis=0).astype(bf16)
            vg = jnp.concatenate([p[pl.ds(g, page, stride=DSA_KV_HEADS), :] for p in vp]
                                 + [vn_ref[e, :, gsl], padn], axis=0).astype(bf16)
            qg = jnp.concatenate([q_ref[e, :, (g * hpg + j) * DSA_HEAD_DIM:(g * hpg + j + 1) * DSA_HEAD_DIM]
                                  for j in range(hpg)], axis=0).astype(bf16)
            s = _dot_nt(qg, kg) * (DSA_HEAD_DIM ** -0.5) + bias
            m = jnp.max(s, axis=1, keepdims=True)
            p = jnp.exp(s - m)
            l = jnp.sum(p, axis=1, keepdims=True)
            o = _dot(p.astype(bf16), vg) / l
            for j in range(hpg):
                hsl = slice((g * hpg + j) * DSA_HEAD_DIM, (g * hpg + j + 1) * DSA_HEAD_DIM)
                o_ref[e, :, hsl] = o[j * tp:j * tp + ts]


def _dsa_sample(page_table, dq, sel, kn, vn, ck, cv):
    db, tp, _ = dq.shape
    ts = sel.shape[1]
    lp = sel.shape[2]
    n_pages = page_table.shape[1]
    rows = ck.shape[1]
    page = rows // DSA_KV_HEADS
    eb = DSA_SAMPLE_BATCH
    kern = functools.partial(_dsa_sample_kernel, n_pages=n_pages, page=page)
    pspec = [pl.BlockSpec((None, rows, DSA_HEAD_DIM),
                          functools.partial(lambda b, pt, e, p: (pt[b * eb + e, p], 0, 0), e=e, p=p))
             for e in range(eb) for p in range(n_pages)]
    per_b = lambda r, w: pl.BlockSpec((eb, r, w), lambda b, pt: (b, 0, 0))
    grid_spec = pltpu.PrefetchScalarGridSpec(
        num_scalar_prefetch=1,
        grid=(db // eb,),
        in_specs=[per_b(tp, DQ_W), per_b(ts, lp), per_b(tp, DKV_W), per_b(tp, DKV_W)] + pspec + pspec,
        out_specs=per_b(ts, DQ_W),
    )
    return pl.pallas_call(
        kern, grid_spec=grid_spec,
        out_shape=jax.ShapeDtypeStruct((db, ts, DQ_W), f32),
        compiler_params=_params(48),
        name="dsa_sample",
    )(page_table, dq, sel, kn, vn, *([ck] * (eb * n_pages)), *([cv] * (eb * n_pages)))


def _dsa_ring_kernel(pt_ref, q_ref, sel_ref, kn_ref, vn_ref, ck_hbm, cv_hbm, o_ref, kbuf, vbuf, sem,
                     *, n_pages, page):
    s = pl.program_id(0)
    n_steps = pl.num_programs(0)
    eb, tp, _ = q_ref.shape
    ts = sel_ref.shape[1]
    lp = sel_ref.shape[2]
    past = n_pages * page
    hpg = DSA_HEADS // DSA_KV_HEADS

    def copies(step, slot):
        out = []
        for e in range(eb):
            for p in range(n_pages):
                pg = pt_ref[step * eb + e, p]
                out.append(pltpu.make_async_copy(ck_hbm.at[pg], kbuf.at[slot, e * n_pages + p], sem.at[slot, 0]))
                out.append(pltpu.make_async_copy(cv_hbm.at[pg], vbuf.at[slot, e * n_pages + p], sem.at[slot, 1]))
        return out

    @pl.when(s == 0)
    def _():
        for k in range(DSA_RING - 1):
            for c in copies(k, k):
                c.start()

    ahead = s + (DSA_RING - 1)

    @pl.when(ahead < n_steps)
    def _():
        for c in copies(ahead, ahead % DSA_RING):
            c.start()

    slot = s % DSA_RING
    for c in copies(s, slot):
        c.wait()

    padn = jnp.zeros((lp - past - tp, DSA_HEAD_DIM), f32)
    for e in range(eb):
        bias = jnp.concatenate([sel_ref[e], jnp.zeros((tp - ts, lp), f32)], axis=0)
        bias = jnp.concatenate([bias] * hpg, axis=0)
        for g in range(DSA_KV_HEADS):
            gsl = slice(g * DSA_HEAD_DIM, (g + 1) * DSA_HEAD_DIM)
            rows_g = pl.ds(g, page, stride=DSA_KV_HEADS)
            kg = jnp.concatenate([kbuf[slot, e * n_pages + p, rows_g, :] for p in range(n_pages)]
                                 + [kn_ref[e, :, gsl], padn], axis=0).astype(bf16)
            vg = jnp.concatenate([vbuf[slot, e * n_pages + p, rows_g, :] for p in range(n_pages)]
                                 + [vn_ref[e, :, gsl], padn], axis=0).astype(bf16)
            qg = jnp.concatenate([q_ref[e, :, (g * hpg + j) * DSA_HEAD_DIM:(g * hpg + j + 1) * DSA_HEAD_DIM]
                                  for j in range(hpg)], axis=0).astype(bf16)
            sc = _dot_nt(qg, kg) * (DSA_HEAD_DIM ** -0.5) + bias
            m = jnp.max(sc, axis=1, keepdims=True)
            p_ = jnp.exp(sc - m)
            l = jnp.sum(p_, axis=1, keepdims=True)
            o = _dot(p_.astype(bf16), vg) / l
            for j in range(hpg):
                hsl = slice((g * hpg + j) * DSA_HEAD_DIM, (g * hpg + j + 1) * DSA_HEAD_DIM)
                o_ref[e, :, hsl] = o[j * tp:j * tp + ts]


def _dsa_sample_ring(page_table, dq, sel, kn, vn, ck, cv):
    db, tp, _ = dq.shape
    ts = sel.shape[1]
    lp = sel.shape[2]
    n_pages = page_table.shape[1]
    rows = ck.shape[1]
    page = rows // DSA_KV_HEADS
    eb = DSA_RING_BATCH
    assert db // eb >= DSA_RING
    kern = functools.partial(_dsa_ring_kernel, n_pages=n_pages, page=page)
    per_b = lambda r, w: pl.BlockSpec((eb, r, w), lambda b, pt: (b, 0, 0))
    any_spec = pl.BlockSpec(memory_space=pl.ANY)
    buf = pltpu.VMEM((DSA_RING, eb * n_pages, rows, DSA_HEAD_DIM), f32)
    grid_spec = pltpu.PrefetchScalarGridSpec(
        num_scalar_prefetch=1,
        grid=(db // eb,),
        in_specs=[per_b(tp, DQ_W), per_b(ts, lp), per_b(tp, DKV_W), per_b(tp, DKV_W), any_spec, any_spec],
        out_specs=per_b(ts, DQ_W),
        scratch_shapes=[buf, buf, pltpu.SemaphoreType.DMA((DSA_RING, 2))],
    )
    return pl.pallas_call(
        kern, grid_spec=grid_spec,
        out_shape=jax.ShapeDtypeStruct((db, ts, DQ_W), f32),
        compiler_params=_params(48),
        name="dsa_sample_ring",
    )(page_table, dq, sel, kn, vn, ck, cv)


def _merge_kernel(og_ref, od_ref, gg_ref, gd_ref, x_ref, gt_ref, sc_ref, sh_ref, g2_ref,
                  wg_ref, wd_ref, wo_ref, x1_ref, h2_ref):
    mix = (_sigmoid(gg_ref[...]) * _dot(og_ref[...], wg_ref[...])
           + _sigmoid(gd_ref[...]) * _dot(od_ref[...], wd_ref[...]))
    x1 = x_ref[...] + gt_ref[...] * _dot(mix.astype(bf16), wo_ref[...])
    x1_ref[...] = x1
    y = x1 * lax.rsqrt(jnp.mean(x1 * x1, axis=-1, keepdims=True) + NORM_EPS)
    h2_ref[...] = ((y * g2_ref[...]) * (1.0 + sc_ref[...]) + sh_ref[...]).astype(bf16)


def _merge(o_gdn, o_dsa, y, x, ada, g2, wg, wd, wo, group, rows_per_batch, tm):
    m, d = x.shape
    row = lambda w, blk=0: pl.BlockSpec((tm, w), lambda i: (i, blk))
    res = lambda a: pl.BlockSpec(a.shape, lambda i: (0, 0), pipeline_mode=pl.Buffered(1))
    return pl.pallas_call(
        _merge_kernel,
        grid=(m // tm,),
        in_specs=[row(Z_W), row(DQ_W), row(d, 0), row(d, 1), row(d),
                  _mod_spec(group, d, rows_per_batch, tm, 2),
                  _mod_spec(group, d, rows_per_batch, tm, 4),
                  _mod_spec(group, d, rows_per_batch, tm, 3),
                  pl.BlockSpec((1, d), lambda i: (0, 0)),
                  res(wg), res(wd), res(wo)],
        out_specs=[row(d), row(d)],
        out_shape=[jax.ShapeDtypeStruct((m, d), f32), jax.ShapeDtypeStruct((m, d), bf16)],
        compiler_params=_params(48),
        name="merge_" + group,
    )(o_gdn, o_dsa, y, y, x, ada, ada, ada, g2.reshape(1, d), wg, wd, wo)


def _ffn_epilogue(acc_ref, x1_ref, gt_ref, gf_ref, y_ref):
    tm = acc_ref.shape[0]
    r = gt_ref.shape[0] if gt_ref.shape[0] > 1 else tm
    for s0 in range(0, tm, r):
        sl = slice(s0, s0 + r)
        x2 = x1_ref[sl, :] + gt_ref[...] * acc_ref[sl, :]
        y_ref[sl, :] = (x2 * lax.rsqrt(jnp.mean(x2 * x2, axis=-1, keepdims=True) + NORM_EPS)) * gf_ref[...]


def _ffn_prompt_kernel(h_ref, wug_ref, wuv_ref, wcg_ref, wcv_ref, bg_ref, bv_ref, wd_ref,
                       x1_ref, gt_ref, gf_ref, y_ref, ug_ref, uv_ref, acc_ref, act_s, prev_s,
                       *, blocks_per_seq, n_up):
    i = pl.program_id(0)
    j = pl.program_id(1)
    tm = h_ref.shape[0]
    keep = (i % blocks_per_seq) > 0
    sr = min(FFN_SUB_ROWS, tm)

    @pl.when(jnp.logical_and(i == 0, j == 0))
    def _():
        prev_s[...] = jnp.zeros_like(prev_s)

    def conv(u, prev, wc_ref, b_ref):
        ux = jnp.concatenate([prev, u], axis=0)
        w = wc_ref[...]
        y = u * w[FFN_CONV - 1:FFN_CONV] + b_ref[...]
        for t in range(FFN_CONV - 1):
            sft = FFN_CONV - 1 - t
            y = y + ux[SUBLANES - sft:SUBLANES - sft + sr] * w[t:t + 1]
        return y

    def up(store):
        pg = jnp.where(keep, prev_s[j, 0], 0.0)
        pv = jnp.where(keep, prev_s[j, 1], 0.0)
        for s in range(tm // sr):
            rows = slice(s * sr, (s + 1) * sr)
            hs = h_ref[rows, :]
            ug = _dot(hs, wug_ref[...])
            uv = _dot(hs, wuv_ref[...])
            store(rows, _silu(conv(ug, pg, wcg_ref, bg_ref)) * conv(uv, pv, wcv_ref, bv_ref))
            pg = ug[sr - SUBLANES:]
            pv = uv[sr - SUBLANES:]
        ug_ref[...] = pg
        uv_ref[...] = pv
        prev_s[j, 0] = pg
        prev_s[j, 1] = pv

    _ffn_phases(j, n_up, up, act_s, wd_ref, acc_ref, x1_ref, gt_ref, gf_ref, y_ref)


def _ffn_phases(j, n_up, up, act_s, wd_ref, acc_ref, x1_ref, gt_ref, gf_ref, y_ref):
    tf = act_s.shape[1] // n_up
    tn = wd_ref.shape[1]

    @pl.when(j < n_up)
    def _():
        cols = pl.ds(pl.multiple_of(j * tf, tf), tf)

        def store(rows, act):
            act_s[rows, cols] = act.astype(bf16)
        up(store)

    @pl.when(j >= n_up)
    def _():
        acc_ref[:, pl.ds(pl.multiple_of((j - n_up) * tn, tn), tn)] = _dot(act_s[...], wd_ref[...])

    @pl.when(j == pl.num_programs(1) - 1)
    def _():
        _ffn_epilogue(acc_ref, x1_ref, gt_ref, gf_ref, y_ref)


def _ffn_sample_kernel(h_ref, bufg_ref, bufv_ref, wug_ref, wuv_ref, wcg_ref, wcv_ref, bg_ref, bv_ref, wd_ref,
                       x1_ref, gt_ref, gf_ref, y_ref, ug_ref, uv_ref, acc_ref, act_s, *, ts, n_up):
    j = pl.program_id(1)
    db = h_ref.shape[0] // ts
    nbuf = FFN_CONV - 1

    def branch(wu_ref, wc_ref, b_ref, buf_ref, ubuf_ref):
        u = _dot(h_ref[...], wu_ref[...])
        rows = [buf_ref[:, r, :] for r in range(nbuf)]
        rows += [u[t * db:(t + 1) * db] for t in range(ts)]
        for r in range(nbuf):
            ubuf_ref[:, r, :] = rows[ts + r]
        w = wc_ref[...]
        outs = []
        for t in range(ts):
            y = rows[t] * w[0:1] + b_ref[...]
            for r in range(1, FFN_CONV):
                y = y + rows[t + r] * w[r:r + 1]
            outs.append(y)
        return jnp.concatenate(outs, axis=0)

    def up(store):
        store(slice(0, ts * db),
              _silu(branch(wug_ref, wcg_ref, bg_ref, bufg_ref, ug_ref))
              * branch(wuv_ref, wcv_ref, bv_ref, bufv_ref, uv_ref))

    _ffn_phases(j, n_up, up, act_s, wd_ref, acc_ref, x1_ref, gt_ref, gf_ref, y_ref)


def _ffn(h2, x1, ada, g_final, w_up, w_conv, b_conv, w_down, group, rows_per_batch, tm, tf, buf=None):
    m, d = h2.shape
    dff = w_down.shape[0]
    nj = dff // tf
    tn = min(FFN_DOWN_TILE, d)
    b2 = b_conv.reshape(1, 2 * dff)
    up = lambda j: jnp.minimum(j, nj - 1)
    common_w = [pl.BlockSpec((d, tf), lambda i, j: (0, up(j))),
                pl.BlockSpec((d, tf), lambda i, j: (0, nj + up(j))),
                pl.BlockSpec((FFN_CONV, tf), lambda i, j: (0, up(j))),
                pl.BlockSpec((FFN_CONV, tf), lambda i, j: (0, nj + up(j))),
                pl.BlockSpec((1, tf), lambda i, j: (0, up(j))),
                pl.BlockSpec((1, tf), lambda i, j: (0, nj + up(j))),
                pl.BlockSpec((dff, tn), lambda i, j: (0, jnp.maximum(j - nj, 0)))]
    tail = [pl.BlockSpec((tm, d), lambda i, j: (i, 0)),
            _mod_spec(group, d, rows_per_batch, tm, 5),
            pl.BlockSpec((1, d), lambda i, j: (0, 0))]
    y_spec = pl.BlockSpec((tm, d), lambda i, j: (i, 0))
    if group == "prompt":
        bps = rows_per_batch // tm
        nb = m // rows_per_batch
        kern = functools.partial(_ffn_prompt_kernel, blocks_per_seq=bps, n_up=nj)
        first = [pl.BlockSpec((tm, d), lambda i, j: (i, 0))]
        ubuf_spec = pl.BlockSpec((None, SUBLANES, tf), lambda i, j: (i, 0, up(j)))
        ubuf_shape = jax.ShapeDtypeStruct((m // tm, SUBLANES, dff), f32)
        args = (h2,)
        extra_scratch = [pltpu.VMEM((nj, 2, SUBLANES, tf), f32)]
    else:
        extra_scratch = []
        ts = m // rows_per_batch
        kern = functools.partial(_ffn_sample_kernel, ts=ts, n_up=nj)
        first = [pl.BlockSpec((tm, d), lambda i, j: (i, 0)),
                 pl.BlockSpec((rows_per_batch, FFN_CONV - 1, tf), lambda i, j: (0, 0, up(j))),
                 pl.BlockSpec((rows_per_batch, FFN_CONV - 1, tf), lambda i, j: (0, 0, nj + up(j)))]
        ubuf_spec = pl.BlockSpec((rows_per_batch, FFN_CONV - 1, tf), lambda i, j: (0, 0, up(j)))
        ubuf_shape = jax.ShapeDtypeStruct((rows_per_batch, FFN_CONV - 1, dff), f32)
        args = (h2, buf, buf)
    return pl.pallas_call(
        kern,
        grid=(m // tm, nj + d // tn),
        in_specs=first + common_w + tail,
        out_specs=[y_spec, ubuf_spec, ubuf_spec],
        out_shape=[jax.ShapeDtypeStruct((m, d), f32), ubuf_shape, ubuf_shape],
        scratch_shapes=[pltpu.VMEM((tm, d), f32), pltpu.VMEM((tm, dff), bf16)] + extra_scratch,
        compiler_params=_params(56, 2),
        name="ffn_" + group,
    )(*args, w_up, w_up, w_conv, w_conv, b2, b2, w_down, x1, ada, g_final.reshape(1, d))


def _split_w_in(w_in, d):
    sizes = (QKV_W, Z_W, GDN_HEADS, GDN_HEADS, DQ_W, DKV_W, DKV_W, IQ_W, IDX_DIM, IDX_HEADS, d, d)
    offs = [0]
    for s in sizes:
        offs.append(offs[-1] + s)
    seg = lambda i, j: w_in[:, offs[i]:offs[j]]
    pad = jnp.zeros((w_in.shape[0], LANES - (IDX_DIM + 2 * GDN_HEADS + IDX_HEADS)), w_in.dtype)
    small = jnp.concatenate([seg(8, 9), seg(2, 4), seg(9, 10), pad], axis=1)
    return tuple(w.astype(bf16) for w in (seg(0, 2), seg(4, 8), seg(10, 12), small))


def _in_proj(h, w_parts, tm, name):
    return tuple(_matmul(h, w, tm, _mm_tile(w.shape[1]), name + "_" + part)
                 for w, part in zip(w_parts, ("gdn", "dsa", "gate", "small")))


def _ffn_tile(dff):
    for tf in (512, 256, 128):
        if dff % tf == 0:
            return tf
    raise ValueError("d_ff must be a multiple of 128")


def _mm_tile(n):
    for tn in (1024, 512, 256, 128):
        if n % tn == 0:
            return tn
    raise ValueError("projection width must be a multiple of 128")


def kernel(x_prompt, x_sample, c_prompt, c_sample, cache_k, cache_v, cache_idx_k, page_table, state_gdn, state_gdn_conv, state_ffn_conv, w_ada, b_ada, g_norm1, w_in, w_gdn_conv, a_log, dt_bias, g_gdn_norm, w_gdn_out, w_dsa_out, w_o, g_norm2, w_up, w_ffn_conv, b_ffn_conv, w_down, g_final):
    nb, t, d = x_prompt.shape
    db, ts, _ = x_sample.shape
    depth = w_ada.shape[0]
    assert depth == 1 and db == LANES and ts >= GDN_CONV - 1
    n_pages = page_table.shape[1]
    page = cache_k.shape[2]
    past = n_pages * page
    dff = w_down.shape[1]
    l = 0

    n_c = nb + db
    pad_c = (-n_c) % SUBLANES
    c_all = jnp.concatenate([c_prompt, c_sample, jnp.zeros((pad_c, d), f32)], axis=0)
    ada = _ada(c_all, w_ada[l], b_ada[l])
    ada_p = ada[:nb].reshape(nb, 1, 6 * d)
    ada_s = ada[nb:nb + db]

    w_in_parts = _split_w_in(w_in[l], d)
    wg = w_gdn_out[l].astype(bf16)
    wd = w_dsa_out[l].astype(bf16)
    wo = w_o[l].astype(bf16)
    wup = w_up[l].astype(bf16)
    wdn = w_down[l].astype(bf16)
    tf = _ffn_tile(dff)

    xp = x_prompt.reshape(nb * t, d)
    tm_p = min(512, t)
    h1 = _prep(xp, g_norm1[l], ada_p, "prompt", t, tm_p)
    yp_gdn, yp_dsa, yp_gate, yp_small = _in_proj(h1, w_in_parts, min(IN_PROJ_ROWS, nb * t), "in_proj_prompt")
    og_p, s_p = _gdn_prompt(yp_gdn, yp_small, w_gdn_conv[l], a_log[l], dt_bias[l], g_gdn_norm[l], nb, t)
    dq_p, dk_p, iq_p, ik_p, sm_p, kc_p, vc_p = _rope(yp_dsa, yp_small, jnp.arange(t), tm_p, "prompt")
    od_p = _dsa_prompt(iq_p, sm_p, ik_p, dq_p, dk_p, yp_dsa, nb, t, min(DSA_TOPK, t // 4))
    x1_p, h2_p = _merge(og_p, od_p, yp_gate, xp, ada_p, g_norm2[l], wg, wd, wo, "prompt", t, min(256, t))
    y_p, ug_p, uv_p = _ffn(h2_p, x1_p, ada_p, g_final, wup, w_ffn_conv[l], b_ffn_conv[l], wdn,
                           "prompt", t, tm_p, tf)

    nfb = FFN_CONV - 1
    bps_p = t // tm_p
    out_p = (
        y_p.reshape(nb, t, d),
        kc_p.reshape(1, nb, t, DSA_KV_HEADS, DSA_HEAD_DIM),
        vc_p.reshape(1, nb, t, DSA_KV_HEADS, DSA_HEAD_DIM),
        ik_p.reshape(1, nb, t, IDX_DIM),
        s_p[None],
        yp_gdn.reshape(nb, t, -1)[:, t - (GDN_CONV - 1):, :QKV_W][None],
        jnp.concatenate([ug_p[bps_p - 1::bps_p, SUBLANES - nfb:], uv_p[bps_p - 1::bps_p, SUBLANES - nfb:]],
                        axis=-1)[None],
    )

    xs = x_sample.transpose(1, 0, 2).reshape(ts * db, d)
    h1s = _prep(xs, g_norm1[l], ada_s, "sample", db, db)
    ys_gdn, ys_dsa, ys_gate, ys_small = _in_proj(h1s, w_in_parts, ts * db, "in_proj_sample")
    ys_gdn3 = ys_gdn.reshape(ts, db, -1)
    og_s, s_s = _gdn_sample(ys_gdn3, ys_small.reshape(ts, db, LANES), state_gdn_conv[l].transpose(1, 0, 2),
                            state_gdn[l], w_gdn_conv[l], a_log[l], dt_bias[l], g_gdn_norm[l])
    pos_s = jnp.repeat(past + jnp.arange(ts), db)
    dq_s, dk_s, iq_s, ik_s, sm_s, _, _ = _rope(ys_dsa, ys_small, pos_s, db, "sample", db)
    ik_s_b = ik_s.reshape(ts, db, IDX_DIM).transpose(1, 0, 2)
    lp = past + LANES
    tp = -(-ts // SUBLANES) * SUBLANES
    pad_t = lambda a: jnp.pad(a.astype(f32), ((0, 0), (0, tp - ts), (0, 0)))
    scores = _idx_scores(page_table, pad_t(iq_s.reshape(db, ts, IQ_W)), pad_t(sm_s.reshape(db, ts, LANES)),
                         pad_t(ik_s_b), jnp.swapaxes(cache_idx_k[l], 1, 2), lp, ts)
    sel = _select_sample(scores.reshape(db * ts, lp), min(DSA_TOPK, (past + ts) // 4), past, ts)
    dv_s_b = ys_dsa.reshape(ts, db, -1)[:, :, OFF_DV:OFF_DV + DKV_W].transpose(1, 0, 2)
    ck = cache_k[l].reshape(cache_k.shape[1], page * DSA_KV_HEADS, DSA_HEAD_DIM)
    cv = cache_v[l].reshape(cache_v.shape[1], page * DSA_KV_HEADS, DSA_HEAD_DIM)
    od_s = _dsa_sample_ring(page_table, pad_t(dq_s.reshape(db, ts, DQ_W)), sel.reshape(db, ts, lp),
                       pad_t(dk_s.reshape(db, ts, DKV_W)), pad_t(dv_s_b), ck, cv)
    od_s = od_s.astype(bf16).transpose(1, 0, 2).reshape(ts * db, DQ_W)
    x1_s, h2_s = _merge(og_s.reshape(ts * db, Z_W), od_s, ys_gate, xs, ada_s, g_norm2[l], wg, wd, wo,
                        "sample", db, db)
    y_s, ug_s, uv_s = _ffn(h2_s, x1_s, ada_s, g_final, wup, w_ffn_conv[l], b_ffn_conv[l], wdn,
                           "sample", db, ts * db, tf, buf=state_ffn_conv[l])

    out_s = (
        y_s.reshape(ts, db, d).transpose(1, 0, 2),
        dk_s.reshape(1, db, ts, DSA_KV_HEADS, DSA_HEAD_DIM),
        dv_s_b.reshape(1, db, ts, DSA_KV_HEADS, DSA_HEAD_DIM),
        ik_s_b[None],
        s_s[None],
        ys_gdn3[ts - (GDN_CONV - 1):, :, :QKV_W].transpose(1, 0, 2)[None],
        jnp.concatenate([ug_s, uv_s], axis=-1)[None],
    )
    return (out_p[0], out_s[0]) + out_p[1:] + out_s[1:]
```

```python
import functools

import jax
import jax.numpy as jnp
from jax import lax
from jax.experimental import pallas as pl
from jax.experimental.pallas import tpu as pltpu

f32 = jnp.float32
bf16 = jnp.bfloat16

GDN_HEADS = 8
GDN_DK = 128
GDN_DV = 128
GDN_CONV = 4
GDN_CHUNK = 64
GDN_GROUP = 4
GDN_SEQS = 4
DSA_HEADS = 8
DSA_KV_HEADS = 2
DSA_HEAD_DIM = 128
IDX_HEADS = 8
IDX_DIM = 64
IDX_SCALE = IDX_HEADS ** -0.5 * IDX_DIM ** -0.5
DSA_TOPK = 256
ROPE_THETA = 500000.0
ROPE_FRACTION = 4
FFN_CONV = 3
NORM_EPS = 1e-6

LANES = 128
SUBLANES = 8
NEG = float(jnp.finfo(jnp.float32).min)
POS = float(jnp.finfo(jnp.float32).max)
BISECT_UNROLL = 4
SNAP_FROM = 4
BISECT_CAP = 1024
DSA_KEY_TILE = 256
DSA_KEY_EXTENT = 256
IN_PROJ_ROWS = 1024
IDX_BATCH = 8
DSA_SAMPLE_BATCH = 4
DSA_RING = 3
DSA_RING_BATCH = 2
DSA_HEAD_GROUP = 4
FFN_DOWN_TILE = 512
FFN_SUB_ROWS = 256

QKV_W =2 * GDN_HEADS * GDN_DK + GDN_HEADS * GDN_DV
Z_W = GDN_HEADS * GDN_DV
DQ_W = DSA_HEADS * DSA_HEAD_DIM
DKV_W = DSA_KV_HEADS * DSA_HEAD_DIM
IQ_W = IDX_HEADS * IDX_DIM
OFF_Z = QKV_W
GDN_PROJ_W = QKV_W + Z_W
OFF_DQ = 0
OFF_DK = OFF_DQ + DQ_W
OFF_DV = OFF_DK + DKV_W
OFF_IQ = OFF_DV + DKV_W
DSA_PROJ_W = OFF_IQ + IQ_W
SM_IK = 0
SM_A = IDX_DIM
SM_B = SM_A + GDN_HEADS
SM_IW = SM_B + GDN_HEADS


def _sigmoid(x):
    return 1.0 / (1.0 + jnp.exp(-x))


def _silu(x):
    return x * _sigmoid(x)


def _dot(a, b):
    return jnp.dot(a, b, preferred_element_type=f32)


def _dot_nt(a, b):
    return lax.dot_general(a, b, (((1,), (1,)), ((), ())), preferred_element_type=f32)


def _dot_tn(a, b):
    return lax.dot_general(a, b, (((0,), (0,)), ((), ())), preferred_element_type=f32)


def _split3(a):
    hi = a.astype(bf16)
    lo = (a - hi.astype(f32)).astype(bf16)
    return hi, lo


def _mm3(a, b):
    ah, al = a
    bh, bl = b
    return _dot(ah, bh) + (_dot(ah, bl) + _dot(al, bh))


def _dot_exact(a, b):
    return jnp.dot(a, b, preferred_element_type=f32, precision=lax.Precision.HIGHEST)


def _params(vmem_mb=None, n_axes=1):
    kw = dict(dimension_semantics=("arbitrary",) * n_axes)
    if vmem_mb is not None:
        kw["vmem_limit_bytes"] = vmem_mb * 1024 * 1024
    return pltpu.CompilerParams(**kw)


def _ada_kernel(c_ref, w_ref, b_ref, o_ref):
    s = _silu(c_ref[...]).astype(bf16)
    o_ref[...] = _dot(s, w_ref[...].astype(bf16)) + b_ref[...]


def _ada(c_all, w_ada, b_ada):
    m, d = c_all.shape
    n = w_ada.shape[1]
    tn = 1024
    return pl.pallas_call(
        _ada_kernel,
        grid=(n // tn,),
        in_specs=[pl.BlockSpec((m, d), lambda j: (0, 0)),
                  pl.BlockSpec((d, tn), lambda j: (0, j)),
                  pl.BlockSpec((1, tn), lambda j: (0, j))],
        out_specs=pl.BlockSpec((m, tn), lambda j: (0, j)),
        out_shape=jax.ShapeDtypeStruct((m, n), f32),
        compiler_params=_params(48),
        name="ada",
    )(c_all, w_ada, b_ada.reshape(1, n))


def _mod_spec(group, d, rows_per_batch, tm, col):
    if group == "prompt":
        return pl.BlockSpec((None, 1, d), lambda i, *_: ((i * tm) // rows_per_batch, 0, col))
    return pl.BlockSpec((rows_per_batch, d), lambda i, *_: (0, col))


def _prep_kernel(x_ref, g_ref, sc_ref, sh_ref, o_ref):
    x = x_ref[...]
    y = x * lax.rsqrt(jnp.mean(x * x, axis=-1, keepdims=True) + NORM_EPS)
    o_ref[...] = ((y * g_ref[...]) * (1.0 + sc_ref[...]) + sh_ref[...]).astype(bf16)


def _prep(x, g, ada, group, rows_per_batch, tm):
    m, d = x.shape
    return pl.pallas_call(
        _prep_kernel,
        grid=(m // tm,),
        in_specs=[pl.BlockSpec((tm, d), lambda i: (i, 0)),
                  pl.BlockSpec((1, d), lambda i: (0, 0)),
                  _mod_spec(group, d, rows_per_batch, tm, 1),
                  _mod_spec(group, d, rows_per_batch, tm, 0)],
        out_specs=pl.BlockSpec((tm, d), lambda i: (i, 0)),
        out_shape=jax.ShapeDtypeStruct((m, d), bf16),
        compiler_params=_params(),
        name="prep_" + group,
    )(x, g.reshape(1, d), ada, ada)


def _mm_kernel(a_ref, w_ref, o_ref):
    o_ref[...] = _dot(a_ref[...], w_ref[...])


def _matmul(a, w, tm, tn, name):
    m, k = a.shape
    n = w.shape[1]
    return pl.pallas_call(
        _mm_kernel,
        grid=(n // tn, m // tm),
        in_specs=[pl.BlockSpec((tm, k), lambda j, i: (i, 0)),
                  pl.BlockSpec((k, tn), lambda j, i: (0, j))],
        out_specs=pl.BlockSpec((tm, tn), lambda j, i: (i, j)),
        out_shape=jax.ShapeDtypeStruct((m, n), f32),
        compiler_params=_params(48, 2),
        name=name,
    )(a, w)


def _l2n(x):
    return x * lax.rsqrt(jnp.sum(x * x, axis=-1, keepdims=True) + NORM_EPS)


def _gdn_gates(sm, alog, dtb):
    xa = sm + dtb
    softplus = jnp.maximum(xa, 0.0) + jnp.log1p(jnp.exp(-jnp.abs(xa)))
    return -jnp.exp(alog) * softplus, _sigmoid(sm)


def _gated_norm(o, gn, z):
    y = o * lax.rsqrt(jnp.mean(o * o, axis=-1, keepdims=True) + NORM_EPS)
    return (y * gn) * _silu(z)


def _gdn_prompt_kernel(qkv_ref, halo_ref, z_ref, sm_ref, wc_ref, alog_ref, dtb_ref, gn_ref,
                       o_ref, sfin_ref, s_ref):
    c = pl.program_id(1)
    n_seq, C, _ = qkv_ref.shape

    @pl.when(c == 0)
    def _():
        s_ref[...] = jnp.zeros_like(s_ref)

    keep = (c > 0).astype(f32)
    ri = lax.broadcasted_iota(jnp.int32, (C, C), 0)
    ci = lax.broadcasted_iota(jnp.int32, (C, C), 1)
    tril = (ri >= ci).astype(f32)
    gates = [_gdn_gates(sm_ref[s], alog_ref[...], dtb_ref[...]) for s in range(n_seq)]
    beta_all = [g[1] for g in gates]
    gc_all = [_dot_exact(tril, g[0]) for g in gates]
    gc_t = [g.T for g in gc_all]

    def conv(s, col):
        xs = jnp.concatenate([halo_ref[s, :, col:col + LANES] * keep, qkv_ref[s, :, col:col + LANES]], axis=0)
        w = wc_ref[:, col:col + LANES]
        y = xs[SUBLANES:] * w[GDN_CONV - 1:GDN_CONV]
        for i in range(GDN_CONV - 1):
            sft = GDN_CONV - 1 - i
            y = y + xs[SUBLANES - sft:SUBLANES - sft + C] * w[i:i + 1]
        return _silu(y)

    N = GDN_GROUP * C
    rn = lax.broadcasted_iota(jnp.int32, (N, N), 0)
    cn = lax.broadcasted_iota(jnp.int32, (N, N), 1)
    same = (rn // C) == (cn // C)
    incl = jnp.logical_and(same, rn >= cn)
    strict = jnp.logical_and(same, rn > cn)
    eye_f = (rn == cn).astype(f32)
    n_sq = max(1, (C - 1).bit_length() - 1)
    groups = [(s, list(range(g0, g0 + GDN_GROUP))) for s in range(n_seq)
              for g0 in range(0, GDN_HEADS, GDN_GROUP)]
    stack = lambda xs: jnp.concatenate(xs, axis=0)

    qs = [stack([_l2n(conv(s, h * GDN_DK)) * (GDN_DK ** -0.5) for h in hs]) for s, hs in groups]
    ks = [stack([_l2n(conv(s, GDN_HEADS * GDN_DK + h * GDN_DK)) for h in hs]) for s, hs in groups]
    vs = [stack([conv(s, 2 * GDN_HEADS * GDN_DK + h * GDN_DV) for h in hs]) for s, hs in groups]
    gcs = [stack([gc_all[s][:, SM_A + h:SM_A + h + 1] for h in hs]) for s, hs in groups]
    betas = [stack([beta_all[s][:, SM_B + h:SM_B + h + 1] for h in hs]) for s, hs in groups]
    gc_rows = [jnp.concatenate([gc_t[s][SM_A + h:SM_A + h + 1, :] for h in hs], axis=1)
               for s, hs in groups]
    egs = [jnp.exp(gc) for gc in gcs]
    decays = [jnp.where(incl, jnp.exp(jnp.where(incl, gc - gr, 0.0)), 0.0) for gc, gr in zip(gcs, gc_rows)]
    kbs = [k.astype(bf16) for k in ks]
    a_s = [jnp.where(strict, b * _dot_nt(kb, kb) * dc, 0.0) for b, kb, dc in zip(betas, kbs, decays)]
    qks = [(_dot_nt(q.astype(bf16), kb) * dc).astype(bf16) for q, kb, dc in zip(qs, kbs, decays)]

    invs = [eye_f - a for a in a_s]
    pws = [a.astype(bf16) for a in a_s]
    for _ in range(n_sq):
        pws = [_dot(p, p).astype(bf16) for p in pws]
        invs = [inv + _dot(inv.astype(bf16), p) for inv, p in zip(invs, pws)]
    inv_s = [_split3(inv) for inv in invs]
    res = [eye_f - inv - _mm3(_split3(a), sp) for a, inv, sp in zip(a_s, invs, inv_s)]
    invs = [inv + _dot(sp[0], r.astype(bf16)) for inv, sp, r in zip(invs, inv_s, res)]
    rhs = [jnp.concatenate([b * v, (b * eg) * k], axis=1) for b, v, eg, k in zip(betas, vs, egs, ks)]
    sols = [_mm3(_split3(inv), _split3(r)) for inv, r in zip(invs, rhs)]

    for gi, (s, hs) in enumerate(groups):
        sol, q, k, gc, eg = sols[gi], qs[gi], ks[gi], gcs[gi], egs[gi]
        q_dec = (q * eg).astype(bf16)
        w_k = sol[:, GDN_DV:].astype(bf16)
        rows = [slice(j * C, (j + 1) * C) for j in range(GDN_GROUP)]
        s_old = [s_ref[s, h] for h in hs]
        ws = [_dot(jnp.concatenate([w_k[r], q_dec[r]], axis=0), so.astype(bf16)) for r, so in zip(rows, s_old)]
        u = stack([sol[r, :GDN_DV] - w[:C] for r, w in zip(rows, ws)])
        ub = u.astype(bf16)
        o_intra = _dot(qks[gi], ub)
        for j, h in enumerate(hs):
            r = rows[j]
            gl = gc[r][C - 1:C, :]
            k_end = (k[r] * jnp.exp(gl - gc[r])).astype(bf16)
            s_ref[s, h] = s_old[j] * jnp.exp(gl) + _dot_tn(k_end, ub[r])
            o = ws[j][C:] + o_intra[r]
            zh = z_ref[s, :, h * GDN_DV:(h + 1) * GDN_DV]
            o_ref[s, :, h * GDN_DV:(h + 1) * GDN_DV] = _gated_norm(o, gn_ref[...], zh).astype(bf16)

    @pl.when(c == pl.num_programs(1) - 1)
    def _():
        sfin_ref[...] = s_ref[...]


def _gdn_vecs(a_log, dt_bias, g_gdn_norm):
    alog = jnp.zeros((1, LANES), f32).at[0, SM_A:SM_A + GDN_HEADS].set(a_log)
    dtb = jnp.zeros((1, LANES), f32).at[0, SM_A:SM_A + GDN_HEADS].set(dt_bias)
    return alog, dtb, g_gdn_norm.reshape(1, GDN_DV)


def _gdn_prompt(y, y_small, w_conv, a_log, dt_bias, g_gdn_norm, nb, t):
    C = GDN_CHUNK
    nc = t // C
    ns = GDN_SEQS if nb % GDN_SEQS == 0 else 1
    alog, dtb, gn = _gdn_vecs(a_log, dt_bias, g_gdn_norm)
    cst = lambda b, c: (0, 0)
    y3 = y.reshape(nb, t, -1)
    ysm3 = y_small.reshape(nb, t, LANES)
    o, s_fin = pl.pallas_call(
        _gdn_prompt_kernel,
        grid=(nb // ns, nc),
        in_specs=[pl.BlockSpec((ns, C, QKV_W), lambda b, c: (b, c, 0)),
                  pl.BlockSpec((ns, SUBLANES, QKV_W),
                               lambda b, c: (b, jnp.maximum(c * (C // SUBLANES) - 1, 0), 0)),
                  pl.BlockSpec((ns, C, Z_W), lambda b, c: (b, c, OFF_Z // Z_W)),
                  pl.BlockSpec((ns, C, LANES), lambda b, c: (b, c, 0)),
                  pl.BlockSpec((GDN_CONV, QKV_W), cst),
                  pl.BlockSpec((1, LANES), cst),
                  pl.BlockSpec((1, LANES), cst),
                  pl.BlockSpec((1, GDN_DV), cst)],
        out_specs=[pl.BlockSpec((ns, C, Z_W), lambda b, c: (b, c, 0)),
                   pl.BlockSpec((ns, GDN_HEADS, GDN_DK, GDN_DV), lambda b, c: (b, 0, 0, 0))],
        out_shape=[jax.ShapeDtypeStruct((nb, t, Z_W), bf16),
                   jax.ShapeDtypeStruct((nb, GDN_HEADS, GDN_DK, GDN_DV), f32)],
        scratch_shapes=[pltpu.VMEM((ns, GDN_HEADS, GDN_DK, GDN_DV), f32)],
        compiler_params=_params(None, 2),
        name="gdn_prompt",
    )(y3, y3, y3, ysm3, w_conv, alog, dtb, gn)
    return o.reshape(nb * t, Z_W), s_fin


def _gdn_sample_kernel(qkv_ref, buf_ref, z_ref, sm_ref, sin_ref, wc_ref, alog_ref, dtb_ref, gn_ref,
                       o_ref, sout_ref, q_s, k_s, v_s, a_s, b_s, o_s):
    ts, G, _ = qkv_ref.shape
    nbuf = GDN_CONV - 1

    for h in range(GDN_HEADS):
        for part, dst in ((0, q_s), (1, k_s), (2, v_s)):
            col = part * GDN_HEADS * GDN_DK + h * GDN_DK
            w = wc_ref[:, col:col + LANES]
            rows = [buf_ref[i, :, col:col + LANES] for i in range(nbuf)]
            rows += [qkv_ref[t, :, col:col + LANES] for t in range(ts)]
            for t in range(ts):
                y = rows[t] * w[0:1]
                for i in range(1, GDN_CONV):
                    y = y + rows[t + i] * w[i:i + 1]
                y = _silu(y)
                if part == 0:
                    y = _l2n(y) * (GDN_DK ** -0.5)
                elif part == 1:
                    y = _l2n(y)
                dst[t, :, h * LANES:(h + 1) * LANES] = y

    for t in range(ts):
        g_all, beta_all = _gdn_gates(sm_ref[t], alog_ref[...], dtb_ref[...])
        a_all = jnp.exp(g_all)
        for h in range(GDN_HEADS):
            a_s[t, :, h * LANES:(h + 1) * LANES] = jnp.broadcast_to(a_all[:, SM_A + h:SM_A + h + 1], (G, LANES))
            b_s[t, :, h * LANES:(h + 1) * LANES] = jnp.broadcast_to(beta_all[:, SM_B + h:SM_B + h + 1], (G, LANES))

    nv = 2 * ts
    sr = lax.broadcasted_iota(jnp.int32, (3 * nv, nv * LANES), 0)
    sc = lax.broadcasted_iota(jnp.int32, (3 * nv, nv * LANES), 1)
    spread = (sr % nv == sc // LANES).astype(bf16)

    def head(h, carry):
        cols = pl.ds(pl.multiple_of(h * LANES, LANES), LANES)
        for i in range(G):
            row = slice(i, i + 1)
            kq = jnp.concatenate([k_s[t, row, cols] for t in range(ts)]
                                 + [q_s[t, row, cols] for t in range(ts)], axis=0)
            hi = kq.astype(bf16).astype(f32)
            mid = (kq - hi).astype(bf16).astype(f32)
            lo = (kq - hi) - mid
            kq_b = _dot(jnp.concatenate([hi, mid, lo], axis=0).T.astype(bf16), spread)
            s = sin_ref[i, h]
            for t in range(ts):
                kc = kq_b[:, t * LANES:(t + 1) * LANES]
                qc = kq_b[:, (ts + t) * LANES:(ts + t + 1) * LANES]
                a = a_s[t, row, cols]
                b = b_s[t, row, cols]
                ks = jnp.sum(s * kc, axis=0, keepdims=True)
                r = b * (v_s[t, row, cols] - a * ks)
                s = a * s + kc * r
                o_s[t, row, cols] = jnp.sum(s * qc, axis=0, keepdims=True)
            sout_ref[i, h] = s
        return carry

    lax.fori_loop(0, GDN_HEADS, head, 0)

    for t in range(ts):
        for h in range(GDN_HEADS):
            sl = slice(h * GDN_DV, (h + 1) * GDN_DV)
            o_ref[t, :, sl] = _gated_norm(o_s[t, :, sl], gn_ref[...], z_ref[t, :, sl]).astype(bf16)


def _gdn_sample(y3, y3_small, buf3, state, w_conv, a_log, dt_bias, g_gdn_norm):
    ts, db, _ = y3.shape
    G = SUBLANES
    alog, dtb, gn = _gdn_vecs(a_log, dt_bias, g_gdn_norm)
    cst = lambda g: (0, 0)
    st_spec = pl.BlockSpec((G, GDN_HEADS, GDN_DK, GDN_DV), lambda g: (g, 0, 0, 0))
    scr = pltpu.VMEM((ts, G, Z_W), f32)
    return pl.pallas_call(
        _gdn_sample_kernel,
        grid=(db // G,),
        in_specs=[pl.BlockSpec((ts, G, QKV_W), lambda g: (0, g, 0)),
                  pl.BlockSpec((GDN_CONV - 1, G, QKV_W), lambda g: (0, g, 0)),
                  pl.BlockSpec((ts, G, Z_W), lambda g: (0, g, OFF_Z // Z_W)),
                  pl.BlockSpec((ts, G, LANES), lambda g: (0, g, 0)),
                  st_spec,
                  pl.BlockSpec((GDN_CONV, QKV_W), cst),
                  pl.BlockSpec((1, LANES), cst),
                  pl.BlockSpec((1, LANES), cst),
                  pl.BlockSpec((1, GDN_DV), cst)],
        out_specs=[pl.BlockSpec((ts, G, Z_W), lambda g: (0, g, 0)), st_spec],
        out_shape=[jax.ShapeDtypeStruct((ts, db, Z_W), bf16),
                   jax.ShapeDtypeStruct(state.shape, f32)],
        scratch_shapes=[scr, scr, scr, scr, scr, scr],
        compiler_params=_params(48),
        name="gdn_sample",
    )(y3, buf3, y3, y3_small, state, w_conv, alog, dtb, gn)


def _rope_tables(pos, rot, width):
    half = rot // 2
    inv_freq = ROPE_THETA ** (-jnp.arange(half, dtype=f32) * (2.0 / rot))
    ang = pos.astype(f32)[:, None] * inv_freq[None, :]
    cos, sin = jnp.cos(ang), jnp.sin(ang)
    n = pos.shape[0]
    z = lambda w: jnp.zeros((n, w), f32)
    cosw = jnp.concatenate([cos, cos, jnp.ones((n, width - rot), f32)], axis=1)
    sina = jnp.concatenate([-sin, z(width - half)], axis=1)
    sinb = jnp.concatenate([z(half), sin, z(width - rot)], axis=1)
    reps = LANES // width
    return tuple(jnp.tile(a, (1, reps)) for a in (cosw, sina, sinb))


def _rope_kernel(dq_ref, dk_ref, dv_ref, iq_ref, sm_ref, c1, sa1, sb1, c2, sa2, sb2,
                 dq_o, dk_o, iq_o, ik_o, sm_o, kc_o, vc_o):
    tm = dq_ref.shape[0]
    h1 = DSA_HEAD_DIM // ROPE_FRACTION // 2
    h2 = IDX_DIM // ROPE_FRACTION // 2

    def rot(x, c, sa, sb, half):
        return x * c[...] + pltpu.roll(x, LANES - half, 1) * sa[...] + pltpu.roll(x, half, 1) * sb[...]

    for j in range(DQ_W // LANES):
        sl = slice(j * LANES, (j + 1) * LANES)
        dq_o[:, sl] = rot(dq_ref[:, sl], c1, sa1, sb1, h1).astype(bf16)
    for g in range(DSA_KV_HEADS):
        sl = slice(g * DSA_HEAD_DIM, (g + 1) * DSA_HEAD_DIM)
        kg = rot(dk_ref[:, sl], c1, sa1, sb1, h1)
        dk_o[:, sl] = kg
        kc_o[pl.ds(g, tm, stride=DSA_KV_HEADS), :] = kg
        vc_o[pl.ds(g, tm, stride=DSA_KV_HEADS), :] = dv_ref[:, sl]
    for j in range(IQ_W // LANES):
        sl = slice(j * LANES, (j + 1) * LANES)
        iq_o[:, sl] = rot(iq_ref[:, sl], c2, sa2, sb2, h2).astype(bf16)
    sm = sm_ref[...]
    ik_o[...] = rot(sm, c2, sa2, sb2, h2)[:, :IDX_DIM]
    sm_o[...] = sm


def _rope(y, y_small, pos, tm, group, db=None):
    m = y.shape[0]
    t1 =_rope_tables(pos, DSA_HEAD_DIM // ROPE_FRACTION, DSA_HEAD_DIM)
    t2 = _rope_tables(pos, IDX_DIM // ROPE_FRACTION, IDX_DIM)
    tab_blocks = pos.shape[0] // tm
    tab = pl.BlockSpec((tm, LANES), lambda i: (i % tab_blocks, 0))
    if group == "prompt":
        omap = lambda i: (i, 0)
        rows = lambda w: m
        cols = lambda w: w
    else:
        ts = m // db
        omap = lambda i: (0, i)
        rows = lambda w: db
        cols = lambda w: ts * w
    out = lambda w, dt: jax.ShapeDtypeStruct((rows(w), cols(w)), dt)
    return pl.pallas_call(
        _rope_kernel,
        grid=(m // tm,),
        in_specs=[pl.BlockSpec((tm, DQ_W), lambda i: (i, OFF_DQ // DQ_W)),
                  pl.BlockSpec((tm, DKV_W), lambda i: (i, OFF_DK // DKV_W)),
                  pl.BlockSpec((tm, DKV_W), lambda i: (i, OFF_DV // DKV_W)),
                  pl.BlockSpec((tm, IQ_W), lambda i: (i, OFF_IQ // IQ_W)),
                  pl.BlockSpec((tm, LANES), lambda i: (i, 0)),
                  tab, tab, tab, tab, tab, tab],
        out_specs=[pl.BlockSpec((tm, DQ_W), omap),
                   pl.BlockSpec((tm, DKV_W), omap),
                   pl.BlockSpec((tm, IQ_W), omap),
                   pl.BlockSpec((tm, IDX_DIM), lambda i: (i, 0)),
                   pl.BlockSpec((tm, LANES), omap),
                   pl.BlockSpec((DSA_KV_HEADS * tm, DSA_HEAD_DIM), lambda i: (i, 0)),
                   pl.BlockSpec((DSA_KV_HEADS * tm, DSA_HEAD_DIM), lambda i: (i, 0))],
        out_shape=[out(DQ_W, bf16), out(DKV_W, f32), out(IQ_W, bf16),
                   jax.ShapeDtypeStruct((m, IDX_DIM), f32), out(LANES, f32),
                   jax.ShapeDtypeStruct((DSA_KV_HEADS * m, DSA_HEAD_DIM), f32),
                   jax.ShapeDtypeStruct((DSA_KV_HEADS * m, DSA_HEAD_DIM), f32)],
        compiler_params=_params(),
        name="rope_" + group,
    )(y, y, y, y, y_small, *t1, *t2)


def _topk_bias(x_s, bias_s, lo0, mx, few, nkt, kt_w, k):
    R = x_s.shape[0]
    kf = float(k)
    tiles = [slice(kt * kt_w, (kt + 1) * kt_w) for kt in range(nkt)]
    n_grp = 2 if R % (2 * SUBLANES) == 0 else 1
    grps = [slice(g * (R // n_grp), (g + 1) * (R // n_grp)) for g in range(n_grp)]

    def count_ge(rows, th):
        acc = jnp.where(x_s[rows, tiles[0]] >= th, 1.0, 0.0)
        for t in tiles[1:]:
            acc = acc + jnp.where(x_s[rows, t] >= th, 1.0, 0.0)
        return jnp.sum(acc, axis=1, keepdims=True)

    def bisect(rows, lo, hi, done, last):
        mid = 0.5 * lo + 0.5 * hi
        cnt = count_ge(rows, mid)
        if last:
            stuck = jnp.logical_or(mid <= lo, mid >= hi)
            done = jnp.where(jnp.logical_or(stuck, cnt == kf), 1.0, done)
        return jnp.where(cnt >= kf, mid, lo), jnp.where(cnt <= kf, mid, hi), done

    def snap(rows, lo, hi, done):
        a = b = None
        for t in tiles:
            x = x_s[rows, t]
            at = jnp.where(x >= lo, x, POS)
            bt = jnp.where(x < hi, x, NEG)
            a = at if a is None else jnp.minimum(a, at)
            b = bt if b is None else jnp.maximum(b, bt)
        a = jnp.min(a, axis=1, keepdims=True)
        b = jnp.max(b, axis=1, keepdims=True)
        live = done < 0.5
        return jnp.where(live, a, lo), jnp.where(jnp.logical_and(live, a >= b), 1.0, done)

    def cond(c):
        it, st = c
        left = st[0][2]
        for s in st[1:]:
            left = jnp.minimum(left, s[2])
        return jnp.logical_and(it < BISECT_CAP, jnp.min(left) < 0.5)

    def body(c):
        it, st = c
        for u in range(BISECT_UNROLL):
            st = [bisect(rows, *s, last=u == BISECT_UNROLL - 1) for rows, s in zip(grps, st)]

        def snapped():
            out = []
            for rows, (lo, hi, done) in zip(grps, st):
                lo2, done2 = snap(rows, lo, hi, done)
                out.append((lo2, hi, done2))
            return out

        st = lax.cond(it >= SNAP_FROM, snapped, lambda: st)
        return it + 1, st

    hi0 = jnp.where(few, lo0, mx + (jnp.abs(mx) + 1.0))
    st0 = [(lo0[rows], hi0[rows], few[rows].astype(f32)) for rows in grps]
    _, st = lax.while_loop(cond, body, (jnp.int32(0), st0))
    lo = jnp.concatenate([s[0] for s in st], axis=0)
    hi = jnp.concatenate([s[1] for s in st], axis=0)

    has_run = jnp.max(jnp.where(lo < hi, 1.0, 0.0)) > 0.5
    rows_all = slice(0, R)

    @pl.when(jnp.logical_not(has_run))
    def _():
        for t in tiles:
            bias_s[:, t] = jnp.where(x_s[:, t] >= hi, 0.0, NEG)

    @pl.when(has_run)
    def _():
        need = kf - count_ge(rows_all, hi)
        ai = lax.broadcasted_iota(jnp.int32, (kt_w, kt_w), 0)
        bi = lax.broadcasted_iota(jnp.int32, (kt_w, kt_w), 1)
        before = (ai < bi).astype(bf16)
        seen = jnp.zeros((R, 1), f32)
        for t in tiles:
            x = x_s[:, t]
            run = jnp.logical_and(x >= lo, x < hi)
            runf = run.astype(f32)
            rank = seen + _dot(runf.astype(bf16), before)
            take = jnp.logical_or(x >= hi, jnp.logical_and(run, rank < need))
            bias_s[:, t] = jnp.where(take, 0.0, NEG)
            seen = seen + jnp.sum(runf, axis=1, keepdims=True)


def _dsa_prompt_kernel(iq_ref, sm_ref, ik_ref, q_ref, k_ref, v_ref, o_ref,
                       ikb, kb, vb, x_s, bias_s, s_s, *, topk, kt_w, ext_w):
    qi = pl.program_id(1)
    R = iq_ref.shape[0]
    T = ik_ref.shape[0]

    @pl.when(qi == 0)
    def _():
        ikb[...] = ik_ref[...].astype(bf16)
        kb[...] = k_ref[...].astype(bf16)
        vb[...] = v_ref[...].astype(bf16)

    def block(ext):
        tiles = [slice(j * kt_w, (j + 1) * kt_w) for j in range(ext // kt_w)]
        wgt = sm_ref[...] * IDX_SCALE
        qpos = qi * R + lax.broadcasted_iota(jnp.int32, (R, kt_w), 0)
        key0 = lax.broadcasted_iota(jnp.int32, (R, kt_w), 1)
        mn = mx = None
        for j, t in enumerate(tiles):
            keys = ikb[t, :]
            sc = jnp.zeros((R, kt_w), f32)
            for h in range(IDX_HEADS):
                rel = jnp.maximum(_dot_nt(iq_ref[:, h * IDX_DIM:(h + 1) * IDX_DIM], keys), 0.0)
                sc = sc + rel * wgt[:, SM_IW + h:SM_IW + h + 1]
            causal = key0 + j * kt_w <= qpos
            x_s[:, t] = jnp.where(causal, sc, NEG)
            lo_t = jnp.where(causal, sc, POS)
            hi_t = jnp.where(causal, sc, NEG)
            mn = lo_t if mn is None else jnp.minimum(mn, lo_t)
            mx = hi_t if mx is None else jnp.maximum(mx, hi_t)
        n_causal = qi * R + lax.broadcasted_iota(jnp.int32, (R, 1), 0) + 1
        _topk_bias(x_s, bias_s, jnp.min(mn, axis=1, keepdims=True), jnp.max(mx, axis=1, keepdims=True),
                   n_causal <= topk, len(tiles), kt_w, topk)

        def heads(hp, carry):
            h0 = hp * DSA_HEAD_GROUP
            g = h0 // (DSA_HEADS // DSA_KV_HEADS)
            gc = pl.ds(pl.multiple_of(g * DSA_HEAD_DIM, DSA_HEAD_DIM), DSA_HEAD_DIM)
            hcs = [pl.ds(pl.multiple_of((h0 + i) * DSA_HEAD_DIM, DSA_HEAD_DIM), DSA_HEAD_DIM)
                   for i in range(DSA_HEAD_GROUP)]
            qs = [q_ref[:, hc] for hc in hcs]
            mxa = [None] * DSA_HEAD_GROUP
            for t in tiles:
                kt = kb[t, gc]
                bt = bias_s[:, t]
                for i in range(DSA_HEAD_GROUP):
                    s = _dot_nt(qs[i], kt) * (DSA_HEAD_DIM ** -0.5) + bt
                    s_s[i, :, t] = s
                    mxa[i] = s if mxa[i] is None else jnp.maximum(mxa[i], s)
            ms = [jnp.max(a, axis=1, keepdims=True) for a in mxa]
            la = [jnp.zeros((R, kt_w), f32)] * DSA_HEAD_GROUP
            acc = [jnp.zeros((R, DSA_HEAD_DIM), f32)] * DSA_HEAD_GROUP
            for t in tiles:
                vt = vb[t, gc]
                for i in range(DSA_HEAD_GROUP):
                    p = jnp.exp(s_s[i, :, t] - ms[i])
                    la[i] = la[i] + p
                    acc[i] = acc[i] + _dot(p.astype(bf16), vt)
            for i in range(DSA_HEAD_GROUP):
                o_ref[:, hcs[i]] = (acc[i] / jnp.sum(la[i], axis=1, keepdims=True)).astype(bf16)
            return carry

        lax.fori_loop(0, DSA_HEADS // DSA_HEAD_GROUP, heads, 0)

    for e in range(T // ext_w):
        pl.when(qi // (ext_w // R) == e)(functools.partial(block, (e + 1) * ext_w))


def _dsa_prompt(iq, sm, ik, dq, dk, y, nb, t, topk):
    R = LANES
    nq = t // R
    kt_w = min(DSA_KEY_TILE, t)
    ext_w = min(DSA_KEY_EXTENT, t)
    kern = functools.partial(_dsa_prompt_kernel, topk=topk, kt_w=kt_w, ext_w=ext_w)
    return pl.pallas_call(
        kern,
        grid=(nb, nq),
        in_specs=[pl.BlockSpec((R, IQ_W), lambda b, i: (b * nq + i, 0)),
                  pl.BlockSpec((R, LANES), lambda b, i: (b * nq + i, 0)),
                  pl.BlockSpec((t, IDX_DIM), lambda b, i: (b, 0)),
                  pl.BlockSpec((R, DQ_W), lambda b, i: (b * nq + i, 0)),
                  pl.BlockSpec((t, DKV_W), lambda b, i: (b, 0)),
                  pl.BlockSpec((t, DKV_W), lambda b, i: (b, OFF_DV // DKV_W))],
        out_specs=pl.BlockSpec((R, DQ_W), lambda b, i: (b * nq + i, 0)),
        out_shape=jax.ShapeDtypeStruct((nb * t, DQ_W), bf16),
        scratch_shapes=[pltpu.VMEM((t, IDX_DIM), bf16), pltpu.VMEM((t, DKV_W), bf16),
                        pltpu.VMEM((t, DKV_W), bf16), pltpu.VMEM((R, t), f32), pltpu.VMEM((R, t), f32),
                        pltpu.VMEM((DSA_HEAD_GROUP, R, t), f32)],
        compiler_params=_params(48, 2),
        name="dsa_prompt",
    )(iq, sm, ik, dq, dk, y)


def _idx_score_kernel(pt_ref, iq_ref, sm_ref, ikn_ref, *rest, n_pages, page):
    eb, tp, _ = iq_ref.shape
    o_ref = rest[eb * n_pages]
    ts = o_ref.shape[1]
    past = n_pages * page
    lp = o_ref.shape[2]
    for e in range(eb):
        pages = rest[e * n_pages:(e + 1) * n_pages]
        keys_t = jnp.concatenate([p[...] for p in pages], axis=1).astype(bf16)
        keys_n = jnp.concatenate([ikn_ref[e], jnp.zeros((lp - past - tp, IDX_DIM), f32)], axis=0).astype(bf16)
        wgt = sm_ref[e] * IDX_SCALE
        iq = iq_ref[e].astype(bf16)
        score = jnp.zeros((tp, lp), f32)
        for h in range(IDX_HEADS):
            iqh = iq[:, h * IDX_DIM:(h + 1) * IDX_DIM]
            rel = jnp.maximum(jnp.concatenate([_dot(iqh, keys_t), _dot_nt(iqh, keys_n)], axis=1), 0.0)
            score = score + rel * wgt[:, SM_IW + h:SM_IW + h + 1]
        o_ref[e] = score[:ts]


def _idx_scores(page_table, iq, sm, ikn, cache_idx, lp, ts):
    db, tp, _ = iq.shape
    n_pages = page_table.shape[1]
    page = cache_idx.shape[2]
    eb = IDX_BATCH
    kern = functools.partial(_idx_score_kernel, n_pages=n_pages, page=page)
    page_specs = [pl.BlockSpec((None, IDX_DIM, page),
                               functools.partial(lambda b, pt, e, p: (pt[b * eb + e, p], 0, 0), e=e, p=p))
                  for e in range(eb) for p in range(n_pages)]
    grid_spec = pltpu.PrefetchScalarGridSpec(
        num_scalar_prefetch=1,
        grid=(db // eb,),
        in_specs=[pl.BlockSpec((eb, tp, IQ_W), lambda b, pt: (b, 0, 0)),
                  pl.BlockSpec((eb, tp, LANES), lambda b, pt: (b, 0, 0)),
                  pl.BlockSpec((eb, tp, IDX_DIM), lambda b, pt: (b, 0, 0))] + page_specs,
        out_specs=pl.BlockSpec((eb, ts, lp), lambda b, pt: (b, 0, 0)),
    )
    return pl.pallas_call(
        kern, grid_spec=grid_spec,
        out_shape=jax.ShapeDtypeStruct((db, ts, lp), f32),
        compiler_params=_params(),
        name="idx_scores_sample",
    )(page_table, iq, sm, ikn, *([cache_idx] * (eb * n_pages)))


def _idx_ring_kernel(pt_ref, iq_ref, sm_ref, ikn_ref, cache_hbm, o_ref, pbuf, sem, *, n_pages, page):
    s = pl.program_id(0)
    n_steps = pl.num_programs(0)
    eb, tp, _ = iq_ref.shape
    ts = o_ref.shape[1]
    past = n_pages * page
    lp = o_ref.shape[2]

    def copies(step, slot):
        return [pltpu.make_async_copy(cache_hbm.at[pt_ref[step * eb + e, p]], pbuf.at[slot, e * n_pages + p],
                                      sem.at[slot])
                for e in range(eb) for p in range(n_pages)]

    @pl.when(s == 0)
    def _():
        for k in range(DSA_RING - 1):
            for c in copies(k, k):
                c.start()

    ahead = s + (DSA_RING - 1)

    @pl.when(ahead < n_steps)
    def _():
        for c in copies(ahead, ahead % DSA_RING):
            c.start()

    slot = s % DSA_RING
    for c in copies(s, slot):
        c.wait()

    for e in range(eb):
        keys_t = jnp.concatenate([pbuf[slot, e * n_pages + p] for p in range(n_pages)], axis=1).astype(bf16)
        keys_n = jnp.concatenate([ikn_ref[e], jnp.zeros((lp - past - tp, IDX_DIM), f32)], axis=0).astype(bf16)
        wgt = sm_ref[e] * IDX_SCALE
        iq = iq_ref[e].astype(bf16)
        score = jnp.zeros((tp, lp), f32)
        for h in range(IDX_HEADS):
            iqh = iq[:, h * IDX_DIM:(h + 1) * IDX_DIM]
            rel = jnp.maximum(jnp.concatenate([_dot(iqh, keys_t), _dot_nt(iqh, keys_n)], axis=1), 0.0)
            score = score + rel * wgt[:, SM_IW + h:SM_IW + h + 1]
        o_ref[e] = score[:ts]


def _idx_scores_ring(page_table, iq, sm, ikn, cache_idx, lp, ts):
    db, tp, _ = iq.shape
    n_pages = page_table.shape[1]
    page = cache_idx.shape[2]
    eb = IDX_BATCH
    assert db // eb >= DSA_RING
    kern = functools.partial(_idx_ring_kernel, n_pages=n_pages, page=page)
    grid_spec = pltpu.PrefetchScalarGridSpec(
        num_scalar_prefetch=1,
        grid=(db // eb,),
        in_specs=[pl.BlockSpec((eb, tp, IQ_W), lambda b, pt: (b, 0, 0)),
                  pl.BlockSpec((eb, tp, LANES), lambda b, pt: (b, 0, 0)),
                  pl.BlockSpec((eb, tp, IDX_DIM), lambda b, pt: (b, 0, 0)),
                  pl.BlockSpec(memory_space=pl.ANY)],
        out_specs=pl.BlockSpec((eb, ts, lp), lambda b, pt: (b, 0, 0)),
        scratch_shapes=[pltpu.VMEM((DSA_RING, eb * n_pages, IDX_DIM, page), f32),
                        pltpu.SemaphoreType.DMA((DSA_RING,))],
    )
    return pl.pallas_call(
        kern, grid_spec=grid_spec,
        out_shape=jax.ShapeDtypeStruct((db, ts, lp), f32),
        compiler_params=_params(48),
        name="idx_scores_ring",
    )(page_table, iq, sm, ikn, cache_idx)


def _select_sample_kernel(x_ref, bias_ref, x_s, *, topk, past, ts):
    R, lp = x_ref.shape
    key = lax.broadcasted_iota(jnp.int32, (R, lp), 1)
    t = lax.broadcasted_iota(jnp.int32, (R, lp), 0) % ts
    causal = key <= past + t
    x = x_ref[...]
    x_s[...] = jnp.where(causal, x, NEG)
    lo0 = jnp.min(jnp.where(causal, x, POS), axis=1, keepdims=True)
    mx = jnp.max(jnp.where(causal, x, NEG), axis=1, keepdims=True)
    n_causal = past + lax.broadcasted_iota(jnp.int32, (R, 1), 0) % ts + 1
    _topk_bias(x_s, bias_ref, lo0, mx, n_causal <= topk, lp // LANES, LANES, topk)


def _select_sample(scores, topk, past, ts):
    m, lp = scores.shape
    R = LANES
    kern = functools.partial(_select_sample_kernel, topk=topk, past=past, ts=ts)
    return pl.pallas_call(
        kern, grid=(m // R,),
        in_specs=[pl.BlockSpec((R, lp), lambda i: (i, 0))],
        out_specs=pl.BlockSpec((R, lp), lambda i: (i, 0)),
        out_shape=jax.ShapeDtypeStruct((m, lp), f32),
        scratch_shapes=[pltpu.VMEM((R, lp), f32)],
        compiler_params=_params(),
        name="select_sample",
    )(scores)


def _dsa_sample_kernel(pt_ref, q_ref, sel_ref, kn_ref, vn_ref, *rest, n_pages, page):
    eb, tp, _ = q_ref.shape
    o_ref = rest[2 * eb * n_pages]
    ts = sel_ref.shape[1]
    past = n_pages * page
    lp = sel_ref.shape[2]
    hpg = DSA_HEADS // DSA_KV_HEADS
    padn = jnp.zeros((lp - past - tp, DSA_HEAD_DIM), f32)
    for e in range(eb):
        kp = rest[e * n_pages:(e + 1) * n_pages]
        vp = rest[(eb + e) * n_pages:(eb + e + 1) * n_pages]
        bias = jnp.concatenate([sel_ref[e], jnp.zeros((tp - ts, lp), f32)], axis=0)
        bias = jnp.concatenate([bias] * hpg, axis=0)
        for g in range(DSA_KV_HEADS):
            gsl = slice(g * DSA_HEAD_DIM, (g + 1) * DSA_HEAD_DIM)
            kg = jnp.concatenate([p[pl.ds(g, page, stride=DSA_KV_HEADS), :] for p in kp]
                                 + [kn_ref[e, :, gsl], padn], axis=0).astype(bf16)
            vg = jnp.concatenate([p[pl.ds(g, page, stride=DSA_KV_HEADS), :] for p in vp]
                                 + [vn_ref[e, :, gsl], padn], axis=0).astype(bf16)
            qg = jnp.concatenate([q_ref[e, :, (g * hpg + j) * DSA_HEAD_DIM:(g * hpg + j + 1) * DSA_HEAD_DIM]
                                  for j in range(hpg)], axis=0).astype(bf16)
            s = _dot_nt(qg, kg) * (DSA_HEAD_DIM ** -0.5) + bias
            m = jnp.max(s, axis=1, keepdims=True)
            p = jnp.exp(s - m)
            l = jnp.sum(p, axis=1, keepdims=True)
            o = _dot(p.astype(bf16), vg) / l
            for j in range(hpg):
                hsl = slice((g * hpg + j) * DSA_HEAD_DIM, (g * hpg + j + 1) * DSA_HEAD_DIM)
                o_ref[e, :, hsl] = o[j * tp:j * tp + ts]


def _dsa_sample(page_table, dq, sel, kn, vn, ck, cv):
    db, tp, _ = dq.shape
    ts = sel.shape[1]
    lp = sel.shape[2]
    n_pages = page_table.shape[1]
    rows = ck.shape[1]
    page = rows // DSA_KV_HEADS
    eb = DSA_SAMPLE_BATCH
    kern = functools.partial(_dsa_sample_kernel, n_pages=n_pages, page=page)
    pspec = [pl.BlockSpec((None, rows, DSA_HEAD_DIM),
                          functools.partial(lambda b, pt, e, p: (pt[b * eb + e, p], 0, 0), e=e, p=p))
             for e in range(eb) for p in range(n_pages)]
    per_b = lambda r, w: pl.BlockSpec((eb, r, w), lambda b, pt: (b, 0, 0))
    grid_spec = pltpu.PrefetchScalarGridSpec(
        num_scalar_prefetch=1,
        grid=(db // eb,),
        in_specs=[per_b(tp, DQ_W), per_b(ts, lp), per_b(tp, DKV_W), per_b(tp, DKV_W)] + pspec + pspec,
        out_specs=per_b(ts, DQ_W),
    )
    return pl.pallas_call(
        kern, grid_spec=grid_spec,
        out_shape=jax.ShapeDtypeStruct((db, ts, DQ_W), f32),
        compiler_params=_params(48),
        name="dsa_sample",
    )(page_table, dq, sel, kn, vn, *([ck] * (eb * n_pages)), *([cv] * (eb * n_pages)))


def _dsa_ring_kernel(pt_ref, q_ref, sel_ref, kn_ref, vn_ref, ck_hbm, cv_hbm, o_ref, kbuf, vbuf, sem,
                     *, n_pages, page):
    s = pl.program_id(0)
    n_steps = pl.num_programs(0)
    eb, tp, _ = q_ref.shape
    ts = sel_ref.shape[1]
    lp = sel_ref.shape[2]
    past = n_pages * page
    hpg = DSA_HEADS // DSA_KV_HEADS

    def copies(step, slot):
        out = []
        for e in range(eb):
            for p in range(n_pages):
                pg = pt_ref[step * eb + e, p]
                out.append(pltpu.make_async_copy(ck_hbm.at[pg], kbuf.at[slot, e * n_pages + p], sem.at[slot, 0]))
                out.append(pltpu.make_async_copy(cv_hbm.at[pg], vbuf.at[slot, e * n_pages + p], sem.at[slot, 1]))
        return out

    @pl.when(s == 0)
    def _():
        for k in range(DSA_RING - 1):
            for c in copies(k, k):
                c.start()

    ahead = s + (DSA_RING - 1)

    @pl.when(ahead < n_steps)
    def _():
        for c in copies(ahead, ahead % DSA_RING):
            c.start()

    slot = s % DSA_RING
    for c in copies(s, slot):
        c.wait()

    padn = jnp.zeros((lp - past - tp, DSA_HEAD_DIM), f32)
    for e in range(eb):
        bias = jnp.concatenate([sel_ref[e], jnp.zeros((tp - ts, lp), f32)], axis=0)
        bias = jnp.concatenate([bias] * hpg, axis=0)
        for g in range(DSA_KV_HEADS):
            gsl = slice(g * DSA_HEAD_DIM, (g + 1) * DSA_HEAD_DIM)
            rows_g = pl.ds(g, page, stride=DSA_KV_HEADS)
            kg = jnp.concatenate([kbuf[slot, e * n_pages + p, rows_g, :] for p in range(n_pages)]
                                 + [kn_ref[e, :, gsl], padn], axis=0).astype(bf16)
            vg = jnp.concatenate([vbuf[slot, e * n_pages + p, rows_g, :] for p in range(n_pages)]
                                 + [vn_ref[e, :, gsl], padn], axis=0).astype(bf16)
            qg = jnp.concatenate([q_ref[e, :, (g * hpg + j) * DSA_HEAD_DIM:(g * hpg + j + 1) * DSA_HEAD_DIM]
                                  for j in range(hpg)], axis=0).astype(bf16)
            sc = _dot_nt(qg, kg) * (DSA_HEAD_DIM ** -0.5) + bias
            m = jnp.max(sc, axis=1, keepdims=True)
            p_ = jnp.exp(sc - m)
            l = jnp.sum(p_, axis=1, keepdims=True)
            o = _dot(p_.astype(bf16), vg) / l
            for j in range(hpg):
                hsl = slice((g * hpg + j) * DSA_HEAD_DIM, (g * hpg + j + 1) * DSA_HEAD_DIM)
                o_ref[e, :, hsl] = o[j * tp:j * tp + ts]


def _dsa_sample_ring(page_table, dq, sel, kn, vn, ck, cv):
    db, tp, _ = dq.shape
    ts = sel.shape[1]
    lp = sel.shape[2]
    n_pages = page_table.shape[1]
    rows = ck.shape[1]
    page = rows // DSA_KV_HEADS
    eb = DSA_RING_BATCH
    assert db // eb >= DSA_RING
    kern = functools.partial(_dsa_ring_kernel, n_pages=n_pages, page=page)
    per_b = lambda r, w: pl.BlockSpec((eb, r, w), lambda b, pt: (b, 0, 0))
    any_spec = pl.BlockSpec(memory_space=pl.ANY)
    buf = pltpu.VMEM((DSA_RING, eb * n_pages, rows, DSA_HEAD_DIM), f32)
    grid_spec = pltpu.PrefetchScalarGridSpec(
        num_scalar_prefetch=1,
        grid=(db // eb,),
        in_specs=[per_b(tp, DQ_W), per_b(ts, lp), per_b(tp, DKV_W), per_b(tp, DKV_W), any_spec, any_spec],
        out_specs=per_b(ts, DQ_W),
        scratch_shapes=[buf, buf, pltpu.SemaphoreType.DMA((DSA_RING, 2))],
    )
    return pl.pallas_call(
        kern, grid_spec=grid_spec,
        out_shape=jax.ShapeDtypeStruct((db, ts, DQ_W), f32),
        compiler_params=_params(48),
        name="dsa_sample_ring",
    )(page_table, dq, sel, kn, vn, ck, cv)


def _merge_kernel(og_ref, od_ref, gg_ref, gd_ref, x_ref, gt_ref, sc_ref, sh_ref, g2_ref,
                  wg_ref, wd_ref, wo_ref, x1_ref, h2_ref):
    mix = (_sigmoid(gg_ref[...]) * _dot(og_ref[...], wg_ref[...])
           + _sigmoid(gd_ref[...]) * _dot(od_ref[...], wd_ref[...]))
    x1 = x_ref[...] + gt_ref[...] * _dot(mix.astype(bf16), wo_ref[...])
    x1_ref[...] = x1
    y = x1 * lax.rsqrt(jnp.mean(x1 * x1, axis=-1, keepdims=True) + NORM_EPS)
    h2_ref[...] = ((y * g2_ref[...]) * (1.0 + sc_ref[...]) + sh_ref[...]).astype(bf16)


def _merge(o_gdn, o_dsa, y, x, ada, g2, wg, wd, wo, group, rows_per_batch, tm):
    m, d = x.shape
    row = lambda w, blk=0: pl.BlockSpec((tm, w), lambda i: (i, blk))
    res = lambda a: pl.BlockSpec(a.shape, lambda i: (0, 0), pipeline_mode=pl.Buffered(1))
    return pl.pallas_call(
        _merge_kernel,
        grid=(m // tm,),
        in_specs=[row(Z_W), row(DQ_W), row(d, 0), row(d, 1), row(d),
                  _mod_spec(group, d, rows_per_batch, tm, 2),
                  _mod_spec(group, d, rows_per_batch, tm, 4),
                  _mod_spec(group, d, rows_per_batch, tm, 3),
                  pl.BlockSpec((1, d), lambda i: (0, 0)),
                  res(wg), res(wd), res(wo)],
        out_specs=[row(d), row(d)],
        out_shape=[jax.ShapeDtypeStruct((m, d), f32), jax.ShapeDtypeStruct((m, d), bf16)],
        compiler_params=_params(48),
        name="merge_" + group,
    )(o_gdn, o_dsa, y, y, x, ada, ada, ada, g2.reshape(1, d), wg, wd, wo)


def _ffn_epilogue(acc_ref, x1_ref, gt_ref, gf_ref, y_ref):
    tm = acc_ref.shape[0]
    r = gt_ref.shape[0] if gt_ref.shape[0] > 1 else tm
    for s0 in range(0, tm, r):
        sl = slice(s0, s0 + r)
        x2 = x1_ref[sl, :] + gt_ref[...] * acc_ref[sl, :]
        y_ref[sl, :] = (x2 * lax.rsqrt(jnp.mean(x2 * x2, axis=-1, keepdims=True) + NORM_EPS)) * gf_ref[...]


def _ffn_prompt_kernel(h_ref, wug_ref, wuv_ref, wcg_ref, wcv_ref, bg_ref, bv_ref, wd_ref,
                       x1_ref, gt_ref, gf_ref, y_ref, ug_ref, uv_ref, acc_ref, act_s, prev_s,
                       *, blocks_per_seq, n_up):
    i = pl.program_id(0)
    j = pl.program_id(1)
    tm = h_ref.shape[0]
    keep = (i % blocks_per_seq) > 0
    sr = min(FFN_SUB_ROWS, tm)

    @pl.when(jnp.logical_and(i == 0, j == 0))
    def _():
        prev_s[...] = jnp.zeros_like(prev_s)

    def conv(u, prev, wc_ref, b_ref):
        ux = jnp.concatenate([prev, u], axis=0)
        w = wc_ref[...]
        y = u * w[FFN_CONV - 1:FFN_CONV] + b_ref[...]
        for t in range(FFN_CONV - 1):
            sft = FFN_CONV - 1 - t
            y = y + ux[SUBLANES - sft:SUBLANES - sft + sr] * w[t:t + 1]
        return y

    def up(store):
        pg = jnp.where(keep, prev_s[j, 0], 0.0)
        pv = jnp.where(keep, prev_s[j, 1], 0.0)
        for s in range(tm // sr):
            rows = slice(s * sr, (s + 1) * sr)
            hs = h_ref[rows, :]
            ug = _dot(hs, wug_ref[...])
            uv = _dot(hs, wuv_ref[...])
            store(rows, _silu(conv(ug, pg, wcg_ref, bg_ref)) * conv(uv, pv, wcv_ref, bv_ref))
            pg = ug[sr - SUBLANES:]
            pv = uv[sr - SUBLANES:]
        ug_ref[...] = pg
        uv_ref[...] = pv
        prev_s[j, 0] = pg
        prev_s[j, 1] = pv

    _ffn_phases(j, n_up, up, act_s, wd_ref, acc_ref, x1_ref, gt_ref, gf_ref, y_ref)


def _ffn_phases(j, n_up, up, act_s, wd_ref, acc_ref, x1_ref, gt_ref, gf_ref, y_ref):
    tf = act_s.shape[1] // n_up
    tn = wd_ref.shape[1]

    @pl.when(j < n_up)
    def _():
        cols = pl.ds(pl.multiple_of(j * tf, tf), tf)

        def store(rows, act):
            act_s[rows, cols] = act.astype(bf16)
        up(store)

    @pl.when(j >= n_up)
    def _():
        acc_ref[:, pl.ds(pl.multiple_of((j - n_up) * tn, tn), tn)] = _dot(act_s[...], wd_ref[...])

    @pl.when(j == pl.num_programs(1) - 1)
    def _():
        _ffn_epilogue(acc_ref, x1_ref, gt_ref, gf_ref, y_ref)


def _ffn_sample_kernel(h_ref, bufg_ref, bufv_ref, wug_ref, wuv_ref, wcg_ref, wcv_ref, bg_ref, bv_ref, wd_ref,
                       x1_ref, gt_ref, gf_ref, y_ref, ug_ref, uv_ref, acc_ref, act_s, *, ts, n_up):
    j = pl.program_id(1)
    db = h_ref.shape[0] // ts
    nbuf = FFN_CONV - 1

    def branch(wu_ref, wc_ref, b_ref, buf_ref, ubuf_ref):
        u = _dot(h_ref[...], wu_ref[...])
        rows = [buf_ref[:, r, :] for r in range(nbuf)]
        rows += [u[t * db:(t + 1) * db] for t in range(ts)]
        for r in range(nbuf):
            ubuf_ref[:, r, :] = rows[ts + r]
        w = wc_ref[...]
        outs = []
        for t in range(ts):
            y = rows[t] * w[0:1] + b_ref[...]
            for r in range(1, FFN_CONV):
                y = y + rows[t + r] * w[r:r + 1]
            outs.append(y)
        return jnp.concatenate(outs, axis=0)

    def up(store):
        store(slice(0, ts * db),
              _silu(branch(wug_ref, wcg_ref, bg_ref, bufg_ref, ug_ref))
              * branch(wuv_ref, wcv_ref, bv_ref, bufv_ref, uv_ref))

    _ffn_phases(j, n_up, up, act_s, wd_ref, acc_ref, x1_ref, gt_ref, gf_ref, y_ref)


def _ffn(h2, x1, ada, g_final, w_up, w_conv, b_conv, w_down, group, rows_per_batch, tm, tf, buf=None):
    m, d = h2.shape
    dff = w_down.shape[0]
    nj = dff // tf
    tn = min(FFN_DOWN_TILE, d)
    b2 = b_conv.reshape(1, 2 * dff)
    up = lambda j: jnp.minimum(j, nj - 1)
    common_w = [pl.BlockSpec((d, tf), lambda i, j: (0, up(j))),
                pl.BlockSpec((d, tf), lambda i, j: (0, nj + up(j))),
                pl.BlockSpec((FFN_CONV, tf), lambda i, j: (0, up(j))),
                pl.BlockSpec((FFN_CONV, tf), lambda i, j: (0, nj + up(j))),
                pl.BlockSpec((1, tf), lambda i, j: (0, up(j))),
                pl.BlockSpec((1, tf), lambda i, j: (0, nj + up(j))),
                pl.BlockSpec((dff, tn), lambda i, j: (0, jnp.maximum(j - nj, 0)))]
    tail = [pl.BlockSpec((tm, d), lambda i, j: (i, 0)),
            _mod_spec(group, d, rows_per_batch, tm, 5),
            pl.BlockSpec((1, d), lambda i, j: (0, 0))]
    y_spec = pl.BlockSpec((tm, d), lambda i, j: (i, 0))
    if group == "prompt":
        bps = rows_per_batch // tm
        nb = m // rows_per_batch
        kern = functools.partial(_ffn_prompt_kernel, blocks_per_seq=bps, n_up=nj)
        first = [pl.BlockSpec((tm, d), lambda i, j: (i, 0))]
        ubuf_spec = pl.BlockSpec((None, SUBLANES, tf), lambda i, j: (i, 0, up(j)))
        ubuf_shape = jax.ShapeDtypeStruct((m // tm, SUBLANES, dff), f32)
        args = (h2,)
        extra_scratch = [pltpu.VMEM((nj, 2, SUBLANES, tf), f32)]
    else:
        extra_scratch = []
        ts = m // rows_per_batch
        kern = functools.partial(_ffn_sample_kernel, ts=ts, n_up=nj)
        first = [pl.BlockSpec((tm, d), lambda i, j: (i, 0)),
                 pl.BlockSpec((rows_per_batch, FFN_CONV - 1, tf), lambda i, j: (0, 0, up(j))),
                 pl.BlockSpec((rows_per_batch, FFN_CONV - 1, tf), lambda i, j: (0, 0, nj + up(j)))]
        ubuf_spec = pl.BlockSpec((rows_per_batch, FFN_CONV - 1, tf), lambda i, j: (0, 0, up(j)))
        ubuf_shape = jax.ShapeDtypeStruct((rows_per_batch, FFN_CONV - 1, dff), f32)
        args = (h2, buf, buf)
    return pl.pallas_call(
        kern,
        grid=(m // tm, nj + d // tn),
        in_specs=first + common_w + tail,
        out_specs=[y_spec, ubuf_spec, ubuf_spec],
        out_shape=[jax.ShapeDtypeStruct((m, d), f32), ubuf_shape, ubuf_shape],
        scratch_shapes=[pltpu.VMEM((tm, d), f32), pltpu.VMEM((tm, dff), bf16)] + extra_scratch,
        compiler_params=_params(56, 2),
        name="ffn_" + group,
    )(*args, w_up, w_up, w_conv, w_conv, b2, b2, w_down, x1, ada, g_final.reshape(1, d))


def _split_w_in(w_in, d):
    sizes = (QKV_W, Z_W, GDN_HEADS, GDN_HEADS, DQ_W, DKV_W, DKV_W, IQ_W, IDX_DIM, IDX_HEADS, d, d)
    offs = [0]
    for s in sizes:
        offs.append(offs[-1] + s)
    seg = lambda i, j: w_in[:, offs[i]:offs[j]]
    pad = jnp.zeros((w_in.shape[0], LANES - (IDX_DIM + 2 * GDN_HEADS + IDX_HEADS)), w_in.dtype)
    small = jnp.concatenate([seg(8, 9), seg(2, 4), seg(9, 10), pad], axis=1)
    return tuple(w.astype(bf16) for w in (seg(0, 2), seg(4, 8), seg(10, 12), small))


def _in_proj(h, w_parts, tm, name):
    return tuple(_matmul(h, w, tm, _mm_tile(w.shape[1]), name + "_" + part)
                 for w, part in zip(w_parts, ("gdn", "dsa", "gate", "small")))


def _ffn_tile(dff):
    for tf in (512, 256, 128):
        if dff % tf == 0:
            return tf
    raise ValueError("d_ff must be a multiple of 128")


def _mm_tile(n):
    for tn in (1024, 512, 256, 128):
        if n % tn == 0:
            return tn
    raise ValueError("projection width must be a multiple of 128")


def kernel(x_prompt, x_sample, c_prompt, c_sample, cache_k, cache_v, cache_idx_k, page_table, state_gdn, state_gdn_conv, state_ffn_conv, w_ada, b_ada, g_norm1, w_in, w_gdn_conv, a_log, dt_bias, g_gdn_norm, w_gdn_out, w_dsa_out, w_o, g_norm2, w_up, w_ffn_conv, b_ffn_conv, w_down, g_final):
    nb, t, d = x_prompt.shape
    db, ts, _ = x_sample.shape
    depth = w_ada.shape[0]
    assert depth == 1 and db == LANES and ts >= GDN_CONV - 1
    n_pages = page_table.shape[1]
    page = cache_k.shape[2]
    past = n_pages * page
    dff = w_down.shape[1]
    l = 0

    n_c = nb + db
    pad_c = (-n_c) % SUBLANES
    c_all = jnp.concatenate([c_prompt, c_sample, jnp.zeros((pad_c, d), f32)], axis=0)
    ada = _ada(c_all, w_ada[l], b_ada[l])
    ada_p = ada[:nb].reshape(nb, 1, 6 * d)
    ada_s = ada[nb:nb + db]

    w_in_parts = _split_w_in(w_in[l], d)
    wg = w_gdn_out[l].astype(bf16)
    wd = w_dsa_out[l].astype(bf16)
    wo = w_o[l].astype(bf16)
    wup = w_up[l].astype(bf16)
    wdn = w_down[l].astype(bf16)
    tf = _ffn_tile(dff)

    xp = x_prompt.reshape(nb * t, d)
    tm_p = min(512, t)
    h1 = _prep(xp, g_norm1[l], ada_p, "prompt", t, tm_p)
    yp_gdn, yp_dsa, yp_gate, yp_small = _in_proj(h1, w_in_parts, min(IN_PROJ_ROWS, nb * t), "in_proj_prompt")
    og_p, s_p = _gdn_prompt(yp_gdn, yp_small, w_gdn_conv[l], a_log[l], dt_bias[l], g_gdn_norm[l], nb, t)
    dq_p, dk_p, iq_p, ik_p, sm_p, kc_p, vc_p = _rope(yp_dsa, yp_small, jnp.arange(t), tm_p, "prompt")
    od_p = _dsa_prompt(iq_p, sm_p, ik_p, dq_p, dk_p, yp_dsa, nb, t, min(DSA_TOPK, t // 4))
    x1_p, h2_p = _merge(og_p, od_p, yp_gate, xp, ada_p, g_norm2[l], wg, wd, wo, "prompt", t, min(256, t))
    y_p, ug_p, uv_p = _ffn(h2_p, x1_p, ada_p, g_final, wup, w_ffn_conv[l], b_ffn_conv[l], wdn,
                           "prompt", t, tm_p, tf)

    nfb = FFN_CONV - 1
    bps_p = t // tm_p
    out_p = (
        y_p.reshape(nb, t, d),
        kc_p.reshape(1, nb, t, DSA_KV_HEADS, DSA_HEAD_DIM),
        vc_p.reshape(1, nb, t, DSA_KV_HEADS, DSA_HEAD_DIM),
        ik_p.reshape(1, nb, t, IDX_DIM),
        s_p[None],
        yp_gdn.reshape(nb, t, -1)[:, t - (GDN_CONV - 1):, :QKV_W][None],
        jnp.concatenate([ug_p[bps_p - 1::bps_p, SUBLANES - nfb:], uv_p[bps_p - 1::bps_p, SUBLANES - nfb:]],
                        axis=-1)[None],
    )

    xs = x_sample.transpose(1, 0, 2).reshape(ts * db, d)
    h1s = _prep(xs, g_norm1[l], ada_s, "sample", db, db)
    ys_gdn, ys_dsa, ys_gate, ys_small = _in_proj(h1s, w_in_parts, ts * db, "in_proj_sample")
    ys_gdn3 = ys_gdn.reshape(ts, db, -1)
    og_s, s_s = _gdn_sample(ys_gdn3, ys_small.reshape(ts, db, LANES), state_gdn_conv[l].transpose(1, 0, 2),
                            state_gdn[l], w_gdn_conv[l], a_log[l], dt_bias[l], g_gdn_norm[l])
    pos_s = jnp.repeat(past + jnp.arange(ts), db)
    dq_s, dk_s, iq_s, ik_s, sm_s, _, _ = _rope(ys_dsa, ys_small, pos_s, db, "sample", db)
    ik_s_b = ik_s.reshape(ts, db, IDX_DIM).transpose(1, 0, 2)
    lp = past + LANES
    tp = -(-ts // SUBLANES) * SUBLANES
    pad_t = lambda a: jnp.pad(a.astype(f32), ((0, 0), (0, tp - ts), (0, 0)))
    scores = _idx_scores_ring(page_table, pad_t(iq_s.reshape(db, ts, IQ_W)), pad_t(sm_s.reshape(db, ts, LANES)),
                         pad_t(ik_s_b), jnp.swapaxes(cache_idx_k[l], 1, 2), lp, ts)
    sel = _select_sample(scores.reshape(db * ts, lp), min(DSA_TOPK, (past + ts) // 4), past, ts)
    dv_s_b = ys_dsa.reshape(ts, db, -1)[:, :, OFF_DV:OFF_DV + DKV_W].transpose(1, 0, 2)
    ck = cache_k[l].reshape(cache_k.shape[1], page * DSA_KV_HEADS, DSA_HEAD_DIM)
    cv = cache_v[l].reshape(cache_v.shape[1], page * DSA_KV_HEADS, DSA_HEAD_DIM)
    od_s = _dsa_sample_ring(page_table, pad_t(dq_s.reshape(db, ts, DQ_W)), sel.reshape(db, ts, lp),
                       pad_t(dk_s.reshape(db, ts, DKV_W)), pad_t(dv_s_b), ck, cv)
    od_s = od_s.astype(bf16).transpose(1, 0, 2).reshape(ts * db, DQ_W)
    x1_s, h2_s = _merge(og_s.reshape(ts * db, Z_W), od_s, ys_gate, xs, ada_s, g_norm2[l], wg, wd, wo,
                        "sample", db, db)
    y_s, ug_s, uv_s = _ffn(h2_s, x1_s, ada_s, g_final, wup, w_ffn_conv[l], b_ffn_conv[l], wdn,
                           "sample", db, ts * db, tf, buf=state_ffn_conv[l])

    out_s = (
        y_s.reshape(ts, db, d).transpose(1, 0, 2),
        dk_s.reshape(1, db, ts, DSA_KV_HEADS, DSA_HEAD_DIM),
        dv_s_b.reshape(1, db, ts, DSA_KV_HEADS, DSA_HEAD_DIM),
        ik_s_b[None],
        s_s[None],
        ys_gdn3[ts - (GDN_CONV - 1):, :, :QKV_W].transpose(1, 0, 2)[None],
        jnp.concatenate([ug_s, uv_s], axis=-1)[None],
    )
    return (out_p[0], out_s[0]) + out_p[1:] + out_s[1:]
```
